```python
import jax, jax.numpy as jnp
from jax import lax
import numpy as np

D_MODEL = 1024
BATCH = 8
SEQ = 2048
DEPTH = 2
DEC_BATCH = 128
DEC_SEQ = 4
PAST_LEN = 16384
PAGE_SIZE = 128

N_MIXERS = 2
N_HGRN_LAYERS = (DEPTH + 1) // 2
N_POOL_LAYERS = DEPTH // 2
HGRN_EXPAND = 128
HGRN_HEADS = D_MODEL // HGRN_EXPAND
HGRN_DK = HGRN_EXPAND
HGRN_DV = D_MODEL // HGRN_HEADS
HGRN_CHUNK = 32
POOL_WINDOWS = (2, 4, 8, 16)
POOL_GROUPS = len(POOL_WINDOWS)
POOL_DG = D_MODEL // POOL_GROUPS
POOL_BUF = max(POOL_WINDOWS) - 1
N_EXPERTS = 32
TOP_K = 4
D_EXPERT = D_MODEL
SWIGLU_LIMIT = 7.0
SWIGLU_ALPHA = 1.702
MOE_BLOCK = 128
RMS_EPS = 1e-6

kernel_name = "hgrn2_pool_moe_adaln_decode_step"


def rms_norm(x, g):
    x32 = x.astype(jnp.float32)
    y = x32 * lax.rsqrt(jnp.mean(x32 * x32, axis=-1, keepdims=True) + RMS_EPS)
    return (y * g.astype(jnp.float32)).astype(x.dtype)


def hgrn2_recurrence(q, k, v, logf, s0):
    B, T, H, DK = q.shape
    C = min(HGRN_CHUNK, T)
    n = -(-T // C)
    pad = n * C - T

    def blocks(a):
        a = jnp.pad(a, ((0, 0), (0, pad), (0, 0), (0, 0)))
        return a.reshape(B, n, C, H, a.shape[-1]).swapaxes(0, 1)

    causal = jnp.tril(jnp.ones((C, C), dtype=bool))[None, :, :, None, None]

    def step(S, blk):
        qc, kc, vc, lc = blk
        b = jnp.cumsum(lc, axis=1)
        decay = jnp.exp(jnp.where(causal, b[:, :, None] - b[:, None], -jnp.inf))
        att = jnp.einsum('bthd,bshd,btshd->bhts', qc, kc, decay)
        o = (jnp.einsum('bhts,bshv->bthv', att, vc)
             + jnp.einsum('bthd,bhdv->bthv', qc * jnp.exp(b), S))
        b_last = b[:, -1]
        S = (S * jnp.exp(b_last)[..., None]
             + jnp.einsum('bshd,bshv->bhdv', kc * jnp.exp(b_last[:, None] - b), vc))
        return S, o

    S, o = lax.scan(step, s0, (blocks(q), blocks(k), blocks(v), blocks(logf)))
    o = o.swapaxes(0, 1).reshape(B, n * C, H, -1)[:, :T]
    return o, S


def hgrn2_mixer(h, s0, w_in, lb, g_out, w_out):
    B, T, _ = h.shape
    HK = HGRN_HEADS * HGRN_DK
    HV = HGRN_HEADS * HGRN_DV
    proj = (h @ w_in).astype(jnp.float32)
    q = jax.nn.silu(proj[..., :HK]).reshape(B, T, HGRN_HEADS, HGRN_DK)
    zf = proj[..., HK:2 * HK].reshape(B, T, HGRN_HEADS, HGRN_DK)
    v = proj[..., 2 * HK:2 * HK + HV].reshape(B, T, HGRN_HEADS, HGRN_DV)
    g = proj[..., 2 * HK + HV:].reshape(B, T, HGRN_HEADS, HGRN_DV)
    lb = lb.astype(jnp.float32).reshape(HGRN_HEADS, HGRN_DK)
    logf = jnp.logaddexp(jnp.log(lb), jnp.log1p(-lb) + jax.nn.log_sigmoid(zf))
    k = (1.0 - lb) * jax.nn.sigmoid(-zf)
    o, S = hgrn2_recurrence(q, k, v, logf, s0.astype(jnp.float32))
    o = o * lax.rsqrt(jnp.mean(o * o, axis=-1, keepdims=True) + RMS_EPS) * g_out.astype(jnp.float32)
    o = o * jax.nn.silu(g)
    y = o.reshape(B, T, HV).astype(h.dtype) @ w_out
    return y, S.astype(s0.dtype)


def pool_mixer(h, buf, start_pos, w_grp, scale):
    B, T, D = h.shape
    ext = jnp.concatenate([buf.astype(h.dtype), h], axis=1)
    cs = jnp.pad(jnp.cumsum(ext.astype(jnp.float32), axis=1), ((0, 0), (1, 0), (0, 0)))
    P = POOL_BUF
    t = jnp.arange(T)
    means = []
    for gi, w in enumerate(POOL_WINDOWS):
        sl = slice(gi * POOL_DG, (gi + 1) * POOL_DG)
        win = cs[:, P + 1:, sl] - cs[:, P + 1 - w:P + 1 - w + T, sl]
        cnt = jnp.minimum(start_pos + t + 1, w).astype(jnp.float32)[None, :, None]
        means.append(win / cnt)
    pooled = jnp.stack(means, axis=2) - h.astype(jnp.float32).reshape(B, T, POOL_GROUPS, POOL_DG)
    y = jnp.einsum('btgc,gcd->btgd', pooled.astype(h.dtype), w_grp).reshape(B, T, D) * scale
    return y, ext[:, -P:]


def moe_ffn(h, w_r, b_r, w_gu, b_gu, w_dn, b_dn):
    B, T, D = h.shape
    N = B * T
    xt = h.reshape(N, D)
    logits = (xt @ w_r + b_r).astype(jnp.float32)
    top_v, top_i = lax.top_k(logits, TOP_K)
    gates = jax.nn.softmax(top_v, axis=-1)
    A = N * TOP_K
    flat_e = top_i.reshape(-1)
    flat_tok = jnp.arange(A, dtype=jnp.int32) // TOP_K
    flat_w = gates.reshape(-1)
    order = jnp.argsort(flat_e)
    se = flat_e[order]
    counts = jnp.zeros((N_EXPERTS,), jnp.int32).at[flat_e].add(1)
    padded = (counts + MOE_BLOCK - 1) // MOE_BLOCK * MOE_BLOCK
    pad_end = jnp.cumsum(padded)
    pad_start = pad_end - padded
    start = jnp.cumsum(counts) - counts
    dest = pad_start[se] + (jnp.arange(A, dtype=jnp.int32) - start[se])
    n_blocks = -(-(A + N_EXPERTS * (MOE_BLOCK - 1)) // MOE_BLOCK)
    n_slots = n_blocks * MOE_BLOCK
    slot_tok = jnp.full((n_slots,), N, jnp.int32).at[dest].set(flat_tok[order])
    slot_w = jnp.zeros((n_slots,), jnp.float32).at[dest].set(flat_w[order])
    block_start = jnp.arange(n_blocks, dtype=jnp.int32) * MOE_BLOCK
    block_e = jnp.minimum(jnp.sum(block_start[:, None] >= pad_end[None, :], axis=1), N_EXPERTS - 1)
    x_pad = jnp.concatenate([xt, jnp.zeros((1, D), xt.dtype)], axis=0)
    xb = x_pad[slot_tok].reshape(n_blocks, MOE_BLOCK, D)

    def expert_rows(args):
        xr, e = args
        gu = xr @ w_gu[e] + b_gu[e]
        gate = jnp.minimum(gu[..., :D_EXPERT], SWIGLU_LIMIT)
        up = jnp.clip(gu[..., D_EXPERT:], -SWIGLU_LIMIT, SWIGLU_LIMIT)
        glu = gate * jax.nn.sigmoid(SWIGLU_ALPHA * gate)
        return ((up + 1.0) * glu) @ w_dn[e] + b_dn[e]

    yb = lax.map(expert_rows, (xb, block_e))
    ys = yb.reshape(n_slots, D) * slot_w[:, None].astype(yb.dtype)
    out = jax.ops.segment_sum(ys, slot_tok, num_segments=N + 1)[:N]
    return out.reshape(B, T, D)


def run_trunk(x, c, s_hgrn, buf_pool, start_pos, g_norm_mix, g_norm_ffn, w_ada, b_ada,
              w_in_hgrn, lb_logits, g_out_hgrn, w_out_hgrn, w_grp_pool, scale_pool,
              w_router, b_router, w_gate_up, b_gate_up, w_down, b_down, g_final):
    lb_all = jnp.cumsum(jax.nn.softmax(lb_logits.astype(jnp.float32), axis=0), axis=0)
    new_s, new_buf = [], []
    for l in range(DEPTH):
        mod = (jax.nn.silu(c) @ w_ada[l] + b_ada[l])[:, None, :]
        sh1, sc1, gt1, sh2, sc2, gt2 = jnp.split(mod, 6, axis=-1)
        h = rms_norm(x, g_norm_mix[l]) * (1.0 + sc1) + sh1
        j = l // N_MIXERS
        if l % N_MIXERS == 0:
            y, s = hgrn2_mixer(h, s_hgrn[j], w_in_hgrn[j], lb_all[j], g_out_hgrn[j], w_out_hgrn[j])
            new_s.append(s)
        else:
            y, bnew = pool_mixer(h, buf_pool[j], start_pos, w_grp_pool[j], scale_pool[j])
            new_buf.append(bnew)
        x = x + gt1 * y
        h = rms_norm(x, g_norm_ffn[l]) * (1.0 + sc2) + sh2
        x = x + gt2 * moe_ffn(h, w_router[l], b_router[l], w_gate_up[l], b_gate_up[l], w_down[l], b_down[l])
    return rms_norm(x, g_final), jnp.stack(new_s), jnp.stack(new_buf)


def setup_inputs(seed: int = 0) -> dict:
    key = jax.random.key(seed)
    ks = jax.random.split(key, 24)

    def nrm(k, shape, s):
        return jax.random.normal(k, shape, jnp.float32) * s

    HK = HGRN_HEADS * HGRN_DK
    HV = HGRN_HEADS * HGRN_DV
    return {
        "x_prompt": nrm(ks[0], (BATCH, SEQ, D_MODEL), 1.0),
        "x_sample": nrm(ks[1], (DEC_BATCH, DEC_SEQ, D_MODEL), 1.0),
        "c_prompt": nrm(ks[2], (BATCH, D_MODEL), 1.0),
        "c_sample": nrm(ks[3], (DEC_BATCH, D_MODEL), 1.0),
        "state_hgrn": nrm(ks[4], (N_HGRN_LAYERS, DEC_BATCH, HGRN_HEADS, HGRN_DK, HGRN_DV), 0.5),
        "cache_pool": nrm(ks[5], (N_POOL_LAYERS, DEC_BATCH, POOL_BUF, D_MODEL), 1.0),
        "g_norm_mix": 1.0 + nrm(ks[6], (DEPTH, D_MODEL), 0.02),
        "g_norm_ffn": 1.0 + nrm(ks[7], (DEPTH, D_MODEL), 0.02),
        "w_ada": nrm(ks[8], (DEPTH, D_MODEL, 6 * D_MODEL), 0.5 * D_MODEL ** -0.5),
        "b_ada": nrm(ks[9], (DEPTH, 6 * D_MODEL), 0.02),
        "w_in_hgrn": nrm(ks[10], (N_HGRN_LAYERS, D_MODEL, 2 * HK + 2 * HV), D_MODEL ** -0.5),
        "lb_logits": nrm(ks[11], (N_HGRN_LAYERS + 1, HK), 1.0),
        "g_out_hgrn": 1.0 + nrm(ks[12], (N_HGRN_LAYERS, HGRN_DV), 0.02),
        "w_out_hgrn": nrm(ks[13], (N_HGRN_LAYERS, HV, D_MODEL), HV ** -0.5),
        "w_grp_pool": nrm(ks[14], (N_POOL_LAYERS, POOL_GROUPS, POOL_DG, POOL_DG), POOL_DG ** -0.5),
        "scale_pool": 1.0 + nrm(ks[15], (N_POOL_LAYERS, D_MODEL), 0.1),
        "w_router": nrm(ks[16], (DEPTH, D_MODEL, N_EXPERTS), D_MODEL ** -0.5),
        "b_router": nrm(ks[17], (DEPTH, N_EXPERTS), 0.01),
        "w_gate_up": nrm(ks[18], (DEPTH, N_EXPERTS, D_MODEL, 2 * D_EXPERT), D_MODEL ** -0.5),
        "b_gate_up": nrm(ks[19], (DEPTH, N_EXPERTS, 2 * D_EXPERT), 0.02),
        "w_down": nrm(ks[20], (DEPTH, N_EXPERTS, D_EXPERT, D_MODEL), D_EXPERT ** -0.5),
        "b_down": nrm(ks[21], (DEPTH, N_EXPERTS, D_MODEL), 0.02),
        "g_final": 1.0 + nrm(ks[22], (D_MODEL,), 0.02),
    }


def reference(x_prompt, x_sample, c_prompt, c_sample, state_hgrn, cache_pool,
              g_norm_mix, g_norm_ffn, w_ada, b_ada, w_in_hgrn, lb_logits, g_out_hgrn, w_out_hgrn,
              w_grp_pool, scale_pool, w_router, b_router, w_gate_up, b_gate_up, w_down, b_down, g_final):
    s0_prompt = jnp.zeros((N_HGRN_LAYERS, x_prompt.shape[0], HGRN_HEADS, HGRN_DK, HGRN_DV), state_hgrn.dtype)
    buf0_prompt = jnp.zeros((N_POOL_LAYERS, x_prompt.shape[0], POOL_BUF, D_MODEL), cache_pool.dtype)
    y_prompt, state_hgrn_prompt, cache_pool_prompt = run_trunk(
        x_prompt, c_prompt, s0_prompt, buf0_prompt, 0,
        g_norm_mix, g_norm_ffn, w_ada, b_ada, w_in_hgrn, lb_logits, g_out_hgrn, w_out_hgrn,
        w_grp_pool, scale_pool, w_router, b_router, w_gate_up, b_gate_up, w_down, b_down, g_final)
    y_sample, state_hgrn_sample, cache_pool_sample = run_trunk(
        x_sample, c_sample, state_hgrn, cache_pool, PAST_LEN,
        g_norm_mix, g_norm_ffn, w_ada, b_ada, w_in_hgrn, lb_logits, g_out_hgrn, w_out_hgrn,
        w_grp_pool, scale_pool, w_router, b_router, w_gate_up, b_gate_up, w_down, b_down, g_final)
    return (y_prompt, y_sample, state_hgrn_prompt, state_hgrn_sample, cache_pool_prompt, cache_pool_sample)
```

```python
import functools

import jax
import jax.numpy as jnp
from jax import lax
from jax.experimental import pallas as pl
from jax.experimental.pallas import tpu as pltpu

F32 = jnp.float32
BF16 = jnp.bfloat16

RMS_EPS = 1e-6
LANES = 128
HEAD_DIM = 128
VMEM_LIMIT = 56 * 1024 * 1024

_dot = functools.partial(jnp.dot, preferred_element_type=F32)


def _params(*semantics):
    return pltpu.CompilerParams(dimension_semantics=semantics, vmem_limit_bytes=VMEM_LIMIT)


def _split_bf16(x, n):
    parts, r = [], x
    for _ in range(n):
        p = r.astype(BF16)
        parts.append(p)
        r = r - p.astype(F32)
    return parts


def _dot_hp(a, b):
    a_hi, a_lo = _split_bf16(a, 2)
    b_hi, b_lo = _split_bf16(b, 2)
    return _dot(a_hi, b_hi) + (_dot(a_hi, b_lo) + _dot(a_lo, b_hi))


def _sigmoid(x):
    return 1.0 / (1.0 + jnp.exp(-x))


def _silu(x):
    return x * _sigmoid(x)


def _rms_norm(x, g):
    ms = jnp.mean(x * x, axis=-1, keepdims=True)
    return x * lax.rsqrt(ms + RMS_EPS) * g


def _adaln_kernel(c_ref, w_ref, b_ref, o_ref):
    o_ref[0] = _dot_hp(_silu(c_ref[...]), w_ref[0]) + b_ref[0]


def adaln(c_all, w_ada, b_ada, *, col_block=1536):
    n_seq, d = c_all.shape
    n_layers, _, d6 = w_ada.shape
    return pl.pallas_call(
        _adaln_kernel,
        grid=(n_layers, d6 // col_block),
        in_specs=[
            pl.BlockSpec((n_seq, d), lambda l, j: (0, 0)),
            pl.BlockSpec((1, d, col_block), lambda l, j: (l, 0, j)),
            pl.BlockSpec((1, 1, col_block), lambda l, j: (l, 0, j)),
        ],
        out_specs=pl.BlockSpec((1, n_seq, col_block), lambda l, j: (l, 0, j)),
        out_shape=jax.ShapeDtypeStruct((n_layers, n_seq, d6), F32),
        compiler_params=_params("parallel", "parallel"),
        name="adaln",
    )(c_all, w_ada, b_ada.reshape(n_layers, 1, d6))


def _mod_spec(mod, k, d, tile, rows_per_seq):
    if rows_per_seq is None:
        return pl.BlockSpec((tile, d), lambda i: (0, k))
    tiles_per_seq = rows_per_seq // tile
    return pl.BlockSpec((1, 1, d), lambda i: (i // tiles_per_seq, 0, k))


def _norm_proj_kernel(x_ref, sh_ref, sc_ref, g_ref, w_ref, o_ref):
    d = x_ref.shape[-1]
    h = _rms_norm(x_ref[...], g_ref[...]) * (1.0 + sc_ref[...].reshape(-1, d)) + sh_ref[...].reshape(-1, d)
    o_ref[...] = _dot(h.astype(BF16), w_ref[...])


def norm_proj(x, mod, g, w_bf16, *, tile, rows_per_seq):
    n, d = x.shape
    p = w_bf16.shape[1]
    return pl.pallas_call(
        _norm_proj_kernel,
        grid=(n // tile,),
        in_specs=[
            pl.BlockSpec((tile, d), lambda i: (i, 0)),
            _mod_spec(mod, 0, d, tile, rows_per_seq),
            _mod_spec(mod, 1, d, tile, rows_per_seq),
            pl.BlockSpec((1, d), lambda i: (0, 0)),
            pl.BlockSpec((d, p), lambda i: (0, 0)),
        ],
        out_specs=pl.BlockSpec((tile, p), lambda i: (i, 0)),
        out_shape=jax.ShapeDtypeStruct((n, p), F32),
        compiler_params=_params("parallel"),
        name="hgrn_norm_proj",
    )(x, mod, mod, g.reshape(1, d), w_bf16)


def _cumsum_rows(x, tri):
    hi, mid, lo = _split_bf16(x, 3)
    return _dot(tri, hi) + (_dot(tri, mid) + _dot(tri, lo))


def _hgrn_chunk(proj, lb, gout, st_refs, seq, n_valid, c_sub):
    c = proj.shape[0]
    hk = proj.shape[1] // 4
    n_heads = hk // HEAD_DIM
    row = lax.broadcasted_iota(jnp.int32, (c, c), 0)
    col = lax.broadcasted_iota(jnp.int32, (c, c), 1)
    causal = row >= col
    tri = causal.astype(BF16)

    zf = proj[:, hk:2 * hk]
    e = jnp.exp(-jnp.abs(zf))
    r = 1.0 / (1.0 + e)
    pos = zf >= 0
    sig_p = jnp.where(pos, 1.0, e) * r
    sig_n = jnp.where(pos, e, 1.0) * r
    logf = jnp.log(lb + (1.0 - lb) * sig_p)
    k = (1.0 - lb) * sig_n
    if n_valid < c:
        live = lax.broadcasted_iota(jnp.int32, (c, 1), 0) < n_valid
        logf = jnp.where(live, logf, 0.0)
        k = jnp.where(live, k, 0.0)
    b = _cumsum_rows(logf, tri)
    q = _silu(proj[:, :hk])
    v = proj[:, 2 * hk:3 * hk]
    gate = _silu(proj[:, 3 * hk:])

    outs = []
    for h in range(n_heads):
        hs = slice(h * HEAD_DIM, (h + 1) * HEAD_DIM)
        bh, qh, kh, vh = b[:, hs], q[:, hs], k[:, hs], v[:, hs]
        vb = vh.astype(BF16)
        a_rows = []
        for i in range(c // c_sub):
            rs = slice(i * c_sub, (i + 1) * c_sub)
            ref = bh[i * c_sub + c_sub // 2:i * c_sub + c_sub // 2 + 1, :]
            q_hat = (qh[rs] * jnp.exp(bh[rs] - ref)).astype(BF16)
            k_hat = (kh * jnp.exp(jnp.minimum(ref - bh, 80.0))).astype(BF16)
            a_rows.append(lax.dot_general(q_hat, k_hat, (((1,), (1,)), ((), ())), preferred_element_type=F32))
        att = jnp.where(causal, jnp.concatenate(a_rows, axis=0) if len(a_rows) > 1 else a_rows[0], 0.0)
        o = _dot(att.astype(BF16), vb)
        st = st_refs[seq, h]
        o = o + lax.dot_general((qh * jnp.exp(bh)).astype(BF16), st.astype(BF16),
                                (((1,), (1,)), ((), ())), preferred_element_type=F32)
        b_last = bh[c - 1:c, :]
        k_dec = (kh * jnp.exp(b_last - bh)).astype(BF16)
        st_refs[seq, h] = st * jnp.exp(b_last) + lax.dot_general(
            vb, k_dec, (((0,), (0,)), ((), ())), preferred_element_type=F32)
        o = o * lax.rsqrt(jnp.mean(o * o, axis=-1, keepdims=True) + RMS_EPS) * gout
        outs.append(o * gate[:, hs])
    return jnp.concatenate(outs, axis=-1)


def _lower_bound(lb_logits, layer):
    e = jnp.exp(lb_logits - jnp.max(lb_logits, axis=0, keepdims=True))
    return jnp.sum(e[:layer + 1], axis=0, keepdims=True) / jnp.sum(e, axis=0, keepdims=True)


def _hgrn_rec_kernel(*refs, chunk, c_sub, n_valid, has_state, layer):
    if has_state:
        proj_ref, lb_ref, gout_ref, s0_ref, o_ref, sout_ref, st_ref = refs
    else:
        proj_ref, lb_ref, gout_ref, o_ref, sout_ref, st_ref = refs
    bb, tb, _ = proj_ref.shape
    n_heads = st_ref.shape[1]
    j = pl.program_id(1)

    @pl.when(j == 0)
    def _():
        if has_state:
            for s in range(bb):
                for h in range(n_heads):
                    st_ref[s, h] = s0_ref[s, h].T
        else:
            st_ref[...] = jnp.zeros_like(st_ref)

    lb = _lower_bound(lb_ref[...], layer)
    gout = gout_ref[...]

    def seq_body(s, carry):
        def chunk_body(ci, carry2):
            rows = pl.ds(pl.multiple_of(ci * chunk, chunk), chunk)
            o_ref[s, rows, :] = _hgrn_chunk(proj_ref[s, rows, :], lb, gout, st_ref, s, n_valid, c_sub)
            return carry2
        return lax.fori_loop(0, tb // chunk, chunk_body, carry)
    lax.fori_loop(0, bb, seq_body, 0)

    @pl.when(j == pl.num_programs(1) - 1)
    def _():
        for s in range(bb):
            for h in range(n_heads):
                sout_ref[s, h] = st_ref[s, h].T


def hgrn_recurrence(proj, lb_logits, g_out, s0, *, layer, seq_block, time_block, chunk, c_sub, n_valid):
    bsz, t, p = proj.shape
    hk = p // 4
    n_heads = hk // HEAD_DIM
    has_state = s0 is not None
    st_shape = (seq_block, n_heads, HEAD_DIM, HEAD_DIM)
    st_spec = pl.BlockSpec(st_shape, lambda i, j: (i, 0, 0, 0))
    in_specs = [
        pl.BlockSpec((seq_block, time_block, p), lambda i, j: (i, j, 0)),
        pl.BlockSpec(lb_logits.shape, lambda i, j: (0, 0)),
        pl.BlockSpec((1, HEAD_DIM), lambda i, j: (0, 0)),
    ]
    args = [proj, lb_logits, g_out.reshape(1, HEAD_DIM)]
    if has_state:
        in_specs.append(st_spec)
        args.append(s0)
    return pl.pallas_call(
        functools.partial(_hgrn_rec_kernel, chunk=chunk, c_sub=c_sub, n_valid=n_valid,
                          has_state=has_state, layer=layer),
        grid=(bsz // seq_block, t // time_block),
        in_specs=in_specs,
        out_specs=[pl.BlockSpec((seq_block, time_block, hk), lambda i, j: (i, j, 0)), st_spec],
        out_shape=[jax.ShapeDtypeStruct((bsz, t, hk), F32),
                   jax.ShapeDtypeStruct((bsz, n_heads, HEAD_DIM, HEAD_DIM), F32)],
        scratch_shapes=[pltpu.VMEM(st_shape, F32)],
        compiler_params=_params("parallel", "arbitrary"),
        name="hgrn_recurrence",
    )(*args)


def _resid_router_kernel(*refs, top_k, n_experts, has_w_out):
    if has_w_out:
        (x_ref, y_ref, wo_ref, gt_ref, sh_ref, sc_ref, g_ref, wr_ref, br_ref, cnt_in_ref,
         x1_ref, h_ref, idx_ref, gate_ref, rank_ref, cnt_ref, carry_ref) = refs
    else:
        (x_ref, y_ref, gt_ref, sh_ref, sc_ref, g_ref, wr_ref, br_ref, cnt_in_ref,
         x1_ref, h_ref, idx_ref, gate_ref, rank_ref, cnt_ref, carry_ref) = refs
    tile, d = x_ref.shape
    i = pl.program_id(0)

    @pl.when(i == 0)
    def _():
        carry_ref[...] = cnt_in_ref[...]

    y = y_ref[...]
    if has_w_out:
        y = _dot(y.astype(BF16), wo_ref[...])
    x1 = x_ref[...] + gt_ref[...].reshape(-1, d) * y
    x1_ref[...] = x1
    h = _rms_norm(x1, g_ref[...]) * (1.0 + sc_ref[...].reshape(-1, d)) + sh_ref[...].reshape(-1, d)
    h_ref[...] = h

    lane = lax.broadcasted_iota(jnp.int32, (tile, LANES), 1).astype(F32)
    logits = jnp.where(lane < n_experts, _dot_hp(h, wr_ref[...]) + br_ref[...], -jnp.inf)
    picks, vals = [], []
    for _ in range(top_k):
        m = jnp.max(logits, axis=-1, keepdims=True)
        pick = jnp.min(jnp.where(logits == m, lane, float(LANES)), axis=-1, keepdims=True)
        picks.append(pick)
        vals.append(m)
        logits = jnp.where(lane == pick, -jnp.inf, logits)
    exps = [jnp.exp(v - vals[0]) for v in vals]
    denom = exps[0]
    for e in exps[1:]:
        denom = denom + e

    onehots = [(lane == p).astype(F32) for p in picks]
    oh_sum = onehots[0]
    for oh in onehots[1:]:
        oh_sum = oh_sum + oh
    row = lax.broadcasted_iota(jnp.int32, (tile, tile), 0)
    col = lax.broadcasted_iota(jnp.int32, (tile, tile), 1)
    before = _dot((row > col).astype(BF16), oh_sum.astype(BF16)) + carry_ref[...]
    idx_out = jnp.zeros((tile, LANES), F32)
    gate_out = jnp.zeros((tile, LANES), F32)
    rank_out = jnp.zeros((tile, LANES), F32)
    for k in range(top_k):
        rank_k = jnp.sum(onehots[k] * before, axis=-1, keepdims=True)
        idx_out = jnp.where(lane == k, picks[k], idx_out)
        gate_out = jnp.where(lane == k, exps[k] / denom, gate_out)
        rank_out = jnp.where(lane == k, rank_k, rank_out)
    idx_ref[...] = idx_out.astype(jnp.int32)
    gate_ref[...] = gate_out
    rank_ref[...] = rank_out.astype(jnp.int32)
    carry_ref[...] = carry_ref[...] + jnp.sum(oh_sum, axis=0, keepdims=True)
    cnt_ref[...] = carry_ref[...]


def resid_router(x, y, w_out_bf16, mod, g, w_r, b_r, cnt_in, *, tile, rows_per_seq, top_k):
    n, d = x.shape
    n_experts = w_r.shape[1]
    w_r_pad = jnp.pad(w_r, ((0, 0), (0, LANES - n_experts)))
    b_r_pad = jnp.pad(b_r, (0, LANES - n_experts)).reshape(1, LANES)
    has_w_out = w_out_bf16 is not None
    row_spec = pl.BlockSpec((tile, d), lambda i: (i, 0))
    lane_spec = pl.BlockSpec((tile, LANES), lambda i: (i, 0))
    full = lambda a: pl.BlockSpec(a.shape, lambda i: (0,) * a.ndim)
    in_specs = [row_spec, pl.BlockSpec((tile, y.shape[1]), lambda i: (i, 0))]
    args = [x, y]
    if has_w_out:
        in_specs.append(full(w_out_bf16))
        args.append(w_out_bf16)
    in_specs += [_mod_spec(mod, 2, d, tile, rows_per_seq), _mod_spec(mod, 3, d, tile, rows_per_seq),
                 _mod_spec(mod, 4, d, tile, rows_per_seq), pl.BlockSpec((1, d), lambda i: (0, 0)),
                 full(w_r_pad), full(b_r_pad), full(cnt_in)]
    args += [mod, mod, mod, g.reshape(1, d), w_r_pad, b_r_pad, cnt_in]
    return pl.pallas_call(
        functools.partial(_resid_router_kernel, top_k=top_k, n_experts=n_experts, has_w_out=has_w_out),
        grid=(n // tile,),
        in_specs=in_specs,
        out_specs=[row_spec, row_spec, lane_spec, lane_spec, lane_spec, pl.BlockSpec((1, LANES), lambda i: (0, 0))],
        out_shape=[jax.ShapeDtypeStruct((n, d), F32), jax.ShapeDtypeStruct((n, d), F32),
                   jax.ShapeDtypeStruct((n, LANES), jnp.int32), jax.ShapeDtypeStruct((n, LANES), F32),
                   jax.ShapeDtypeStruct((n, LANES), jnp.int32), jax.ShapeDtypeStruct((1, LANES), F32)],
        scratch_shapes=[pltpu.VMEM((1, LANES), F32)],
        compiler_params=_params("arbitrary"),
        name="resid_router",
    )(*args)


def _dispatch_kernel(zero_tile_ref, dest_ref, h_ref, xs_ref, zero_ref, sem, zsem, *, top_k, row_tile):
    tile = h_ref.shape[0]
    i = pl.program_id(0)

    @pl.when(i == 0)
    def _():
        zero_ref[...] = jnp.zeros_like(zero_ref)
        n_zero = zero_tile_ref.shape[0]

        def zero_copy(e):
            start = pl.multiple_of(zero_tile_ref[e] * row_tile, row_tile)
            return pltpu.make_async_copy(zero_ref, xs_ref.at[pl.ds(start, row_tile), :], zsem)

        def start(e, c):
            @pl.when(zero_tile_ref[e] >= 0)
            def _():
                zero_copy(e).start()
            return c

        def wait(e, c):
            @pl.when(zero_tile_ref[e] >= 0)
            def _():
                zero_copy(e).wait()
            return c
        lax.fori_loop(0, n_zero, start, 0)
        lax.fori_loop(0, n_zero, wait, 0)

    def row_copy(r, k):
        return pltpu.make_async_copy(h_ref.at[pl.ds(r, 1), :],
                                     xs_ref.at[pl.ds(dest_ref[0, 0, r * top_k + k], 1), :], sem)

    def issue(r, c):
        for k in range(top_k):
            row_copy(r, k).start()
        return c

    def drain(r, c):
        for k in range(top_k):
            row_copy(r, k).wait()
        return c
    lax.fori_loop(0, tile, issue, 0)
    lax.fori_loop(0, tile, drain, 0)


def moe_dispatch(h, dest, zero_tiles, *, n_slots, tile, row_tile):
    n, d = h.shape
    top_k = dest.shape[1]
    return pl.pallas_call(
        functools.partial(_dispatch_kernel, top_k=top_k, row_tile=row_tile),
        grid_spec=pltpu.PrefetchScalarGridSpec(
            num_scalar_prefetch=1,
            grid=(n // tile,),
            in_specs=[
                pl.BlockSpec((1, 1, tile * top_k), lambda i, lt: (i, 0, 0), memory_space=pltpu.SMEM),
                pl.BlockSpec((tile, d), lambda i, lt: (i, 0)),
            ],
            out_specs=pl.BlockSpec(memory_space=pl.ANY),
            scratch_shapes=[pltpu.VMEM((row_tile, d), F32), pltpu.SemaphoreType.DMA, pltpu.SemaphoreType.DMA],
        ),
        out_shape=jax.ShapeDtypeStruct((n_slots, d), F32),
        compiler_params=_params("arbitrary"),
        name="moe_dispatch",
    )(zero_tiles, dest.reshape(n // tile, 1, tile * top_k), h)


def _experts_kernel(te_ref, first_ref, used_ref, xs_ref, wgu_ref, bgu_ref, wdn_ref, bdn_ref, ys_ref,
                    wgu_bf, wdn_bf, *, limit, alpha, col_chunk):
    i = pl.program_id(0)

    @pl.when(i < used_ref[0])
    def _():
        @pl.when(first_ref[i] == 1)
        def _():
            wgu_bf[...] = wgu_ref[0].astype(BF16)
            wdn_bf[...] = wdn_ref[0].astype(BF16)

        d_ff = wdn_bf.shape[0]
        x = xs_ref[...].astype(BF16)
        y = None
        for c0 in range(0, d_ff, col_chunk):
            cs = slice(c0, c0 + col_chunk)
            us = slice(d_ff + c0, d_ff + c0 + col_chunk)
            gate = jnp.minimum(_dot(x, wgu_bf[:, cs]) + bgu_ref[0, :, cs], limit)
            up = jnp.clip(_dot(x, wgu_bf[:, us]) + bgu_ref[0, :, us], -limit, limit)
            act = ((up + 1.0) * (gate * _sigmoid(alpha * gate))).astype(BF16)
            part = _dot(act, wdn_bf[cs, :])
            y = part if y is None else y + part
        ys_ref[...] = y + bdn_ref[0]

    @pl.when(i >= used_ref[0])
    def _():
        ys_ref[...] = jnp.zeros_like(ys_ref)


def moe_experts(xs, w_gu, b_gu, w_dn, b_dn, tile_expert, tile_first, n_used, *, row_tile, limit, alpha):
    n_slots, d = xs.shape
    n_experts, _, d_gu = w_gu.shape
    d_ff = w_dn.shape[1]
    n_tiles = n_slots // row_tile

    def row_map(i, te, first, used):
        return (jnp.minimum(i, used[0] - 1), 0)

    def w_map(i, te, first, used):
        return (te[jnp.minimum(i, used[0] - 1)], 0, 0)

    return pl.pallas_call(
        functools.partial(_experts_kernel, limit=limit, alpha=alpha, col_chunk=256),
        grid_spec=pltpu.PrefetchScalarGridSpec(
            num_scalar_prefetch=3,
            grid=(n_tiles,),
            in_specs=[
                pl.BlockSpec((row_tile, d), row_map),
                pl.BlockSpec((1, d, d_gu), w_map),
                pl.BlockSpec((1, 1, d_gu), w_map),
                pl.BlockSpec((1, d_ff, d), w_map),
                pl.BlockSpec((1, 1, d), w_map),
            ],
            out_specs=pl.BlockSpec((row_tile, d), lambda i, te, first, used: (i, 0)),
            scratch_shapes=[pltpu.VMEM((d, d_gu), BF16), pltpu.VMEM((d_ff, d), BF16)],
        ),
        out_shape=jax.ShapeDtypeStruct((n_slots, d), F32),
        compiler_params=_params("arbitrary"),
        name="moe_experts",
    )(tile_expert, tile_first, n_used, xs, w_gu, b_gu.reshape(n_experts, 1, d_gu), w_dn,
      b_dn.reshape(n_experts, 1, d))


def _combine_kernel(dest_ref, dest_next_ref, x_ref, gate_ref, gt_ref, gfin_ref, ys_ref, o_ref, buf, sem,
                    *, top_k, final_norm):
    tile, d = x_ref.shape
    i = pl.program_id(0)
    n = pl.num_programs(0)
    slot = i % 2

    def row_copy(idx_ref, buf_slot, r, k):
        return pltpu.make_async_copy(ys_ref.at[pl.ds(idx_ref[0, 0, r * top_k + k], 1), :],
                                     buf.at[buf_slot, k, pl.ds(r, 1), :], sem.at[buf_slot])

    def issue(idx_ref, buf_slot):
        def body(r, c):
            for k in range(top_k):
                row_copy(idx_ref, buf_slot, r, k).start()
            return c
        lax.fori_loop(0, tile, body, 0)

    @pl.when(i == 0)
    def _():
        issue(dest_ref, 0)

    @pl.when(i + 1 < n)
    def _():
        issue(dest_next_ref, 1 - slot)

    def drain(r, c):
        for k in range(top_k):
            row_copy(dest_ref, slot, r, k).wait()
        return c
    lax.fori_loop(0, tile, drain, 0)

    gates = gate_ref[...]
    acc = gates[:, 0:1] * buf[slot, 0]
    for k in range(1, top_k):
        acc = acc + gates[:, k:k + 1] * buf[slot, k]
    out = x_ref[...] + gt_ref[...].reshape(-1, d) * acc
    if final_norm:
        out = _rms_norm(out, gfin_ref[...])
    o_ref[...] = out


def moe_combine(x, ys, dest, gates, mod, g_final, *, tile, rows_per_seq, final_norm):
    n, d = x.shape
    top_k = dest.shape[1]
    n_tiles = n // tile
    dest3 = dest.reshape(n_tiles, 1, tile * top_k)
    smem = lambda f: pl.BlockSpec((1, 1, tile * top_k), f, memory_space=pltpu.SMEM)
    return pl.pallas_call(
        functools.partial(_combine_kernel, top_k=top_k, final_norm=final_norm),
        grid=(n_tiles,),
        in_specs=[
            smem(lambda i: (i, 0, 0)),
            smem(lambda i: (jnp.minimum(i + 1, n_tiles - 1), 0, 0)),
            pl.BlockSpec((tile, d), lambda i: (i, 0)),
            pl.BlockSpec((tile, LANES), lambda i: (i, 0)),
            _mod_spec(mod, 5, d, tile, rows_per_seq),
            pl.BlockSpec((1, d), lambda i: (0, 0)),
            pl.BlockSpec(memory_space=pl.ANY),
        ],
        out_specs=pl.BlockSpec((tile, d), lambda i: (i, 0)),
        out_shape=jax.ShapeDtypeStruct((n, d), F32),
        scratch_shapes=[pltpu.VMEM((2, top_k, tile, d), F32), pltpu.SemaphoreType.DMA((2,))],
        compiler_params=_params("arbitrary"),
        name="moe_combine",
    )(dest3, dest3, x, gates, mod, g_final.reshape(1, d), ys)


def _pool_groups(h, window_sum, counts, w_ref, scale):
    n_groups = w_ref.shape[0]
    dg = h.shape[-1] // n_groups
    outs = []
    for gi in range(n_groups):
        cols = slice(gi * dg, (gi + 1) * dg)
        pooled = window_sum(gi, cols) / counts[gi] - h[:, cols]
        outs.append(_dot(pooled.astype(BF16), w_ref[gi]))
    return jnp.concatenate(outs, axis=-1) * scale


def _pool_prompt_kernel(x_ref, sh_ref, sc_ref, g_ref, w_ref, scale_ref, y_ref, cache_ref, ext_ref,
                        *, windows, halo):
    tile, d = x_ref.shape[1:]
    j = pl.program_id(1)

    @pl.when(j == 0)
    def _():
        ext_ref[0:halo, :] = jnp.zeros((halo, d), F32)

    h = _rms_norm(x_ref[0], g_ref[...]) * (1.0 + sc_ref[...].reshape(-1, d)) + sh_ref[...].reshape(-1, d)
    ext_ref[halo:halo + tile, :] = h
    pos = j * tile + lax.broadcasted_iota(jnp.int32, (tile, 1), 0)

    def window_sum(gi, cols):
        acc = h[:, cols]
        for s in range(1, windows[gi]):
            acc = acc + ext_ref[halo - s:halo - s + tile, cols]
        return acc
    counts = [jnp.minimum(pos + 1, w).astype(F32) for w in windows]
    y_ref[0] = _pool_groups(h, window_sum, counts, w_ref, scale_ref[...])

    n_keep = cache_ref.shape[1]
    @pl.when(j == pl.num_programs(1) - 1)
    def _():
        cache_ref[0] = ext_ref[halo + tile - n_keep:halo + tile, :]
    ext_ref[0:halo, :] = ext_ref[tile:tile + halo, :]


def pool_mixer_prompt(x, mod, g, w_grp_bf16, scale, *, tile, windows, n_keep):
    bsz, t, d = x.shape
    halo = 16
    assert max(windows) <= halo <= tile and n_keep <= tile
    mod_spec = lambda k: pl.BlockSpec((1, 1, d), lambda b, j: (b, 0, k))
    return pl.pallas_call(
        functools.partial(_pool_prompt_kernel, windows=windows, halo=halo),
        grid=(bsz, t // tile),
        in_specs=[
            pl.BlockSpec((1, tile, d), lambda b, j: (b, j, 0)),
            mod_spec(0), mod_spec(1),
            pl.BlockSpec((1, d), lambda b, j: (0, 0)),
            pl.BlockSpec(w_grp_bf16.shape, lambda b, j: (0, 0, 0)),
            pl.BlockSpec((1, d), lambda b, j: (0, 0)),
        ],
        out_specs=[pl.BlockSpec((1, tile, d), lambda b, j: (b, j, 0)),
                   pl.BlockSpec((1, n_keep, d), lambda b, j: (b, 0, 0))],
        out_shape=[jax.ShapeDtypeStruct((bsz, t, d), F32), jax.ShapeDtypeStruct((bsz, n_keep, d), F32)],
        scratch_shapes=[pltpu.VMEM((halo + tile, d), F32)],
        compiler_params=_params("parallel", "arbitrary"),
        name="pool_mixer_prompt",
    )(x, mod, mod, g.reshape(1, d), w_grp_bf16, scale.reshape(1, d))


def _pool_sample_kernel(x_ref, buf_ref, sh_ref, sc_ref, g_ref, w_ref, scale_ref, y_ref, cache_ref,
                        *, windows, start_pos):
    t_len = x_ref.shape[0]
    n_prev = buf_ref.shape[0]
    hs = [_rms_norm(x_ref[t], g_ref[...]) * (1.0 + sc_ref[...]) + sh_ref[...] for t in range(t_len)]

    def ext(r):
        return buf_ref[r] if r < n_prev else hs[r - n_prev]

    for t in range(t_len):
        def window_sum(gi, cols):
            acc = hs[t][:, cols]
            for s in range(1, windows[gi]):
                acc = acc + ext(n_prev + t - s)[:, cols]
            return acc
        counts = [float(min(start_pos + t + 1, w)) for w in windows]
        y_ref[t] = _pool_groups(hs[t], window_sum, counts, w_ref, scale_ref[...])
    for r in range(n_prev):
        cache_ref[r] = ext(t_len + r)


def pool_mixer_sample(x_t, buf_t, mod, g, w_grp_bf16, scale, *, seq_block, windows, start_pos):
    t_len, n_seq, d = x_t.shape
    n_prev = buf_t.shape[0]
    assert start_pos >= n_prev >= max(windows) - 1
    mod_spec = lambda k: pl.BlockSpec((seq_block, d), lambda i: (i, k))
    return pl.pallas_call(
        functools.partial(_pool_sample_kernel, windows=windows, start_pos=start_pos),
        grid=(n_seq // seq_block,),
        in_specs=[
            pl.BlockSpec((t_len, seq_block, d), lambda i: (0, i, 0)),
            pl.BlockSpec((n_prev, seq_block, d), lambda i: (0, i, 0)),
            mod_spec(0), mod_spec(1),
            pl.BlockSpec((1, d), lambda i: (0, 0)),
            pl.BlockSpec(w_grp_bf16.shape, lambda i: (0, 0, 0)),
            pl.BlockSpec((1, d), lambda i: (0, 0)),
        ],
        out_specs=[pl.BlockSpec((t_len, seq_block, d), lambda i: (0, i, 0)),
                   pl.BlockSpec((n_prev, seq_block, d), lambda i: (0, i, 0))],
        out_shape=[jax.ShapeDtypeStruct((t_len, n_seq, d), F32), jax.ShapeDtypeStruct((n_prev, n_seq, d), F32)],
        compiler_params=_params("parallel"),
        name="pool_mixer_sample",
    )(x_t, buf_t, mod, mod, g.reshape(1, d), w_grp_bf16, scale.reshape(1, d))


TOP_K = 4
SWIGLU_LIMIT = 7.0
SWIGLU_ALPHA = 1.702
POOL_WINDOWS = (2, 4, 8, 16)
PAST_LEN = 16384
PROMPT_TILE = 256
MOE_ROW_TILE = 512
HGRN_CHUNK = 128
HGRN_SUB = 32
SAMPLE_T_PAD = 8


def _routing_tables(counts, idx, rank, n_tiles, row_tile):
    n_experts = counts.shape[0]
    padded = (counts + row_tile - 1) // row_tile * row_tile
    pad_end = jnp.cumsum(padded)
    pad_start = pad_end - padded
    dest = pad_start[idx] + rank
    n_used = (pad_end[-1] // row_tile).reshape(1)
    tile_start = jnp.arange(n_tiles, dtype=jnp.int32) * row_tile
    tile_expert = jnp.minimum(jnp.sum(tile_start[:, None] >= pad_end[None, :], axis=1), n_experts - 1)
    tile_first = (tile_start == pad_start[tile_expert]).astype(jnp.int32)
    last_tile = jnp.where(counts > 0, pad_end // row_tile - 1, -1)
    tiles = jnp.arange(n_tiles, dtype=jnp.int32)
    zero_tiles = jnp.concatenate([last_tile, jnp.where(tiles >= n_used[0], tiles, -1)])
    return (dest.astype(jnp.int32), tile_expert.astype(jnp.int32), tile_first,
            n_used.astype(jnp.int32), zero_tiles.astype(jnp.int32))


def kernel(x_prompt, x_sample, c_prompt, c_sample, state_hgrn, cache_pool, g_norm_mix, g_norm_ffn, w_ada, b_ada, w_in_hgrn, lb_logits, g_out_hgrn, w_out_hgrn, w_grp_pool, scale_pool, w_router, b_router, w_gate_up, b_gate_up, w_down, b_down, g_final):
    bp, tp, d = x_prompt.shape
    bs, ts, _ = x_sample.shape
    n_p, n_s = bp * tp, bs * ts
    n_experts = w_router.shape[-1]
    hk = w_out_hgrn.shape[1]
    n_assign = (n_p + n_s) * TOP_K
    n_tiles = -(-(n_assign + n_experts * (MOE_ROW_TILE - 1)) // MOE_ROW_TILE)
    n_slots = n_tiles * MOE_ROW_TILE

    mod = adaln(jnp.concatenate([c_prompt, c_sample], axis=0), w_ada, b_ada)
    mod_p = [mod[l, :bp][:, None, :] for l in range(mod.shape[0])]
    mod_s = [mod[l, bp:] for l in range(mod.shape[0])]

    xp = x_prompt.reshape(n_p, d)
    xs = x_sample.transpose(1, 0, 2).reshape(n_s, d)

    def moe(layer, x1_p, h_p, r_p, x1_s, h_s, r_s, counts, final_norm):
        idx = jnp.concatenate([r_p[0][:, :TOP_K], r_s[0][:, :TOP_K]], axis=0)
        rank = jnp.concatenate([r_p[2][:, :TOP_K], r_s[2][:, :TOP_K]], axis=0)
        dest, tile_expert, tile_first, n_used, last_tile = _routing_tables(
            counts[0, :n_experts].astype(jnp.int32), idx, rank, n_tiles, MOE_ROW_TILE)
        slots = moe_dispatch(jnp.concatenate([h_p, h_s], axis=0), dest, last_tile,
                             n_slots=n_slots, tile=PROMPT_TILE, row_tile=MOE_ROW_TILE)
        ys = moe_experts(slots, w_gate_up[layer], b_gate_up[layer], w_down[layer], b_down[layer],
                         tile_expert, tile_first, n_used,
                         row_tile=MOE_ROW_TILE, limit=SWIGLU_LIMIT, alpha=SWIGLU_ALPHA)
        out_p = moe_combine(x1_p, ys, dest[:n_p], r_p[1], mod_p[layer], g_final,
                            tile=PROMPT_TILE, rows_per_seq=tp, final_norm=final_norm)
        out_s = moe_combine(x1_s, ys, dest[n_p:], r_s[1], mod_s[layer], g_final,
                            tile=bs, rows_per_seq=None, final_norm=final_norm)
        return out_p, out_s

    def route(layer, x_p, y_p, x_s, y_s, w_out):
        cnt0 = jnp.zeros((1, LANES), F32)
        *r_p, cnt = resid_router(x_p, y_p, w_out, mod_p[layer], g_norm_ffn[layer], w_router[layer],
                                 b_router[layer], cnt0, tile=PROMPT_TILE, rows_per_seq=tp, top_k=TOP_K)
        *r_s, cnt = resid_router(x_s, y_s, w_out, mod_s[layer], g_norm_ffn[layer], w_router[layer],
                                 b_router[layer], cnt, tile=bs, rows_per_seq=None, top_k=TOP_K)
        return r_p, r_s, cnt

    w_in = w_in_hgrn[0].astype(BF16)
    proj_p = norm_proj(xp, mod_p[0], g_norm_mix[0], w_in, tile=PROMPT_TILE, rows_per_seq=tp)
    proj_s = norm_proj(xs, mod_s[0], g_norm_mix[0], w_in, tile=bs, rows_per_seq=None)
    o_p, state_p = hgrn_recurrence(proj_p.reshape(bp, tp, 4 * hk), lb_logits, g_out_hgrn[0], None,
                                   layer=0, seq_block=1, time_block=PROMPT_TILE, chunk=HGRN_CHUNK,
                                   c_sub=HGRN_SUB, n_valid=HGRN_CHUNK)
    proj_sb = jnp.pad(proj_s.reshape(ts, bs, 4 * hk).transpose(1, 0, 2), ((0, 0), (0, SAMPLE_T_PAD - ts), (0, 0)))
    o_s, state_s = hgrn_recurrence(proj_sb, lb_logits, g_out_hgrn[0], state_hgrn[0],
                                   layer=0, seq_block=8, time_block=SAMPLE_T_PAD, chunk=SAMPLE_T_PAD,
                                   c_sub=SAMPLE_T_PAD, n_valid=ts)
    o_s = o_s[:, :ts].transpose(1, 0, 2).reshape(n_s, hk)
    r_p, r_s, cnt = route(0, xp, o_p.reshape(n_p, hk), xs, o_s, w_out_hgrn[0].astype(BF16))
    x_p, x_s = moe(0, r_p[0], r_p[1], r_p[2:], r_s[0], r_s[1], r_s[2:], cnt, False)

    w_grp = w_grp_pool[0].astype(BF16)
    n_keep = cache_pool.shape[2]
    y_p, cache_p = pool_mixer_prompt(x_p.reshape(bp, tp, d), mod_p[1], g_norm_mix[1], w_grp, scale_pool[0],
                                     tile=PROMPT_TILE, windows=POOL_WINDOWS, n_keep=n_keep)
    y_s, cache_s = pool_mixer_sample(x_s.reshape(ts, bs, d), cache_pool[0].transpose(1, 0, 2), mod_s[1],
                                     g_norm_mix[1], w_grp, scale_pool[0],
                                     seq_block=32, windows=POOL_WINDOWS, start_pos=PAST_LEN)
    r_p, r_s, cnt = route(1, x_p, y_p.reshape(n_p, d), x_s, y_s.reshape(n_s, d), None)
    x_p, x_s = moe(1, r_p[0], r_p[1], r_p[2:], r_s[0], r_s[1], r_s[2:], cnt, True)

    return (x_p.reshape(bp, tp, d), x_s.reshape(ts, bs, d).transpose(1, 0, 2),
            state_p[None], state_s[None], cache_p[None], cache_s.transpose(1, 0, 2)[None])
```

```python
import functools

import jax
import jax.numpy as jnp
from jax import lax
from jax.experimental import pallas as pl
from jax.experimental.pallas import tpu as pltpu

F32 = jnp.float32
BF16 = jnp.bfloat16

RMS_EPS = 1e-6
LANES = 128
SUBLANES = 8
BF16_ROWS = 16
HEAD_DIM = 128
VMEM_LIMIT = 56 * 1024 * 1024

_dot = functools.partial(jnp.dot, preferred_element_type=F32)


def _params(*semantics):
    return pltpu.CompilerParams(dimension_semantics=semantics, vmem_limit_bytes=VMEM_LIMIT)


def _split_bf16(x, n):
    parts, r = [], x
    for _ in range(n):
        p = r.astype(BF16)
        parts.append(p)
        r = r - p.astype(F32)
    return parts


def _dot_hp(a, b):
    a_hi, a_lo = _split_bf16(a, 2)
    b_hi, b_lo = _split_bf16(b, 2)
    return _dot(a_hi, b_hi) + (_dot(a_hi, b_lo) + _dot(a_lo, b_hi))


def _sigmoid(x):
    return 1.0 / (1.0 + jnp.exp(-x))


def _silu(x):
    return x * _sigmoid(x)


def _rms_norm(x, g):
    ms = jnp.mean(x * x, axis=-1, keepdims=True)
    return x * lax.rsqrt(ms + RMS_EPS) * g


def _adaln_kernel(c_ref, w_ref, b_ref, o_ref):
    o_ref[0] = _dot_hp(_silu(c_ref[...]), w_ref[0]) + b_ref[0]


def adaln(c_all, w_ada, b_ada, *, col_block=1536):
    n_seq, d = c_all.shape
    n_layers, _, d6 = w_ada.shape
    return pl.pallas_call(
        _adaln_kernel,
        grid=(n_layers, d6 // col_block),
        in_specs=[
            pl.BlockSpec((n_seq, d), lambda l, j: (0, 0)),
            pl.BlockSpec((1, d, col_block), lambda l, j: (l, 0, j)),
            pl.BlockSpec((1, 1, col_block), lambda l, j: (l, 0, j)),
        ],
        out_specs=pl.BlockSpec((1, n_seq, col_block), lambda l, j: (l, 0, j)),
        out_shape=jax.ShapeDtypeStruct((n_layers, n_seq, d6), F32),
        compiler_params=_params("parallel", "parallel"),
        name="adaln",
    )(c_all, w_ada, b_ada.reshape(n_layers, 1, d6))


def _mod_spec(mod, k, d, tile, rows_per_seq):
    if rows_per_seq is None:
        return pl.BlockSpec((tile, d), lambda i: (0, k))
    tiles_per_seq = rows_per_seq // tile
    return pl.BlockSpec((1, 1, d), lambda i: (i // tiles_per_seq, 0, k))


def _norm_proj_kernel(x_ref, sh_ref, sc_ref, g_ref, w_ref, o_ref):
    d = x_ref.shape[-1]
    h = _rms_norm(x_ref[...], g_ref[...]) * (1.0 + sc_ref[...].reshape(-1, d)) + sh_ref[...].reshape(-1, d)
    o_ref[...] = _dot(h.astype(BF16), w_ref[...])


def norm_proj(x, mod, g, w_bf16, *, tile, rows_per_seq):
    n, d = x.shape
    p = w_bf16.shape[1]
    return pl.pallas_call(
        _norm_proj_kernel,
        grid=(n // tile,),
        in_specs=[
            pl.BlockSpec((tile, d), lambda i: (i, 0)),
            _mod_spec(mod, 0, d, tile, rows_per_seq),
            _mod_spec(mod, 1, d, tile, rows_per_seq),
            pl.BlockSpec((1, d), lambda i: (0, 0)),
            pl.BlockSpec((d, p), lambda i: (0, 0)),
        ],
        out_specs=pl.BlockSpec((tile, p), lambda i: (i, 0)),
        out_shape=jax.ShapeDtypeStruct((n, p), F32),
        compiler_params=_params("parallel"),
        name="hgrn_norm_proj",
    )(x, mod, mod, g.reshape(1, d), w_bf16)


def _cumsum_rows(x, tri):
    hi, mid, lo = _split_bf16(x, 3)
    return _dot(tri, hi) + (_dot(tri, mid) + _dot(tri, lo))


def _hgrn_chunk(proj, lb, gout, st_refs, seq, n_valid, c_sub):
    c = proj.shape[0]
    hk = proj.shape[1] // 4
    n_heads = hk // HEAD_DIM
    row = lax.broadcasted_iota(jnp.int32, (c, c), 0)
    col = lax.broadcasted_iota(jnp.int32, (c, c), 1)
    causal = row >= col
    tri = causal.astype(BF16)

    zf = proj[:, hk:2 * hk]
    e = jnp.exp(-jnp.abs(zf))
    r = 1.0 / (1.0 + e)
    pos = zf >= 0
    sig_p = jnp.where(pos, 1.0, e) * r
    sig_n = jnp.where(pos, e, 1.0) * r
    logf = jnp.log(lb + (1.0 - lb) * sig_p)
    k = (1.0 - lb) * sig_n
    if n_valid < c:
        live = lax.broadcasted_iota(jnp.int32, (c, 1), 0) < n_valid
        logf = jnp.where(live, logf, 0.0)
        k = jnp.where(live, k, 0.0)
    b = _cumsum_rows(logf, tri)
    q = _silu(proj[:, :hk])
    v = proj[:, 2 * hk:3 * hk]
    gate = _silu(proj[:, 3 * hk:])

    outs = []
    for h in range(n_heads):
        hs = slice(h * HEAD_DIM, (h + 1) * HEAD_DIM)
        bh, qh, kh, vh = b[:, hs], q[:, hs], k[:, hs], v[:, hs]
        vb = vh.astype(BF16)
        a_rows = []
        for i in range(c // c_sub):
            rs = slice(i * c_sub, (i + 1) * c_sub)
            ref = bh[i * c_sub + c_sub // 2:i * c_sub + c_sub // 2 + 1, :]
            q_hat = (qh[rs] * jnp.exp(bh[rs] - ref)).astype(BF16)
            k_hat = (kh * jnp.exp(jnp.minimum(ref - bh, 80.0))).astype(BF16)
            a_rows.append(lax.dot_general(q_hat, k_hat, (((1,), (1,)), ((), ())), preferred_element_type=F32))
        att = jnp.where(causal, jnp.concatenate(a_rows, axis=0) if len(a_rows) > 1 else a_rows[0], 0.0)
        o = _dot(att.astype(BF16), vb)
        st = st_refs[seq, h]
        o = o + lax.dot_general((qh * jnp.exp(bh)).astype(BF16), st.astype(BF16),
                                (((1,), (1,)), ((), ())), preferred_element_type=F32)
        b_last = bh[c - 1:c, :]
        k_dec = (kh * jnp.exp(b_last - bh)).astype(BF16)
        st_refs[seq, h] = st * jnp.exp(b_last) + lax.dot_general(
            vb, k_dec, (((0,), (0,)), ((), ())), preferred_element_type=F32)
        o = o * lax.rsqrt(jnp.mean(o * o, axis=-1, keepdims=True) + RMS_EPS) * gout
        outs.append(o * gate[:, hs])
    return jnp.concatenate(outs, axis=-1)


def _lower_bound(lb_logits, layer):
    e = jnp.exp(lb_logits - jnp.max(lb_logits, axis=0, keepdims=True))
    return jnp.sum(e[:layer + 1], axis=0, keepdims=True) / jnp.sum(e, axis=0, keepdims=True)


def _hgrn_rec_kernel(*refs, chunk, c_sub, n_valid, has_state, layer):
    if has_state:
        proj_ref, lb_ref, gout_ref, s0_ref, o_ref, sout_ref, st_ref = refs
    else:
        proj_ref, lb_ref, gout_ref, o_ref, sout_ref, st_ref = refs
    bb, tb, _ = proj_ref.shape
    n_heads = st_ref.shape[1]
    j = pl.program_id(1)

    @pl.when(j == 0)
    def _():
        if has_state:
            for s in range(bb):
                for h in range(n_heads):
                    st_ref[s, h] = s0_ref[s, h].T
        else:
            st_ref[...] = jnp.zeros_like(st_ref)

    lb = _lower_bound(lb_ref[...], layer)
    gout = gout_ref[...]

    def seq_body(s, carry):
        def chunk_body(ci, carry2):
            rows = pl.ds(pl.multiple_of(ci * chunk, chunk), chunk)
            o_ref[s, rows, :] = _hgrn_chunk(proj_ref[s, rows, :], lb, gout, st_ref, s, n_valid, c_sub)
            return carry2
        return lax.fori_loop(0, tb // chunk, chunk_body, carry)
    lax.fori_loop(0, bb, seq_body, 0)

    @pl.when(j == pl.num_programs(1) - 1)
    def _():
        for s in range(bb):
            for h in range(n_heads):
                sout_ref[s, h] = st_ref[s, h].T


def hgrn_recurrence(proj, lb_logits, g_out, s0, *, layer, seq_block, time_block, chunk, c_sub, n_valid):
    bsz, t, p = proj.shape
    hk = p // 4
    n_heads = hk // HEAD_DIM
    has_state = s0 is not None
    st_shape = (seq_block, n_heads, HEAD_DIM, HEAD_DIM)
    st_spec = pl.BlockSpec(st_shape, lambda i, j: (i, 0, 0, 0))
    in_specs = [
        pl.BlockSpec((seq_block, time_block, p), lambda i, j: (i, j, 0)),
        pl.BlockSpec(lb_logits.shape, lambda i, j: (0, 0)),
        pl.BlockSpec((1, HEAD_DIM), lambda i, j: (0, 0)),
    ]
    args = [proj, lb_logits, g_out.reshape(1, HEAD_DIM)]
    if has_state:
        in_specs.append(st_spec)
        args.append(s0)
    return pl.pallas_call(
        functools.partial(_hgrn_rec_kernel, chunk=chunk, c_sub=c_sub, n_valid=n_valid,
                          has_state=has_state, layer=layer),
        grid=(bsz // seq_block, t // time_block),
        in_specs=in_specs,
        out_specs=[pl.BlockSpec((seq_block, time_block, hk), lambda i, j: (i, j, 0)), st_spec],
        out_shape=[jax.ShapeDtypeStruct((bsz, t, hk), F32),
                   jax.ShapeDtypeStruct((bsz, n_heads, HEAD_DIM, HEAD_DIM), F32)],
        scratch_shapes=[pltpu.VMEM(st_shape, F32)],
        compiler_params=_params("parallel", "arbitrary"),
        name="hgrn_recurrence",
    )(*args)


def _sorted_rows(tile, top_k, n_experts):
    return tile * top_k + n_experts * BF16_ROWS


def _resid_router_kernel(*refs, top_k, n_experts, has_w_out, chained, row_chunk):
    refs = list(refs)
    x_ref, y_ref = refs[:2]
    del refs[:2]
    wo_ref = refs.pop(0) if has_w_out else None
    gt_ref, sh_ref, sc_ref, g_ref, wr_ref, br_ref = refs[:6]
    del refs[:6]
    if chained:
        refs.pop(0)
    x1_ref, xs_ref, pos_ref, gate_ref, cnt_ref = refs
    tile, d = x_ref.shape
    n_sorted = xs_ref.shape[0]

    y = y_ref[...]
    if has_w_out:
        y = _dot(y.astype(BF16), wo_ref[...])
    x1 = x_ref[...] + gt_ref[...].reshape(-1, d) * y
    x1_ref[...] = x1
    h = _rms_norm(x1, g_ref[...]) * (1.0 + sc_ref[...].reshape(-1, d)) + sh_ref[...].reshape(-1, d)

    lane = lax.broadcasted_iota(jnp.int32, (tile, LANES), 1).astype(F32)
    logits = jnp.where(lane < n_experts, _dot_hp(h, wr_ref[...]) + br_ref[...], -jnp.inf)
    picks, vals = [], []
    for _ in range(top_k):
        m = jnp.max(logits, axis=-1, keepdims=True)
        pick = jnp.min(jnp.where(logits == m, lane, float(LANES)), axis=-1, keepdims=True)
        picks.append(pick)
        vals.append(m)
        logits = jnp.where(lane == pick, -jnp.inf, logits)
    exps = [jnp.exp(v - vals[0]) for v in vals]
    denom = exps[0]
    for e in exps[1:]:
        denom = denom + e

    onehots = [(lane == p).astype(F32) for p in picks]
    oh_sum = onehots[0]
    for oh in onehots[1:]:
        oh_sum = oh_sum + oh
    row = lax.broadcasted_iota(jnp.int32, (tile, tile), 0)
    col = lax.broadcasted_iota(jnp.int32, (tile, tile), 1)
    before = _dot((row > col).astype(BF16), oh_sum.astype(BF16))
    count = jnp.sum(oh_sum, axis=0, keepdims=True)
    cnt_pad = jnp.floor((count + (BF16_ROWS - 1.0)) * (1.0 / BF16_ROWS)) * BF16_ROWS
    lane8 = lax.broadcasted_iota(jnp.int32, (SUBLANES, LANES), 1)
    run = jnp.broadcast_to(cnt_pad, (SUBLANES, LANES))
    shift = 1
    while shift < n_experts:
        run = run + jnp.where(lane8 >= shift, pltpu.roll(run, shift, 1), 0.0)
        shift *= 2
    pos = before + (run[0:1] - cnt_pad)
    pos_out = jnp.zeros((tile, LANES), F32)
    gate_out = jnp.zeros((tile, LANES), F32)
    for k in range(top_k):
        pos_k = jnp.sum(onehots[k] * pos, axis=-1, keepdims=True)
        pos_out = jnp.where(lane == k, pos_k, pos_out)
        gate_out = jnp.where(lane == k, exps[k] / denom, gate_out)
    pos_ref[...] = pos_out.astype(jnp.int32)
    gate_ref[...] = gate_out
    cnt_ref[0] = cnt_pad.astype(jnp.int32)

    pos_t = pos_out.T
    hb = h.astype(BF16)
    for r0 in range(0, n_sorted, row_chunk):
        slot = (lax.broadcasted_iota(jnp.int32, (row_chunk, tile), 0) + r0).astype(F32)
        hit = slot == pos_t[0:1]
        for k in range(1, top_k):
            hit = hit | (slot == pos_t[k:k + 1])
        sel = jnp.where(hit, 1.0, 0.0).astype(BF16)
        xs_ref[r0:r0 + row_chunk, :] = _dot(sel, hb).astype(BF16)


def resid_router(x, y, w_out_bf16, mod, g, w_r, b_r, sorted_in, *, tile, rows_per_seq, top_k,
                 block_offset, n_blocks_total):
    n, d = x.shape
    n_experts = w_r.shape[1]
    n_sorted = _sorted_rows(tile, top_k, n_experts)
    w_r_pad = jnp.pad(w_r, ((0, 0), (0, LANES - n_experts)))
    b_r_pad = jnp.pad(b_r, (0, LANES - n_experts)).reshape(1, LANES)
    has_w_out = w_out_bf16 is not None
    chained = sorted_in is not None
    row_spec = pl.BlockSpec((tile, d), lambda i: (i, 0))
    lane_spec = pl.BlockSpec((tile, LANES), lambda i: (i, 0))
    full = lambda a: pl.BlockSpec(a.shape, lambda i: (0,) * a.ndim)
    in_specs = [row_spec, pl.BlockSpec((tile, y.shape[1]), lambda i: (i, 0))]
    args = [x, y]
    if has_w_out:
        in_specs.append(full(w_out_bf16))
        args.append(w_out_bf16)
    in_specs += [_mod_spec(mod, 2, d, tile, rows_per_seq), _mod_spec(mod, 3, d, tile, rows_per_seq),
                 _mod_spec(mod, 4, d, tile, rows_per_seq), pl.BlockSpec((1, d), lambda i: (0, 0)),
                 full(w_r_pad), full(b_r_pad)]
    args += [mod, mod, mod, g.reshape(1, d), w_r_pad, b_r_pad]
    aliases = {}
    if chained:
        aliases = {len(args): 1}
        in_specs.append(pl.BlockSpec(memory_space=pl.ANY))
        args.append(sorted_in)
    n_tiles = n // tile
    return pl.pallas_call(
        functools.partial(_resid_router_kernel, top_k=top_k, n_experts=n_experts, has_w_out=has_w_out,
                          chained=chained, row_chunk=256),
        grid=(n_tiles,),
        in_specs=in_specs,
        out_specs=[row_spec,
                   pl.BlockSpec((n_sorted, d), lambda i: (i + block_offset, 0)),
                   lane_spec, lane_spec,
                   pl.BlockSpec((1, 1, LANES), lambda i: (i, 0, 0))],
        out_shape=[jax.ShapeDtypeStruct((n, d), F32),
                   jax.ShapeDtypeStruct((n_blocks_total * n_sorted, d), BF16),
                   jax.ShapeDtypeStruct((n, LANES), jnp.int32), jax.ShapeDtypeStruct((n, LANES), F32),
                   jax.ShapeDtypeStruct((n_tiles, 1, LANES), jnp.int32)],
        input_output_aliases=aliases,
        compiler_params=_params("parallel"),
        name="resid_router",
    )(*args)


def _experts_kernel(te_ref, first_ref, rows_ref, plo_ref, phi_ref, wslot_ref, nexte_ref, used_ref,
                    psrc_ref, pdst_ref, pnum_ref,
                    xs_hbm, wgu_hbm, bgu_ref, wdn_hbm, bdn_ref, ys_hbm,
                    xbuf, ybuf, wgu_f, wdn_f, wgu_b, wdn_b, in_sem, out_sem, w_sem,
                    *, layer, limit, alpha, col_chunk):
    del ys_hbm
    i = pl.program_id(0)
    used = used_ref[0]

    def piece_copies(tile_idx, slot, inbound, wait):
        def piece(p, c):
            def unit(j, c2):
                hbm = xs_hbm.at[pl.ds(pl.multiple_of(psrc_ref[p] + j * BF16_ROWS, BF16_ROWS), BF16_ROWS), :]
                dst = pl.ds(pl.multiple_of(pdst_ref[p] + j * BF16_ROWS, BF16_ROWS), BF16_ROWS)
                if inbound:
                    cp = pltpu.make_async_copy(hbm, xbuf.at[slot, dst, :], in_sem.at[slot])
                else:
                    cp = pltpu.make_async_copy(ybuf.at[slot, dst, :], hbm, out_sem.at[slot])
                cp.wait() if wait else cp.start()
                return c2
            return lax.fori_loop(0, pnum_ref[p], unit, c)
        lax.fori_loop(plo_ref[tile_idx], phi_ref[tile_idx], piece, 0)

    def weight_copies(e, slot):
        return (pltpu.make_async_copy(wgu_hbm.at[layer, e], wgu_f.at[slot], w_sem.at[slot]),
                pltpu.make_async_copy(wdn_hbm.at[layer, e], wdn_f.at[slot], w_sem.at[slot]))

    @pl.when(i == 0)
    def _():
        xbuf[...] = jnp.zeros_like(xbuf)
        for cp in weight_copies(te_ref[0], 0):
            cp.start()
        piece_copies(0, 0, True, False)

    @pl.when(i < used)
    def _():
        slot = i % 2

        @pl.when(i + 1 < used)
        def _():
            piece_copies(i + 1, 1 - slot, True, False)

        @pl.when(first_ref[i] == 1)
        def _():
            ws = wslot_ref[i]
            for cp in weight_copies(te_ref[i], ws):
                cp.wait()

            @pl.when(nexte_ref[i] >= 0)
            def _():
                for cp in weight_copies(nexte_ref[i], 1 - ws):
                    cp.start()
            wgu_b[...] = wgu_f[ws].astype(BF16)
            wdn_b[...] = wdn_f[ws].astype(BF16)

        piece_copies(i, slot, True, True)

        @pl.when(i >= 2)
        def _():
            piece_copies(i - 2, slot, False, True)

        d_ff = wdn_b.shape[0]
        x = xbuf[slot]
        e = te_ref[i]
        b_gu = bgu_ref[pl.ds(e, 1), :]
        y = None
        for c0 in range(0, d_ff, col_chunk):
            cs = slice(c0, c0 + col_chunk)
            us = slice(d_ff + c0, d_ff + c0 + col_chunk)
            gate = jnp.minimum(_dot(x, wgu_b[:, cs]) + b_gu[:, cs], limit)
            up = jnp.clip(_dot(x, wgu_b[:, us]) + b_gu[:, us], -limit, limit)
            act = ((up + 1.0) * (gate * _sigmoid(alpha * gate))).astype(BF16)
            part = _dot(act, wdn_b[cs, :])
            y = part if y is None else y + part
        ybuf[slot] = (y + bdn_ref[pl.ds(e, 1), :]).astype(BF16)
        piece_copies(i, slot, False, False)

    @pl.when(i == pl.num_programs(0) - 1)
    def _():
        @pl.when(used >= 2)
        def _():
            piece_copies(used - 2, used % 2, False, True)
        piece_copies(used - 1, (used - 1) % 2, False, True)


def moe_experts(sorted_rows, w_gu, b_gu, w_dn, b_dn, tables, *, layer, row_tile, limit, alpha):
    n_rows, d = sorted_rows.shape
    d_gu = w_gu.shape[-1]
    d_ff = w_dn.shape[-2]
    n_tiles = tables[0].shape[0]
    vmem = lambda a: pl.BlockSpec(a.shape, lambda i, *_: (0,) * a.ndim)
    any_spec = pl.BlockSpec(memory_space=pl.ANY)
    return pl.pallas_call(
        functools.partial(_experts_kernel, layer=layer, limit=limit, alpha=alpha, col_chunk=256),
        grid_spec=pltpu.PrefetchScalarGridSpec(
            num_scalar_prefetch=len(tables),
            grid=(n_tiles,),
            in_specs=[any_spec, any_spec, vmem(b_gu), any_spec, vmem(b_dn)],
            out_specs=any_spec,
            scratch_shapes=[
                pltpu.VMEM((2, row_tile, d), BF16), pltpu.VMEM((2, row_tile, d), BF16),
                pltpu.VMEM((2, d, d_gu), F32), pltpu.VMEM((2, d_ff, d), F32),
                pltpu.VMEM((d, d_gu), BF16), pltpu.VMEM((d_ff, d), BF16),
                pltpu.SemaphoreType.DMA((2,)), pltpu.SemaphoreType.DMA((2,)), pltpu.SemaphoreType.DMA((2,)),
            ],
        ),
        out_shape=jax.ShapeDtypeStruct((n_rows, d), BF16),
        input_output_aliases={len(tables): 0},
        compiler_params=_params("arbitrary"),
        name="moe_experts",
    )(*tables, sorted_rows, w_gu, b_gu, w_dn, b_dn)


def _expert_tables(cnt, n_sorted, n_tiles, row_tile):
    n_blocks, n_experts = cnt.shape
    local_off = jnp.cumsum(cnt, axis=1) - cnt
    total = jnp.sum(cnt, axis=0)
    padded = (total + row_tile - 1) // row_tile * row_tile
    pad_end = jnp.cumsum(padded)
    pad_start = pad_end - padded
    seg_n = cnt.T.reshape(-1)
    seg_start = (pad_start[:, None] + jnp.cumsum(cnt.T, axis=1) - cnt.T).reshape(-1)
    seg_src = (jnp.arange(n_blocks, dtype=jnp.int32)[None, :] * n_sorted + local_off.T).reshape(-1)
    first_tile = seg_start // row_tile
    n_a = jnp.minimum(seg_n, (first_tile + 1) * row_tile - seg_start)
    n_b = seg_n - n_a
    piece_tile = jnp.stack([first_tile, jnp.where(n_b > 0, first_tile + 1, first_tile)], axis=1).reshape(-1)
    piece_src = jnp.stack([seg_src, seg_src + n_a], axis=1).reshape(-1)
    piece_dst = jnp.stack([seg_start - first_tile * row_tile, jnp.zeros_like(seg_start)], axis=1).reshape(-1)
    piece_units = jnp.stack([n_a, n_b], axis=1).reshape(-1) // BF16_ROWS
    tiles = jnp.arange(n_tiles, dtype=jnp.int32)
    piece_lo = jnp.sum(piece_tile[None, :] < tiles[:, None], axis=1)
    piece_hi = jnp.sum(piece_tile[None, :] <= tiles[:, None], axis=1)
    n_used = pad_end[-1] // row_tile
    tile_expert = jnp.minimum(jnp.sum(tiles[:, None] * row_tile >= pad_end[None, :], axis=1), n_experts - 1)
    live = tiles < n_used
    tile_rows = jnp.where(live, jnp.clip(pad_start[tile_expert] + total[tile_expert] - tiles * row_tile,
                                         0, row_tile), 0)
    tile_first = (live & (tiles * row_tile == pad_start[tile_expert])).astype(jnp.int32)
    owns = total > 0
    order = jnp.cumsum(owns.astype(jnp.int32)) - 1
    experts = jnp.arange(n_experts, dtype=jnp.int32)
    later = (experts[None, :] > experts[:, None]) & owns[None, :]
    next_expert = jnp.min(jnp.where(later, experts[None, :], n_experts), axis=1)
    next_expert = jnp.where(next_expert < n_experts, next_expert, -1)
    i32 = lambda a: a.astype(jnp.int32)
    return (i32(tile_expert), tile_first, i32(tile_rows), i32(piece_lo), i32(piece_hi),
            i32(order[tile_expert] % 2), i32(next_expert[tile_expert]), i32(n_used).reshape(1),
            i32(piece_src), i32(piece_dst), i32(piece_units))


def _combine_kernel(x_ref, pos_ref, gate_ref, gt_ref, gfin_ref, ys_ref, o_ref, *, top_k, final_norm, k_chunk):
    tile, d = x_ref.shape
    n_sorted = ys_ref.shape[0]
    pos = pos_ref[...].astype(F32)
    gates = gate_ref[...]
    acc = None
    for r0 in range(0, n_sorted, k_chunk):
        slot = (lax.broadcasted_iota(jnp.int32, (tile, k_chunk), 1) + r0).astype(F32)
        w = jnp.where(slot == pos[:, 0:1], gates[:, 0:1], 0.0)
        for k in range(1, top_k):
            w = w + jnp.where(slot == pos[:, k:k + 1], gates[:, k:k + 1], 0.0)
        part = _dot(w.astype(BF16), ys_ref[r0:r0 + k_chunk, :])
        acc = part if acc is None else acc + part
    out = x_ref[...] + gt_ref[...].reshape(-1, d) * acc
    if final_norm:
        out = _rms_norm(out, gfin_ref[...])
    o_ref[...] = out


def moe_combine(x, ys, pos, gates, mod, g_final, *, tile, rows_per_seq, final_norm, block_offset, n_sorted):
    n, d = x.shape
    return pl.pallas_call(
        functools.partial(_combine_kernel, top_k=TOP_K, final_norm=final_norm, k_chunk=512),
        grid=(n // tile,),
        in_specs=[
            pl.BlockSpec((tile, d), lambda i: (i, 0)),
            pl.BlockSpec((tile, LANES), lambda i: (i, 0)),
            pl.BlockSpec((tile, LANES), lambda i: (i, 0)),
            _mod_spec(mod, 5, d, tile, rows_per_seq),
            pl.BlockSpec((1, d), lambda i: (0, 0)),
            pl.BlockSpec((n_sorted, d), lambda i: (i + block_offset, 0)),
        ],
        out_specs=pl.BlockSpec((tile, d), lambda i: (i, 0)),
        out_shape=jax.ShapeDtypeStruct((n, d), F32),
        compiler_params=_params("parallel"),
        name="moe_combine",
    )(x, pos, gates, mod, g_final.reshape(1, d), ys)


def _pool_groups(h, window_sum, counts, w_ref, scale):
    n_groups = w_ref.shape[0]
    dg = h.shape[-1] // n_groups
    outs = []
    for gi in range(n_groups):
        cols = slice(gi * dg, (gi + 1) * dg)
        pooled = window_sum(gi, cols) / counts[gi] - h[:, cols]
        outs.append(_dot(pooled.astype(BF16), w_ref[gi]))
    return jnp.concatenate(outs, axis=-1) * scale


def _pool_prompt_kernel(x_ref, sh_ref, sc_ref, g_ref, w_ref, scale_ref, y_ref, cache_ref, ext_ref,
                        *, windows, halo):
    tile, d = x_ref.shape[1:]
    j = pl.program_id(1)

    @pl.when(j == 0)
    def _():
        ext_ref[0:halo, :] = jnp.zeros((halo, d), F32)

    h = _rms_norm(x_ref[0], g_ref[...]) * (1.0 + sc_ref[...].reshape(-1, d)) + sh_ref[...].reshape(-1, d)
    ext_ref[halo:halo + tile, :] = h
    pos = j * tile + lax.broadcasted_iota(jnp.int32, (tile, 1), 0)

    def window_sum(gi, cols):
        acc = h[:, cols]
        for s in range(1, windows[gi]):
            acc = acc + ext_ref[halo - s:halo - s + tile, cols]
        return acc
    counts = [jnp.minimum(pos + 1, w).astype(F32) for w in windows]
    y_ref[0] = _pool_groups(h, window_sum, counts, w_ref, scale_ref[...])

    n_keep = cache_ref.shape[1]
    @pl.when(j == pl.num_programs(1) - 1)
    def _():
        cache_ref[0] = ext_ref[halo + tile - n_keep:halo + tile, :]
    ext_ref[0:halo, :] = ext_ref[tile:tile + halo, :]


def pool_mixer_prompt(x, mod, g, w_grp_bf16, scale, *, tile, windows, n_keep):
    bsz, t, d = x.shape
    halo = 16
    assert max(windows) <= halo <= tile and n_keep <= tile
    mod_spec = lambda k: pl.BlockSpec((1, 1, d), lambda b, j: (b, 0, k))
    return pl.pallas_call(
        functools.partial(_pool_prompt_kernel, windows=windows, halo=halo),
        grid=(bsz, t // tile),
        in_specs=[
            pl.BlockSpec((1, tile, d), lambda b, j: (b, j, 0)),
            mod_spec(0), mod_spec(1),
            pl.BlockSpec((1, d), lambda b, j: (0, 0)),
            pl.BlockSpec(w_grp_bf16.shape, lambda b, j: (0, 0, 0)),
            pl.BlockSpec((1, d), lambda b, j: (0, 0)),
        ],
        out_specs=[pl.BlockSpec((1, tile, d), lambda b, j: (b, j, 0)),
                   pl.BlockSpec((1, n_keep, d), lambda b, j: (b, 0, 0))],
        out_shape=[jax.ShapeDtypeStruct((bsz, t, d), F32), jax.ShapeDtypeStruct((bsz, n_keep, d), F32)],
        scratch_shapes=[pltpu.VMEM((halo + tile, d), F32)],
        compiler_params=_params("parallel", "arbitrary"),
        name="pool_mixer_prompt",
    )(x, mod, mod, g.reshape(1, d), w_grp_bf16, scale.reshape(1, d))


def _pool_sample_kernel(x_ref, buf_ref, sh_ref, sc_ref, g_ref, w_ref, scale_ref, y_ref, cache_ref,
                        *, windows, start_pos):
    t_len = x_ref.shape[0]
    n_prev = buf_ref.shape[0]
    hs = [_rms_norm(x_ref[t], g_ref[...]) * (1.0 + sc_ref[...]) + sh_ref[...] for t in range(t_len)]

    def ext(r):
        return buf_ref[r] if r < n_prev else hs[r - n_prev]

    for t in range(t_len):
        def window_sum(gi, cols):
            acc = hs[t][:, cols]
            for s in range(1, windows[gi]):
                acc = acc + ext(n_prev + t - s)[:, cols]
            return acc
        counts = [float(min(start_pos + t + 1, w)) for w in windows]
        y_ref[t] = _pool_groups(hs[t], window_sum, counts, w_ref, scale_ref[...])
    for r in range(n_prev):
        cache_ref[r] = ext(t_len + r)


def pool_mixer_sample(x_t, buf_t, mod, g, w_grp_bf16, scale, *, seq_block, windows, start_pos):
    t_len, n_seq, d = x_t.shape
    n_prev = buf_t.shape[0]
    assert start_pos >= n_prev >= max(windows) - 1
    mod_spec = lambda k: pl.BlockSpec((seq_block, d), lambda i: (i, k))
    return pl.pallas_call(
        functools.partial(_pool_sample_kernel, windows=windows, start_pos=start_pos),
        grid=(n_seq // seq_block,),
        in_specs=[
            pl.BlockSpec((t_len, seq_block, d), lambda i: (0, i, 0)),
            pl.BlockSpec((n_prev, seq_block, d), lambda i: (0, i, 0)),
            mod_spec(0), mod_spec(1),
            pl.BlockSpec((1, d), lambda i: (0, 0)),
            pl.BlockSpec(w_grp_bf16.shape, lambda i: (0, 0, 0)),
            pl.BlockSpec((1, d), lambda i: (0, 0)),
        ],
        out_specs=[pl.BlockSpec((t_len, seq_block, d), lambda i: (0, i, 0)),
                   pl.BlockSpec((n_prev, seq_block, d), lambda i: (0, i, 0))],
        out_shape=[jax.ShapeDtypeStruct((t_len, n_seq, d), F32), jax.ShapeDtypeStruct((n_prev, n_seq, d), F32)],
        compiler_params=_params("parallel"),
        name="pool_mixer_sample",
    )(x_t, buf_t, mod, mod, g.reshape(1, d), w_grp_bf16, scale.reshape(1, d))


TOP_K = 4
SWIGLU_LIMIT = 7.0
SWIGLU_ALPHA = 1.702
POOL_WINDOWS = (2, 4, 8, 16)
PAST_LEN = 16384
PROJ_TILE = 256
MOE_TOKEN_TILE = 512
MOE_ROW_TILE = 512
HGRN_CHUNK = 128
HGRN_SUB = 32
SAMPLE_T_PAD = 8


def kernel(x_prompt, x_sample, c_prompt, c_sample, state_hgrn, cache_pool, g_norm_mix, g_norm_ffn, w_ada, b_ada, w_in_hgrn, lb_logits, g_out_hgrn, w_out_hgrn, w_grp_pool, scale_pool, w_router, b_router, w_gate_up, b_gate_up, w_down, b_down, g_final):
    bp, tp, d = x_prompt.shape
    bs, ts, _ = x_sample.shape
    n_p, n_s = bp * tp, bs * ts
    n_experts = w_router.shape[-1]
    hk = w_out_hgrn.shape[1]
    assert n_s == MOE_TOKEN_TILE and n_p % MOE_TOKEN_TILE == 0
    blocks_p = n_p // MOE_TOKEN_TILE
    n_blocks = blocks_p + 1
    n_sorted = _sorted_rows(MOE_TOKEN_TILE, TOP_K, n_experts)
    n_row_tiles = -(-(n_blocks * n_sorted + n_experts * (MOE_ROW_TILE - BF16_ROWS)) // MOE_ROW_TILE)

    mod = adaln(jnp.concatenate([c_prompt, c_sample], axis=0), w_ada, b_ada)
    mod_p = [mod[l, :bp][:, None, :] for l in range(mod.shape[0])]
    mod_s = [mod[l, bp:] for l in range(mod.shape[0])]
    mod_st = [jnp.tile(m, (ts, 1)) for m in mod_s]

    xp = x_prompt.reshape(n_p, d)
    xs = x_sample.transpose(1, 0, 2).reshape(n_s, d)

    def moe(layer, x_p, y_p, x_s, y_s, w_out, final_norm):
        route = functools.partial(resid_router, g=g_norm_ffn[layer], w_r=w_router[layer], b_r=b_router[layer],
                                  tile=MOE_TOKEN_TILE, top_k=TOP_K, n_blocks_total=n_blocks)
        x1_p, sorted_rows, pos_p, gate_p, cnt_p = route(
            x_p, y_p, w_out, mod_p[layer], sorted_in=None, rows_per_seq=tp, block_offset=0)
        x1_s, sorted_rows, pos_s, gate_s, cnt_s = route(
            x_s, y_s, w_out, mod_st[layer], sorted_in=sorted_rows, rows_per_seq=None, block_offset=blocks_p)
        cnt8 = jnp.concatenate([cnt_p, cnt_s], axis=0)[:, 0, :n_experts]
        tables = _expert_tables(cnt8, n_sorted, n_row_tiles, MOE_ROW_TILE)
        ys = moe_experts(sorted_rows, w_gate_up, b_gate_up[layer], w_down, b_down[layer], tables,
                         layer=layer, row_tile=MOE_ROW_TILE, limit=SWIGLU_LIMIT, alpha=SWIGLU_ALPHA)
        combine = functools.partial(moe_combine, ys=ys, g_final=g_final, tile=MOE_TOKEN_TILE,
                                    final_norm=final_norm, n_sorted=n_sorted)
        out_p = combine(x1_p, pos=pos_p, gates=gate_p, mod=mod_p[layer], rows_per_seq=tp, block_offset=0)
        out_s = combine(x1_s, pos=pos_s, gates=gate_s, mod=mod_st[layer], rows_per_seq=None,
                        block_offset=blocks_p)
        return out_p, out_s

    w_in = w_in_hgrn[0].astype(BF16)
    proj_p = norm_proj(xp, mod_p[0], g_norm_mix[0], w_in, tile=PROJ_TILE, rows_per_seq=tp)
    proj_s = norm_proj(xs, mod_s[0], g_norm_mix[0], w_in, tile=bs, rows_per_seq=None)
    o_p, state_p = hgrn_recurrence(proj_p.reshape(bp, tp, 4 * hk), lb_logits, g_out_hgrn[0], None,
                                   layer=0, seq_block=1, time_block=PROJ_TILE, chunk=HGRN_CHUNK,
                                   c_sub=HGRN_SUB, n_valid=HGRN_CHUNK)
    proj_sb = jnp.pad(proj_s.reshape(ts, bs, 4 * hk).transpose(1, 0, 2), ((0, 0), (0, SAMPLE_T_PAD - ts), (0, 0)))
    o_s, state_s = hgrn_recurrence(proj_sb, lb_logits, g_out_hgrn[0], state_hgrn[0],
                                   layer=0, seq_block=8, time_block=SAMPLE_T_PAD, chunk=SAMPLE_T_PAD,
                                   c_sub=SAMPLE_T_PAD, n_valid=ts)
    o_s = o_s[:, :ts].transpose(1, 0, 2).reshape(n_s, hk)
    x_p, x_s = moe(0, xp, o_p.reshape(n_p, hk), xs, o_s, w_out_hgrn[0].astype(BF16), False)

    w_grp = w_grp_pool[0].astype(BF16)
    n_keep = cache_pool.shape[2]
    y_p, cache_p = pool_mixer_prompt(x_p.reshape(bp, tp, d), mod_p[1], g_norm_mix[1], w_grp, scale_pool[0],
                                     tile=PROJ_TILE, windows=POOL_WINDOWS, n_keep=n_keep)
    y_s, cache_s = pool_mixer_sample(x_s.reshape(ts, bs, d), cache_pool[0].transpose(1, 0, 2), mod_s[1],
                                     g_norm_mix[1], w_grp, scale_pool[0],
                                     seq_block=32, windows=POOL_WINDOWS, start_pos=PAST_LEN)
    x_p, x_s = moe(1, x_p, y_p.reshape(n_p, d), x_s, y_s.reshape(n_s, d), None, True)

    return (x_p.reshape(bp, tp, d), x_s.reshape(ts, bs, d).transpose(1, 0, 2),
            state_p[None], state_s[None], cache_p[None], cache_s.transpose(1, 0, 2)[None])
```

```python
import functools

import jax
import jax.numpy as jnp
from jax import lax
from jax.experimental import pallas as pl
from jax.experimental.pallas import tpu as pltpu

F32 = jnp.float32
BF16 = jnp.bfloat16

RMS_EPS = 1e-6
LANES = 128
SUBLANES = 8
BF16_ROWS = 16
HEAD_DIM = 128
VMEM_LIMIT = 56 * 1024 * 1024

_dot = functools.partial(jnp.dot, preferred_element_type=F32)


def _params(*semantics):
    return pltpu.CompilerParams(dimension_semantics=semantics, vmem_limit_bytes=VMEM_LIMIT)


def _split_bf16(x, n):
    parts, r = [], x
    for _ in range(n):
        p = r.astype(BF16)
        parts.append(p)
        r = r - p.astype(F32)
    return parts


def _dot_hp(a, b):
    a_hi, a_lo = _split_bf16(a, 2)
    b_hi, b_lo = _split_bf16(b, 2)
    return _dot(a_hi, b_hi) + (_dot(a_hi, b_lo) + _dot(a_lo, b_hi))


def _sigmoid(x):
    return 1.0 / (1.0 + jnp.exp(-x))


def _silu(x):
    return x * _sigmoid(x)


def _rms_norm(x, g):
    ms = jnp.mean(x * x, axis=-1, keepdims=True)
    return x * lax.rsqrt(ms + RMS_EPS) * g


def _adaln_kernel(c_ref, w_ref, b_ref, o_ref):
    o_ref[0] = _dot_hp(_silu(c_ref[...]), w_ref[0]) + b_ref[0]


def adaln(c_all, w_ada, b_ada, *, col_block=1536):
    n_seq, d = c_all.shape
    n_layers, _, d6 = w_ada.shape
    return pl.pallas_call(
        _adaln_kernel,
        grid=(n_layers, d6 // col_block),
        in_specs=[
            pl.BlockSpec((n_seq, d), lambda l, j: (0, 0)),
            pl.BlockSpec((1, d, col_block), lambda l, j: (l, 0, j)),
            pl.BlockSpec((1, 1, col_block), lambda l, j: (l, 0, j)),
        ],
        out_specs=pl.BlockSpec((1, n_seq, col_block), lambda l, j: (l, 0, j)),
        out_shape=jax.ShapeDtypeStruct((n_layers, n_seq, d6), F32),
        compiler_params=_params("parallel", "parallel"),
        name="adaln",
    )(c_all, w_ada, b_ada.reshape(n_layers, 1, d6))


def _mod_spec(mod, k, d, tile, rows_per_seq):
    if rows_per_seq is None:
        return pl.BlockSpec((tile, d), lambda i: (0, k))
    tiles_per_seq = rows_per_seq // tile
    return pl.BlockSpec((1, 1, d), lambda i: (i // tiles_per_seq, 0, k))


def _norm_proj_kernel(x_ref, sh_ref, sc_ref, g_ref, w_ref, o_ref):
    d = x_ref.shape[-1]
    h = _rms_norm(x_ref[...], g_ref[...]) * (1.0 + sc_ref[...].reshape(-1, d)) + sh_ref[...].reshape(-1, d)
    o_ref[...] = _dot(h.astype(BF16), w_ref[...])


def norm_proj(x, mod, g, w_bf16, *, tile, rows_per_seq):
    n, d = x.shape
    p = w_bf16.shape[1]
    return pl.pallas_call(
        _norm_proj_kernel,
        grid=(n // tile,),
        in_specs=[
            pl.BlockSpec((tile, d), lambda i: (i, 0)),
            _mod_spec(mod, 0, d, tile, rows_per_seq),
            _mod_spec(mod, 1, d, tile, rows_per_seq),
            pl.BlockSpec((1, d), lambda i: (0, 0)),
            pl.BlockSpec((d, p), lambda i: (0, 0)),
        ],
        out_specs=pl.BlockSpec((tile, p), lambda i: (i, 0)),
        out_shape=jax.ShapeDtypeStruct((n, p), F32),
        compiler_params=_params("parallel"),
        name="hgrn_norm_proj",
    )(x, mod, mod, g.reshape(1, d), w_bf16)


def _cumsum_rows(x, tri):
    hi, mid, lo = _split_bf16(x, 3)
    return _dot(tri, hi) + (_dot(tri, mid) + _dot(tri, lo))


MAX_LOG_DECAY_RANGE = 80.0


def _hgrn_prep(proj, lb, n_valid):
    c = proj.shape[0]
    hk = proj.shape[1] // 4
    row = lax.broadcasted_iota(jnp.int32, (c, c), 0)
    col = lax.broadcasted_iota(jnp.int32, (c, c), 1)
    zf = proj[:, hk:2 * hk]
    e = jnp.exp(-jnp.abs(zf))
    r = 1.0 / (1.0 + e)
    pos = zf >= 0
    sig_p = jnp.where(pos, 1.0, e) * r
    sig_n = jnp.where(pos, e, 1.0) * r
    logf = jnp.log(lb + (1.0 - lb) * sig_p)
    k = (1.0 - lb) * sig_n
    if n_valid < c:
        live = lax.broadcasted_iota(jnp.int32, (c, 1), 0) < n_valid
        logf = jnp.where(live, logf, 0.0)
        k = jnp.where(live, k, 0.0)
    b = _cumsum_rows(logf, (row >= col).astype(BF16))
    return _silu(proj[:, :hk]), k, proj[:, 2 * hk:3 * hk], _silu(proj[:, 3 * hk:]), b


def _decay_range(b, c_sub):
    c = b.shape[0]
    worst = None
    for i in range(c // c_sub):
        span = b[i * c_sub:i * c_sub + 1, :] - b[(i + 1) * c_sub - 1:(i + 1) * c_sub, :]
        worst = span if worst is None else jnp.maximum(worst, span)
    return jnp.max(worst)


def _hgrn_finish(prep, gout, st_refs, seq, c_sub, exact_refs):
    q, k, v, gate, b = prep
    c = q.shape[0]
    n_heads = q.shape[1] // HEAD_DIM
    row = lax.broadcasted_iota(jnp.int32, (c, c), 0)
    col = lax.broadcasted_iota(jnp.int32, (c, c), 1)
    causal = row >= col

    if exact_refs is not None:
        q_ref, b_ref, oi_ref = exact_refs
        q_ref[...] = q
        b_ref[...] = b
        key_row = lax.broadcasted_iota(jnp.int32, (c, 1), 0)

        def row_group(g, carry):
            rows = pl.ds(pl.multiple_of(g * SUBLANES, SUBLANES), SUBLANES)
            for h in range(n_heads):
                hs = slice(h * HEAD_DIM, (h + 1) * HEAD_DIM)
                q_g, b_g = q_ref[rows, hs], b_ref[rows, hs]
                o_rows = []
                for r in range(SUBLANES):
                    decay = jnp.exp(jnp.minimum(b_g[r:r + 1] - b[:, hs], 0.0))
                    score = jnp.sum(decay * k[:, hs] * q_g[r:r + 1], axis=-1, keepdims=True)
                    score = jnp.where(key_row <= g * SUBLANES + r, score, 0.0)
                    o_rows.append(jnp.sum(score * v[:, hs], axis=0, keepdims=True))
                oi_ref[rows, hs] = jnp.concatenate(o_rows, axis=0)
            return carry
        lax.fori_loop(0, c // SUBLANES, row_group, 0)

    outs = []
    for h in range(n_heads):
        hs = slice(h * HEAD_DIM, (h + 1) * HEAD_DIM)
        bh, qh, kh, vh = b[:, hs], q[:, hs], k[:, hs], v[:, hs]
        vb = vh.astype(BF16)
        if exact_refs is not None:
            o = exact_refs[2][:, hs]
        else:
            a_rows = []
            for i in range(c // c_sub):
                rs = slice(i * c_sub, (i + 1) * c_sub)
                ref = bh[i * c_sub + c_sub // 2:i * c_sub + c_sub // 2 + 1, :]
                q_hat = (qh[rs] * jnp.exp(bh[rs] - ref)).astype(BF16)
                k_hat = (kh * jnp.exp(jnp.minimum(ref - bh, MAX_LOG_DECAY_RANGE))).astype(BF16)
                a_rows.append(lax.dot_general(q_hat, k_hat, (((1,), (1,)), ((), ())),
                                              preferred_element_type=F32))
            att = jnp.where(causal, jnp.concatenate(a_rows, axis=0) if len(a_rows) > 1 else a_rows[0], 0.0)
            o = _dot(att.astype(BF16), vb)
        st = st_refs[seq, h]
        o = o + lax.dot_general((qh * jnp.exp(bh)).astype(BF16), st.astype(BF16),
                                (((1,), (1,)), ((), ())), preferred_element_type=F32)
        b_last = bh[c - 1:c, :]
        k_dec = (kh * jnp.exp(b_last - bh)).astype(BF16)
        st_refs[seq, h] = st * jnp.exp(b_last) + lax.dot_general(
            vb, k_dec, (((0,), (0,)), ((), ())), preferred_element_type=F32)
        o = o * lax.rsqrt(jnp.mean(o * o, axis=-1, keepdims=True) + RMS_EPS) * gout
        outs.append(o * gate[:, hs])
    return jnp.concatenate(outs, axis=-1)


def _lower_bound(lb_logits, layer):
    e = jnp.exp(lb_logits - jnp.max(lb_logits, axis=0, keepdims=True))
    return jnp.sum(e[:layer + 1], axis=0, keepdims=True) / jnp.sum(e, axis=0, keepdims=True)


def _hgrn_rec_kernel(*refs, chunk, c_sub, n_valid, has_state, layer):
    if has_state:
        proj_ref, lb_ref, gout_ref, s0_ref, o_ref, sout_ref, st_ref, *exact_refs = refs
    else:
        proj_ref, lb_ref, gout_ref, o_ref, sout_ref, st_ref, *exact_refs = refs
    bb, tb, _ = proj_ref.shape
    n_heads = st_ref.shape[1]
    j = pl.program_id(1)

    @pl.when(j == 0)
    def _():
        if has_state:
            for s in range(bb):
                for h in range(n_heads):
                    st_ref[s, h] = s0_ref[s, h].T
        else:
            st_ref[...] = jnp.zeros_like(st_ref)

    lb = _lower_bound(lb_ref[...], layer)
    gout = gout_ref[...]

    def chunk_body(ci, carry):
        rows = pl.ds(pl.multiple_of(ci * chunk, chunk), chunk)
        preps = [_hgrn_prep(proj_ref[s, rows, :], lb, n_valid) for s in range(bb)]
        span = _decay_range(preps[0][4], c_sub)
        for prep in preps[1:]:
            span = jnp.maximum(span, _decay_range(prep[4], c_sub))

        @pl.when(span <= MAX_LOG_DECAY_RANGE)
        def _():
            for s in range(bb):
                o_ref[s, rows, :] = _hgrn_finish(preps[s], gout, st_ref, s, c_sub, None)

        @pl.when(jnp.logical_not(span <= MAX_LOG_DECAY_RANGE))
        def _():
            for s in range(bb):
                o_ref[s, rows, :] = _hgrn_finish(preps[s], gout, st_ref, s, c_sub, exact_refs)
        return carry
    lax.fori_loop(0, tb // chunk, chunk_body, 0)

    @pl.when(j == pl.num_programs(1) - 1)
    def _():
        for s in range(bb):
            for h in range(n_heads):
                sout_ref[s, h] = st_ref[s, h].T


def hgrn_recurrence(proj, lb_logits, g_out, s0, *, layer, seq_block, time_block, chunk, c_sub, n_valid):
    bsz, t, p = proj.shape
    hk = p // 4
    n_heads = hk // HEAD_DIM
    has_state = s0 is not None
    st_shape = (seq_block, n_heads, HEAD_DIM, HEAD_DIM)
    st_spec = pl.BlockSpec(st_shape, lambda i, j: (i, 0, 0, 0))
    in_specs = [
        pl.BlockSpec((seq_block, time_block, p), lambda i, j: (i, j, 0)),
        pl.BlockSpec(lb_logits.shape, lambda i, j: (0, 0)),
        pl.BlockSpec((1, HEAD_DIM), lambda i, j: (0, 0)),
    ]
    args = [proj, lb_logits, g_out.reshape(1, HEAD_DIM)]
    if has_state:
        in_specs.append(st_spec)
        args.append(s0)
    return pl.pallas_call(
        functools.partial(_hgrn_rec_kernel, chunk=chunk, c_sub=c_sub, n_valid=n_valid,
                          has_state=has_state, layer=layer),
        grid=(bsz // seq_block, t // time_block),
        in_specs=in_specs,
        out_specs=[pl.BlockSpec((seq_block, time_block, hk), lambda i, j: (i, j, 0)), st_spec],
        out_shape=[jax.ShapeDtypeStruct((bsz, t, hk), F32),
                   jax.ShapeDtypeStruct((bsz, n_heads, HEAD_DIM, HEAD_DIM), F32)],
        scratch_shapes=[pltpu.VMEM(st_shape, F32)] + [pltpu.VMEM((chunk, hk), F32)] * 3,
        compiler_params=_params("parallel", "arbitrary"),
        name="hgrn_recurrence",
    )(*args)


def _sorted_rows(tile, top_k, n_experts):
    return tile * top_k + n_experts * BF16_ROWS


def _resid_router_kernel(*refs, top_k, n_experts, has_w_out, chained, row_chunk):
    refs = list(refs)
    x_ref, y_ref = refs[:2]
    del refs[:2]
    wo_ref = refs.pop(0) if has_w_out else None
    gt_ref, sh_ref, sc_ref, g_ref, wr_ref, br_ref = refs[:6]
    del refs[:6]
    if chained:
        refs.pop(0)
    x1_ref, xs_ref, pos_ref, gate_ref, cnt_ref = refs
    tile, d = x_ref.shape
    n_sorted = xs_ref.shape[0]

    y = y_ref[...]
    if has_w_out:
        y = _dot(y.astype(BF16), wo_ref[...])
    x1 = x_ref[...] + gt_ref[...].reshape(-1, d) * y
    x1_ref[...] = x1
    h = _rms_norm(x1, g_ref[...]) * (1.0 + sc_ref[...].reshape(-1, d)) + sh_ref[...].reshape(-1, d)

    lane = lax.broadcasted_iota(jnp.int32, (tile, LANES), 1).astype(F32)
    logits = jnp.where(lane < n_experts, _dot_hp(h, wr_ref[...]) + br_ref[...], -jnp.inf)
    picks, vals = [], []
    for _ in range(top_k):
        m = jnp.max(logits, axis=-1, keepdims=True)
        pick = jnp.min(jnp.where(logits == m, lane, float(LANES)), axis=-1, keepdims=True)
        picks.append(pick)
        vals.append(m)
        logits = jnp.where(lane == pick, -jnp.inf, logits)
    exps = [jnp.exp(v - vals[0]) for v in vals]
    denom = exps[0]
    for e in exps[1:]:
        denom = denom + e

    onehots = [(lane == p).astype(F32) for p in picks]
    oh_sum = onehots[0]
    for oh in onehots[1:]:
        oh_sum = oh_sum + oh
    row = lax.broadcasted_iota(jnp.int32, (tile, tile), 0)
    col = lax.broadcasted_iota(jnp.int32, (tile, tile), 1)
    before = _dot((row > col).astype(BF16), oh_sum.astype(BF16))
    count = jnp.sum(oh_sum, axis=0, keepdims=True)
    cnt_pad = jnp.floor((count + (BF16_ROWS - 1.0)) * (1.0 / BF16_ROWS)) * BF16_ROWS
    lane8 = lax.broadcasted_iota(jnp.int32, (SUBLANES, LANES), 1)
    run = jnp.broadcast_to(cnt_pad, (SUBLANES, LANES))
    shift = 1
    while shift < n_experts:
        run = run + jnp.where(lane8 >= shift, pltpu.roll(run, shift, 1), 0.0)
        shift *= 2
    pos = before + (run[0:1] - cnt_pad)
    pos_out = jnp.zeros((tile, LANES), F32)
    gate_out = jnp.zeros((tile, LANES), F32)
    for k in range(top_k):
        pos_k = jnp.sum(onehots[k] * pos, axis=-1, keepdims=True)
        pos_out = jnp.where(lane == k, pos_k, pos_out)
        gate_out = jnp.where(lane == k, exps[k] / denom, gate_out)
    pos_ref[...] = pos_out.astype(jnp.int32)
    gate_ref[...] = gate_out
    cnt_ref[0] = cnt_pad.astype(jnp.int32)

    pos_t = pos_out.T
    hb = h.astype(BF16)
    for r0 in range(0, n_sorted, row_chunk):
        slot = (lax.broadcasted_iota(jnp.int32, (row_chunk, tile), 0) + r0).astype(F32)
        sel = jnp.where(slot == pos_t[0:1], 1.0, 0.0)
        for k in range(1, top_k):
            sel = sel + jnp.where(slot == pos_t[k:k + 1], 1.0, 0.0)
        xs_ref[r0:r0 + row_chunk, :] = _dot(sel.astype(BF16), hb).astype(BF16)


def resid_router(x, y, w_out_bf16, mod, g, w_r, b_r, sorted_in, *, tile, rows_per_seq, top_k,
                 block_offset, n_blocks_total):
    n, d = x.shape
    n_experts = w_r.shape[1]
    n_sorted = _sorted_rows(tile, top_k, n_experts)
    w_r_pad = jnp.pad(w_r, ((0, 0), (0, LANES - n_experts)))
    b_r_pad = jnp.pad(b_r, (0, LANES - n_experts)).reshape(1, LANES)
    has_w_out = w_out_bf16 is not None
    chained = sorted_in is not None
    row_spec = pl.BlockSpec((tile, d), lambda i: (i, 0))
    lane_spec = pl.BlockSpec((tile, LANES), lambda i: (i, 0))
    full = lambda a: pl.BlockSpec(a.shape, lambda i: (0,) * a.ndim)
    in_specs = [row_spec, pl.BlockSpec((tile, y.shape[1]), lambda i: (i, 0))]
    args = [x, y]
    if has_w_out:
        in_specs.append(full(w_out_bf16))
        args.append(w_out_bf16)
    in_specs += [_mod_spec(mod, 2, d, tile, rows_per_seq), _mod_spec(mod, 3, d, tile, rows_per_seq),
                 _mod_spec(mod, 4, d, tile, rows_per_seq), pl.BlockSpec((1, d), lambda i: (0, 0)),
                 full(w_r_pad), full(b_r_pad)]
    args += [mod, mod, mod, g.reshape(1, d), w_r_pad, b_r_pad]
    aliases = {}
    if chained:
        aliases = {len(args): 1}
        in_specs.append(pl.BlockSpec(memory_space=pl.ANY))
        args.append(sorted_in)
    n_tiles = n // tile
    return pl.pallas_call(
        functools.partial(_resid_router_kernel, top_k=top_k, n_experts=n_experts, has_w_out=has_w_out,
                          chained=chained, row_chunk=256),
        grid=(n_tiles,),
        in_specs=in_specs,
        out_specs=[row_spec,
                   pl.BlockSpec((n_sorted, d), lambda i: (i + block_offset, 0)),
                   lane_spec, lane_spec,
                   pl.BlockSpec((1, 1, LANES), lambda i: (i, 0, 0))],
        out_shape=[jax.ShapeDtypeStruct((n, d), F32),
                   jax.ShapeDtypeStruct((n_blocks_total * n_sorted, d), BF16),
                   jax.ShapeDtypeStruct((n, LANES), jnp.int32), jax.ShapeDtypeStruct((n, LANES), F32),
                   jax.ShapeDtypeStruct((n_tiles, 1, LANES), jnp.int32)],
        input_output_aliases=aliases,
        compiler_params=_params("parallel"),
        name="resid_router",
    )(*args)


def _experts_kernel(te_ref, first_ref, rows_ref, plo_ref, phi_ref, wslot_ref, nexte_ref, used_ref,
                    psrc_ref, pdst_ref, pnum_ref,
                    xs_hbm, wgu_hbm, bgu_ref, wdn_hbm, bdn_ref, ys_hbm,
                    xbuf, ybuf, wgu_f, wdn_f, wgu_b, wdn_b, in_sem, out_sem, w_sem,
                    *, layer, limit, alpha, col_chunk):
    del ys_hbm
    i = pl.program_id(0)
    used = used_ref[0]

    def copy(src_rows, dst_rows, slot, inbound):
        if inbound:
            return pltpu.make_async_copy(xs_hbm.at[src_rows, :], xbuf.at[slot, dst_rows, :], in_sem.at[slot])
        return pltpu.make_async_copy(ybuf.at[slot, dst_rows, :], xs_hbm.at[src_rows, :], out_sem.at[slot])

    def piece_copies(tile_idx, slot, inbound, wait):
        if wait:
            rows = pl.ds(0, pl.multiple_of(rows_ref[tile_idx], BF16_ROWS))
            copy(rows, rows, slot, inbound).wait()
            return

        def piece(p, c):
            def unit(j, c2):
                src = pl.ds(pl.multiple_of(psrc_ref[p] + j * BF16_ROWS, BF16_ROWS), BF16_ROWS)
                dst = pl.ds(pl.multiple_of(pdst_ref[p] + j * BF16_ROWS, BF16_ROWS), BF16_ROWS)
                copy(src, dst, slot, inbound).start()
                return c2
            return lax.fori_loop(0, pnum_ref[p], unit, c)
        lax.fori_loop(plo_ref[tile_idx], phi_ref[tile_idx], piece, 0)

    def weight_copies(e, slot):
        return (pltpu.make_async_copy(wgu_hbm.at[layer, e], wgu_f.at[slot], w_sem.at[slot]),
                pltpu.make_async_copy(wdn_hbm.at[layer, e], wdn_f.at[slot], w_sem.at[slot]))

    @pl.when(i == 0)
    def _():
        xbuf[...] = jnp.zeros_like(xbuf)
        for cp in weight_copies(te_ref[0], 0):
            cp.start()
        piece_copies(0, 0, True, False)

    @pl.when(i < used)
    def _():
        slot = i % 2

        @pl.when(i + 1 < used)
        def _():
            piece_copies(i + 1, 1 - slot, True, False)

        @pl.when(first_ref[i] == 1)
        def _():
            ws = wslot_ref[i]
            for cp in weight_copies(te_ref[i], ws):
                cp.wait()

            @pl.when(nexte_ref[i] >= 0)
            def _():
                for cp in weight_copies(nexte_ref[i], 1 - ws):
                    cp.start()
            wgu_b[...] = wgu_f[ws].astype(BF16)
            wdn_b[...] = wdn_f[ws].astype(BF16)

        piece_copies(i, slot, True, True)

        @pl.when(i >= 2)
        def _():
            piece_copies(i - 2, slot, False, True)

        d_ff = wdn_b.shape[0]
        tm = xbuf.shape[1]
        e = te_ref[i]
        b_gu = bgu_ref[pl.ds(e, 1), :]
        b_dn = bdn_ref[pl.ds(e, 1), :]

        def mlp(n_rows):
            x = xbuf[slot, :n_rows, :]
            y = None
            for c0 in range(0, d_ff, col_chunk):
                cs = slice(c0, c0 + col_chunk)
                us = slice(d_ff + c0, d_ff + c0 + col_chunk)
                gate = jnp.minimum(_dot(x, wgu_b[:, cs]) + b_gu[:, cs], limit)
                up = jnp.clip(_dot(x, wgu_b[:, us]) + b_gu[:, us], -limit, limit)
                act = ((up + 1.0) * (gate * _sigmoid(alpha * gate))).astype(BF16)
                part = _dot(act, wdn_b[cs, :])
                y = part if y is None else y + part
            ybuf[slot, :n_rows, :] = (y + b_dn).astype(BF16)

        @pl.when(rows_ref[i] > tm // 2)
        def _():
            mlp(tm)

        @pl.when(rows_ref[i] <= tm // 2)
        def _():
            mlp(tm // 2)
        piece_copies(i, slot, False, False)

    @pl.when(i == pl.num_programs(0) - 1)
    def _():
        @pl.when(used >= 2)
        def _():
            piece_copies(used - 2, used % 2, False, True)
        piece_copies(used - 1, (used - 1) % 2, False, True)


def moe_experts(sorted_rows, w_gu, b_gu, w_dn, b_dn, tables, *, layer, row_tile, limit, alpha, col_chunk=512):
    n_rows, d = sorted_rows.shape
    d_gu = w_gu.shape[-1]
    d_ff = w_dn.shape[-2]
    n_tiles = tables[0].shape[0]
    vmem = lambda a: pl.BlockSpec(a.shape, lambda i, *_: (0,) * a.ndim)
    any_spec = pl.BlockSpec(memory_space=pl.ANY)
    return pl.pallas_call(
        functools.partial(_experts_kernel, layer=layer, limit=limit, alpha=alpha,
                          col_chunk=min(col_chunk, d_ff)),
        grid_spec=pltpu.PrefetchScalarGridSpec(
            num_scalar_prefetch=len(tables),
            grid=(n_tiles,),
            in_specs=[any_spec, any_spec, vmem(b_gu), any_spec, vmem(b_dn)],
            out_specs=any_spec,
            scratch_shapes=[
                pltpu.VMEM((2, row_tile, d), BF16), pltpu.VMEM((2, row_tile, d), BF16),
                pltpu.VMEM((2, d, d_gu), F32), pltpu.VMEM((2, d_ff, d), F32),
                pltpu.VMEM((d, d_gu), BF16), pltpu.VMEM((d_ff, d), BF16),
                pltpu.SemaphoreType.DMA((2,)), pltpu.SemaphoreType.DMA((2,)), pltpu.SemaphoreType.DMA((2,)),
            ],
        ),
        out_shape=jax.ShapeDtypeStruct((n_rows, d), BF16),
        input_output_aliases={len(tables): 0},
        compiler_params=_params("arbitrary"),
        name="moe_experts",
    )(*tables, sorted_rows, w_gu, b_gu, w_dn, b_dn)


def _expert_tables(cnt, n_sorted, n_tiles, row_tile):
    n_blocks, n_experts = cnt.shape
    local_off = jnp.cumsum(cnt, axis=1) - cnt
    total = jnp.sum(cnt, axis=0)
    padded = (total + row_tile - 1) // row_tile * row_tile
    pad_end = jnp.cumsum(padded)
    pad_start = pad_end - padded
    seg_n = cnt.T.reshape(-1)
    seg_start = (pad_start[:, None] + jnp.cumsum(cnt.T, axis=1) - cnt.T).reshape(-1)
    seg_src = (jnp.arange(n_blocks, dtype=jnp.int32)[None, :] * n_sorted + local_off.T).reshape(-1)
    first_tile = seg_start // row_tile
    n_a = jnp.minimum(seg_n, (first_tile + 1) * row_tile - seg_start)
    n_b = seg_n - n_a
    piece_tile = jnp.stack([first_tile, jnp.where(n_b > 0, first_tile + 1, first_tile)], axis=1).reshape(-1)
    piece_src = jnp.stack([seg_src, seg_src + n_a], axis=1).reshape(-1)
    piece_dst = jnp.stack([seg_start - first_tile * row_tile, jnp.zeros_like(seg_start)], axis=1).reshape(-1)
    piece_units = jnp.stack([n_a, n_b], axis=1).reshape(-1) // BF16_ROWS
    tiles = jnp.arange(n_tiles, dtype=jnp.int32)
    piece_lo = jnp.sum(piece_tile[None, :] < tiles[:, None], axis=1)
    piece_hi = jnp.sum(piece_tile[None, :] <= tiles[:, None], axis=1)
    n_used = pad_end[-1] // row_tile
    tile_expert = jnp.minimum(jnp.sum(tiles[:, None] * row_tile >= pad_end[None, :], axis=1), n_experts - 1)
    live = tiles < n_used
    tile_rows = jnp.where(live, jnp.clip(pad_start[tile_expert] + total[tile_expert] - tiles * row_tile,
                                         0, row_tile), 0)
    tile_first = (live & (tiles * row_tile == pad_start[tile_expert])).astype(jnp.int32)
    owns = total > 0
    order = jnp.cumsum(owns.astype(jnp.int32)) - 1
    experts = jnp.arange(n_experts, dtype=jnp.int32)
    later = (experts[None, :] > experts[:, None]) & owns[None, :]
    next_expert = jnp.min(jnp.where(later, experts[None, :], n_experts), axis=1)
    next_expert = jnp.where(next_expert < n_experts, next_expert, -1)
    i32 = lambda a: a.astype(jnp.int32)
    return (i32(tile_expert), tile_first, i32(tile_rows), i32(piece_lo), i32(piece_hi),
            i32(order[tile_expert] % 2), i32(next_expert[tile_expert]), i32(n_used).reshape(1),
            i32(piece_src), i32(piece_dst), i32(piece_units))


def _combine_kernel(x_ref, pos_ref, gate_ref, gt_ref, gfin_ref, ys_ref, o_ref, *, top_k, final_norm, k_chunk):
    tile, d = x_ref.shape
    n_sorted = ys_ref.shape[0]
    pos = pos_ref[...].astype(F32)
    gates = gate_ref[...]
    acc = None
    for r0 in range(0, n_sorted, k_chunk):
        slot = (lax.broadcasted_iota(jnp.int32, (tile, k_chunk), 1) + r0).astype(F32)
        w = jnp.where(slot == pos[:, 0:1], gates[:, 0:1], 0.0)
        for k in range(1, top_k):
            w = w + jnp.where(slot == pos[:, k:k + 1], gates[:, k:k + 1], 0.0)
        part = _dot(w.astype(BF16), ys_ref[r0:r0 + k_chunk, :])
        acc = part if acc is None else acc + part
    out = x_ref[...] + gt_ref[...].reshape(-1, d) * acc
    if final_norm:
        out = _rms_norm(out, gfin_ref[...])
    o_ref[...] = out


def moe_combine(x, ys, pos, gates, mod, g_final, *, tile, rows_per_seq, final_norm, block_offset, n_sorted):
    n, d = x.shape
    return pl.pallas_call(
        functools.partial(_combine_kernel, top_k=TOP_K, final_norm=final_norm, k_chunk=512),
        grid=(n // tile,),
        in_specs=[
            pl.BlockSpec((tile, d), lambda i: (i, 0)),
            pl.BlockSpec((tile, LANES), lambda i: (i, 0)),
            pl.BlockSpec((tile, LANES), lambda i: (i, 0)),
            _mod_spec(mod, 5, d, tile, rows_per_seq),
            pl.BlockSpec((1, d), lambda i: (0, 0)),
            pl.BlockSpec((n_sorted, d), lambda i: (i + block_offset, 0)),
        ],
        out_specs=pl.BlockSpec((tile, d), lambda i: (i, 0)),
        out_shape=jax.ShapeDtypeStruct((n, d), F32),
        compiler_params=_params("parallel"),
        name="moe_combine",
    )(x, pos, gates, mod, g_final.reshape(1, d), ys)


def _pool_groups(h, window_sum, counts, w_ref, scale):
    n_groups = w_ref.shape[0]
    dg = h.shape[-1] // n_groups
    outs = []
    for gi in range(n_groups):
        cols = slice(gi * dg, (gi + 1) * dg)
        pooled = window_sum(gi, cols) / counts[gi] - h[:, cols]
        outs.append(_dot(pooled.astype(BF16), w_ref[gi]))
    return jnp.concatenate(outs, axis=-1) * scale


def _pool_prompt_kernel(x_ref, sh_ref, sc_ref, g_ref, w_ref, scale_ref, y_ref, cache_ref, ext_ref,
                        *, windows, halo):
    tile, d = x_ref.shape[1:]
    j = pl.program_id(1)

    @pl.when(j == 0)
    def _():
        ext_ref[0:halo, :] = jnp.zeros((halo, d), F32)

    h = _rms_norm(x_ref[0], g_ref[...]) * (1.0 + sc_ref[...].reshape(-1, d)) + sh_ref[...].reshape(-1, d)
    ext_ref[halo:halo + tile, :] = h
    pos = j * tile + lax.broadcasted_iota(jnp.int32, (tile, 1), 0)

    def window_sum(gi, cols):
        acc = h[:, cols]
        for s in range(1, windows[gi]):
            acc = acc + ext_ref[halo - s:halo - s + tile, cols]
        return acc
    counts = [jnp.minimum(pos + 1, w).astype(F32) for w in windows]
    y_ref[0] = _pool_groups(h, window_sum, counts, w_ref, scale_ref[...])

    n_keep = cache_ref.shape[1]
    @pl.when(j == pl.num_programs(1) - 1)
    def _():
        cache_ref[0] = ext_ref[halo + tile - n_keep:halo + tile, :]
    ext_ref[0:halo, :] = ext_ref[tile:tile + halo, :]


def pool_mixer_prompt(x, mod, g, w_grp_bf16, scale, *, tile, windows, n_keep):
    bsz, t, d = x.shape
    halo = 16
    assert max(windows) <= halo <= tile and n_keep <= tile
    mod_spec = lambda k: pl.BlockSpec((1, 1, d), lambda b, j: (b, 0, k))
    return pl.pallas_call(
        functools.partial(_pool_prompt_kernel, windows=windows, halo=halo),
        grid=(bsz, t // tile),
        in_specs=[
            pl.BlockSpec((1, tile, d), lambda b, j: (b, j, 0)),
            mod_spec(0), mod_spec(1),
            pl.BlockSpec((1, d), lambda b, j: (0, 0)),
            pl.BlockSpec(w_grp_bf16.shape, lambda b, j: (0, 0, 0)),
            pl.BlockSpec((1, d), lambda b, j: (0, 0)),
        ],
        out_specs=[pl.BlockSpec((1, tile, d), lambda b, j: (b, j, 0)),
                   pl.BlockSpec((1, n_keep, d), lambda b, j: (b, 0, 0))],
        out_shape=[jax.ShapeDtypeStruct((bsz, t, d), F32), jax.ShapeDtypeStruct((bsz, n_keep, d), F32)],
        scratch_shapes=[pltpu.VMEM((halo + tile, d), F32)],
        compiler_params=_params("parallel", "arbitrary"),
        name="pool_mixer_prompt",
    )(x, mod, mod, g.reshape(1, d), w_grp_bf16, scale.reshape(1, d))


def _pool_sample_kernel(x_ref, buf_ref, sh_ref, sc_ref, g_ref, w_ref, scale_ref, y_ref, cache_ref,
                        *, windows, start_pos):
    t_len = x_ref.shape[0]
    n_prev = buf_ref.shape[0]
    hs = [_rms_norm(x_ref[t], g_ref[...]) * (1.0 + sc_ref[...]) + sh_ref[...] for t in range(t_len)]

    def ext(r):
        return buf_ref[r] if r < n_prev else hs[r - n_prev]

    for t in range(t_len):
        def window_sum(gi, cols):
            acc = hs[t][:, cols]
            for s in range(1, windows[gi]):
                acc = acc + ext(n_prev + t - s)[:, cols]
            return acc
        counts = [float(min(start_pos + t + 1, w)) for w in windows]
        y_ref[t] = _pool_groups(hs[t], window_sum, counts, w_ref, scale_ref[...])
    for r in range(n_prev):
        cache_ref[r] = ext(t_len + r)


def pool_mixer_sample(x_t, buf_t, mod, g, w_grp_bf16, scale, *, seq_block, windows, start_pos):
    t_len, n_seq, d = x_t.shape
    n_prev = buf_t.shape[0]
    assert start_pos >= n_prev >= max(windows) - 1
    mod_spec = lambda k: pl.BlockSpec((seq_block, d), lambda i: (i, k))
    return pl.pallas_call(
        functools.partial(_pool_sample_kernel, windows=windows, start_pos=start_pos),
        grid=(n_seq // seq_block,),
        in_specs=[
            pl.BlockSpec((t_len, seq_block, d), lambda i: (0, i, 0)),
            pl.BlockSpec((n_prev, seq_block, d), lambda i: (0, i, 0)),
            mod_spec(0), mod_spec(1),
            pl.BlockSpec((1, d), lambda i: (0, 0)),
            pl.BlockSpec(w_grp_bf16.shape, lambda i: (0, 0, 0)),
            pl.BlockSpec((1, d), lambda i: (0, 0)),
        ],
        out_specs=[pl.BlockSpec((t_len, seq_block, d), lambda i: (0, i, 0)),
                   pl.BlockSpec((n_prev, seq_block, d), lambda i: (0, i, 0))],
        out_shape=[jax.ShapeDtypeStruct((t_len, n_seq, d), F32), jax.ShapeDtypeStruct((n_prev, n_seq, d), F32)],
        compiler_params=_params("parallel"),
        name="pool_mixer_sample",
    )(x_t, buf_t, mod, mod, g.reshape(1, d), w_grp_bf16, scale.reshape(1, d))


TOP_K = 4
SWIGLU_LIMIT = 7.0
SWIGLU_ALPHA = 1.702
POOL_WINDOWS = (2, 4, 8, 16)
PAST_LEN = 16384
PROJ_TILE = 256
MOE_TOKEN_TILE = 512
MOE_ROW_TILE = 512
HGRN_CHUNK = 128
HGRN_SUB = 32
SAMPLE_T_PAD = 8
SAMPLE_SEQ_BLOCK = 8


def kernel(x_prompt, x_sample, c_prompt, c_sample, state_hgrn, cache_pool, g_norm_mix, g_norm_ffn, w_ada, b_ada, w_in_hgrn, lb_logits, g_out_hgrn, w_out_hgrn, w_grp_pool, scale_pool, w_router, b_router, w_gate_up, b_gate_up, w_down, b_down, g_final):
    bp, tp, d = x_prompt.shape
    bs, ts, _ = x_sample.shape
    n_p, n_s = bp * tp, bs * ts
    n_experts = w_router.shape[-1]
    hk = w_out_hgrn.shape[1]
    assert n_s == MOE_TOKEN_TILE and n_p % MOE_TOKEN_TILE == 0
    blocks_p = n_p // MOE_TOKEN_TILE
    n_blocks = blocks_p + 1
    n_sorted = _sorted_rows(MOE_TOKEN_TILE, TOP_K, n_experts)
    n_row_tiles = -(-(n_blocks * n_sorted + n_experts * (MOE_ROW_TILE - BF16_ROWS)) // MOE_ROW_TILE)

    mod = adaln(jnp.concatenate([c_prompt, c_sample], axis=0), w_ada, b_ada)
    mod_p = [mod[l, :bp][:, None, :] for l in range(mod.shape[0])]
    mod_s = [mod[l, bp:] for l in range(mod.shape[0])]
    mod_st = [jnp.tile(m, (ts, 1)) for m in mod_s]

    xp = x_prompt.reshape(n_p, d)
    xs = x_sample.transpose(1, 0, 2).reshape(n_s, d)

    def moe(layer, x_p, y_p, x_s, y_s, w_out, final_norm):
        route = functools.partial(resid_router, g=g_norm_ffn[layer], w_r=w_router[layer], b_r=b_router[layer],
                                  tile=MOE_TOKEN_TILE, top_k=TOP_K, n_blocks_total=n_blocks)
        x1_p, sorted_rows, pos_p, gate_p, cnt_p = route(
            x_p, y_p, w_out, mod_p[layer], sorted_in=None, rows_per_seq=tp, block_offset=0)
        x1_s, sorted_rows, pos_s, gate_s, cnt_s = route(
            x_s, y_s, w_out, mod_st[layer], sorted_in=sorted_rows, rows_per_seq=None, block_offset=blocks_p)
        cnt8 = jnp.concatenate([cnt_p, cnt_s], axis=0)[:, 0, :n_experts]
        tables = _expert_tables(cnt8, n_sorted, n_row_tiles, MOE_ROW_TILE)
        ys = moe_experts(sorted_rows, w_gate_up, b_gate_up[layer], w_down, b_down[layer], tables,
                         layer=layer, row_tile=MOE_ROW_TILE, limit=SWIGLU_LIMIT, alpha=SWIGLU_ALPHA)
        combine = functools.partial(moe_combine, ys=ys, g_final=g_final, tile=MOE_TOKEN_TILE,
                                    final_norm=final_norm, n_sorted=n_sorted)
        out_p = combine(x1_p, pos=pos_p, gates=gate_p, mod=mod_p[layer], rows_per_seq=tp, block_offset=0)
        out_s = combine(x1_s, pos=pos_s, gates=gate_s, mod=mod_st[layer], rows_per_seq=None,
                        block_offset=blocks_p)
        return out_p, out_s

    w_in = w_in_hgrn[0].astype(BF16)
    proj_p = norm_proj(xp, mod_p[0], g_norm_mix[0], w_in, tile=PROJ_TILE, rows_per_seq=tp)
    proj_s = norm_proj(xs, mod_s[0], g_norm_mix[0], w_in, tile=bs, rows_per_seq=None)
    o_p, state_p = hgrn_recurrence(proj_p.reshape(bp, tp, 4 * hk), lb_logits, g_out_hgrn[0], None,
                                   layer=0, seq_block=1, time_block=PROJ_TILE, chunk=HGRN_CHUNK,
                                   c_sub=HGRN_SUB, n_valid=HGRN_CHUNK)
    proj_sb = jnp.pad(proj_s.reshape(ts, bs, 4 * hk).transpose(1, 0, 2), ((0, 0), (0, SAMPLE_T_PAD - ts), (0, 0)))
    o_s, state_s = hgrn_recurrence(proj_sb, lb_logits, g_out_hgrn[0], state_hgrn[0],
                                   layer=0, seq_block=SAMPLE_SEQ_BLOCK, time_block=SAMPLE_T_PAD,
                                   chunk=SAMPLE_T_PAD, c_sub=SAMPLE_T_PAD, n_valid=ts)
    o_s = o_s[:, :ts].transpose(1, 0, 2).reshape(n_s, hk)
    x_p, x_s = moe(0, xp, o_p.reshape(n_p, hk), xs, o_s, w_out_hgrn[0].astype(BF16), False)

    w_grp = w_grp_pool[0].astype(BF16)
    n_keep = cache_pool.shape[2]
    y_p, cache_p = pool_mixer_prompt(x_p.reshape(bp, tp, d), mod_p[1], g_norm_mix[1], w_grp, scale_pool[0],
                                     tile=PROJ_TILE, windows=POOL_WINDOWS, n_keep=n_keep)
    y_s, cache_s = pool_mixer_sample(x_s.reshape(ts, bs, d), cache_pool[0].transpose(1, 0, 2), mod_s[1],
                                     g_norm_mix[1], w_grp, scale_pool[0],
                                     seq_block=32, windows=POOL_WINDOWS, start_pos=PAST_LEN)
    x_p, x_s = moe(1, x_p, y_p.reshape(n_p, d), x_s, y_s.reshape(n_s, d), None, True)

    return (x_p.reshape(bp, tp, d), x_s.reshape(ts, bs, d).transpose(1, 0, 2),
            state_p[None], state_s[None], cache_p[None], cache_s.transpose(1, 0, 2)[None])
```

```python
import functools

import jax
import jax.numpy as jnp
from jax import lax
from jax.experimental import pallas as pl
from jax.experimental.pallas import tpu as pltpu

F32 = jnp.float32
BF16 = jnp.bfloat16

RMS_EPS = 1e-6
LANES = 128
SUBLANES = 8
BF16_ROWS = 16
HEAD_DIM = 128
VMEM_LIMIT = 56 * 1024 * 1024

_dot = functools.partial(jnp.dot, preferred_element_type=F32)


def _params(*semantics):
    return pltpu.CompilerParams(dimension_semantics=semantics, vmem_limit_bytes=VMEM_LIMIT)


def _split_bf16(x, n):
    parts, r = [], x
    for _ in range(n):
        p = r.astype(BF16)
        parts.append(p)
        r = r - p.astype(F32)
    return parts


def _dot_hp(a, b):
    a_hi, a_lo = _split_bf16(a, 2)
    b_hi, b_lo = _split_bf16(b, 2)
    return _dot(a_hi, b_hi) + (_dot(a_hi, b_lo) + _dot(a_lo, b_hi))


def _sigmoid(x):
    return 1.0 / (1.0 + jnp.exp(-x))


def _silu(x):
    return x * _sigmoid(x)


def _rms_norm(x, g):
    ms = jnp.mean(x * x, axis=-1, keepdims=True)
    return x * lax.rsqrt(ms + RMS_EPS) * g


def _adaln_kernel(c_ref, w_ref, b_ref, o_ref):
    o_ref[0] = _dot_hp(_silu(c_ref[...]), w_ref[0]) + b_ref[0]


def adaln(c_all, w_ada, b_ada, *, col_block=1536):
    n_seq, d = c_all.shape
    n_layers, _, d6 = w_ada.shape
    return pl.pallas_call(
        _adaln_kernel,
        grid=(n_layers, d6 // col_block),
        in_specs=[
            pl.BlockSpec((n_seq, d), lambda l, j: (0, 0)),
            pl.BlockSpec((1, d, col_block), lambda l, j: (l, 0, j)),
            pl.BlockSpec((1, 1, col_block), lambda l, j: (l, 0, j)),
        ],
        out_specs=pl.BlockSpec((1, n_seq, col_block), lambda l, j: (l, 0, j)),
        out_shape=jax.ShapeDtypeStruct((n_layers, n_seq, d6), F32),
        compiler_params=_params("parallel", "parallel"),
        name="adaln",
    )(c_all, w_ada, b_ada.reshape(n_layers, 1, d6))


def _mod_spec(mod, k, d, tile, rows_per_seq):
    if rows_per_seq is None:
        return pl.BlockSpec((tile, d), lambda i: (0, k))
    tiles_per_seq = rows_per_seq // tile
    return pl.BlockSpec((1, 1, d), lambda i: (i // tiles_per_seq, 0, k))


def _norm_proj_kernel(x_ref, sh_ref, sc_ref, g_ref, w_ref, o_ref):
    d = x_ref.shape[-1]
    h = _rms_norm(x_ref[...], g_ref[...]) * (1.0 + sc_ref[...].reshape(-1, d)) + sh_ref[...].reshape(-1, d)
    o_ref[...] = _dot(h.astype(BF16), w_ref[...])


def norm_proj(x, mod, g, w_bf16, *, tile, rows_per_seq):
    n, d = x.shape
    p = w_bf16.shape[1]
    return pl.pallas_call(
        _norm_proj_kernel,
        grid=(n // tile,),
        in_specs=[
            pl.BlockSpec((tile, d), lambda i: (i, 0)),
            _mod_spec(mod, 0, d, tile, rows_per_seq),
            _mod_spec(mod, 1, d, tile, rows_per_seq),
            pl.BlockSpec((1, d), lambda i: (0, 0)),
            pl.BlockSpec((d, p), lambda i: (0, 0)),
        ],
        out_specs=pl.BlockSpec((tile, p), lambda i: (i, 0)),
        out_shape=jax.ShapeDtypeStruct((n, p), F32),
        compiler_params=_params("parallel"),
        name="hgrn_norm_proj",
    )(x, mod, mod, g.reshape(1, d), w_bf16)


def _cumsum_rows(x, tri):
    hi, mid, lo = _split_bf16(x, 3)
    return _dot(tri, hi) + (_dot(tri, mid) + _dot(tri, lo))


MAX_LOG_DECAY_RANGE = 80.0


def _hgrn_prep(proj, lb, n_valid):
    c = proj.shape[0]
    hk = proj.shape[1] // 4
    row = lax.broadcasted_iota(jnp.int32, (c, c), 0)
    col = lax.broadcasted_iota(jnp.int32, (c, c), 1)
    zf = proj[:, hk:2 * hk]
    e = jnp.exp(-jnp.abs(zf))
    r = 1.0 / (1.0 + e)
    pos = zf >= 0
    sig_p = jnp.where(pos, 1.0, e) * r
    sig_n = jnp.where(pos, e, 1.0) * r
    logf = jnp.log(lb + (1.0 - lb) * sig_p)
    k = (1.0 - lb) * sig_n
    if n_valid < c:
        live = lax.broadcasted_iota(jnp.int32, (c, 1), 0) < n_valid
        logf = jnp.where(live, logf, 0.0)
        k = jnp.where(live, k, 0.0)
    b = _cumsum_rows(logf, (row >= col).astype(BF16))
    return _silu(proj[:, :hk]), k, proj[:, 2 * hk:3 * hk], _silu(proj[:, 3 * hk:]), b


def _decay_range(b, c_sub):
    c = b.shape[0]
    worst = None
    for i in range(c // c_sub):
        span = b[i * c_sub:i * c_sub + 1, :] - b[(i + 1) * c_sub - 1:(i + 1) * c_sub, :]
        worst = span if worst is None else jnp.maximum(worst, span)
    return jnp.max(worst)


def _hgrn_finish(prep, gout, st_refs, seq, c_sub, exact_refs):
    q, k, v, gate, b = prep
    c = q.shape[0]
    n_heads = q.shape[1] // HEAD_DIM
    row = lax.broadcasted_iota(jnp.int32, (c, c), 0)
    col = lax.broadcasted_iota(jnp.int32, (c, c), 1)
    causal = row >= col

    if exact_refs is not None:
        q_ref, b_ref, oi_ref = exact_refs
        q_ref[...] = q
        b_ref[...] = b
        key_row = lax.broadcasted_iota(jnp.int32, (c, 1), 0)

        def row_group(g, carry):
            rows = pl.ds(pl.multiple_of(g * SUBLANES, SUBLANES), SUBLANES)
            for h in range(n_heads):
                hs = slice(h * HEAD_DIM, (h + 1) * HEAD_DIM)
                q_g, b_g = q_ref[rows, hs], b_ref[rows, hs]
                o_rows = []
                for r in range(SUBLANES):
                    decay = jnp.exp(jnp.minimum(b_g[r:r + 1] - b[:, hs], 0.0))
                    score = jnp.sum(decay * k[:, hs] * q_g[r:r + 1], axis=-1, keepdims=True)
                    score = jnp.where(key_row <= g * SUBLANES + r, score, 0.0)
                    o_rows.append(jnp.sum(score * v[:, hs], axis=0, keepdims=True))
                oi_ref[rows, hs] = jnp.concatenate(o_rows, axis=0)
            return carry
        lax.fori_loop(0, c // SUBLANES, row_group, 0)

    outs = []
    for h in range(n_heads):
        hs = slice(h * HEAD_DIM, (h + 1) * HEAD_DIM)
        bh, qh, kh, vh = b[:, hs], q[:, hs], k[:, hs], v[:, hs]
        vb = vh.astype(BF16)
        if exact_refs is not None:
            o = exact_refs[2][:, hs]
        else:
            n_sub = c // c_sub
            subs = [slice(i * c_sub, (i + 1) * c_sub) for i in range(n_sub)]
            refs = [bh[i * c_sub + c_sub // 2:i * c_sub + c_sub // 2 + 1, :] for i in range(n_sub)]
            k_own = [kh[rs] * jnp.exp(jnp.minimum(ref - bh[rs], MAX_LOG_DECAY_RANGE))
                     for rs, ref in zip(subs, refs)]
            a_rows = []
            for i in range(n_sub):
                q_hat = (qh[subs[i]] * jnp.exp(bh[subs[i]] - refs[i])).astype(BF16)
                parts = [k_own[j] * jnp.exp(refs[i] - refs[j]) for j in range(i)] + [k_own[i]]
                parts += [jnp.zeros((c_sub, HEAD_DIM), F32)] * (n_sub - 1 - i)
                k_hat = (jnp.concatenate(parts, axis=0) if n_sub > 1 else parts[0]).astype(BF16)
                a_rows.append(lax.dot_general(q_hat, k_hat, (((1,), (1,)), ((), ())),
                                              preferred_element_type=F32))
            att = jnp.where(causal, jnp.concatenate(a_rows, axis=0) if len(a_rows) > 1 else a_rows[0], 0.0)
            o = _dot(att.astype(BF16), vb)
        st = st_refs[seq, h]
        o = o + lax.dot_general((qh * jnp.exp(bh)).astype(BF16), st.astype(BF16),
                                (((1,), (1,)), ((), ())), preferred_element_type=F32)
        b_last = bh[c - 1:c, :]
        k_dec = (kh * jnp.exp(b_last - bh)).astype(BF16)
        st_refs[seq, h] = st * jnp.exp(b_last) + lax.dot_general(
            vb, k_dec, (((0,), (0,)), ((), ())), preferred_element_type=F32)
        o = o * lax.rsqrt(jnp.mean(o * o, axis=-1, keepdims=True) + RMS_EPS) * gout
        outs.append(o * gate[:, hs])
    return jnp.concatenate(outs, axis=-1)


def _lower_bound(lb_logits, layer):
    e = jnp.exp(lb_logits - jnp.max(lb_logits, axis=0, keepdims=True))
    return jnp.sum(e[:layer + 1], axis=0, keepdims=True) / jnp.sum(e, axis=0, keepdims=True)


def _hgrn_rec_kernel(*refs, chunk, c_sub, n_valid, has_state, layer):
    if has_state:
        proj_ref, lb_ref, gout_ref, s0_ref, o_ref, sout_ref, st_ref, *exact_refs = refs
    else:
        proj_ref, lb_ref, gout_ref, o_ref, sout_ref, st_ref, *exact_refs = refs
    bb, tb, _ = proj_ref.shape
    n_heads = st_ref.shape[1]
    j = pl.program_id(1)

    @pl.when(j == 0)
    def _():
        if has_state:
            for s in range(bb):
                for h in range(n_heads):
                    st_ref[s, h] = s0_ref[s, h].T
        else:
            st_ref[...] = jnp.zeros_like(st_ref)

    lb = _lower_bound(lb_ref[...], layer)
    gout = gout_ref[...]

    def chunk_body(ci, carry):
        rows = pl.ds(pl.multiple_of(ci * chunk, chunk), chunk)
        preps = [_hgrn_prep(proj_ref[s, rows, :], lb, n_valid) for s in range(bb)]
        span = _decay_range(preps[0][4], c_sub)
        for prep in preps[1:]:
            span = jnp.maximum(span, _decay_range(prep[4], c_sub))

        @pl.when(span <= MAX_LOG_DECAY_RANGE)
        def _():
            for s in range(bb):
                o_ref[s, rows, :] = _hgrn_finish(preps[s], gout, st_ref, s, c_sub, None)

        @pl.when(jnp.logical_not(span <= MAX_LOG_DECAY_RANGE))
        def _():
            for s in range(bb):
                o_ref[s, rows, :] = _hgrn_finish(preps[s], gout, st_ref, s, c_sub, exact_refs)
        return carry
    lax.fori_loop(0, tb // chunk, chunk_body, 0)

    @pl.when(j == pl.num_programs(1) - 1)
    def _():
        for s in range(bb):
            for h in range(n_heads):
                sout_ref[s, h] = st_ref[s, h].T


def hgrn_recurrence(proj, lb_logits, g_out, s0, *, layer, seq_block, time_block, chunk, c_sub, n_valid):
    bsz, t, p = proj.shape
    hk = p // 4
    n_heads = hk // HEAD_DIM
    has_state = s0 is not None
    st_shape = (seq_block, n_heads, HEAD_DIM, HEAD_DIM)
    st_spec = pl.BlockSpec(st_shape, lambda i, j: (i, 0, 0, 0))
    in_specs = [
        pl.BlockSpec((seq_block, time_block, p), lambda i, j: (i, j, 0)),
        pl.BlockSpec(lb_logits.shape, lambda i, j: (0, 0)),
        pl.BlockSpec((1, HEAD_DIM), lambda i, j: (0, 0)),
    ]
    args = [proj, lb_logits, g_out.reshape(1, HEAD_DIM)]
    if has_state:
        in_specs.append(st_spec)
        args.append(s0)
    return pl.pallas_call(
        functools.partial(_hgrn_rec_kernel, chunk=chunk, c_sub=c_sub, n_valid=n_valid,
                          has_state=has_state, layer=layer),
        grid=(bsz // seq_block, t // time_block),
        in_specs=in_specs,
        out_specs=[pl.BlockSpec((seq_block, time_block, hk), lambda i, j: (i, j, 0)), st_spec],
        out_shape=[jax.ShapeDtypeStruct((bsz, t, hk), F32),
                   jax.ShapeDtypeStruct((bsz, n_heads, HEAD_DIM, HEAD_DIM), F32)],
        scratch_shapes=[pltpu.VMEM(st_shape, F32)] + [pltpu.VMEM((chunk, hk), F32)] * 3,
        compiler_params=_params("parallel", "arbitrary"),
        name="hgrn_recurrence",
    )(*args)


def _sorted_rows(tile, top_k, n_experts):
    return tile * top_k + n_experts * BF16_ROWS


def _resid_router_kernel(*refs, top_k, n_experts, has_w_out, chained, row_chunk):
    refs = list(refs)
    x_ref, y_ref = refs[:2]
    del refs[:2]
    wo_ref = refs.pop(0) if has_w_out else None
    gt_ref, sh_ref, sc_ref, g_ref, wr_ref, br_ref = refs[:6]
    del refs[:6]
    if chained:
        refs.pop(0)
    x1_ref, xs_ref, pos_ref, gate_ref, cnt_ref = refs
    tile, d = x_ref.shape
    n_sorted = xs_ref.shape[0]

    y = y_ref[...]
    if has_w_out:
        y = _dot(y.astype(BF16), wo_ref[...])
    x1 = x_ref[...] + gt_ref[...].reshape(-1, d) * y
    x1_ref[...] = x1
    h = _rms_norm(x1, g_ref[...]) * (1.0 + sc_ref[...].reshape(-1, d)) + sh_ref[...].reshape(-1, d)

    lane = lax.broadcasted_iota(jnp.int32, (tile, LANES), 1).astype(F32)
    logits = jnp.where(lane < n_experts, _dot_hp(h, wr_ref[...]) + br_ref[...], -jnp.inf)
    picks, vals = [], []
    for _ in range(top_k):
        m = jnp.max(logits, axis=-1, keepdims=True)
        pick = jnp.min(jnp.where(logits == m, lane, float(LANES)), axis=-1, keepdims=True)
        picks.append(pick)
        vals.append(m)
        logits = jnp.where(lane == pick, -jnp.inf, logits)
    exps = [jnp.exp(v - vals[0]) for v in vals]
    denom = exps[0]
    for e in exps[1:]:
        denom = denom + e

    onehots = [(lane == p).astype(F32) for p in picks]
    oh_sum = onehots[0]
    for oh in onehots[1:]:
        oh_sum = oh_sum + oh
    row = lax.broadcasted_iota(jnp.int32, (tile, tile), 0)
    col = lax.broadcasted_iota(jnp.int32, (tile, tile), 1)
    before = _dot((row > col).astype(BF16), oh_sum.astype(BF16))
    count = jnp.sum(oh_sum, axis=0, keepdims=True)
    cnt_pad = jnp.floor((count + (BF16_ROWS - 1.0)) * (1.0 / BF16_ROWS)) * BF16_ROWS
    lane8 = lax.broadcasted_iota(jnp.int32, (SUBLANES, LANES), 1)
    run = jnp.broadcast_to(cnt_pad, (SUBLANES, LANES))
    shift = 1
    while shift < n_experts:
        run = run + jnp.where(lane8 >= shift, pltpu.roll(run, shift, 1), 0.0)
        shift *= 2
    pos = before + (run[0:1] - cnt_pad)
    pos_out = jnp.zeros((tile, LANES), F32)
    gate_out = jnp.zeros((tile, LANES), F32)
    for k in range(top_k):
        pos_k = jnp.sum(onehots[k] * pos, axis=-1, keepdims=True)
        pos_out = jnp.where(lane == k, pos_k, pos_out)
        gate_out = jnp.where(lane == k, exps[k] / denom, gate_out)
    pos_ref[...] = pos_out.astype(jnp.int32)
    gate_ref[...] = gate_out
    cnt_ref[0] = cnt_pad.astype(jnp.int32)

    pos_t = pos_out.T
    hb = h.astype(BF16)
    for r0 in range(0, n_sorted, row_chunk):
        slot = (lax.broadcasted_iota(jnp.int32, (row_chunk, tile), 0) + r0).astype(F32)
        sel = jnp.where(slot == pos_t[0:1], 1.0, 0.0)
        for k in range(1, top_k):
            sel = sel + jnp.where(slot == pos_t[k:k + 1], 1.0, 0.0)
        xs_ref[r0:r0 + row_chunk, :] = _dot(sel.astype(BF16), hb).astype(BF16)


def resid_router(x, y, w_out_bf16, mod, g, w_r, b_r, sorted_in, *, tile, rows_per_seq, top_k,
                 block_offset, n_blocks_total):
    n, d = x.shape
    n_experts = w_r.shape[1]
    n_sorted = _sorted_rows(tile, top_k, n_experts)
    w_r_pad = jnp.pad(w_r, ((0, 0), (0, LANES - n_experts)))
    b_r_pad = jnp.pad(b_r, (0, LANES - n_experts)).reshape(1, LANES)
    has_w_out = w_out_bf16 is not None
    chained = sorted_in is not None
    row_spec = pl.BlockSpec((tile, d), lambda i: (i, 0))
    lane_spec = pl.BlockSpec((tile, LANES), lambda i: (i, 0))
    full = lambda a: pl.BlockSpec(a.shape, lambda i: (0,) * a.ndim)
    in_specs = [row_spec, pl.BlockSpec((tile, y.shape[1]), lambda i: (i, 0))]
    args = [x, y]
    if has_w_out:
        in_specs.append(full(w_out_bf16))
        args.append(w_out_bf16)
    in_specs += [_mod_spec(mod, 2, d, tile, rows_per_seq), _mod_spec(mod, 3, d, tile, rows_per_seq),
                 _mod_spec(mod, 4, d, tile, rows_per_seq), pl.BlockSpec((1, d), lambda i: (0, 0)),
                 full(w_r_pad), full(b_r_pad)]
    args += [mod, mod, mod, g.reshape(1, d), w_r_pad, b_r_pad]
    aliases = {}
    if chained:
        aliases = {len(args): 1}
        in_specs.append(pl.BlockSpec(memory_space=pl.ANY))
        args.append(sorted_in)
    n_tiles = n // tile
    return pl.pallas_call(
        functools.partial(_resid_router_kernel, top_k=top_k, n_experts=n_experts, has_w_out=has_w_out,
                          chained=chained, row_chunk=256),
        grid=(n_tiles,),
        in_specs=in_specs,
        out_specs=[row_spec,
                   pl.BlockSpec((n_sorted, d), lambda i: (i + block_offset, 0)),
                   lane_spec, lane_spec,
                   pl.BlockSpec((1, 1, LANES), lambda i: (i, 0, 0))],
        out_shape=[jax.ShapeDtypeStruct((n, d), F32),
                   jax.ShapeDtypeStruct((n_blocks_total * n_sorted, d), BF16),
                   jax.ShapeDtypeStruct((n, LANES), jnp.int32), jax.ShapeDtypeStruct((n, LANES), F32),
                   jax.ShapeDtypeStruct((n_tiles, 1, LANES), jnp.int32)],
        input_output_aliases=aliases,
        compiler_params=_params("parallel"),
        name="resid_router",
    )(*args)


def _experts_kernel(te_ref, first_ref, rows_ref, base_ref, slo_ref, shi_ref, wslot_ref, nexte_ref, used_ref,
                    sstart_ref, slen_ref, ssrc_ref,
                    xs_hbm, wgu_hbm, bgu_ref, wdn_hbm, bdn_ref, ys_hbm,
                    xbuf, ybuf, wgu_f, wdn_f, wgu_b, wdn_b, in_sem, out_sem, w_sem,
                    *, layer, limit, alpha, col_chunk):
    del ys_hbm
    i = pl.program_id(0)
    used = used_ref[0]
    tm = xbuf.shape[1]

    def copy(src_rows, dst_rows, slot, inbound):
        if inbound:
            return pltpu.make_async_copy(xs_hbm.at[src_rows, :], xbuf.at[slot, dst_rows, :], in_sem.at[slot])
        return pltpu.make_async_copy(ybuf.at[slot, dst_rows, :], xs_hbm.at[src_rows, :], out_sem.at[slot])

    def piece_copies(tile_idx, slot, inbound, wait):
        if wait:
            rows = pl.ds(0, pl.multiple_of(rows_ref[tile_idx], BF16_ROWS))
            copy(rows, rows, slot, inbound).wait()
            return
        base = base_ref[tile_idx]

        def piece(s, c):
            first = sstart_ref[s] - base
            lo = jnp.maximum(first, 0)
            n_rows = pl.multiple_of(jnp.minimum(first + slen_ref[s], tm) - lo, BF16_ROWS)

            @pl.when(n_rows > 0)
            def _():
                src = pl.multiple_of(ssrc_ref[s] + (lo - first), BF16_ROWS)
                copy(pl.ds(src, n_rows), pl.ds(pl.multiple_of(lo, BF16_ROWS), n_rows), slot, inbound).start()
            return c
        lax.fori_loop(slo_ref[tile_idx], shi_ref[tile_idx], piece, 0)

    def weight_copies(e, slot):
        return (pltpu.make_async_copy(wgu_hbm.at[layer, e], wgu_f.at[slot], w_sem.at[slot]),
                pltpu.make_async_copy(wdn_hbm.at[layer, e], wdn_f.at[slot], w_sem.at[slot]))

    @pl.when(i == 0)
    def _():
        xbuf[...] = jnp.zeros_like(xbuf)
        for cp in weight_copies(te_ref[0], 0):
            cp.start()
        piece_copies(0, 0, True, False)

    @pl.when(i < used)
    def _():
        slot = i % 2

        @pl.when(i + 1 < used)
        def _():
            piece_copies(i + 1, 1 - slot, True, False)

        @pl.when(first_ref[i] == 1)
        def _():
            ws = wslot_ref[i]
            for cp in weight_copies(te_ref[i], ws):
                cp.wait()

            @pl.when(nexte_ref[i] >= 0)
            def _():
                for cp in weight_copies(nexte_ref[i], 1 - ws):
                    cp.start()
            wgu_b[...] = wgu_f[ws].astype(BF16)
            wdn_b[...] = wdn_f[ws].astype(BF16)

        piece_copies(i, slot, True, True)

        @pl.when(i >= 2)
        def _():
            piece_copies(i - 2, slot, False, True)

        d_ff = wdn_b.shape[0]
        tm = xbuf.shape[1]
        e = te_ref[i]
        b_gu = bgu_ref[pl.ds(e, 1), :]
        b_dn = bdn_ref[pl.ds(e, 1), :]

        def mlp(n_rows):
            x = xbuf[slot, :n_rows, :]
            y = None
            for c0 in range(0, d_ff, col_chunk):
                cs = slice(c0, c0 + col_chunk)
                us = slice(d_ff + c0, d_ff + c0 + col_chunk)
                gate = jnp.minimum(_dot(x, wgu_b[:, cs]) + b_gu[:, cs], limit)
                up = jnp.clip(_dot(x, wgu_b[:, us]) + b_gu[:, us], -limit, limit)
                act = ((up + 1.0) * (gate * _sigmoid(alpha * gate))).astype(BF16)
                part = _dot(act, wdn_b[cs, :])
                y = part if y is None else y + part
            ybuf[slot, :n_rows, :] = (y + b_dn).astype(BF16)

        @pl.when(rows_ref[i] > tm // 2)
        def _():
            mlp(tm)

        @pl.when(rows_ref[i] <= tm // 2)
        def _():
            mlp(tm // 2)
        piece_copies(i, slot, False, False)

    @pl.when(i == pl.num_programs(0) - 1)
    def _():
        @pl.when(used >= 2)
        def _():
            piece_copies(used - 2, used % 2, False, True)
        piece_copies(used - 1, (used - 1) % 2, False, True)


def moe_experts(sorted_rows, w_gu, b_gu, w_dn, b_dn, tables, *, layer, row_tile, limit, alpha, col_chunk=512):
    n_rows, d = sorted_rows.shape
    d_gu = w_gu.shape[-1]
    d_ff = w_dn.shape[-2]
    n_tiles = tables[0].shape[0]
    vmem = lambda a: pl.BlockSpec(a.shape, lambda i, *_: (0,) * a.ndim)
    any_spec = pl.BlockSpec(memory_space=pl.ANY)
    return pl.pallas_call(
        functools.partial(_experts_kernel, layer=layer, limit=limit, alpha=alpha,
                          col_chunk=min(col_chunk, d_ff)),
        grid_spec=pltpu.PrefetchScalarGridSpec(
            num_scalar_prefetch=len(tables),
            grid=(n_tiles,),
            in_specs=[any_spec, any_spec, vmem(b_gu), any_spec, vmem(b_dn)],
            out_specs=any_spec,
            scratch_shapes=[
                pltpu.VMEM((2, row_tile, d), BF16), pltpu.VMEM((2, row_tile, d), BF16),
                pltpu.VMEM((2, d, d_gu), F32), pltpu.VMEM((2, d_ff, d), F32),
                pltpu.VMEM((d, d_gu), BF16), pltpu.VMEM((d_ff, d), BF16),
                pltpu.SemaphoreType.DMA((2,)), pltpu.SemaphoreType.DMA((2,)), pltpu.SemaphoreType.DMA((2,)),
            ],
        ),
        out_shape=jax.ShapeDtypeStruct((n_rows, d), BF16),
        input_output_aliases={len(tables): 0},
        compiler_params=_params("arbitrary"),
        name="moe_experts",
    )(*tables, sorted_rows, w_gu, b_gu, w_dn, b_dn)


def _expert_tables(cnt, n_sorted, n_tiles, row_tile):
    n_blocks, n_experts = cnt.shape
    local_off = jnp.cumsum(cnt, axis=1) - cnt
    seg_end = jnp.cumsum(cnt.T, axis=1)
    seg_start = seg_end - cnt.T
    seg_src = jnp.arange(n_blocks, dtype=jnp.int32)[None, :] * n_sorted + local_off.T
    total = seg_end[:, -1]
    padded = (total + row_tile - 1) // row_tile * row_tile
    pad_end = jnp.cumsum(padded)
    pad_start = pad_end - padded
    tiles = jnp.arange(n_tiles, dtype=jnp.int32)
    n_used = pad_end[-1] // row_tile
    tile_expert = jnp.minimum(jnp.sum(tiles[:, None] * row_tile >= pad_end[None, :], axis=1), n_experts - 1)
    live = tiles < n_used
    tile_base = tiles * row_tile - pad_start[tile_expert]
    tile_rows = jnp.where(live, jnp.clip(total[tile_expert] - tile_base, 0, row_tile), 0)
    tile_first = (live & (tile_base == 0)).astype(jnp.int32)
    seg_lo = jnp.sum(seg_end[tile_expert] <= tile_base[:, None], axis=1)
    seg_hi = jnp.sum(seg_start[tile_expert] < tile_base[:, None] + row_tile, axis=1)
    seg_lo = tile_expert * n_blocks + jnp.minimum(seg_lo, seg_hi)
    seg_hi = tile_expert * n_blocks + seg_hi
    owns = total > 0
    order = jnp.cumsum(owns.astype(jnp.int32)) - 1
    experts = jnp.arange(n_experts, dtype=jnp.int32)
    later = (experts[None, :] > experts[:, None]) & owns[None, :]
    next_expert = jnp.min(jnp.where(later, experts[None, :], n_experts), axis=1)
    next_expert = jnp.where(next_expert < n_experts, next_expert, -1)
    i32 = lambda a: a.astype(jnp.int32)
    return (i32(tile_expert), tile_first, i32(tile_rows), i32(tile_base), i32(seg_lo), i32(seg_hi),
            i32(order[tile_expert] % 2), i32(next_expert[tile_expert]), i32(n_used).reshape(1),
            i32(seg_start.reshape(-1)), i32(cnt.T.reshape(-1)), i32(seg_src.reshape(-1)))


def _combine_kernel(x_ref, pos_ref, gate_ref, gt_ref, gfin_ref, ys_ref, o_ref, *, top_k, final_norm, k_chunk):
    tile, d = x_ref.shape
    n_sorted = ys_ref.shape[0]
    pos = pos_ref[...].astype(F32)
    gates = gate_ref[...]
    acc = None
    for r0 in range(0, n_sorted, k_chunk):
        slot = (lax.broadcasted_iota(jnp.int32, (tile, k_chunk), 1) + r0).astype(F32)
        w = jnp.where(slot == pos[:, 0:1], gates[:, 0:1], 0.0)
        for k in range(1, top_k):
            w = w + jnp.where(slot == pos[:, k:k + 1], gates[:, k:k + 1], 0.0)
        part = _dot(w.astype(BF16), ys_ref[r0:r0 + k_chunk, :])
        acc = part if acc is None else acc + part
    out = x_ref[...] + gt_ref[...].reshape(-1, d) * acc
    if final_norm:
        out = _rms_norm(out, gfin_ref[...])
    o_ref[...] = out


def moe_combine(x, ys, pos, gates, mod, g_final, *, tile, rows_per_seq, final_norm, block_offset, n_sorted):
    n, d = x.shape
    return pl.pallas_call(
        functools.partial(_combine_kernel, top_k=TOP_K, final_norm=final_norm, k_chunk=512),
        grid=(n // tile,),
        in_specs=[
            pl.BlockSpec((tile, d), lambda i: (i, 0)),
            pl.BlockSpec((tile, LANES), lambda i: (i, 0)),
            pl.BlockSpec((tile, LANES), lambda i: (i, 0)),
            _mod_spec(mod, 5, d, tile, rows_per_seq),
            pl.BlockSpec((1, d), lambda i: (0, 0)),
            pl.BlockSpec((n_sorted, d), lambda i: (i + block_offset, 0)),
        ],
        out_specs=pl.BlockSpec((tile, d), lambda i: (i, 0)),
        out_shape=jax.ShapeDtypeStruct((n, d), F32),
        compiler_params=_params("parallel"),
        name="moe_combine",
    )(x, pos, gates, mod, g_final.reshape(1, d), ys)


def _pool_groups(h, window_sum, counts, w_ref, scale):
    n_groups = w_ref.shape[0]
    dg = h.shape[-1] // n_groups
    outs = []
    for gi in range(n_groups):
        cols = slice(gi * dg, (gi + 1) * dg)
        pooled = window_sum(gi, cols) / counts[gi] - h[:, cols]
        outs.append(_dot(pooled.astype(BF16), w_ref[gi]))
    return jnp.concatenate(outs, axis=-1) * scale


def _pool_prompt_kernel(x_ref, sh_ref, sc_ref, g_ref, w_ref, scale_ref, y_ref, cache_ref, ext_ref,
                        *, windows, halo):
    tile, d = x_ref.shape[1:]
    j = pl.program_id(1)

    @pl.when(j == 0)
    def _():
        ext_ref[0:halo, :] = jnp.zeros((halo, d), F32)

    h = _rms_norm(x_ref[0], g_ref[...]) * (1.0 + sc_ref[...].reshape(-1, d)) + sh_ref[...].reshape(-1, d)
    ext_ref[halo:halo + tile, :] = h
    pos = j * tile + lax.broadcasted_iota(jnp.int32, (tile, 1), 0)

    def window_sum(gi, cols):
        acc = h[:, cols]
        for s in range(1, windows[gi]):
            acc = acc + ext_ref[halo - s:halo - s + tile, cols]
        return acc
    counts = [jnp.minimum(pos + 1, w).astype(F32) for w in windows]
    y_ref[0] = _pool_groups(h, window_sum, counts, w_ref, scale_ref[...])

    n_keep = cache_ref.shape[1]
    @pl.when(j == pl.num_programs(1) - 1)
    def _():
        cache_ref[0] = ext_ref[halo + tile - n_keep:halo + tile, :]
    ext_ref[0:halo, :] = ext_ref[tile:tile + halo, :]


def pool_mixer_prompt(x, mod, g, w_grp_bf16, scale, *, tile, windows, n_keep):
    bsz, t, d = x.shape
    halo = 16
    assert max(windows) <= halo <= tile and n_keep <= tile
    mod_spec = lambda k: pl.BlockSpec((1, 1, d), lambda b, j: (b, 0, k))
    return pl.pallas_call(
        functools.partial(_pool_prompt_kernel, windows=windows, halo=halo),
        grid=(bsz, t // tile),
        in_specs=[
            pl.BlockSpec((1, tile, d), lambda b, j: (b, j, 0)),
            mod_spec(0), mod_spec(1),
            pl.BlockSpec((1, d), lambda b, j: (0, 0)),
            pl.BlockSpec(w_grp_bf16.shape, lambda b, j: (0, 0, 0)),
            pl.BlockSpec((1, d), lambda b, j: (0, 0)),
        ],
        out_specs=[pl.BlockSpec((1, tile, d), lambda b, j: (b, j, 0)),
                   pl.BlockSpec((1, n_keep, d), lambda b, j: (b, 0, 0))],
        out_shape=[jax.ShapeDtypeStruct((bsz, t, d), F32), jax.ShapeDtypeStruct((bsz, n_keep, d), F32)],
        scratch_shapes=[pltpu.VMEM((halo + tile, d), F32)],
        compiler_params=_params("parallel", "arbitrary"),
        name="pool_mixer_prompt",
    )(x, mod, mod, g.reshape(1, d), w_grp_bf16, scale.reshape(1, d))


def _pool_sample_kernel(x_ref, buf_ref, sh_ref, sc_ref, g_ref, w_ref, scale_ref, y_ref, cache_ref,
                        *, windows, start_pos):
    t_len = x_ref.shape[0]
    n_prev = buf_ref.shape[0]
    hs = [_rms_norm(x_ref[t], g_ref[...]) * (1.0 + sc_ref[...]) + sh_ref[...] for t in range(t_len)]

    def ext(r):
        return buf_ref[r] if r < n_prev else hs[r - n_prev]

    for t in range(t_len):
        def window_sum(gi, cols):
            acc = hs[t][:, cols]
            for s in range(1, windows[gi]):
                acc = acc + ext(n_prev + t - s)[:, cols]
            return acc
        counts = [float(min(start_pos + t + 1, w)) for w in windows]
        y_ref[t] = _pool_groups(hs[t], window_sum, counts, w_ref, scale_ref[...])
    for r in range(n_prev):
        cache_ref[r] = ext(t_len + r)


def pool_mixer_sample(x_t, buf_t, mod, g, w_grp_bf16, scale, *, seq_block, windows, start_pos):
    t_len, n_seq, d = x_t.shape
    n_prev = buf_t.shape[0]
    assert start_pos >= n_prev >= max(windows) - 1
    mod_spec = lambda k: pl.BlockSpec((seq_block, d), lambda i: (i, k))
    return pl.pallas_call(
        functools.partial(_pool_sample_kernel, windows=windows, start_pos=start_pos),
        grid=(n_seq // seq_block,),
        in_specs=[
            pl.BlockSpec((t_len, seq_block, d), lambda i: (0, i, 0)),
            pl.BlockSpec((n_prev, seq_block, d), lambda i: (0, i, 0)),
            mod_spec(0), mod_spec(1),
            pl.BlockSpec((1, d), lambda i: (0, 0)),
            pl.BlockSpec(w_grp_bf16.shape, lambda i: (0, 0, 0)),
            pl.BlockSpec((1, d), lambda i: (0, 0)),
        ],
        out_specs=[pl.BlockSpec((t_len, seq_block, d), lambda i: (0, i, 0)),
                   pl.BlockSpec((n_prev, seq_block, d), lambda i: (0, i, 0))],
        out_shape=[jax.ShapeDtypeStruct((t_len, n_seq, d), F32), jax.ShapeDtypeStruct((n_prev, n_seq, d), F32)],
        compiler_params=_params("parallel"),
        name="pool_mixer_sample",
    )(x_t, buf_t, mod, mod, g.reshape(1, d), w_grp_bf16, scale.reshape(1, d))


TOP_K = 4
SWIGLU_LIMIT = 7.0
SWIGLU_ALPHA = 1.702
POOL_WINDOWS = (2, 4, 8, 16)
PAST_LEN = 16384
PROJ_TILE = 256
MOE_TOKEN_TILE = 512
MOE_ROW_TILE = 512
HGRN_CHUNK = 128
HGRN_SUB = 32
SAMPLE_T_PAD = 8
SAMPLE_SEQ_BLOCK = 8


def kernel(x_prompt, x_sample, c_prompt, c_sample, state_hgrn, cache_pool, g_norm_mix, g_norm_ffn, w_ada, b_ada, w_in_hgrn, lb_logits, g_out_hgrn, w_out_hgrn, w_grp_pool, scale_pool, w_router, b_router, w_gate_up, b_gate_up, w_down, b_down, g_final):
    bp, tp, d = x_prompt.shape
    bs, ts, _ = x_sample.shape
    n_p, n_s = bp * tp, bs * ts
    n_experts = w_router.shape[-1]
    hk = w_out_hgrn.shape[1]
    assert n_s == MOE_TOKEN_TILE and n_p % MOE_TOKEN_TILE == 0
    blocks_p = n_p // MOE_TOKEN_TILE
    n_blocks = blocks_p + 1
    n_sorted = _sorted_rows(MOE_TOKEN_TILE, TOP_K, n_experts)
    n_row_tiles = -(-(n_blocks * n_sorted + n_experts * (MOE_ROW_TILE - BF16_ROWS)) // MOE_ROW_TILE)

    mod = adaln(jnp.concatenate([c_prompt, c_sample], axis=0), w_ada, b_ada)
    mod_p = [mod[l, :bp][:, None, :] for l in range(mod.shape[0])]
    mod_s = [mod[l, bp:] for l in range(mod.shape[0])]
    mod_st = [jnp.tile(m, (ts, 1)) for m in mod_s]

    xp = x_prompt.reshape(n_p, d)
    xs = x_sample.transpose(1, 0, 2).reshape(n_s, d)

    def moe(layer, x_p, y_p, x_s, y_s, w_out, final_norm):
        route = functools.partial(resid_router, g=g_norm_ffn[layer], w_r=w_router[layer], b_r=b_router[layer],
                                  tile=MOE_TOKEN_TILE, top_k=TOP_K, n_blocks_total=n_blocks)
        x1_p, sorted_rows, pos_p, gate_p, cnt_p = route(
            x_p, y_p, w_out, mod_p[layer], sorted_in=None, rows_per_seq=tp, block_offset=0)
        x1_s, sorted_rows, pos_s, gate_s, cnt_s = route(
            x_s, y_s, w_out, mod_st[layer], sorted_in=sorted_rows, rows_per_seq=None, block_offset=blocks_p)
        cnt8 = jnp.concatenate([cnt_p, cnt_s], axis=0)[:, 0, :n_experts]
        tables = _expert_tables(cnt8, n_sorted, n_row_tiles, MOE_ROW_TILE)
        ys = moe_experts(sorted_rows, w_gate_up, b_gate_up[layer], w_down, b_down[layer], tables,
                         layer=layer, row_tile=MOE_ROW_TILE, limit=SWIGLU_LIMIT, alpha=SWIGLU_ALPHA)
        combine = functools.partial(moe_combine, ys=ys, g_final=g_final, tile=MOE_TOKEN_TILE,
                                    final_norm=final_norm, n_sorted=n_sorted)
        out_p = combine(x1_p, pos=pos_p, gates=gate_p, mod=mod_p[layer], rows_per_seq=tp, block_offset=0)
        out_s = combine(x1_s, pos=pos_s, gates=gate_s, mod=mod_st[layer], rows_per_seq=None,
                        block_offset=blocks_p)
        return out_p, out_s

    w_in = w_in_hgrn[0].astype(BF16)
    proj_p = norm_proj(xp, mod_p[0], g_norm_mix[0], w_in, tile=PROJ_TILE, rows_per_seq=tp)
    proj_s = norm_proj(xs, mod_s[0], g_norm_mix[0], w_in, tile=bs, rows_per_seq=None)
    o_p, state_p = hgrn_recurrence(proj_p.reshape(bp, tp, 4 * hk), lb_logits, g_out_hgrn[0], None,
                                   layer=0, seq_block=1, time_block=PROJ_TILE, chunk=HGRN_CHUNK,
                                   c_sub=HGRN_SUB, n_valid=HGRN_CHUNK)
    proj_sb = jnp.pad(proj_s.reshape(ts, bs, 4 * hk).transpose(1, 0, 2), ((0, 0), (0, SAMPLE_T_PAD - ts), (0, 0)))
    o_s, state_s = hgrn_recurrence(proj_sb, lb_logits, g_out_hgrn[0], state_hgrn[0],
                                   layer=0, seq_block=SAMPLE_SEQ_BLOCK, time_block=SAMPLE_T_PAD,
                                   chunk=SAMPLE_T_PAD, c_sub=SAMPLE_T_PAD, n_valid=ts)
    o_s = o_s[:, :ts].transpose(1, 0, 2).reshape(n_s, hk)
    x_p, x_s = moe(0, xp, o_p.reshape(n_p, hk), xs, o_s, w_out_hgrn[0].astype(BF16), False)

    w_grp = w_grp_pool[0].astype(BF16)
    n_keep = cache_pool.shape[2]
    y_p, cache_p = pool_mixer_prompt(x_p.reshape(bp, tp, d), mod_p[1], g_norm_mix[1], w_grp, scale_pool[0],
                                     tile=PROJ_TILE, windows=POOL_WINDOWS, n_keep=n_keep)
    y_s, cache_s = pool_mixer_sample(x_s.reshape(ts, bs, d), cache_pool[0].transpose(1, 0, 2), mod_s[1],
                                     g_norm_mix[1], w_grp, scale_pool[0],
                                     seq_block=32, windows=POOL_WINDOWS, start_pos=PAST_LEN)
    x_p, x_s = moe(1, x_p, y_p.reshape(n_p, d), x_s, y_s.reshape(n_s, d), None, True)

    return (x_p.reshape(bp, tp, d), x_s.reshape(ts, bs, d).transpose(1, 0, 2),
            state_p[None], state_s[None], cache_p[None], cache_s.transpose(1, 0, 2)[None])
```

```python
import functools

import jax
import jax.numpy as jnp
from jax import lax
from jax.experimental import pallas as pl
from jax.experimental.pallas import tpu as pltpu

F32 = jnp.float32
BF16 = jnp.bfloat16

RMS_EPS = 1e-6
LANES = 128
SUBLANES = 8
BF16_ROWS = 16
HEAD_DIM = 128
VMEM_LIMIT = 56 * 1024 * 1024

_dot = functools.partial(jnp.dot, preferred_element_type=F32)


def _params(*semantics):
    return pltpu.CompilerParams(dimension_semantics=semantics, vmem_limit_bytes=VMEM_LIMIT)


def _split_bf16(x, n):
    parts, r = [], x
    for _ in range(n):
        p = r.astype(BF16)
        parts.append(p)
        r = r - p.astype(F32)
    return parts


def _dot_hp(a, b):
    a_hi, a_lo = _split_bf16(a, 2)
    b_hi, b_lo = _split_bf16(b, 2)
    return _dot(a_hi, b_hi) + (_dot(a_hi, b_lo) + _dot(a_lo, b_hi))


def _sigmoid(x):
    return 1.0 / (1.0 + jnp.exp(-x))


def _silu(x):
    return x * _sigmoid(x)


def _rms_norm(x, g):
    ms = jnp.mean(x * x, axis=-1, keepdims=True)
    return x * lax.rsqrt(ms + RMS_EPS) * g


def _adaln_kernel(c_ref, w_ref, b_ref, o_ref):
    o_ref[0] = _dot_hp(_silu(c_ref[...]), w_ref[0]) + b_ref[0]


def adaln(c_all, w_ada, b_ada, *, col_block=1536):
    n_seq, d = c_all.shape
    n_layers, _, d6 = w_ada.shape
    return pl.pallas_call(
        _adaln_kernel,
        grid=(n_layers, d6 // col_block),
        in_specs=[
            pl.BlockSpec((n_seq, d), lambda l, j: (0, 0)),
            pl.BlockSpec((1, d, col_block), lambda l, j: (l, 0, j)),
            pl.BlockSpec((1, 1, col_block), lambda l, j: (l, 0, j)),
        ],
        out_specs=pl.BlockSpec((1, n_seq, col_block), lambda l, j: (l, 0, j)),
        out_shape=jax.ShapeDtypeStruct((n_layers, n_seq, d6), F32),
        compiler_params=_params("parallel", "parallel"),
        name="adaln",
    )(c_all, w_ada, b_ada.reshape(n_layers, 1, d6))


def _mod_spec(mod, k, d, tile, rows_per_seq):
    if rows_per_seq is None:
        return pl.BlockSpec((tile, d), lambda i: (0, k))
    tiles_per_seq = rows_per_seq // tile
    return pl.BlockSpec((1, 1, d), lambda i: (i // tiles_per_seq, 0, k))


def _norm_proj_kernel(x_ref, sh_ref, sc_ref, g_ref, w_ref, o_ref):
    d = x_ref.shape[-1]
    h = _rms_norm(x_ref[...], g_ref[...]) * (1.0 + sc_ref[...].reshape(-1, d)) + sh_ref[...].reshape(-1, d)
    o_ref[...] = _dot(h.astype(BF16), w_ref[...])


def norm_proj(x, mod, g, w_bf16, *, tile, rows_per_seq):
    n, d = x.shape
    p = w_bf16.shape[1]
    return pl.pallas_call(
        _norm_proj_kernel,
        grid=(n // tile,),
        in_specs=[
            pl.BlockSpec((tile, d), lambda i: (i, 0)),
            _mod_spec(mod, 0, d, tile, rows_per_seq),
            _mod_spec(mod, 1, d, tile, rows_per_seq),
            pl.BlockSpec((1, d), lambda i: (0, 0)),
            pl.BlockSpec((d, p), lambda i: (0, 0)),
        ],
        out_specs=pl.BlockSpec((tile, p), lambda i: (i, 0)),
        out_shape=jax.ShapeDtypeStruct((n, p), F32),
        compiler_params=_params("parallel"),
        name="hgrn_norm_proj",
    )(x, mod, mod, g.reshape(1, d), w_bf16)


def _cumsum_rows(x, tri):
    hi, mid, lo = _split_bf16(x, 3)
    return _dot(tri, hi) + (_dot(tri, mid) + _dot(tri, lo))


MAX_LOG_DECAY_RANGE = 80.0


def _hgrn_prep(proj, lb, n_valid):
    c = proj.shape[0]
    hk = proj.shape[1] // 4
    row = lax.broadcasted_iota(jnp.int32, (c, c), 0)
    col = lax.broadcasted_iota(jnp.int32, (c, c), 1)
    zf = proj[:, hk:2 * hk]
    e = jnp.exp(-jnp.abs(zf))
    r = 1.0 / (1.0 + e)
    pos = zf >= 0
    sig_p = jnp.where(pos, 1.0, e) * r
    sig_n = jnp.where(pos, e, 1.0) * r
    logf = jnp.log(lb + (1.0 - lb) * sig_p)
    k = (1.0 - lb) * sig_n
    if n_valid < c:
        live = lax.broadcasted_iota(jnp.int32, (c, 1), 0) < n_valid
        logf = jnp.where(live, logf, 0.0)
        k = jnp.where(live, k, 0.0)
    b = _cumsum_rows(logf, (row >= col).astype(BF16))
    return _silu(proj[:, :hk]), k, proj[:, 2 * hk:3 * hk], _silu(proj[:, 3 * hk:]), b


def _decay_range(b, c_sub):
    c = b.shape[0]
    worst = None
    for i in range(c // c_sub):
        span = b[i * c_sub:i * c_sub + 1, :] - b[(i + 1) * c_sub - 1:(i + 1) * c_sub, :]
        worst = span if worst is None else jnp.maximum(worst, span)
    return jnp.max(worst)


def _head_norm_gate(o, gate, gout):
    outs = []
    for h in range(o.shape[1] // HEAD_DIM):
        hs = slice(h * HEAD_DIM, (h + 1) * HEAD_DIM)
        oh = o[:, hs]
        outs.append(oh * lax.rsqrt(jnp.mean(oh * oh, axis=-1, keepdims=True) + RMS_EPS) * gout * gate[:, hs])
    return jnp.concatenate(outs, axis=-1)


def _hgrn_chunk(prep, gout, st_refs, seq, c_sub):
    q, k, v, gate, b = prep
    c = q.shape[0]
    n_heads = q.shape[1] // HEAD_DIM
    row = lax.broadcasted_iota(jnp.int32, (c, c), 0)
    col = lax.broadcasted_iota(jnp.int32, (c, c), 1)
    causal = row >= col
    n_sub = c // c_sub
    subs = [slice(i * c_sub, (i + 1) * c_sub) for i in range(n_sub)]

    intra, inter = [], []
    for h in range(n_heads):
        hs = slice(h * HEAD_DIM, (h + 1) * HEAD_DIM)
        bh, qh, kh, vh = b[:, hs], q[:, hs], k[:, hs], v[:, hs]
        vb = vh.astype(BF16)
        refs = [bh[i * c_sub + c_sub // 2:i * c_sub + c_sub // 2 + 1, :] for i in range(n_sub)]
        k_own = [kh[rs] * jnp.exp(jnp.minimum(ref - bh[rs], MAX_LOG_DECAY_RANGE))
                 for rs, ref in zip(subs, refs)]
        a_rows = []
        for i in range(n_sub):
            q_hat = (qh[subs[i]] * jnp.exp(bh[subs[i]] - refs[i])).astype(BF16)
            parts = [k_own[j] * jnp.exp(refs[i] - refs[j]) for j in range(i)] + [k_own[i]]
            parts += [jnp.zeros((c_sub, HEAD_DIM), F32)] * (n_sub - 1 - i)
            k_hat = (jnp.concatenate(parts, axis=0) if n_sub > 1 else parts[0]).astype(BF16)
            a_rows.append(lax.dot_general(q_hat, k_hat, (((1,), (1,)), ((), ())),
                                          preferred_element_type=F32))
        att = jnp.where(causal, jnp.concatenate(a_rows, axis=0) if len(a_rows) > 1 else a_rows[0], 0.0)
        intra.append(_dot(att.astype(BF16), vb))
        st = st_refs[seq, h]
        inter.append(lax.dot_general((qh * jnp.exp(bh)).astype(BF16), st.astype(BF16),
                                     (((1,), (1,)), ((), ())), preferred_element_type=F32))
        b_last = bh[c - 1:c, :]
        k_dec = (kh * jnp.exp(b_last - bh)).astype(BF16)
        st_refs[seq, h] = st * jnp.exp(b_last) + lax.dot_general(
            vb, k_dec, (((0,), (0,)), ((), ())), preferred_element_type=F32)
    inter = jnp.concatenate(inter, axis=-1)
    return _head_norm_gate(jnp.concatenate(intra, axis=-1) + inter, gate, gout), inter


def _hgrn_chunk_exact(prep, inter, gout, q_ref, b_ref, oi_ref):
    q, k, v, gate, b = prep
    c = q.shape[0]
    n_heads = q.shape[1] // HEAD_DIM
    q_ref[...] = q
    b_ref[...] = b
    key_row = lax.broadcasted_iota(jnp.int32, (c, 1), 0)

    def row_group(g, carry):
        rows = pl.ds(pl.multiple_of(g * SUBLANES, SUBLANES), SUBLANES)
        for h in range(n_heads):
            hs = slice(h * HEAD_DIM, (h + 1) * HEAD_DIM)
            q_g, b_g = q_ref[rows, hs], b_ref[rows, hs]
            o_rows = []
            for r in range(SUBLANES):
                decay = jnp.exp(jnp.minimum(b_g[r:r + 1] - b[:, hs], 0.0))
                score = jnp.sum(decay * k[:, hs] * q_g[r:r + 1], axis=-1, keepdims=True)
                score = jnp.where(key_row <= g * SUBLANES + r, score, 0.0)
                o_rows.append(jnp.sum(score * v[:, hs], axis=0, keepdims=True))
            oi_ref[rows, hs] = jnp.concatenate(o_rows, axis=0)
        return carry
    lax.fori_loop(0, c // SUBLANES, row_group, 0)
    return _head_norm_gate(oi_ref[...] + inter, gate, gout)


def _lower_bound(lb_logits, layer):
    e = jnp.exp(lb_logits - jnp.max(lb_logits, axis=0, keepdims=True))
    return jnp.sum(e[:layer + 1], axis=0, keepdims=True) / jnp.sum(e, axis=0, keepdims=True)


def _hgrn_rec_kernel(*refs, chunk, c_sub, n_valid, has_state, layer, fused_proj):
    refs = list(refs)
    if fused_proj:
        x_ref, sh_ref, sc_ref, g_ref, w_ref = refs[:5]
        del refs[:5]
        proj_ref = None
    else:
        proj_ref = refs.pop(0)
    lb_ref, gout_ref = refs[:2]
    del refs[:2]
    s0_ref = refs.pop(0) if has_state else None
    o_ref, sout_ref, st_ref, inter_ref, q_ref, b_ref, oi_ref = refs[:7]
    proj_buf = refs[7] if fused_proj else None
    bb, tb, _ = o_ref.shape
    n_heads = st_ref.shape[1]
    n_chunks = tb // chunk
    j = pl.program_id(1)

    @pl.when(j == 0)
    def _():
        if has_state:
            for s in range(bb):
                for h in range(n_heads):
                    st_ref[s, h] = s0_ref[s, h].T
        else:
            st_ref[...] = jnp.zeros_like(st_ref)

    lb = _lower_bound(lb_ref[...], layer)
    gout = gout_ref[...]

    def project(s, ci):
        rows = pl.ds(pl.multiple_of(ci * chunk, chunk), chunk)
        h = _rms_norm(x_ref[s, rows, :], g_ref[...]) * (1.0 + sc_ref[s]) + sh_ref[s]
        return _dot(h.astype(BF16), w_ref[...])

    if fused_proj:
        for s in range(bb):
            proj_buf[s, 0] = project(s, 0)

    def chunk_body(ci, carry):
        rows = pl.ds(pl.multiple_of(ci * chunk, chunk), chunk)

        def load_proj(s):
            return proj_buf[s, ci % 2] if fused_proj else proj_ref[s, rows, :]
        span = None
        for s in range(bb):
            prep = _hgrn_prep(load_proj(s), lb, n_valid)
            o_ref[s, rows, :], inter_ref[s] = _hgrn_chunk(prep, gout, st_ref, s, c_sub)
            worst = _decay_range(prep[4], c_sub)
            span = worst if span is None else jnp.maximum(span, worst)
            if fused_proj:
                proj_buf[s, (ci + 1) % 2] = project(s, jnp.minimum(ci + 1, n_chunks - 1))

        @pl.when(jnp.logical_not(span <= MAX_LOG_DECAY_RANGE))
        def _():
            for s in range(bb):
                prep = _hgrn_prep(load_proj(s), lb, n_valid)
                o_ref[s, rows, :] = _hgrn_chunk_exact(prep, inter_ref[s], gout, q_ref, b_ref, oi_ref)
        return carry
    lax.fori_loop(0, n_chunks, chunk_body, 0)

    @pl.when(j == pl.num_programs(1) - 1)
    def _():
        for s in range(bb):
            for h in range(n_heads):
                sout_ref[s, h] = st_ref[s, h].T


def hgrn_recurrence(proj, lb_logits, g_out, s0, *, layer, seq_block, time_block, chunk, c_sub, n_valid,
                    norm_proj_of=None):
    fused = norm_proj_of is not None
    if fused:
        x, mod, g, w_in = norm_proj_of
        bsz, t, d = x.shape
        p = w_in.shape[1]
        mod_spec = lambda k: pl.BlockSpec((seq_block, 1, d), lambda i, j: (i, 0, k))
        in_specs = [pl.BlockSpec((seq_block, time_block, d), lambda i, j: (i, j, 0)), mod_spec(0), mod_spec(1),
                    pl.BlockSpec((1, d), lambda i, j: (0, 0)), pl.BlockSpec((d, p), lambda i, j: (0, 0))]
        args = [x, mod, mod, g.reshape(1, d), w_in]
    else:
        bsz, t, p = proj.shape
        in_specs = [pl.BlockSpec((seq_block, time_block, p), lambda i, j: (i, j, 0))]
        args = [proj]
    hk = p // 4
    n_heads = hk // HEAD_DIM
    has_state = s0 is not None
    st_shape = (seq_block, n_heads, HEAD_DIM, HEAD_DIM)
    st_spec = pl.BlockSpec(st_shape, lambda i, j: (i, 0, 0, 0))
    in_specs += [pl.BlockSpec(lb_logits.shape, lambda i, j: (0, 0)),
                 pl.BlockSpec((1, HEAD_DIM), lambda i, j: (0, 0))]
    args += [lb_logits, g_out.reshape(1, HEAD_DIM)]
    if has_state:
        in_specs.append(st_spec)
        args.append(s0)
    scratch = [pltpu.VMEM(st_shape, F32), pltpu.VMEM((seq_block, chunk, hk), F32)]
    scratch += [pltpu.VMEM((chunk, hk), F32)] * 3
    if fused:
        scratch.append(pltpu.VMEM((seq_block, 2, chunk, p), F32))
    return pl.pallas_call(
        functools.partial(_hgrn_rec_kernel, chunk=chunk, c_sub=c_sub, n_valid=n_valid,
                          has_state=has_state, layer=layer, fused_proj=fused),
        grid=(bsz // seq_block, t // time_block),
        in_specs=in_specs,
        out_specs=[pl.BlockSpec((seq_block, time_block, hk), lambda i, j: (i, j, 0)), st_spec],
        out_shape=[jax.ShapeDtypeStruct((bsz, t, hk), F32),
                   jax.ShapeDtypeStruct((bsz, n_heads, HEAD_DIM, HEAD_DIM), F32)],
        scratch_shapes=scratch,
        compiler_params=_params("parallel", "arbitrary"),
        name="hgrn_recurrence",
    )(*args)


def _sorted_rows(tile, top_k, n_experts):
    return tile * top_k + n_experts * BF16_ROWS


def _resid_router_kernel(*refs, top_k, n_experts, has_w_out, chained, row_chunk):
    refs = list(refs)
    x_ref, y_ref = refs[:2]
    del refs[:2]
    wo_ref = refs.pop(0) if has_w_out else None
    gt_ref, sh_ref, sc_ref, g_ref, wr_ref, br_ref = refs[:6]
    del refs[:6]
    if chained:
        refs.pop(0)
    x1_ref, xs_ref, pos_ref, gate_ref, cnt_ref = refs
    tile, d = x_ref.shape
    n_sorted = xs_ref.shape[0]

    y = y_ref[...]
    if has_w_out:
        y = _dot(y.astype(BF16), wo_ref[...])
    x1 = x_ref[...] + gt_ref[...].reshape(-1, d) * y
    x1_ref[...] = x1
    h = _rms_norm(x1, g_ref[...]) * (1.0 + sc_ref[...].reshape(-1, d)) + sh_ref[...].reshape(-1, d)

    lane = lax.broadcasted_iota(jnp.int32, (tile, LANES), 1).astype(F32)
    logits = jnp.where(lane < n_experts, _dot_hp(h, wr_ref[...]) + br_ref[...], -jnp.inf)
    picks, vals = [], []
    for _ in range(top_k):
        m = jnp.max(logits, axis=-1, keepdims=True)
        pick = jnp.min(jnp.where(logits == m, lane, float(LANES)), axis=-1, keepdims=True)
        picks.append(pick)
        vals.append(m)
        logits = jnp.where(lane == pick, -jnp.inf, logits)
    exps = [jnp.exp(v - vals[0]) for v in vals]
    denom = exps[0]
    for e in exps[1:]:
        denom = denom + e

    onehots = [(lane == p).astype(F32) for p in picks]
    oh_sum = onehots[0]
    for oh in onehots[1:]:
        oh_sum = oh_sum + oh
    row = lax.broadcasted_iota(jnp.int32, (tile, tile), 0)
    col = lax.broadcasted_iota(jnp.int32, (tile, tile), 1)
    before = _dot((row > col).astype(BF16), oh_sum.astype(BF16))
    count = jnp.sum(oh_sum, axis=0, keepdims=True)
    cnt_pad = jnp.floor((count + (BF16_ROWS - 1.0)) * (1.0 / BF16_ROWS)) * BF16_ROWS
    lane8 = lax.broadcasted_iota(jnp.int32, (SUBLANES, LANES), 1)
    run = jnp.broadcast_to(cnt_pad, (SUBLANES, LANES))
    shift = 1
    while shift < n_experts:
        run = run + jnp.where(lane8 >= shift, pltpu.roll(run, shift, 1), 0.0)
        shift *= 2
    pos = before + (run[0:1] - cnt_pad)
    pos_out = jnp.zeros((tile, LANES), F32)
    gate_out = jnp.zeros((tile, LANES), F32)
    for k in range(top_k):
        pos_k = jnp.sum(onehots[k] * pos, axis=-1, keepdims=True)
        pos_out = jnp.where(lane == k, pos_k, pos_out)
        gate_out = jnp.where(lane == k, exps[k] / denom, gate_out)
    pos_ref[...] = pos_out.astype(jnp.int32)
    gate_ref[...] = gate_out
    cnt_ref[0] = cnt_pad.astype(jnp.int32)

    pos_t = pos_out.T
    hb = h.astype(BF16)
    for r0 in range(0, n_sorted, row_chunk):
        slot = (lax.broadcasted_iota(jnp.int32, (row_chunk, tile), 0) + r0).astype(F32)
        sel = jnp.where(slot == pos_t[0:1], 1.0, 0.0)
        for k in range(1, top_k):
            sel = sel + jnp.where(slot == pos_t[k:k + 1], 1.0, 0.0)
        xs_ref[r0:r0 + row_chunk, :] = _dot(sel.astype(BF16), hb).astype(BF16)


def resid_router(x, y, w_out_bf16, mod, g, w_r, b_r, sorted_in, *, tile, rows_per_seq, top_k,
                 block_offset, n_blocks_total):
    n, d = x.shape
    n_experts = w_r.shape[1]
    n_sorted = _sorted_rows(tile, top_k, n_experts)
    w_r_pad = jnp.pad(w_r, ((0, 0), (0, LANES - n_experts)))
    b_r_pad = jnp.pad(b_r, (0, LANES - n_experts)).reshape(1, LANES)
    has_w_out = w_out_bf16 is not None
    chained = sorted_in is not None
    row_spec = pl.BlockSpec((tile, d), lambda i: (i, 0))
    lane_spec = pl.BlockSpec((tile, LANES), lambda i: (i, 0))
    full = lambda a: pl.BlockSpec(a.shape, lambda i: (0,) * a.ndim)
    in_specs = [row_spec, pl.BlockSpec((tile, y.shape[1]), lambda i: (i, 0))]
    args = [x, y]
    if has_w_out:
        in_specs.append(full(w_out_bf16))
        args.append(w_out_bf16)
    in_specs += [_mod_spec(mod, 2, d, tile, rows_per_seq), _mod_spec(mod, 3, d, tile, rows_per_seq),
                 _mod_spec(mod, 4, d, tile, rows_per_seq), pl.BlockSpec((1, d), lambda i: (0, 0)),
                 full(w_r_pad), full(b_r_pad)]
    args += [mod, mod, mod, g.reshape(1, d), w_r_pad, b_r_pad]
    aliases = {}
    if chained:
        aliases = {len(args): 1}
        in_specs.append(pl.BlockSpec(memory_space=pl.ANY))
        args.append(sorted_in)
    n_tiles = n // tile
    return pl.pallas_call(
        functools.partial(_resid_router_kernel, top_k=top_k, n_experts=n_experts, has_w_out=has_w_out,
                          chained=chained, row_chunk=256),
        grid=(n_tiles,),
        in_specs=in_specs,
        out_specs=[row_spec,
                   pl.BlockSpec((n_sorted, d), lambda i: (i + block_offset, 0)),
                   lane_spec, lane_spec,
                   pl.BlockSpec((1, 1, LANES), lambda i: (i, 0, 0))],
        out_shape=[jax.ShapeDtypeStruct((n, d), F32),
                   jax.ShapeDtypeStruct((n_blocks_total * n_sorted, d), BF16),
                   jax.ShapeDtypeStruct((n, LANES), jnp.int32), jax.ShapeDtypeStruct((n, LANES), F32),
                   jax.ShapeDtypeStruct((n_tiles, 1, LANES), jnp.int32)],
        input_output_aliases=aliases,
        compiler_params=_params("parallel"),
        name="resid_router",
    )(*args)


def _experts_kernel(te_ref, first_ref, rows_ref, base_ref, slo_ref, shi_ref, wslot_ref, nexte_ref, used_ref,
                    sstart_ref, slen_ref, ssrc_ref,
                    xs_hbm, wgu_hbm, bgu_ref, wdn_hbm, bdn_ref, ys_hbm,
                    xbuf, ybuf, wgu_f, wdn_f, wgu_b, wdn_b, in_sem, out_sem, w_sem,
                    *, layer, limit, alpha, col_chunk):
    del ys_hbm
    i = pl.program_id(0)
    used = used_ref[0]
    tm = xbuf.shape[1]

    def copy(src_rows, dst_rows, slot, inbound):
        if inbound:
            return pltpu.make_async_copy(xs_hbm.at[src_rows, :], xbuf.at[slot, dst_rows, :], in_sem.at[slot])
        return pltpu.make_async_copy(ybuf.at[slot, dst_rows, :], xs_hbm.at[src_rows, :], out_sem.at[slot])

    def piece_copies(tile_idx, slot, inbound, wait):
        if wait:
            rows = pl.ds(0, pl.multiple_of(rows_ref[tile_idx], BF16_ROWS))
            copy(rows, rows, slot, inbound).wait()
            return
        base = base_ref[tile_idx]

        def piece(s, c):
            first = sstart_ref[s] - base
            lo = jnp.maximum(first, 0)
            n_rows = pl.multiple_of(jnp.minimum(first + slen_ref[s], tm) - lo, BF16_ROWS)

            @pl.when(n_rows > 0)
            def _():
                src = pl.multiple_of(ssrc_ref[s] + (lo - first), BF16_ROWS)
                copy(pl.ds(src, n_rows), pl.ds(pl.multiple_of(lo, BF16_ROWS), n_rows), slot, inbound).start()
            return c
        lax.fori_loop(slo_ref[tile_idx], shi_ref[tile_idx], piece, 0)

    def weight_copies(e, slot):
        return (pltpu.make_async_copy(wgu_hbm.at[layer, e], wgu_f.at[slot], w_sem.at[slot]),
                pltpu.make_async_copy(wdn_hbm.at[layer, e], wdn_f.at[slot], w_sem.at[slot]))

    @pl.when(i == 0)
    def _():
        xbuf[...] = jnp.zeros_like(xbuf)
        for cp in weight_copies(te_ref[0], 0):
            cp.start()
        piece_copies(0, 0, True, False)

    @pl.when(i < used)
    def _():
        slot = i % 2

        @pl.when(i + 1 < used)
        def _():
            piece_copies(i + 1, 1 - slot, True, False)

        @pl.when(first_ref[i] == 1)
        def _():
            ws = wslot_ref[i]
            for cp in weight_copies(te_ref[i], ws):
                cp.wait()

            @pl.when(nexte_ref[i] >= 0)
            def _():
                for cp in weight_copies(nexte_ref[i], 1 - ws):
                    cp.start()
            wgu_b[...] = wgu_f[ws].astype(BF16)
            wdn_b[...] = wdn_f[ws].astype(BF16)

        piece_copies(i, slot, True, True)

        @pl.when(i >= 2)
        def _():
            piece_copies(i - 2, slot, False, True)

        d_ff = wdn_b.shape[0]
        tm = xbuf.shape[1]
        e = te_ref[i]
        b_gu = bgu_ref[pl.ds(e, 1), :]
        b_dn = bdn_ref[pl.ds(e, 1), :]

        def mlp(n_rows):
            x = xbuf[slot, :n_rows, :]
            y = None
            for c0 in range(0, d_ff, col_chunk):
                cs = slice(c0, c0 + col_chunk)
                us = slice(d_ff + c0, d_ff + c0 + col_chunk)
                gate = jnp.minimum(_dot(x, wgu_b[:, cs]) + b_gu[:, cs], limit)
                up = jnp.clip(_dot(x, wgu_b[:, us]) + b_gu[:, us], -limit, limit)
                act = ((up + 1.0) * (gate * _sigmoid(alpha * gate))).astype(BF16)
                part = _dot(act, wdn_b[cs, :])
                y = part if y is None else y + part
            ybuf[slot, :n_rows, :] = (y + b_dn).astype(BF16)

        @pl.when(rows_ref[i] > tm // 2)
        def _():
            mlp(tm)

        @pl.when(rows_ref[i] <= tm // 2)
        def _():
            mlp(tm // 2)
        piece_copies(i, slot, False, False)

    @pl.when(i == pl.num_programs(0) - 1)
    def _():
        @pl.when(used >= 2)
        def _():
            piece_copies(used - 2, used % 2, False, True)
        piece_copies(used - 1, (used - 1) % 2, False, True)


def moe_experts(sorted_rows, w_gu, b_gu, w_dn, b_dn, tables, *, layer, row_tile, limit, alpha, col_chunk=512):
    n_rows, d = sorted_rows.shape
    d_gu = w_gu.shape[-1]
    d_ff = w_dn.shape[-2]
    n_tiles = tables[0].shape[0]
    vmem = lambda a: pl.BlockSpec(a.shape, lambda i, *_: (0,) * a.ndim)
    any_spec = pl.BlockSpec(memory_space=pl.ANY)
    return pl.pallas_call(
        functools.partial(_experts_kernel, layer=layer, limit=limit, alpha=alpha,
                          col_chunk=min(col_chunk, d_ff)),
        grid_spec=pltpu.PrefetchScalarGridSpec(
            num_scalar_prefetch=len(tables),
            grid=(n_tiles,),
            in_specs=[any_spec, any_spec, vmem(b_gu), any_spec, vmem(b_dn)],
            out_specs=any_spec,
            scratch_shapes=[
                pltpu.VMEM((2, row_tile, d), BF16), pltpu.VMEM((2, row_tile, d), BF16),
                pltpu.VMEM((2, d, d_gu), F32), pltpu.VMEM((2, d_ff, d), F32),
                pltpu.VMEM((d, d_gu), BF16), pltpu.VMEM((d_ff, d), BF16),
                pltpu.SemaphoreType.DMA((2,)), pltpu.SemaphoreType.DMA((2,)), pltpu.SemaphoreType.DMA((2,)),
            ],
        ),
        out_shape=jax.ShapeDtypeStruct((n_rows, d), BF16),
        input_output_aliases={len(tables): 0},
        compiler_params=_params("arbitrary"),
        name="moe_experts",
    )(*tables, sorted_rows, w_gu, b_gu, w_dn, b_dn)


def _expert_tables(cnt, n_sorted, n_tiles, row_tile):
    n_blocks, n_experts = cnt.shape

    def prefix_sum(a):
        n = a.shape[-1]
        upto = jnp.arange(n)[:, None] <= jnp.arange(n)[None, :]
        return jnp.sum(jnp.where(upto, a[..., :, None], 0), axis=-2)

    local_off = prefix_sum(cnt) - cnt
    seg_end = prefix_sum(cnt.T)
    seg_start = seg_end - cnt.T
    seg_src = jnp.arange(n_blocks, dtype=jnp.int32)[None, :] * n_sorted + local_off.T
    total = seg_end[:, -1]
    padded = (total + row_tile - 1) // row_tile * row_tile
    pad_end = prefix_sum(padded)
    pad_start = pad_end - padded
    tiles = jnp.arange(n_tiles, dtype=jnp.int32)
    n_used = pad_end[-1] // row_tile
    tile_expert = jnp.minimum(jnp.sum(tiles[:, None] * row_tile >= pad_end[None, :], axis=1), n_experts - 1)
    live = tiles < n_used
    tile_base = tiles * row_tile - pad_start[tile_expert]
    tile_rows = jnp.where(live, jnp.clip(total[tile_expert] - tile_base, 0, row_tile), 0)
    tile_first = (live & (tile_base == 0)).astype(jnp.int32)
    seg_lo = jnp.sum(seg_end[tile_expert] <= tile_base[:, None], axis=1)
    seg_hi = jnp.sum(seg_start[tile_expert] < tile_base[:, None] + row_tile, axis=1)
    seg_lo = tile_expert * n_blocks + jnp.minimum(seg_lo, seg_hi)
    seg_hi = tile_expert * n_blocks + seg_hi
    owns = total > 0
    order = prefix_sum(owns.astype(jnp.int32)) - 1
    experts = jnp.arange(n_experts, dtype=jnp.int32)
    later = (experts[None, :] > experts[:, None]) & owns[None, :]
    next_expert = jnp.min(jnp.where(later, experts[None, :], n_experts), axis=1)
    next_expert = jnp.where(next_expert < n_experts, next_expert, -1)
    i32 = lambda a: a.astype(jnp.int32)
    return (i32(tile_expert), tile_first, i32(tile_rows), i32(tile_base), i32(seg_lo), i32(seg_hi),
            i32(order[tile_expert] % 2), i32(next_expert[tile_expert]), i32(n_used).reshape(1),
            i32(seg_start.reshape(-1)), i32(cnt.T.reshape(-1)), i32(seg_src.reshape(-1)))


def _combine_kernel(x_ref, pos_ref, gate_ref, gt_ref, gfin_ref, ys_ref, o_ref, *, top_k, final_norm, k_chunk):
    tile, d = x_ref.shape
    n_sorted = ys_ref.shape[0]
    pos = pos_ref[...].astype(F32)
    gates = gate_ref[...]
    acc = None
    for r0 in range(0, n_sorted, k_chunk):
        slot = (lax.broadcasted_iota(jnp.int32, (tile, k_chunk), 1) + r0).astype(F32)
        w = jnp.where(slot == pos[:, 0:1], gates[:, 0:1], 0.0)
        for k in range(1, top_k):
            w = w + jnp.where(slot == pos[:, k:k + 1], gates[:, k:k + 1], 0.0)
        part = _dot(w.astype(BF16), ys_ref[r0:r0 + k_chunk, :])
        acc = part if acc is None else acc + part
    out = x_ref[...] + gt_ref[...].reshape(-1, d) * acc
    if final_norm:
        out = _rms_norm(out, gfin_ref[...])
    o_ref[...] = out


def moe_combine(x, ys, pos, gates, mod, g_final, *, tile, rows_per_seq, final_norm, block_offset, n_sorted):
    n, d = x.shape
    return pl.pallas_call(
        functools.partial(_combine_kernel, top_k=TOP_K, final_norm=final_norm, k_chunk=512),
        grid=(n // tile,),
        in_specs=[
            pl.BlockSpec((tile, d), lambda i: (i, 0)),
            pl.BlockSpec((tile, LANES), lambda i: (i, 0)),
            pl.BlockSpec((tile, LANES), lambda i: (i, 0)),
            _mod_spec(mod, 5, d, tile, rows_per_seq),
            pl.BlockSpec((1, d), lambda i: (0, 0)),
            pl.BlockSpec((n_sorted, d), lambda i: (i + block_offset, 0)),
        ],
        out_specs=pl.BlockSpec((tile, d), lambda i: (i, 0)),
        out_shape=jax.ShapeDtypeStruct((n, d), F32),
        compiler_params=_params("parallel"),
        name="moe_combine",
    )(x, pos, gates, mod, g_final.reshape(1, d), ys)


def _pool_groups(h, window_sum, counts, w_ref, scale):
    n_groups = w_ref.shape[0]
    dg = h.shape[-1] // n_groups
    outs = []
    for gi in range(n_groups):
        cols = slice(gi * dg, (gi + 1) * dg)
        pooled = window_sum(gi, cols) / counts[gi] - h[:, cols]
        outs.append(_dot(pooled.astype(BF16), w_ref[gi]))
    return jnp.concatenate(outs, axis=-1) * scale


def _pool_prompt_kernel(x_ref, sh_ref, sc_ref, g_ref, w_ref, scale_ref, y_ref, cache_ref, ext_ref,
                        *, windows, halo):
    tile, d = x_ref.shape[1:]
    j = pl.program_id(1)

    @pl.when(j == 0)
    def _():
        ext_ref[0:halo, :] = jnp.zeros((halo, d), F32)

    h = _rms_norm(x_ref[0], g_ref[...]) * (1.0 + sc_ref[...].reshape(-1, d)) + sh_ref[...].reshape(-1, d)
    ext_ref[halo:halo + tile, :] = h
    pos = j * tile + lax.broadcasted_iota(jnp.int32, (tile, 1), 0)

    def window_sum(gi, cols):
        acc = h[:, cols]
        for s in range(1, windows[gi]):
            acc = acc + ext_ref[halo - s:halo - s + tile, cols]
        return acc
    counts = [jnp.minimum(pos + 1, w).astype(F32) for w in windows]
    y_ref[0] = _pool_groups(h, window_sum, counts, w_ref, scale_ref[...])

    n_keep = cache_ref.shape[1]
    @pl.when(j == pl.num_programs(1) - 1)
    def _():
        cache_ref[0] = ext_ref[halo + tile - n_keep:halo + tile, :]
    ext_ref[0:halo, :] = ext_ref[tile:tile + halo, :]


def pool_mixer_prompt(x, mod, g, w_grp_bf16, scale, *, tile, windows, n_keep):
    bsz, t, d = x.shape
    halo = 16
    assert max(windows) <= halo <= tile and n_keep <= tile
    mod_spec = lambda k: pl.BlockSpec((1, 1, d), lambda b, j: (b, 0, k))
    return pl.pallas_call(
        functools.partial(_pool_prompt_kernel, windows=windows, halo=halo),
        grid=(bsz, t // tile),
        in_specs=[
            pl.BlockSpec((1, tile, d), lambda b, j: (b, j, 0)),
            mod_spec(0), mod_spec(1),
            pl.BlockSpec((1, d), lambda b, j: (0, 0)),
            pl.BlockSpec(w_grp_bf16.shape, lambda b, j: (0, 0, 0)),
            pl.BlockSpec((1, d), lambda b, j: (0, 0)),
        ],
        out_specs=[pl.BlockSpec((1, tile, d), lambda b, j: (b, j, 0)),
                   pl.BlockSpec((1, n_keep, d), lambda b, j: (b, 0, 0))],
        out_shape=[jax.ShapeDtypeStruct((bsz, t, d), F32), jax.ShapeDtypeStruct((bsz, n_keep, d), F32)],
        scratch_shapes=[pltpu.VMEM((halo + tile, d), F32)],
        compiler_params=_params("parallel", "arbitrary"),
        name="pool_mixer_prompt",
    )(x, mod, mod, g.reshape(1, d), w_grp_bf16, scale.reshape(1, d))


def _pool_sample_kernel(x_ref, buf_ref, sh_ref, sc_ref, g_ref, w_ref, scale_ref, y_ref, cache_ref,
                        *, windows, start_pos):
    t_len = x_ref.shape[0]
    n_prev = buf_ref.shape[0]
    hs = [_rms_norm(x_ref[t], g_ref[...]) * (1.0 + sc_ref[...]) + sh_ref[...] for t in range(t_len)]

    def ext(r):
        return buf_ref[r] if r < n_prev else hs[r - n_prev]

    for t in range(t_len):
        def window_sum(gi, cols):
            acc = hs[t][:, cols]
            for s in range(1, windows[gi]):
                acc = acc + ext(n_prev + t - s)[:, cols]
            return acc
        counts = [float(min(start_pos + t + 1, w)) for w in windows]
        y_ref[t] = _pool_groups(hs[t], window_sum, counts, w_ref, scale_ref[...])
    for r in range(n_prev):
        cache_ref[r] = ext(t_len + r)


def pool_mixer_sample(x_t, buf_t, mod, g, w_grp_bf16, scale, *, seq_block, windows, start_pos):
    t_len, n_seq, d = x_t.shape
    n_prev = buf_t.shape[0]
    assert start_pos >= n_prev >= max(windows) - 1
    mod_spec = lambda k: pl.BlockSpec((seq_block, d), lambda i: (i, k))
    return pl.pallas_call(
        functools.partial(_pool_sample_kernel, windows=windows, start_pos=start_pos),
        grid=(n_seq // seq_block,),
        in_specs=[
            pl.BlockSpec((t_len, seq_block, d), lambda i: (0, i, 0)),
            pl.BlockSpec((n_prev, seq_block, d), lambda i: (0, i, 0)),
            mod_spec(0), mod_spec(1),
            pl.BlockSpec((1, d), lambda i: (0, 0)),
            pl.BlockSpec(w_grp_bf16.shape, lambda i: (0, 0, 0)),
            pl.BlockSpec((1, d), lambda i: (0, 0)),
        ],
        out_specs=[pl.BlockSpec((t_len, seq_block, d), lambda i: (0, i, 0)),
                   pl.BlockSpec((n_prev, seq_block, d), lambda i: (0, i, 0))],
        out_shape=[jax.ShapeDtypeStruct((t_len, n_seq, d), F32), jax.ShapeDtypeStruct((n_prev, n_seq, d), F32)],
        compiler_params=_params("parallel"),
        name="pool_mixer_sample",
    )(x_t, buf_t, mod, mod, g.reshape(1, d), w_grp_bf16, scale.reshape(1, d))


TOP_K = 4
SWIGLU_LIMIT = 7.0
SWIGLU_ALPHA = 1.702
POOL_WINDOWS = (2, 4, 8, 16)
PAST_LEN = 16384
PROJ_TILE = 256
MOE_TOKEN_TILE = 512
MOE_ROW_TILE = 512
HGRN_TIME_BLOCK = 1024
HGRN_CHUNK = 128
HGRN_SUB = 32
SAMPLE_T_PAD = 8
SAMPLE_SEQ_BLOCK = 8


def kernel(x_prompt, x_sample, c_prompt, c_sample, state_hgrn, cache_pool, g_norm_mix, g_norm_ffn, w_ada, b_ada, w_in_hgrn, lb_logits, g_out_hgrn, w_out_hgrn, w_grp_pool, scale_pool, w_router, b_router, w_gate_up, b_gate_up, w_down, b_down, g_final):
    bp, tp, d = x_prompt.shape
    bs, ts, _ = x_sample.shape
    n_p, n_s = bp * tp, bs * ts
    n_experts = w_router.shape[-1]
    hk = w_out_hgrn.shape[1]
    assert n_s == MOE_TOKEN_TILE and n_p % MOE_TOKEN_TILE == 0
    blocks_p = n_p // MOE_TOKEN_TILE
    n_blocks = blocks_p + 1
    n_sorted = _sorted_rows(MOE_TOKEN_TILE, TOP_K, n_experts)
    n_row_tiles = -(-(n_blocks * n_sorted + n_experts * (MOE_ROW_TILE - BF16_ROWS)) // MOE_ROW_TILE)

    mod = adaln(jnp.concatenate([c_prompt, c_sample], axis=0), w_ada, b_ada)
    mod_p = [mod[l, :bp][:, None, :] for l in range(mod.shape[0])]
    mod_s = [mod[l, bp:] for l in range(mod.shape[0])]
    mod_st = [jnp.tile(m, (ts, 1)) for m in mod_s]

    xp = x_prompt.reshape(n_p, d)
    xs = x_sample.transpose(1, 0, 2).reshape(n_s, d)

    def moe(layer, x_p, y_p, x_s, y_s, w_out, final_norm):
        route = functools.partial(resid_router, g=g_norm_ffn[layer], w_r=w_router[layer], b_r=b_router[layer],
                                  tile=MOE_TOKEN_TILE, top_k=TOP_K, n_blocks_total=n_blocks)
        x1_p, sorted_rows, pos_p, gate_p, cnt_p = route(
            x_p, y_p, w_out, mod_p[layer], sorted_in=None, rows_per_seq=tp, block_offset=0)
        x1_s, sorted_rows, pos_s, gate_s, cnt_s = route(
            x_s, y_s, w_out, mod_st[layer], sorted_in=sorted_rows, rows_per_seq=None, block_offset=blocks_p)
        cnt8 = jnp.concatenate([cnt_p, cnt_s], axis=0)[:, 0, :n_experts]
        tables = _expert_tables(cnt8, n_sorted, n_row_tiles, MOE_ROW_TILE)
        ys = moe_experts(sorted_rows, w_gate_up, b_gate_up[layer], w_down, b_down[layer], tables,
                         layer=layer, row_tile=MOE_ROW_TILE, limit=SWIGLU_LIMIT, alpha=SWIGLU_ALPHA)
        combine = functools.partial(moe_combine, ys=ys, g_final=g_final, tile=MOE_TOKEN_TILE,
                                    final_norm=final_norm, n_sorted=n_sorted)
        out_p = combine(x1_p, pos=pos_p, gates=gate_p, mod=mod_p[layer], rows_per_seq=tp, block_offset=0)
        out_s = combine(x1_s, pos=pos_s, gates=gate_s, mod=mod_st[layer], rows_per_seq=None,
                        block_offset=blocks_p)
        return out_p, out_s

    w_in = w_in_hgrn[0].astype(BF16)
    proj_s = norm_proj(xs, mod_s[0], g_norm_mix[0], w_in, tile=bs, rows_per_seq=None)
    o_p, state_p = hgrn_recurrence(None, lb_logits, g_out_hgrn[0], None,
                                   layer=0, seq_block=1, time_block=HGRN_TIME_BLOCK, chunk=HGRN_CHUNK,
                                   c_sub=HGRN_SUB, n_valid=HGRN_CHUNK,
                                   norm_proj_of=(x_prompt, mod_p[0], g_norm_mix[0], w_in))
    proj_sb = jnp.pad(proj_s.reshape(ts, bs, 4 * hk).transpose(1, 0, 2), ((0, 0), (0, SAMPLE_T_PAD - ts), (0, 0)))
    o_s, state_s = hgrn_recurrence(proj_sb, lb_logits, g_out_hgrn[0], state_hgrn[0],
                                   layer=0, seq_block=SAMPLE_SEQ_BLOCK, time_block=SAMPLE_T_PAD,
                                   chunk=SAMPLE_T_PAD, c_sub=SAMPLE_T_PAD, n_valid=ts)
    o_s = o_s[:, :ts].transpose(1, 0, 2).reshape(n_s, hk)
    x_p, x_s = moe(0, xp, o_p.reshape(n_p, hk), xs, o_s, w_out_hgrn[0].astype(BF16), False)

    w_grp = w_grp_pool[0].astype(BF16)
    n_keep = cache_pool.shape[2]
    y_p, cache_p = pool_mixer_prompt(x_p.reshape(bp, tp, d), mod_p[1], g_norm_mix[1], w_grp, scale_pool[0],
                                     tile=PROJ_TILE, windows=POOL_WINDOWS, n_keep=n_keep)
    y_s, cache_s = pool_mixer_sample(x_s.reshape(ts, bs, d), cache_pool[0].transpose(1, 0, 2), mod_s[1],
                                     g_norm_mix[1], w_grp, scale_pool[0],
                                     seq_block=32, windows=POOL_WINDOWS, start_pos=PAST_LEN)
    x_p, x_s = moe(1, x_p, y_p.reshape(n_p, d), x_s, y_s.reshape(n_s, d), None, True)

    return (x_p.reshape(bp, tp, d), x_s.reshape(ts, bs, d).transpose(1, 0, 2),
            state_p[None], state_s[None], cache_p[None], cache_s.transpose(1, 0, 2)[None])
```

```python
import functools

import jax
import jax.numpy as jnp
from jax import lax
from jax.experimental import pallas as pl
from jax.experimental.pallas import tpu as pltpu

F32 = jnp.float32
BF16 = jnp.bfloat16

RMS_EPS = 1e-6
LANES = 128
SUBLANES = 8
BF16_ROWS = 16
HEAD_DIM = 128
VMEM_LIMIT = 56 * 1024 * 1024

_dot = functools.partial(jnp.dot, preferred_element_type=F32)


def _params(*semantics):
    return pltpu.CompilerParams(dimension_semantics=semantics, vmem_limit_bytes=VMEM_LIMIT)


def _split_bf16(x, n):
    parts, r = [], x
    for _ in range(n):
        p = r.astype(BF16)
        parts.append(p)
        r = r - p.astype(F32)
    return parts


def _dot_hp(a, b):
    a_hi, a_lo = _split_bf16(a, 2)
    b_hi, b_lo = _split_bf16(b, 2)
    return _dot(a_hi, b_hi) + (_dot(a_hi, b_lo) + _dot(a_lo, b_hi))


def _sigmoid(x):
    return 1.0 / (1.0 + jnp.exp(-x))


def _silu(x):
    return x * _sigmoid(x)


def _rms_norm(x, g):
    ms = jnp.mean(x * x, axis=-1, keepdims=True)
    return x * lax.rsqrt(ms + RMS_EPS) * g


def _adaln_kernel(c_ref, w_ref, b_ref, o_ref):
    o_ref[0] = _dot_hp(_silu(c_ref[...]), w_ref[0]) + b_ref[0]


def adaln(c_all, w_ada, b_ada, *, col_block=1536):
    n_seq, d = c_all.shape
    n_layers, _, d6 = w_ada.shape
    return pl.pallas_call(
        _adaln_kernel,
        grid=(n_layers, d6 // col_block),
        in_specs=[
            pl.BlockSpec((n_seq, d), lambda l, j: (0, 0)),
            pl.BlockSpec((1, d, col_block), lambda l, j: (l, 0, j)),
            pl.BlockSpec((1, 1, col_block), lambda l, j: (l, 0, j)),
        ],
        out_specs=pl.BlockSpec((1, n_seq, col_block), lambda l, j: (l, 0, j)),
        out_shape=jax.ShapeDtypeStruct((n_layers, n_seq, d6), F32),
        compiler_params=_params("parallel", "parallel"),
        name="adaln",
    )(c_all, w_ada, b_ada.reshape(n_layers, 1, d6))


def _mod_spec(mod, k, d, tile, rows_per_seq):
    if rows_per_seq is None:
        return pl.BlockSpec((mod.shape[0], d), lambda i: (0, k))
    tiles_per_seq = rows_per_seq // tile
    return pl.BlockSpec((1, 1, d), lambda i: (i // tiles_per_seq, 0, k))


def _mod_rows(ref, tile):
    m = ref[...].reshape(-1, ref.shape[-1])
    if m.shape[0] not in (1, tile):
        m = jnp.concatenate([m] * (tile // m.shape[0]), axis=0)
    return m


def _norm_proj_kernel(x_ref, sh_ref, sc_ref, g_ref, w_ref, o_ref):
    tile = x_ref.shape[0]
    h = _rms_norm(x_ref[...], g_ref[...]) * (1.0 + _mod_rows(sc_ref, tile)) + _mod_rows(sh_ref, tile)
    o_ref[...] = _dot(h.astype(BF16), w_ref[...])


def norm_proj(x, mod, g, w_bf16, *, tile, rows_per_seq):
    n, d = x.shape
    p = w_bf16.shape[1]
    return pl.pallas_call(
        _norm_proj_kernel,
        grid=(n // tile,),
        in_specs=[
            pl.BlockSpec((tile, d), lambda i: (i, 0)),
            _mod_spec(mod, 0, d, tile, rows_per_seq),
            _mod_spec(mod, 1, d, tile, rows_per_seq),
            pl.BlockSpec((1, d), lambda i: (0, 0)),
            pl.BlockSpec((d, p), lambda i: (0, 0)),
        ],
        out_specs=pl.BlockSpec((tile, p), lambda i: (i, 0)),
        out_shape=jax.ShapeDtypeStruct((n, p), F32),
        compiler_params=_params("parallel"),
        name="hgrn_norm_proj",
    )(x, mod, mod, g.reshape(1, d), w_bf16)


def _cumsum_rows(x, tri):
    hi, mid, lo = _split_bf16(x, 3)
    return _dot(tri, hi) + (_dot(tri, mid) + _dot(tri, lo))


MAX_LOG_DECAY_RANGE = 80.0


def _hgrn_prep(proj, lb, n_valid):
    c = proj.shape[0]
    hk = proj.shape[1] // 4
    row = lax.broadcasted_iota(jnp.int32, (c, c), 0)
    col = lax.broadcasted_iota(jnp.int32, (c, c), 1)
    zf = proj[:, hk:2 * hk]
    e = jnp.exp(-jnp.abs(zf))
    r = 1.0 / (1.0 + e)
    pos = zf >= 0
    sig_p = jnp.where(pos, 1.0, e) * r
    sig_n = jnp.where(pos, e, 1.0) * r
    logf = jnp.log(lb + (1.0 - lb) * sig_p)
    k = (1.0 - lb) * sig_n
    if n_valid < c:
        live = lax.broadcasted_iota(jnp.int32, (c, 1), 0) < n_valid
        logf = jnp.where(live, logf, 0.0)
        k = jnp.where(live, k, 0.0)
    b = _cumsum_rows(logf, (row >= col).astype(BF16))
    return _silu(proj[:, :hk]), k, proj[:, 2 * hk:3 * hk], _silu(proj[:, 3 * hk:]), b


def _decay_range(b, c_sub):
    c = b.shape[0]
    worst = None
    for i in range(c // c_sub):
        span = b[i * c_sub:i * c_sub + 1, :] - b[(i + 1) * c_sub - 1:(i + 1) * c_sub, :]
        worst = span if worst is None else jnp.maximum(worst, span)
    return jnp.max(worst)


def _head_norm_gate(o, gate, gout):
    outs = []
    for h in range(o.shape[1] // HEAD_DIM):
        hs = slice(h * HEAD_DIM, (h + 1) * HEAD_DIM)
        oh = o[:, hs]
        outs.append(oh * lax.rsqrt(jnp.mean(oh * oh, axis=-1, keepdims=True) + RMS_EPS) * gout * gate[:, hs])
    return jnp.concatenate(outs, axis=-1)


def _hgrn_chunk(prep, gout, st_refs, seq, c_sub):
    q, k, v, gate, b = prep
    c = q.shape[0]
    n_heads = q.shape[1] // HEAD_DIM
    row = lax.broadcasted_iota(jnp.int32, (c, c), 0)
    col = lax.broadcasted_iota(jnp.int32, (c, c), 1)
    causal = row >= col
    n_sub = c // c_sub
    subs = [slice(i * c_sub, (i + 1) * c_sub) for i in range(n_sub)]

    intra, inter = [], []
    for h in range(n_heads):
        hs = slice(h * HEAD_DIM, (h + 1) * HEAD_DIM)
        bh, qh, kh, vh = b[:, hs], q[:, hs], k[:, hs], v[:, hs]
        vb = vh.astype(BF16)
        refs = [bh[i * c_sub + c_sub // 2:i * c_sub + c_sub // 2 + 1, :] for i in range(n_sub)]
        k_own = [kh[rs] * jnp.exp(jnp.minimum(ref - bh[rs], MAX_LOG_DECAY_RANGE))
                 for rs, ref in zip(subs, refs)]
        a_rows = []
        for i in range(n_sub):
            q_hat = (qh[subs[i]] * jnp.exp(bh[subs[i]] - refs[i])).astype(BF16)
            parts = [k_own[j] * jnp.exp(refs[i] - refs[j]) for j in range(i)] + [k_own[i]]
            parts += [jnp.zeros((c_sub, HEAD_DIM), F32)] * (n_sub - 1 - i)
            k_hat = (jnp.concatenate(parts, axis=0) if n_sub > 1 else parts[0]).astype(BF16)
            a_rows.append(lax.dot_general(q_hat, k_hat, (((1,), (1,)), ((), ())),
                                          preferred_element_type=F32))
        att = jnp.where(causal, jnp.concatenate(a_rows, axis=0) if len(a_rows) > 1 else a_rows[0], 0.0)
        intra.append(_dot(att.astype(BF16), vb))
        st = st_refs[seq, h]
        inter.append(lax.dot_general((qh * jnp.exp(bh)).astype(BF16), st.astype(BF16),
                                     (((1,), (1,)), ((), ())), preferred_element_type=F32))
        b_last = bh[c - 1:c, :]
        k_dec = (kh * jnp.exp(b_last - bh)).astype(BF16)
        st_refs[seq, h] = st * jnp.exp(b_last) + lax.dot_general(
            vb, k_dec, (((0,), (0,)), ((), ())), preferred_element_type=F32)
    inter = jnp.concatenate(inter, axis=-1)
    return _head_norm_gate(jnp.concatenate(intra, axis=-1) + inter, gate, gout), inter


def _hgrn_chunk_exact(prep, inter, gout, q_ref, b_ref, oi_ref):
    q, k, v, gate, b = prep
    c = q.shape[0]
    n_heads = q.shape[1] // HEAD_DIM
    q_ref[...] = q
    b_ref[...] = b
    key_row = lax.broadcasted_iota(jnp.int32, (c, 1), 0)

    def row_group(g, carry):
        rows = pl.ds(pl.multiple_of(g * SUBLANES, SUBLANES), SUBLANES)
        for h in range(n_heads):
            hs = slice(h * HEAD_DIM, (h + 1) * HEAD_DIM)
            q_g, b_g = q_ref[rows, hs], b_ref[rows, hs]
            o_rows = []
            for r in range(SUBLANES):
                decay = jnp.exp(jnp.minimum(b_g[r:r + 1] - b[:, hs], 0.0))
                score = jnp.sum(decay * k[:, hs] * q_g[r:r + 1], axis=-1, keepdims=True)
                score = jnp.where(key_row <= g * SUBLANES + r, score, 0.0)
                o_rows.append(jnp.sum(score * v[:, hs], axis=0, keepdims=True))
            oi_ref[rows, hs] = jnp.concatenate(o_rows, axis=0)
        return carry
    lax.fori_loop(0, c // SUBLANES, row_group, 0)
    return _head_norm_gate(oi_ref[...] + inter, gate, gout)


def _lower_bound(lb_logits, layer):
    e = jnp.exp(lb_logits - jnp.max(lb_logits, axis=0, keepdims=True))
    return jnp.sum(e[:layer + 1], axis=0, keepdims=True) / jnp.sum(e, axis=0, keepdims=True)


def _hgrn_rec_kernel(*refs, chunk, c_sub, n_valid, has_state, layer, fused_proj):
    refs = list(refs)
    if fused_proj:
        x_ref, sh_ref, sc_ref, g_ref, w_ref = refs[:5]
        del refs[:5]
        proj_ref = None
    else:
        proj_ref = refs.pop(0)
    lb_ref, gout_ref = refs[:2]
    del refs[:2]
    s0_ref = refs.pop(0) if has_state else None
    o_ref, sout_ref, st_ref, inter_ref, q_ref, b_ref, oi_ref = refs[:7]
    proj_buf = refs[7] if fused_proj else None
    bb, tb, _ = o_ref.shape
    n_heads = st_ref.shape[1]
    n_chunks = tb // chunk
    j = pl.program_id(1)

    @pl.when(j == 0)
    def _():
        if has_state:
            for s in range(bb):
                for h in range(n_heads):
                    st_ref[s, h] = s0_ref[s, h].T
        else:
            st_ref[...] = jnp.zeros_like(st_ref)

    lb = _lower_bound(lb_ref[...], layer)
    gout = gout_ref[...]

    def project(s, ci):
        rows = pl.ds(pl.multiple_of(ci * chunk, chunk), chunk)
        h = _rms_norm(x_ref[s, rows, :], g_ref[...]) * (1.0 + sc_ref[s]) + sh_ref[s]
        return _dot(h.astype(BF16), w_ref[...])

    if fused_proj:
        for s in range(bb):
            proj_buf[s, 0] = project(s, 0)

    def chunk_body(ci, carry):
        rows = pl.ds(pl.multiple_of(ci * chunk, chunk), chunk)

        def load_proj(s):
            return proj_buf[s, ci % 2] if fused_proj else proj_ref[s, rows, :]
        span = None
        for s in range(bb):
            prep = _hgrn_prep(load_proj(s), lb, n_valid)
            o_ref[s, rows, :], inter_ref[s] = _hgrn_chunk(prep, gout, st_ref, s, c_sub)
            worst = _decay_range(prep[4], c_sub)
            span = worst if span is None else jnp.maximum(span, worst)
            if fused_proj:
                proj_buf[s, (ci + 1) % 2] = project(s, jnp.minimum(ci + 1, n_chunks - 1))

        @pl.when(jnp.logical_not(span <= MAX_LOG_DECAY_RANGE))
        def _():
            for s in range(bb):
                prep = _hgrn_prep(load_proj(s), lb, n_valid)
                o_ref[s, rows, :] = _hgrn_chunk_exact(prep, inter_ref[s], gout, q_ref, b_ref, oi_ref)
        return carry
    lax.fori_loop(0, n_chunks, chunk_body, 0)

    @pl.when(j == pl.num_programs(1) - 1)
    def _():
        for s in range(bb):
            for h in range(n_heads):
                sout_ref[s, h] = st_ref[s, h].T


def hgrn_recurrence(proj, lb_logits, g_out, s0, *, layer, seq_block, time_block, chunk, c_sub, n_valid,
                    norm_proj_of=None):
    fused = norm_proj_of is not None
    if fused:
        x, mod, g, w_in = norm_proj_of
        bsz, t, d = x.shape
        p = w_in.shape[1]
        mod_spec = lambda k: pl.BlockSpec((seq_block, 1, d), lambda i, j: (i, 0, k))
        in_specs = [pl.BlockSpec((seq_block, time_block, d), lambda i, j: (i, j, 0)), mod_spec(0), mod_spec(1),
                    pl.BlockSpec((1, d), lambda i, j: (0, 0)), pl.BlockSpec((d, p), lambda i, j: (0, 0))]
        args = [x, mod, mod, g.reshape(1, d), w_in]
    else:
        bsz, t, p = proj.shape
        in_specs = [pl.BlockSpec((seq_block, time_block, p), lambda i, j: (i, j, 0))]
        args = [proj]
    hk = p // 4
    n_heads = hk // HEAD_DIM
    has_state = s0 is not None
    st_shape = (seq_block, n_heads, HEAD_DIM, HEAD_DIM)
    st_spec = pl.BlockSpec(st_shape, lambda i, j: (i, 0, 0, 0))
    in_specs += [pl.BlockSpec(lb_logits.shape, lambda i, j: (0, 0)),
                 pl.BlockSpec((1, HEAD_DIM), lambda i, j: (0, 0))]
    args += [lb_logits, g_out.reshape(1, HEAD_DIM)]
    if has_state:
        in_specs.append(st_spec)
        args.append(s0)
    scratch = [pltpu.VMEM(st_shape, F32), pltpu.VMEM((seq_block, chunk, hk), F32)]
    scratch += [pltpu.VMEM((chunk, hk), F32)] * 3
    if fused:
        scratch.append(pltpu.VMEM((seq_block, 2, chunk, p), F32))
    return pl.pallas_call(
        functools.partial(_hgrn_rec_kernel, chunk=chunk, c_sub=c_sub, n_valid=n_valid,
                          has_state=has_state, layer=layer, fused_proj=fused),
        grid=(bsz // seq_block, t // time_block),
        in_specs=in_specs,
        out_specs=[pl.BlockSpec((seq_block, time_block, hk), lambda i, j: (i, j, 0)), st_spec],
        out_shape=[jax.ShapeDtypeStruct((bsz, t, hk), F32),
                   jax.ShapeDtypeStruct((bsz, n_heads, HEAD_DIM, HEAD_DIM), F32)],
        scratch_shapes=scratch,
        compiler_params=_params("parallel", "arbitrary"),
        name="hgrn_recurrence",
    )(*args)


def _sorted_rows(tile, top_k, n_experts):
    return tile * top_k + n_experts * BF16_ROWS


def _resid_router_kernel(*refs, top_k, n_experts, has_w_out, chained, row_chunk):
    refs = list(refs)
    x_ref, y_ref = refs[:2]
    del refs[:2]
    wo_ref = refs.pop(0) if has_w_out else None
    gt_ref, sh_ref, sc_ref, g_ref, wr_ref, br_ref = refs[:6]
    del refs[:6]
    if chained:
        refs.pop(0)
    x1_ref, xs_ref, pos_ref, gate_ref, cnt_ref = refs
    tile, d = x_ref.shape
    n_sorted = xs_ref.shape[0]

    y = y_ref[...]
    if has_w_out:
        y = _dot(y.astype(BF16), wo_ref[...])
    x1 = x_ref[...] + _mod_rows(gt_ref, tile) * y
    x1_ref[...] = x1
    h = _rms_norm(x1, g_ref[...]) * (1.0 + _mod_rows(sc_ref, tile)) + _mod_rows(sh_ref, tile)

    lane = lax.broadcasted_iota(jnp.int32, (tile, LANES), 1).astype(F32)
    logits = jnp.where(lane < n_experts, _dot_hp(h, wr_ref[...]) + br_ref[...], -jnp.inf)
    picks, vals = [], []
    for _ in range(top_k):
        m = jnp.max(logits, axis=-1, keepdims=True)
        pick = jnp.min(jnp.where(logits == m, lane, float(LANES)), axis=-1, keepdims=True)
        picks.append(pick)
        vals.append(m)
        logits = jnp.where(lane == pick, -jnp.inf, logits)
    exps = [jnp.exp(v - vals[0]) for v in vals]
    denom = exps[0]
    for e in exps[1:]:
        denom = denom + e

    onehots = [(lane == p).astype(F32) for p in picks]
    oh_sum = onehots[0]
    for oh in onehots[1:]:
        oh_sum = oh_sum + oh
    row = lax.broadcasted_iota(jnp.int32, (tile, tile), 0)
    col = lax.broadcasted_iota(jnp.int32, (tile, tile), 1)
    before = _dot((row > col).astype(BF16), oh_sum.astype(BF16))
    count = jnp.sum(oh_sum, axis=0, keepdims=True)
    cnt_pad = jnp.floor((count + (BF16_ROWS - 1.0)) * (1.0 / BF16_ROWS)) * BF16_ROWS
    lane8 = lax.broadcasted_iota(jnp.int32, (SUBLANES, LANES), 1)
    run = jnp.broadcast_to(cnt_pad, (SUBLANES, LANES))
    shift = 1
    while shift < n_experts:
        run = run + jnp.where(lane8 >= shift, pltpu.roll(run, shift, 1), 0.0)
        shift *= 2
    pos = before + (run[0:1] - cnt_pad)
    pos_out = jnp.zeros((tile, LANES), F32)
    gate_out = jnp.zeros((tile, LANES), F32)
    for k in range(top_k):
        pos_k = jnp.sum(onehots[k] * pos, axis=-1, keepdims=True)
        pos_out = jnp.where(lane == k, pos_k, pos_out)
        gate_out = jnp.where(lane == k, exps[k] / denom, gate_out)
    pos_ref[...] = pos_out.astype(jnp.int32)
    gate_ref[...] = gate_out
    cnt_ref[0] = cnt_pad.astype(jnp.int32)

    pos_t = pos_out.T
    hb = h.astype(BF16)
    for r0 in range(0, n_sorted, row_chunk):
        slot = (lax.broadcasted_iota(jnp.int32, (row_chunk, tile), 0) + r0).astype(F32)
        sel = jnp.where(slot == pos_t[0:1], 1.0, 0.0)
        for k in range(1, top_k):
            sel = sel + jnp.where(slot == pos_t[k:k + 1], 1.0, 0.0)
        xs_ref[r0:r0 + row_chunk, :] = _dot(sel.astype(BF16), hb).astype(BF16)


def resid_router(x, y, w_out_bf16, mod, g, w_r, b_r, sorted_in, *, tile, rows_per_seq, top_k,
                 block_offset, n_blocks_total):
    n, d = x.shape
    n_experts = w_r.shape[1]
    n_sorted = _sorted_rows(tile, top_k, n_experts)
    w_r_pad = jnp.pad(w_r, ((0, 0), (0, LANES - n_experts)))
    b_r_pad = jnp.pad(b_r, (0, LANES - n_experts)).reshape(1, LANES)
    has_w_out = w_out_bf16 is not None
    chained = sorted_in is not None
    row_spec = pl.BlockSpec((tile, d), lambda i: (i, 0))
    lane_spec = pl.BlockSpec((tile, LANES), lambda i: (i, 0))
    full = lambda a: pl.BlockSpec(a.shape, lambda i: (0,) * a.ndim)
    in_specs = [row_spec, pl.BlockSpec((tile, y.shape[1]), lambda i: (i, 0))]
    args = [x, y]
    if has_w_out:
        in_specs.append(full(w_out_bf16))
        args.append(w_out_bf16)
    in_specs += [_mod_spec(mod, 2, d, tile, rows_per_seq), _mod_spec(mod, 3, d, tile, rows_per_seq),
                 _mod_spec(mod, 4, d, tile, rows_per_seq), pl.BlockSpec((1, d), lambda i: (0, 0)),
                 full(w_r_pad), full(b_r_pad)]
    args += [mod, mod, mod, g.reshape(1, d), w_r_pad, b_r_pad]
    aliases = {}
    if chained:
        aliases = {len(args): 1}
        in_specs.append(pl.BlockSpec(memory_space=pl.ANY))
        args.append(sorted_in)
    n_tiles = n // tile
    return pl.pallas_call(
        functools.partial(_resid_router_kernel, top_k=top_k, n_experts=n_experts, has_w_out=has_w_out,
                          chained=chained, row_chunk=256),
        grid=(n_tiles,),
        in_specs=in_specs,
        out_specs=[row_spec,
                   pl.BlockSpec((n_sorted, d), lambda i: (i + block_offset, 0)),
                   lane_spec, lane_spec,
                   pl.BlockSpec((1, 1, LANES), lambda i: (i, 0, 0))],
        out_shape=[jax.ShapeDtypeStruct((n, d), F32),
                   jax.ShapeDtypeStruct((n_blocks_total * n_sorted, d), BF16),
                   jax.ShapeDtypeStruct((n, LANES), jnp.int32), jax.ShapeDtypeStruct((n, LANES), F32),
                   jax.ShapeDtypeStruct((n_tiles, 1, LANES), jnp.int32)],
        input_output_aliases=aliases,
        compiler_params=_params("parallel"),
        name="resid_router",
    )(*args)


def _experts_kernel(te_ref, first_ref, rows_ref, base_ref, slo_ref, shi_ref, wslot_ref, nexte_ref, used_ref,
                    sstart_ref, slen_ref, ssrc_ref,
                    xs_hbm, wgu_hbm, bgu_ref, wdn_hbm, bdn_ref, ys_hbm,
                    xbuf, ybuf, wgu_f, wdn_f, wgu_b, wdn_b, in_sem, out_sem, w_sem,
                    *, layer, limit, alpha, col_chunk):
    del ys_hbm
    i = pl.program_id(0)
    used = used_ref[0]
    tm = xbuf.shape[1]

    def copy(src_rows, dst_rows, slot, inbound):
        if inbound:
            return pltpu.make_async_copy(xs_hbm.at[src_rows, :], xbuf.at[slot, dst_rows, :], in_sem.at[slot])
        return pltpu.make_async_copy(ybuf.at[slot, dst_rows, :], xs_hbm.at[src_rows, :], out_sem.at[slot])

    def piece_copies(tile_idx, slot, inbound, wait):
        if wait:
            rows = pl.ds(0, pl.multiple_of(rows_ref[tile_idx], BF16_ROWS))
            copy(rows, rows, slot, inbound).wait()
            return
        base = base_ref[tile_idx]

        def piece(s, c):
            first = sstart_ref[s] - base
            lo = jnp.maximum(first, 0)
            n_rows = pl.multiple_of(jnp.minimum(first + slen_ref[s], tm) - lo, BF16_ROWS)

            @pl.when(n_rows > 0)
            def _():
                src = pl.multiple_of(ssrc_ref[s] + (lo - first), BF16_ROWS)
                copy(pl.ds(src, n_rows), pl.ds(pl.multiple_of(lo, BF16_ROWS), n_rows), slot, inbound).start()
            return c
        lax.fori_loop(slo_ref[tile_idx], shi_ref[tile_idx], piece, 0)

    def weight_copies(e, slot):
        return (pltpu.make_async_copy(wgu_hbm.at[layer, e], wgu_f.at[slot], w_sem.at[slot]),
                pltpu.make_async_copy(wdn_hbm.at[layer, e], wdn_f.at[slot], w_sem.at[slot]))

    @pl.when(i == 0)
    def _():
        xbuf[...] = jnp.zeros_like(xbuf)
        for cp in weight_copies(te_ref[0], 0):
            cp.start()
        piece_copies(0, 0, True, False)

    @pl.when(i < used)
    def _():
        slot = i % 2

        @pl.when(i + 1 < used)
        def _():
            piece_copies(i + 1, 1 - slot, True, False)

        @pl.when(first_ref[i] == 1)
        def _():
            ws = wslot_ref[i]
            for cp in weight_copies(te_ref[i], ws):
                cp.wait()

            @pl.when(nexte_ref[i] >= 0)
            def _():
                for cp in weight_copies(nexte_ref[i], 1 - ws):
                    cp.start()
            wgu_b[...] = wgu_f[ws].astype(BF16)
            wdn_b[...] = wdn_f[ws].astype(BF16)

        piece_copies(i, slot, True, True)

        @pl.when(i >= 2)
        def _():
            piece_copies(i - 2, slot, False, True)

        d_ff = wdn_b.shape[0]
        tm = xbuf.shape[1]
        e = te_ref[i]
        b_gu = bgu_ref[pl.ds(e, 1), :]
        b_dn = bdn_ref[pl.ds(e, 1), :]

        def mlp(n_rows):
            x = xbuf[slot, :n_rows, :]
            y = None
            for c0 in range(0, d_ff, col_chunk):
                cs = slice(c0, c0 + col_chunk)
                us = slice(d_ff + c0, d_ff + c0 + col_chunk)
                gate = jnp.minimum(_dot(x, wgu_b[:, cs]) + b_gu[:, cs], limit)
                up = jnp.clip(_dot(x, wgu_b[:, us]) + b_gu[:, us], -limit, limit)
                act = ((up + 1.0) * (gate * _sigmoid(alpha * gate))).astype(BF16)
                part = _dot(act, wdn_b[cs, :])
                y = part if y is None else y + part
            ybuf[slot, :n_rows, :] = (y + b_dn).astype(BF16)

        @pl.when(rows_ref[i] > tm // 2)
        def _():
            mlp(tm)

        @pl.when(rows_ref[i] <= tm // 2)
        def _():
            mlp(tm // 2)
        piece_copies(i, slot, False, False)

    @pl.when(i == pl.num_programs(0) - 1)
    def _():
        @pl.when(used >= 2)
        def _():
            piece_copies(used - 2, used % 2, False, True)
        piece_copies(used - 1, (used - 1) % 2, False, True)


def moe_experts(sorted_rows, w_gu, b_gu, w_dn, b_dn, tables, *, layer, row_tile, limit, alpha, col_chunk=512):
    n_rows, d = sorted_rows.shape
    d_gu = w_gu.shape[-1]
    d_ff = w_dn.shape[-2]
    n_tiles = tables[0].shape[0]
    vmem = lambda a: pl.BlockSpec(a.shape, lambda i, *_: (0,) * a.ndim)
    any_spec = pl.BlockSpec(memory_space=pl.ANY)
    return pl.pallas_call(
        functools.partial(_experts_kernel, layer=layer, limit=limit, alpha=alpha,
                          col_chunk=min(col_chunk, d_ff)),
        grid_spec=pltpu.PrefetchScalarGridSpec(
            num_scalar_prefetch=len(tables),
            grid=(n_tiles,),
            in_specs=[any_spec, any_spec, vmem(b_gu), any_spec, vmem(b_dn)],
            out_specs=any_spec,
            scratch_shapes=[
                pltpu.VMEM((2, row_tile, d), BF16), pltpu.VMEM((2, row_tile, d), BF16),
                pltpu.VMEM((2, d, d_gu), F32), pltpu.VMEM((2, d_ff, d), F32),
                pltpu.VMEM((d, d_gu), BF16), pltpu.VMEM((d_ff, d), BF16),
                pltpu.SemaphoreType.DMA((2,)), pltpu.SemaphoreType.DMA((2,)), pltpu.SemaphoreType.DMA((2,)),
            ],
        ),
        out_shape=jax.ShapeDtypeStruct((n_rows, d), BF16),
        input_output_aliases={len(tables): 0},
        compiler_params=_params("arbitrary"),
        name="moe_experts",
    )(*tables, sorted_rows, w_gu, b_gu, w_dn, b_dn)


def _expert_tables(cnt, n_sorted, n_tiles, row_tile):
    n_blocks, n_experts = cnt.shape

    def prefix_sum(a):
        n = a.shape[-1]
        upto = jnp.arange(n)[:, None] <= jnp.arange(n)[None, :]
        return jnp.sum(jnp.where(upto, a[..., :, None], 0), axis=-2)

    local_off = prefix_sum(cnt) - cnt
    seg_end = prefix_sum(cnt.T)
    seg_start = seg_end - cnt.T
    seg_src = jnp.arange(n_blocks, dtype=jnp.int32)[None, :] * n_sorted + local_off.T
    total = seg_end[:, -1]
    padded = (total + row_tile - 1) // row_tile * row_tile
    pad_end = prefix_sum(padded)
    pad_start = pad_end - padded
    tiles = jnp.arange(n_tiles, dtype=jnp.int32)
    experts = jnp.arange(n_experts, dtype=jnp.int32)
    n_used = pad_end[-1] // row_tile
    tile_expert = jnp.minimum(jnp.sum(tiles[:, None] * row_tile >= pad_end[None, :], axis=1), n_experts - 1)
    is_expert = tile_expert[:, None] == experts[None, :]

    def of_expert(a):
        if a.ndim == 1:
            return jnp.sum(jnp.where(is_expert, a[None, :], 0), axis=1)
        return jnp.sum(jnp.where(is_expert[:, :, None], a[None, :, :], 0), axis=1)

    live = tiles < n_used
    tile_base = tiles * row_tile - of_expert(pad_start)
    tile_rows = jnp.where(live, jnp.clip(of_expert(total) - tile_base, 0, row_tile), 0)
    tile_first = (live & (tile_base == 0)).astype(jnp.int32)
    seg_lo = jnp.sum(of_expert(seg_end) <= tile_base[:, None], axis=1)
    seg_hi = jnp.sum(of_expert(seg_start) < tile_base[:, None] + row_tile, axis=1)
    seg_lo = tile_expert * n_blocks + jnp.minimum(seg_lo, seg_hi)
    seg_hi = tile_expert * n_blocks + seg_hi
    owns = total > 0
    order = prefix_sum(owns.astype(jnp.int32)) - 1
    experts = jnp.arange(n_experts, dtype=jnp.int32)
    later = (experts[None, :] > experts[:, None]) & owns[None, :]
    next_expert = jnp.min(jnp.where(later, experts[None, :], n_experts), axis=1)
    next_expert = jnp.where(next_expert < n_experts, next_expert, -1)
    i32 = lambda a: a.astype(jnp.int32)
    return (i32(tile_expert), tile_first, i32(tile_rows), i32(tile_base), i32(seg_lo), i32(seg_hi),
            i32(of_expert(order) % 2), i32(of_expert(next_expert)), i32(n_used).reshape(1),
            i32(seg_start.reshape(-1)), i32(cnt.T.reshape(-1)), i32(seg_src.reshape(-1)))


def _combine_kernel(x_ref, pos_ref, gate_ref, gt_ref, gfin_ref, ys_ref, o_ref, *, top_k, final_norm, k_chunk):
    tile, d = x_ref.shape
    n_sorted = ys_ref.shape[0]
    pos = pos_ref[...].astype(F32)
    gates = gate_ref[...]
    acc = None
    for r0 in range(0, n_sorted, k_chunk):
        slot = (lax.broadcasted_iota(jnp.int32, (tile, k_chunk), 1) + r0).astype(F32)
        w = jnp.where(slot == pos[:, 0:1], gates[:, 0:1], 0.0)
        for k in range(1, top_k):
            w = w + jnp.where(slot == pos[:, k:k + 1], gates[:, k:k + 1], 0.0)
        part = _dot(w.astype(BF16), ys_ref[r0:r0 + k_chunk, :])
        acc = part if acc is None else acc + part
    out = x_ref[...] + _mod_rows(gt_ref, tile) * acc
    if final_norm:
        out = _rms_norm(out, gfin_ref[...])
    o_ref[...] = out


def moe_combine(x, ys, pos, gates, mod, g_final, *, tile, rows_per_seq, final_norm, block_offset, n_sorted):
    n, d = x.shape
    return pl.pallas_call(
        functools.partial(_combine_kernel, top_k=TOP_K, final_norm=final_norm, k_chunk=512),
        grid=(n // tile,),
        in_specs=[
            pl.BlockSpec((tile, d), lambda i: (i, 0)),
            pl.BlockSpec((tile, LANES), lambda i: (i, 0)),
            pl.BlockSpec((tile, LANES), lambda i: (i, 0)),
            _mod_spec(mod, 5, d, tile, rows_per_seq),
            pl.BlockSpec((1, d), lambda i: (0, 0)),
            pl.BlockSpec((n_sorted, d), lambda i: (i + block_offset, 0)),
        ],
        out_specs=pl.BlockSpec((tile, d), lambda i: (i, 0)),
        out_shape=jax.ShapeDtypeStruct((n, d), F32),
        compiler_params=_params("parallel"),
        name="moe_combine",
    )(x, pos, gates, mod, g_final.reshape(1, d), ys)


def _pool_groups(h, window_sum, counts, w_ref, scale):
    n_groups = w_ref.shape[0]
    dg = h.shape[-1] // n_groups
    outs = []
    for gi in range(n_groups):
        cols = slice(gi * dg, (gi + 1) * dg)
        pooled = window_sum(gi, cols) / counts[gi] - h[:, cols]
        outs.append(_dot(pooled.astype(BF16), w_ref[gi]))
    return jnp.concatenate(outs, axis=-1) * scale


def _pool_prompt_kernel(x_ref, sh_ref, sc_ref, g_ref, w_ref, scale_ref, y_ref, cache_ref, ext_ref,
                        *, windows, halo):
    tile, d = x_ref.shape[1:]
    j = pl.program_id(1)

    @pl.when(j == 0)
    def _():
        ext_ref[0:halo, :] = jnp.zeros((halo, d), F32)

    h = _rms_norm(x_ref[0], g_ref[...]) * (1.0 + sc_ref[...].reshape(-1, d)) + sh_ref[...].reshape(-1, d)
    ext_ref[halo:halo + tile, :] = h
    pos = j * tile + lax.broadcasted_iota(jnp.int32, (tile, 1), 0)

    def window_sum(gi, cols):
        acc = h[:, cols]
        for s in range(1, windows[gi]):
            acc = acc + ext_ref[halo - s:halo - s + tile, cols]
        return acc
    counts = [jnp.minimum(pos + 1, w).astype(F32) for w in windows]
    y_ref[0] = _pool_groups(h, window_sum, counts, w_ref, scale_ref[...])

    n_keep = cache_ref.shape[1]
    @pl.when(j == pl.num_programs(1) - 1)
    def _():
        cache_ref[0] = ext_ref[halo + tile - n_keep:halo + tile, :]
    ext_ref[0:halo, :] = ext_ref[tile:tile + halo, :]


def pool_mixer_prompt(x, mod, g, w_grp_bf16, scale, *, tile, windows, n_keep):
    bsz, t, d = x.shape
    halo = 16
    assert max(windows) <= halo <= tile and n_keep <= tile
    mod_spec = lambda k: pl.BlockSpec((1, 1, d), lambda b, j: (b, 0, k))
    return pl.pallas_call(
        functools.partial(_pool_prompt_kernel, windows=windows, halo=halo),
        grid=(bsz, t // tile),
        in_specs=[
            pl.BlockSpec((1, tile, d), lambda b, j: (b, j, 0)),
            mod_spec(0), mod_spec(1),
            pl.BlockSpec((1, d), lambda b, j: (0, 0)),
            pl.BlockSpec(w_grp_bf16.shape, lambda b, j: (0, 0, 0)),
            pl.BlockSpec((1, d), lambda b, j: (0, 0)),
        ],
        out_specs=[pl.BlockSpec((1, tile, d), lambda b, j: (b, j, 0)),
                   pl.BlockSpec((1, n_keep, d), lambda b, j: (b, 0, 0))],
        out_shape=[jax.ShapeDtypeStruct((bsz, t, d), F32), jax.ShapeDtypeStruct((bsz, n_keep, d), F32)],
        scratch_shapes=[pltpu.VMEM((halo + tile, d), F32)],
        compiler_params=_params("parallel", "arbitrary"),
        name="pool_mixer_prompt",
    )(x, mod, mod, g.reshape(1, d), w_grp_bf16, scale.reshape(1, d))


def _pool_sample_kernel(x_ref, buf_ref, sh_ref, sc_ref, g_ref, w_ref, scale_ref, y_ref, cache_ref,
                        *, windows, start_pos):
    t_len = x_ref.shape[0]
    n_prev = buf_ref.shape[0]
    hs = [_rms_norm(x_ref[t], g_ref[...]) * (1.0 + sc_ref[...]) + sh_ref[...] for t in range(t_len)]

    def ext(r):
        return buf_ref[r] if r < n_prev else hs[r - n_prev]

    for t in range(t_len):
        def window_sum(gi, cols):
            acc = hs[t][:, cols]
            for s in range(1, windows[gi]):
                acc = acc + ext(n_prev + t - s)[:, cols]
            return acc
        counts = [float(min(start_pos + t + 1, w)) for w in windows]
        y_ref[t] = _pool_groups(hs[t], window_sum, counts, w_ref, scale_ref[...])
    for r in range(n_prev):
        cache_ref[r] = ext(t_len + r)


def pool_mixer_sample(x_t, buf_t, mod, g, w_grp_bf16, scale, *, seq_block, windows, start_pos):
    t_len, n_seq, d = x_t.shape
    n_prev = buf_t.shape[0]
    assert start_pos >= n_prev >= max(windows) - 1
    mod_spec = lambda k: pl.BlockSpec((seq_block, d), lambda i: (i, k))
    return pl.pallas_call(
        functools.partial(_pool_sample_kernel, windows=windows, start_pos=start_pos),
        grid=(n_seq // seq_block,),
        in_specs=[
            pl.BlockSpec((t_len, seq_block, d), lambda i: (0, i, 0)),
            pl.BlockSpec((n_prev, seq_block, d), lambda i: (0, i, 0)),
            mod_spec(0), mod_spec(1),
            pl.BlockSpec((1, d), lambda i: (0, 0)),
            pl.BlockSpec(w_grp_bf16.shape, lambda i: (0, 0, 0)),
            pl.BlockSpec((1, d), lambda i: (0, 0)),
        ],
        out_specs=[pl.BlockSpec((t_len, seq_block, d), lambda i: (0, i, 0)),
                   pl.BlockSpec((n_prev, seq_block, d), lambda i: (0, i, 0))],
        out_shape=[jax.ShapeDtypeStruct((t_len, n_seq, d), F32), jax.ShapeDtypeStruct((n_prev, n_seq, d), F32)],
        compiler_params=_params("parallel"),
        name="pool_mixer_sample",
    )(x_t, buf_t, mod, mod, g.reshape(1, d), w_grp_bf16, scale.reshape(1, d))


TOP_K = 4
SWIGLU_LIMIT = 7.0
SWIGLU_ALPHA = 1.702
POOL_WINDOWS = (2, 4, 8, 16)
PAST_LEN = 16384
PROJ_TILE = 256
MOE_TOKEN_TILE = 512
MOE_ROW_TILE = 512
HGRN_TIME_BLOCK = 1024
HGRN_CHUNK = 128
HGRN_SUB = 32
SAMPLE_T_PAD = 8
SAMPLE_SEQ_BLOCK = 8


def kernel(x_prompt, x_sample, c_prompt, c_sample, state_hgrn, cache_pool, g_norm_mix, g_norm_ffn, w_ada, b_ada, w_in_hgrn, lb_logits, g_out_hgrn, w_out_hgrn, w_grp_pool, scale_pool, w_router, b_router, w_gate_up, b_gate_up, w_down, b_down, g_final):
    bp, tp, d = x_prompt.shape
    bs, ts, _ = x_sample.shape
    n_p, n_s = bp * tp, bs * ts
    n_experts = w_router.shape[-1]
    hk = w_out_hgrn.shape[1]
    assert n_s == MOE_TOKEN_TILE and n_p % MOE_TOKEN_TILE == 0
    blocks_p = n_p // MOE_TOKEN_TILE
    n_blocks = blocks_p + 1
    n_sorted = _sorted_rows(MOE_TOKEN_TILE, TOP_K, n_experts)
    n_row_tiles = -(-(n_blocks * n_sorted + n_experts * (MOE_ROW_TILE - BF16_ROWS)) // MOE_ROW_TILE)

    mod = adaln(jnp.concatenate([c_prompt, c_sample], axis=0), w_ada, b_ada)
    mod_p = [mod[l, :bp][:, None, :] for l in range(mod.shape[0])]
    mod_s = [mod[l, bp:] for l in range(mod.shape[0])]

    xp = x_prompt.reshape(n_p, d)
    xs = x_sample.transpose(1, 0, 2).reshape(n_s, d)

    def moe(layer, x_p, y_p, x_s, y_s, w_out, final_norm):
        route = functools.partial(resid_router, g=g_norm_ffn[layer], w_r=w_router[layer], b_r=b_router[layer],
                                  tile=MOE_TOKEN_TILE, top_k=TOP_K, n_blocks_total=n_blocks)
        x1_p, sorted_rows, pos_p, gate_p, cnt_p = route(
            x_p, y_p, w_out, mod_p[layer], sorted_in=None, rows_per_seq=tp, block_offset=0)
        x1_s, sorted_rows, pos_s, gate_s, cnt_s = route(
            x_s, y_s, w_out, mod_s[layer], sorted_in=sorted_rows, rows_per_seq=None, block_offset=blocks_p)
        cnt8 = jnp.concatenate([cnt_p, cnt_s], axis=0)[:, 0, :n_experts]
        tables = _expert_tables(cnt8, n_sorted, n_row_tiles, MOE_ROW_TILE)
        ys = moe_experts(sorted_rows, w_gate_up, b_gate_up[layer], w_down, b_down[layer], tables,
                         layer=layer, row_tile=MOE_ROW_TILE, limit=SWIGLU_LIMIT, alpha=SWIGLU_ALPHA)
        combine = functools.partial(moe_combine, ys=ys, g_final=g_final, tile=MOE_TOKEN_TILE,
                                    final_norm=final_norm, n_sorted=n_sorted)
        out_p = combine(x1_p, pos=pos_p, gates=gate_p, mod=mod_p[layer], rows_per_seq=tp, block_offset=0)
        out_s = combine(x1_s, pos=pos_s, gates=gate_s, mod=mod_s[layer], rows_per_seq=None,
                        block_offset=blocks_p)
        return out_p, out_s

    w_in = w_in_hgrn[0].astype(BF16)
    proj_s = norm_proj(xs, mod_s[0], g_norm_mix[0], w_in, tile=bs, rows_per_seq=None)
    o_p, state_p = hgrn_recurrence(None, lb_logits, g_out_hgrn[0], None,
                                   layer=0, seq_block=1, time_block=HGRN_TIME_BLOCK, chunk=HGRN_CHUNK,
                                   c_sub=HGRN_SUB, n_valid=HGRN_CHUNK,
                                   norm_proj_of=(x_prompt, mod_p[0], g_norm_mix[0], w_in))
    proj_sb = jnp.pad(proj_s.reshape(ts, bs, 4 * hk).transpose(1, 0, 2), ((0, 0), (0, SAMPLE_T_PAD - ts), (0, 0)))
    o_s, state_s = hgrn_recurrence(proj_sb, lb_logits, g_out_hgrn[0], state_hgrn[0],
                                   layer=0, seq_block=SAMPLE_SEQ_BLOCK, time_block=SAMPLE_T_PAD,
                                   chunk=SAMPLE_T_PAD, c_sub=SAMPLE_T_PAD, n_valid=ts)
    o_s = o_s[:, :ts].transpose(1, 0, 2).reshape(n_s, hk)
    x_p, x_s = moe(0, xp, o_p.reshape(n_p, hk), xs, o_s, w_out_hgrn[0].astype(BF16), False)

    w_grp = w_grp_pool[0].astype(BF16)
    n_keep = cache_pool.shape[2]
    y_p, cache_p = pool_mixer_prompt(x_p.reshape(bp, tp, d), mod_p[1], g_norm_mix[1], w_grp, scale_pool[0],
                                     tile=PROJ_TILE, windows=POOL_WINDOWS, n_keep=n_keep)
    y_s, cache_s = pool_mixer_sample(x_s.reshape(ts, bs, d), cache_pool[0].transpose(1, 0, 2), mod_s[1],
                                     g_norm_mix[1], w_grp, scale_pool[0],
                                     seq_block=32, windows=POOL_WINDOWS, start_pos=PAST_LEN)
    x_p, x_s = moe(1, x_p, y_p.reshape(n_p, d), x_s, y_s.reshape(n_s, d), None, True)

    return (x_p.reshape(bp, tp, d), x_s.reshape(ts, bs, d).transpose(1, 0, 2),
            state_p[None], state_s[None], cache_p[None], cache_s.transpose(1, 0, 2)[None])
```

```python
import functools

import jax
import jax.numpy as jnp
from jax import lax
from jax.experimental import pallas as pl
from jax.experimental.pallas import tpu as pltpu

F32 = jnp.float32
BF16 = jnp.bfloat16

RMS_EPS = 1e-6
LANES = 128
SUBLANES = 8
BF16_ROWS = 16
HEAD_DIM = 128
VMEM_LIMIT = 56 * 1024 * 1024

_dot = functools.partial(jnp.dot, preferred_element_type=F32)


def _params(*semantics):
    return pltpu.CompilerParams(dimension_semantics=semantics, vmem_limit_bytes=VMEM_LIMIT)


def _split_bf16(x, n):
    parts, r = [], x
    for _ in range(n):
        p = r.astype(BF16)
        parts.append(p)
        r = r - p.astype(F32)
    return parts


def _dot_hp(a, b):
    a_hi, a_lo = _split_bf16(a, 2)
    b_hi, b_lo = _split_bf16(b, 2)
    return _dot(a_hi, b_hi) + (_dot(a_hi, b_lo) + _dot(a_lo, b_hi))


def _sigmoid(x):
    return 1.0 / (1.0 + jnp.exp(-x))


def _silu(x):
    return x * _sigmoid(x)


def _rms_norm(x, g):
    ms = jnp.mean(x * x, axis=-1, keepdims=True)
    return x * lax.rsqrt(ms + RMS_EPS) * g


def _adaln_kernel(c_ref, w_ref, b_ref, o_ref):
    o_ref[0] = _dot_hp(_silu(c_ref[...]), w_ref[0]) + b_ref[0]


def adaln(c_all, w_ada, b_ada, *, col_block=1536):
    n_seq, d = c_all.shape
    n_layers, _, d6 = w_ada.shape
    return pl.pallas_call(
        _adaln_kernel,
        grid=(n_layers, d6 // col_block),
        in_specs=[
            pl.BlockSpec((n_seq, d), lambda l, j: (0, 0)),
            pl.BlockSpec((1, d, col_block), lambda l, j: (l, 0, j)),
            pl.BlockSpec((1, 1, col_block), lambda l, j: (l, 0, j)),
        ],
        out_specs=pl.BlockSpec((1, n_seq, col_block), lambda l, j: (l, 0, j)),
        out_shape=jax.ShapeDtypeStruct((n_layers, n_seq, d6), F32),
        compiler_params=_params("parallel", "parallel"),
        name="adaln",
    )(c_all, w_ada, b_ada.reshape(n_layers, 1, d6))


def _mod_spec(mod, k, d, tile, rows_per_seq):
    if rows_per_seq is None:
        return pl.BlockSpec((mod.shape[0], d), lambda i: (0, k))
    tiles_per_seq = rows_per_seq // tile
    return pl.BlockSpec((1, 1, d), lambda i: (i // tiles_per_seq, 0, k))


def _mod_rows(ref, tile):
    m = ref[...].reshape(-1, ref.shape[-1])
    if m.shape[0] not in (1, tile):
        m = jnp.concatenate([m] * (tile // m.shape[0]), axis=0)
    return m


def _norm_proj_kernel(x_ref, sh_ref, sc_ref, g_ref, w_ref, o_ref):
    tile = x_ref.shape[0]
    h = _rms_norm(x_ref[...], g_ref[...]) * (1.0 + _mod_rows(sc_ref, tile)) + _mod_rows(sh_ref, tile)
    o_ref[...] = _dot(h.astype(BF16), w_ref[...])


def norm_proj(x, mod, g, w_bf16, *, tile, rows_per_seq):
    n, d = x.shape
    p = w_bf16.shape[1]
    return pl.pallas_call(
        _norm_proj_kernel,
        grid=(n // tile,),
        in_specs=[
            pl.BlockSpec((tile, d), lambda i: (i, 0)),
            _mod_spec(mod, 0, d, tile, rows_per_seq),
            _mod_spec(mod, 1, d, tile, rows_per_seq),
            pl.BlockSpec((1, d), lambda i: (0, 0)),
            pl.BlockSpec((d, p), lambda i: (0, 0)),
        ],
        out_specs=pl.BlockSpec((tile, p), lambda i: (i, 0)),
        out_shape=jax.ShapeDtypeStruct((n, p), F32),
        compiler_params=_params("parallel"),
        name="hgrn_norm_proj",
    )(x, mod, mod, g.reshape(1, d), w_bf16)


def _cumsum_rows(x, tri):
    hi, mid, lo = _split_bf16(x, 3)
    return _dot(tri, hi) + (_dot(tri, mid) + _dot(tri, lo))


MAX_LOG_DECAY_RANGE = 80.0
MLP_ROW_STEPS = 4


def _hgrn_prep(proj, lb, n_valid):
    c = proj.shape[0]
    hk = proj.shape[1] // 4
    row = lax.broadcasted_iota(jnp.int32, (c, c), 0)
    col = lax.broadcasted_iota(jnp.int32, (c, c), 1)
    zf = proj[:, hk:2 * hk]
    e = jnp.exp(-jnp.abs(zf))
    r = 1.0 / (1.0 + e)
    pos = zf >= 0
    sig_p = jnp.where(pos, 1.0, e) * r
    sig_n = jnp.where(pos, e, 1.0) * r
    logf = jnp.log(lb + (1.0 - lb) * sig_p)
    k = (1.0 - lb) * sig_n
    if n_valid < c:
        live = lax.broadcasted_iota(jnp.int32, (c, 1), 0) < n_valid
        logf = jnp.where(live, logf, 0.0)
        k = jnp.where(live, k, 0.0)
    b = _cumsum_rows(logf, (row >= col).astype(BF16))
    return _silu(proj[:, :hk]), k, proj[:, 2 * hk:3 * hk], _silu(proj[:, 3 * hk:]), b


def _decay_range(b, c_sub):
    c = b.shape[0]
    worst = None
    for i in range(c // c_sub):
        span = b[i * c_sub:i * c_sub + 1, :] - b[(i + 1) * c_sub - 1:(i + 1) * c_sub, :]
        worst = span if worst is None else jnp.maximum(worst, span)
    return jnp.max(worst)


def _head_norm_gate(o, gate, gout):
    outs = []
    for h in range(o.shape[1] // HEAD_DIM):
        hs = slice(h * HEAD_DIM, (h + 1) * HEAD_DIM)
        oh = o[:, hs]
        outs.append(oh * lax.rsqrt(jnp.mean(oh * oh, axis=-1, keepdims=True) + RMS_EPS) * gout * gate[:, hs])
    return jnp.concatenate(outs, axis=-1)


def _hgrn_chunk(prep, gout, st_refs, seq, c_sub):
    q, k, v, gate, b = prep
    c = q.shape[0]
    n_heads = q.shape[1] // HEAD_DIM
    row = lax.broadcasted_iota(jnp.int32, (c, c), 0)
    col = lax.broadcasted_iota(jnp.int32, (c, c), 1)
    causal = row >= col
    n_sub = c // c_sub
    subs = [slice(i * c_sub, (i + 1) * c_sub) for i in range(n_sub)]

    intra, inter = [], []
    for h in range(n_heads):
        hs = slice(h * HEAD_DIM, (h + 1) * HEAD_DIM)
        bh, qh, kh, vh = b[:, hs], q[:, hs], k[:, hs], v[:, hs]
        vb = vh.astype(BF16)
        refs = [bh[i * c_sub + c_sub // 2:i * c_sub + c_sub // 2 + 1, :] for i in range(n_sub)]
        k_own = [kh[rs] * jnp.exp(jnp.minimum(ref - bh[rs], MAX_LOG_DECAY_RANGE))
                 for rs, ref in zip(subs, refs)]
        a_rows = []
        for i in range(n_sub):
            q_hat = (qh[subs[i]] * jnp.exp(bh[subs[i]] - refs[i])).astype(BF16)
            parts = [k_own[j] * jnp.exp(refs[i] - refs[j]) for j in range(i)] + [k_own[i]]
            parts += [jnp.zeros((c_sub, HEAD_DIM), F32)] * (n_sub - 1 - i)
            k_hat = (jnp.concatenate(parts, axis=0) if n_sub > 1 else parts[0]).astype(BF16)
            a_rows.append(lax.dot_general(q_hat, k_hat, (((1,), (1,)), ((), ())),
                                          preferred_element_type=F32))
        att = jnp.where(causal, jnp.concatenate(a_rows, axis=0) if len(a_rows) > 1 else a_rows[0], 0.0)
        intra.append(_dot(att.astype(BF16), vb))
        st = st_refs[seq, h]
        inter.append(lax.dot_general((qh * jnp.exp(bh)).astype(BF16), st.astype(BF16),
                                     (((1,), (1,)), ((), ())), preferred_element_type=F32))
        b_last = bh[c - 1:c, :]
        k_dec = (kh * jnp.exp(b_last - bh)).astype(BF16)
        st_refs[seq, h] = st * jnp.exp(b_last) + lax.dot_general(
            vb, k_dec, (((0,), (0,)), ((), ())), preferred_element_type=F32)
    inter = jnp.concatenate(inter, axis=-1)
    return _head_norm_gate(jnp.concatenate(intra, axis=-1) + inter, gate, gout), inter


def _hgrn_chunk_exact(prep, inter, gout, q_ref, b_ref, oi_ref):
    q, k, v, gate, b = prep
    c = q.shape[0]
    n_heads = q.shape[1] // HEAD_DIM
    q_ref[...] = q
    b_ref[...] = b
    key_row = lax.broadcasted_iota(jnp.int32, (c, 1), 0)

    def row_group(g, carry):
        rows = pl.ds(pl.multiple_of(g * SUBLANES, SUBLANES), SUBLANES)
        for h in range(n_heads):
            hs = slice(h * HEAD_DIM, (h + 1) * HEAD_DIM)
            q_g, b_g = q_ref[rows, hs], b_ref[rows, hs]
            o_rows = []
            for r in range(SUBLANES):
                decay = jnp.exp(jnp.minimum(b_g[r:r + 1] - b[:, hs], 0.0))
                score = jnp.sum(decay * k[:, hs] * q_g[r:r + 1], axis=-1, keepdims=True)
                score = jnp.where(key_row <= g * SUBLANES + r, score, 0.0)
                o_rows.append(jnp.sum(score * v[:, hs], axis=0, keepdims=True))
            oi_ref[rows, hs] = jnp.concatenate(o_rows, axis=0)
        return carry
    lax.fori_loop(0, c // SUBLANES, row_group, 0)
    return _head_norm_gate(oi_ref[...] + inter, gate, gout)


def _lower_bound(lb_logits, layer):
    e = jnp.exp(lb_logits - jnp.max(lb_logits, axis=0, keepdims=True))
    return jnp.sum(e[:layer + 1], axis=0, keepdims=True) / jnp.sum(e, axis=0, keepdims=True)


def _hgrn_rec_kernel(*refs, chunk, c_sub, n_valid, has_state, layer, fused_proj):
    refs = list(refs)
    if fused_proj:
        x_ref, sh_ref, sc_ref, g_ref, w_ref = refs[:5]
        del refs[:5]
        proj_ref = None
    else:
        proj_ref = refs.pop(0)
    lb_ref, gout_ref = refs[:2]
    del refs[:2]
    s0_ref = refs.pop(0) if has_state else None
    o_ref, sout_ref, st_ref, inter_ref, q_ref, b_ref, oi_ref = refs[:7]
    proj_buf = refs[7] if fused_proj else None
    bb, tb, _ = o_ref.shape
    n_heads = st_ref.shape[1]
    n_chunks = tb // chunk
    j = pl.program_id(1)

    @pl.when(j == 0)
    def _():
        if has_state:
            for s in range(bb):
                for h in range(n_heads):
                    st_ref[s, h] = s0_ref[s, h].T
        else:
            st_ref[...] = jnp.zeros_like(st_ref)

    lb = _lower_bound(lb_ref[...], layer)
    gout = gout_ref[...]

    def project(s, ci):
        rows = pl.ds(pl.multiple_of(ci * chunk, chunk), chunk)
        h = _rms_norm(x_ref[s, rows, :], g_ref[...]) * (1.0 + sc_ref[s]) + sh_ref[s]
        return _dot(h.astype(BF16), w_ref[...])

    if fused_proj:
        for s in range(bb):
            proj_buf[s, 0] = project(s, 0)

    def chunk_body(ci, carry):
        rows = pl.ds(pl.multiple_of(ci * chunk, chunk), chunk)

        def load_proj(s):
            return proj_buf[s, ci % 2] if fused_proj else proj_ref[s, rows, :]
        span = None
        for s in range(bb):
            prep = _hgrn_prep(load_proj(s), lb, n_valid)
            o_ref[s, rows, :], inter_ref[s] = _hgrn_chunk(prep, gout, st_ref, s, c_sub)
            worst = _decay_range(prep[4], c_sub)
            span = worst if span is None else jnp.maximum(span, worst)
            if fused_proj:
                proj_buf[s, (ci + 1) % 2] = project(s, jnp.minimum(ci + 1, n_chunks - 1))

        @pl.when(jnp.logical_not(span <= MAX_LOG_DECAY_RANGE))
        def _():
            for s in range(bb):
                prep = _hgrn_prep(load_proj(s), lb, n_valid)
                o_ref[s, rows, :] = _hgrn_chunk_exact(prep, inter_ref[s], gout, q_ref, b_ref, oi_ref)
        return carry
    lax.fori_loop(0, n_chunks, chunk_body, 0)

    @pl.when(j == pl.num_programs(1) - 1)
    def _():
        for s in range(bb):
            for h in range(n_heads):
                sout_ref[s, h] = st_ref[s, h].T


def hgrn_recurrence(proj, lb_logits, g_out, s0, *, layer, seq_block, time_block, chunk, c_sub, n_valid,
                    norm_proj_of=None):
    fused = norm_proj_of is not None
    if fused:
        x, mod, g, w_in = norm_proj_of
        bsz, t, d = x.shape
        p = w_in.shape[1]
        mod_spec = lambda k: pl.BlockSpec((seq_block, 1, d), lambda i, j: (i, 0, k))
        in_specs = [pl.BlockSpec((seq_block, time_block, d), lambda i, j: (i, j, 0)), mod_spec(0), mod_spec(1),
                    pl.BlockSpec((1, d), lambda i, j: (0, 0)), pl.BlockSpec((d, p), lambda i, j: (0, 0))]
        args = [x, mod, mod, g.reshape(1, d), w_in]
    else:
        bsz, t, p = proj.shape
        in_specs = [pl.BlockSpec((seq_block, time_block, p), lambda i, j: (i, j, 0))]
        args = [proj]
    hk = p // 4
    n_heads = hk // HEAD_DIM
    has_state = s0 is not None
    st_shape = (seq_block, n_heads, HEAD_DIM, HEAD_DIM)
    st_spec = pl.BlockSpec(st_shape, lambda i, j: (i, 0, 0, 0))
    in_specs += [pl.BlockSpec(lb_logits.shape, lambda i, j: (0, 0)),
                 pl.BlockSpec((1, HEAD_DIM), lambda i, j: (0, 0))]
    args += [lb_logits, g_out.reshape(1, HEAD_DIM)]
    if has_state:
        in_specs.append(st_spec)
        args.append(s0)
    scratch = [pltpu.VMEM(st_shape, F32), pltpu.VMEM((seq_block, chunk, hk), F32)]
    scratch += [pltpu.VMEM((chunk, hk), F32)] * 3
    if fused:
        scratch.append(pltpu.VMEM((seq_block, 2, chunk, p), F32))
    return pl.pallas_call(
        functools.partial(_hgrn_rec_kernel, chunk=chunk, c_sub=c_sub, n_valid=n_valid,
                          has_state=has_state, layer=layer, fused_proj=fused),
        grid=(bsz // seq_block, t // time_block),
        in_specs=in_specs,
        out_specs=[pl.BlockSpec((seq_block, time_block, hk), lambda i, j: (i, j, 0)), st_spec],
        out_shape=[jax.ShapeDtypeStruct((bsz, t, hk), F32),
                   jax.ShapeDtypeStruct((bsz, n_heads, HEAD_DIM, HEAD_DIM), F32)],
        scratch_shapes=scratch,
        compiler_params=_params("parallel", "arbitrary"),
        name="hgrn_recurrence",
    )(*args)


def _sorted_rows(tile, top_k, n_experts):
    return tile * top_k + n_experts * BF16_ROWS


def _resid_router_kernel(*refs, top_k, n_experts, has_w_out, chained, row_chunk):
    refs = list(refs)
    x_ref, y_ref = refs[:2]
    del refs[:2]
    wo_ref = refs.pop(0) if has_w_out else None
    gt_ref, sh_ref, sc_ref, g_ref, wr_ref, br_ref = refs[:6]
    del refs[:6]
    if chained:
        refs.pop(0)
    x1_ref, xs_ref, pos_ref, gate_ref, cnt_ref = refs
    tile, d = x_ref.shape
    n_sorted = xs_ref.shape[0]

    y = y_ref[...]
    if has_w_out:
        y = _dot(y.astype(BF16), wo_ref[...])
    x1 = x_ref[...] + _mod_rows(gt_ref, tile) * y
    x1_ref[...] = x1
    h = _rms_norm(x1, g_ref[...]) * (1.0 + _mod_rows(sc_ref, tile)) + _mod_rows(sh_ref, tile)

    lane = lax.broadcasted_iota(jnp.int32, (tile, LANES), 1).astype(F32)
    logits = jnp.where(lane < n_experts, _dot_hp(h, wr_ref[...]) + br_ref[...], -jnp.inf)
    picks, vals = [], []
    for _ in range(top_k):
        m = jnp.max(logits, axis=-1, keepdims=True)
        pick = jnp.min(jnp.where(logits == m, lane, float(LANES)), axis=-1, keepdims=True)
        picks.append(pick)
        vals.append(m)
        logits = jnp.where(lane == pick, -jnp.inf, logits)
    exps = [jnp.exp(v - vals[0]) for v in vals]
    denom = exps[0]
    for e in exps[1:]:
        denom = denom + e

    onehots = [(lane == p).astype(F32) for p in picks]
    oh_sum = onehots[0]
    for oh in onehots[1:]:
        oh_sum = oh_sum + oh
    row = lax.broadcasted_iota(jnp.int32, (tile, tile), 0)
    col = lax.broadcasted_iota(jnp.int32, (tile, tile), 1)
    before = _dot((row > col).astype(BF16), oh_sum.astype(BF16))
    count = jnp.sum(oh_sum, axis=0, keepdims=True)
    cnt_pad = jnp.floor((count + (BF16_ROWS - 1.0)) * (1.0 / BF16_ROWS)) * BF16_ROWS
    lane8 = lax.broadcasted_iota(jnp.int32, (SUBLANES, LANES), 1)
    run = jnp.broadcast_to(cnt_pad, (SUBLANES, LANES))
    shift = 1
    while shift < n_experts:
        run = run + jnp.where(lane8 >= shift, pltpu.roll(run, shift, 1), 0.0)
        shift *= 2
    pos = before + (run[0:1] - cnt_pad)
    pos_out = jnp.zeros((tile, LANES), F32)
    gate_out = jnp.zeros((tile, LANES), F32)
    for k in range(top_k):
        pos_k = jnp.sum(onehots[k] * pos, axis=-1, keepdims=True)
        pos_out = jnp.where(lane == k, pos_k, pos_out)
        gate_out = jnp.where(lane == k, exps[k] / denom, gate_out)
    pos_ref[...] = pos_out.astype(jnp.int32)
    gate_ref[...] = gate_out
    cnt_ref[0] = cnt_pad.astype(jnp.int32)

    pos_t = pos_out.T
    hb = h.astype(BF16)
    for r0 in range(0, n_sorted, row_chunk):
        slot = (lax.broadcasted_iota(jnp.int32, (row_chunk, tile), 0) + r0).astype(F32)
        sel = jnp.where(slot == pos_t[0:1], 1.0, 0.0)
        for k in range(1, top_k):
            sel = sel + jnp.where(slot == pos_t[k:k + 1], 1.0, 0.0)
        xs_ref[r0:r0 + row_chunk, :] = _dot(sel.astype(BF16), hb).astype(BF16)


def resid_router(x, y, w_out_bf16, mod, g, w_r, b_r, sorted_in, *, tile, rows_per_seq, top_k,
                 block_offset, n_blocks_total):
    n, d = x.shape
    n_experts = w_r.shape[1]
    n_sorted = _sorted_rows(tile, top_k, n_experts)
    w_r_pad = jnp.pad(w_r, ((0, 0), (0, LANES - n_experts)))
    b_r_pad = jnp.pad(b_r, (0, LANES - n_experts)).reshape(1, LANES)
    has_w_out = w_out_bf16 is not None
    chained = sorted_in is not None
    row_spec = pl.BlockSpec((tile, d), lambda i: (i, 0))
    lane_spec = pl.BlockSpec((tile, LANES), lambda i: (i, 0))
    full = lambda a: pl.BlockSpec(a.shape, lambda i: (0,) * a.ndim)
    in_specs = [row_spec, pl.BlockSpec((tile, y.shape[1]), lambda i: (i, 0))]
    args = [x, y]
    if has_w_out:
        in_specs.append(full(w_out_bf16))
        args.append(w_out_bf16)
    in_specs += [_mod_spec(mod, 2, d, tile, rows_per_seq), _mod_spec(mod, 3, d, tile, rows_per_seq),
                 _mod_spec(mod, 4, d, tile, rows_per_seq), pl.BlockSpec((1, d), lambda i: (0, 0)),
                 full(w_r_pad), full(b_r_pad)]
    args += [mod, mod, mod, g.reshape(1, d), w_r_pad, b_r_pad]
    aliases = {}
    if chained:
        aliases = {len(args): 1}
        in_specs.append(pl.BlockSpec(memory_space=pl.ANY))
        args.append(sorted_in)
    n_tiles = n // tile
    return pl.pallas_call(
        functools.partial(_resid_router_kernel, top_k=top_k, n_experts=n_experts, has_w_out=has_w_out,
                          chained=chained, row_chunk=256),
        grid=(n_tiles,),
        in_specs=in_specs,
        out_specs=[row_spec,
                   pl.BlockSpec((n_sorted, d), lambda i: (i + block_offset, 0)),
                   lane_spec, lane_spec,
                   pl.BlockSpec((1, 1, LANES), lambda i: (i, 0, 0))],
        out_shape=[jax.ShapeDtypeStruct((n, d), F32),
                   jax.ShapeDtypeStruct((n_blocks_total * n_sorted, d), BF16),
                   jax.ShapeDtypeStruct((n, LANES), jnp.int32), jax.ShapeDtypeStruct((n, LANES), F32),
                   jax.ShapeDtypeStruct((n_tiles, 1, LANES), jnp.int32)],
        input_output_aliases=aliases,
        compiler_params=_params("parallel"),
        name="resid_router",
    )(*args)


def _experts_kernel(te_ref, first_ref, rows_ref, base_ref, slo_ref, shi_ref, wslot_ref, nexte_ref, used_ref,
                    sstart_ref, slen_ref, ssrc_ref,
                    xs_hbm, wgu_hbm, bgu_ref, wdn_hbm, bdn_ref, ys_hbm,
                    xbuf, ybuf, wgu_f, wdn_f, wgu_b, wdn_b, in_sem, out_sem, w_sem,
                    *, layer, limit, alpha, col_chunk):
    del ys_hbm
    i = pl.program_id(0)
    used = used_ref[0]
    tm = xbuf.shape[1]

    def copy(src_rows, dst_rows, slot, inbound):
        if inbound:
            return pltpu.make_async_copy(xs_hbm.at[src_rows, :], xbuf.at[slot, dst_rows, :], in_sem.at[slot])
        return pltpu.make_async_copy(ybuf.at[slot, dst_rows, :], xs_hbm.at[src_rows, :], out_sem.at[slot])

    def piece_copies(tile_idx, slot, inbound, wait):
        if wait:
            rows = pl.ds(0, pl.multiple_of(rows_ref[tile_idx], BF16_ROWS))
            copy(rows, rows, slot, inbound).wait()
            return
        base = base_ref[tile_idx]

        def piece(s, c):
            first = sstart_ref[s] - base
            lo = jnp.maximum(first, 0)
            n_rows = pl.multiple_of(jnp.minimum(first + slen_ref[s], tm) - lo, BF16_ROWS)

            @pl.when(n_rows > 0)
            def _():
                src = pl.multiple_of(ssrc_ref[s] + (lo - first), BF16_ROWS)
                copy(pl.ds(src, n_rows), pl.ds(pl.multiple_of(lo, BF16_ROWS), n_rows), slot, inbound).start()
            return c
        lax.fori_loop(slo_ref[tile_idx], shi_ref[tile_idx], piece, 0)

    def weight_copies(e, slot):
        return (pltpu.make_async_copy(wgu_hbm.at[layer, e], wgu_f.at[slot], w_sem.at[slot]),
                pltpu.make_async_copy(wdn_hbm.at[layer, e], wdn_f.at[slot], w_sem.at[slot]))

    @pl.when(i == 0)
    def _():
        xbuf[...] = jnp.zeros_like(xbuf)
        for cp in weight_copies(te_ref[0], 0):
            cp.start()
        piece_copies(0, 0, True, False)

    @pl.when(i < used)
    def _():
        slot = i % 2

        @pl.when(i + 1 < used)
        def _():
            piece_copies(i + 1, 1 - slot, True, False)

        @pl.when(first_ref[i] == 1)
        def _():
            ws = wslot_ref[i]
            for cp in weight_copies(te_ref[i], ws):
                cp.wait()

            @pl.when(nexte_ref[i] >= 0)
            def _():
                for cp in weight_copies(nexte_ref[i], 1 - ws):
                    cp.start()
            wgu_b[...] = wgu_f[ws].astype(BF16)
            wdn_b[...] = wdn_f[ws].astype(BF16)

        piece_copies(i, slot, True, True)

        @pl.when(i >= 2)
        def _():
            piece_copies(i - 2, slot, False, True)

        d_ff = wdn_b.shape[0]
        tm = xbuf.shape[1]
        e = te_ref[i]
        b_gu = bgu_ref[pl.ds(e, 1), :]
        b_dn = bdn_ref[pl.ds(e, 1), :]

        def mlp(n_rows):
            x = xbuf[slot, :n_rows, :]
            y = None
            for c0 in range(0, d_ff, col_chunk):
                cs = slice(c0, c0 + col_chunk)
                us = slice(d_ff + c0, d_ff + c0 + col_chunk)
                gate = jnp.minimum(_dot(x, wgu_b[:, cs]) + b_gu[:, cs], limit)
                up = jnp.clip(_dot(x, wgu_b[:, us]) + b_gu[:, us], -limit, limit)
                act = ((up + 1.0) * (gate * _sigmoid(alpha * gate))).astype(BF16)
                part = _dot(act, wdn_b[cs, :])
                y = part if y is None else y + part
            ybuf[slot, :n_rows, :] = (y + b_dn).astype(BF16)

        step = tm // MLP_ROW_STEPS
        for part in range(1, MLP_ROW_STEPS + 1):
            @pl.when((rows_ref[i] > (part - 1) * step) & (rows_ref[i] <= part * step))
            def _():
                mlp(part * step)
        piece_copies(i, slot, False, False)

    @pl.when(i == pl.num_programs(0) - 1)
    def _():
        @pl.when(used >= 2)
        def _():
            piece_copies(used - 2, used % 2, False, True)
        piece_copies(used - 1, (used - 1) % 2, False, True)


def moe_experts(sorted_rows, w_gu, b_gu, w_dn, b_dn, tables, *, layer, row_tile, limit, alpha, col_chunk=512):
    n_rows, d = sorted_rows.shape
    d_gu = w_gu.shape[-1]
    d_ff = w_dn.shape[-2]
    n_tiles = tables[0].shape[0]
    vmem = lambda a: pl.BlockSpec(a.shape, lambda i, *_: (0,) * a.ndim)
    any_spec = pl.BlockSpec(memory_space=pl.ANY)
    return pl.pallas_call(
        functools.partial(_experts_kernel, layer=layer, limit=limit, alpha=alpha,
                          col_chunk=min(col_chunk, d_ff)),
        grid_spec=pltpu.PrefetchScalarGridSpec(
            num_scalar_prefetch=len(tables),
            grid=(n_tiles,),
            in_specs=[any_spec, any_spec, vmem(b_gu), any_spec, vmem(b_dn)],
            out_specs=any_spec,
            scratch_shapes=[
                pltpu.VMEM((2, row_tile, d), BF16), pltpu.VMEM((2, row_tile, d), BF16),
                pltpu.VMEM((2, d, d_gu), F32), pltpu.VMEM((2, d_ff, d), F32),
                pltpu.VMEM((d, d_gu), BF16), pltpu.VMEM((d_ff, d), BF16),
                pltpu.SemaphoreType.DMA((2,)), pltpu.SemaphoreType.DMA((2,)), pltpu.SemaphoreType.DMA((2,)),
            ],
        ),
        out_shape=jax.ShapeDtypeStruct((n_rows, d), BF16),
        input_output_aliases={len(tables): 0},
        compiler_params=_params("arbitrary"),
        name="moe_experts",
    )(*tables, sorted_rows, w_gu, b_gu, w_dn, b_dn)


def _expert_tables(cnt, n_sorted, n_tiles, row_tile):
    n_blocks, n_experts = cnt.shape

    def prefix_sum(a):
        n = a.shape[-1]
        upto = jnp.arange(n)[:, None] <= jnp.arange(n)[None, :]
        return jnp.sum(jnp.where(upto, a[..., :, None], 0), axis=-2)

    local_off = prefix_sum(cnt) - cnt
    seg_end = prefix_sum(cnt.T)
    seg_start = seg_end - cnt.T
    seg_src = jnp.arange(n_blocks, dtype=jnp.int32)[None, :] * n_sorted + local_off.T
    total = seg_end[:, -1]
    padded = (total + row_tile - 1) // row_tile * row_tile
    pad_end = prefix_sum(padded)
    pad_start = pad_end - padded
    tiles = jnp.arange(n_tiles, dtype=jnp.int32)
    experts = jnp.arange(n_experts, dtype=jnp.int32)
    n_used = pad_end[-1] // row_tile
    tile_expert = jnp.minimum(jnp.sum(tiles[:, None] * row_tile >= pad_end[None, :], axis=1), n_experts - 1)
    is_expert = tile_expert[:, None] == experts[None, :]

    def of_expert(a):
        if a.ndim == 1:
            return jnp.sum(jnp.where(is_expert, a[None, :], 0), axis=1)
        return jnp.sum(jnp.where(is_expert[:, :, None], a[None, :, :], 0), axis=1)

    live = tiles < n_used
    tile_base = tiles * row_tile - of_expert(pad_start)
    tile_rows = jnp.where(live, jnp.clip(of_expert(total) - tile_base, 0, row_tile), 0)
    tile_first = (live & (tile_base == 0)).astype(jnp.int32)
    seg_lo = jnp.sum(of_expert(seg_end) <= tile_base[:, None], axis=1)
    seg_hi = jnp.sum(of_expert(seg_start) < tile_base[:, None] + row_tile, axis=1)
    seg_lo = tile_expert * n_blocks + jnp.minimum(seg_lo, seg_hi)
    seg_hi = tile_expert * n_blocks + seg_hi
    owns = total > 0
    order = prefix_sum(owns.astype(jnp.int32)) - 1
    experts = jnp.arange(n_experts, dtype=jnp.int32)
    later = (experts[None, :] > experts[:, None]) & owns[None, :]
    next_expert = jnp.min(jnp.where(later, experts[None, :], n_experts), axis=1)
    next_expert = jnp.where(next_expert < n_experts, next_expert, -1)
    i32 = lambda a: a.astype(jnp.int32)
    return (i32(tile_expert), tile_first, i32(tile_rows), i32(tile_base), i32(seg_lo), i32(seg_hi),
            i32(of_expert(order) % 2), i32(of_expert(next_expert)), i32(n_used).reshape(1),
            i32(seg_start.reshape(-1)), i32(cnt.T.reshape(-1)), i32(seg_src.reshape(-1)))


def _combine_kernel(x_ref, pos_ref, gate_ref, gt_ref, gfin_ref, ys_ref, o_ref, *, top_k, final_norm, k_chunk):
    tile, d = x_ref.shape
    n_sorted = ys_ref.shape[0]
    pos = pos_ref[...].astype(F32)
    gates = gate_ref[...]
    acc = None
    for r0 in range(0, n_sorted, k_chunk):
        slot = (lax.broadcasted_iota(jnp.int32, (tile, k_chunk), 1) + r0).astype(F32)
        w = jnp.where(slot == pos[:, 0:1], gates[:, 0:1], 0.0)
        for k in range(1, top_k):
            w = w + jnp.where(slot == pos[:, k:k + 1], gates[:, k:k + 1], 0.0)
        part = _dot(w.astype(BF16), ys_ref[r0:r0 + k_chunk, :])
        acc = part if acc is None else acc + part
    out = x_ref[...] + _mod_rows(gt_ref, tile) * acc
    if final_norm:
        out = _rms_norm(out, gfin_ref[...])
    o_ref[...] = out


def moe_combine(x, ys, pos, gates, mod, g_final, *, tile, rows_per_seq, final_norm, block_offset, n_sorted):
    n, d = x.shape
    return pl.pallas_call(
        functools.partial(_combine_kernel, top_k=TOP_K, final_norm=final_norm, k_chunk=512),
        grid=(n // tile,),
        in_specs=[
            pl.BlockSpec((tile, d), lambda i: (i, 0)),
            pl.BlockSpec((tile, LANES), lambda i: (i, 0)),
            pl.BlockSpec((tile, LANES), lambda i: (i, 0)),
            _mod_spec(mod, 5, d, tile, rows_per_seq),
            pl.BlockSpec((1, d), lambda i: (0, 0)),
            pl.BlockSpec((n_sorted, d), lambda i: (i + block_offset, 0)),
        ],
        out_specs=pl.BlockSpec((tile, d), lambda i: (i, 0)),
        out_shape=jax.ShapeDtypeStruct((n, d), F32),
        compiler_params=_params("parallel"),
        name="moe_combine",
    )(x, pos, gates, mod, g_final.reshape(1, d), ys)


def _pool_groups(h, window_sum, counts, w_ref, scale):
    n_groups = w_ref.shape[0]
    dg = h.shape[-1] // n_groups
    outs = []
    for gi in range(n_groups):
        cols = slice(gi * dg, (gi + 1) * dg)
        pooled = window_sum(gi, cols) / counts[gi] - h[:, cols]
        outs.append(_dot(pooled.astype(BF16), w_ref[gi]))
    return jnp.concatenate(outs, axis=-1) * scale


def _pool_prompt_kernel(x_ref, sh_ref, sc_ref, g_ref, w_ref, scale_ref, y_ref, cache_ref, ext_ref,
                        *, windows, halo):
    tile, d = x_ref.shape[1:]
    j = pl.program_id(1)

    @pl.when(j == 0)
    def _():
        ext_ref[0:halo, :] = jnp.zeros((halo, d), F32)

    h = _rms_norm(x_ref[0], g_ref[...]) * (1.0 + sc_ref[...].reshape(-1, d)) + sh_ref[...].reshape(-1, d)
    ext_ref[halo:halo + tile, :] = h
    pos = j * tile + lax.broadcasted_iota(jnp.int32, (tile, 1), 0)

    def window_sum(gi, cols):
        acc = h[:, cols]
        for s in range(1, windows[gi]):
            acc = acc + ext_ref[halo - s:halo - s + tile, cols]
        return acc
    counts = [jnp.minimum(pos + 1, w).astype(F32) for w in windows]
    y_ref[0] = _pool_groups(h, window_sum, counts, w_ref, scale_ref[...])

    n_keep = cache_ref.shape[1]
    @pl.when(j == pl.num_programs(1) - 1)
    def _():
        cache_ref[0] = ext_ref[halo + tile - n_keep:halo + tile, :]
    ext_ref[0:halo, :] = ext_ref[tile:tile + halo, :]


def pool_mixer_prompt(x, mod, g, w_grp_bf16, scale, *, tile, windows, n_keep):
    bsz, t, d = x.shape
    halo = 16
    assert max(windows) <= halo <= tile and n_keep <= tile
    mod_spec = lambda k: pl.BlockSpec((1, 1, d), lambda b, j: (b, 0, k))
    return pl.pallas_call(
        functools.partial(_pool_prompt_kernel, windows=windows, halo=halo),
        grid=(bsz, t // tile),
        in_specs=[
            pl.BlockSpec((1, tile, d), lambda b, j: (b, j, 0)),
            mod_spec(0), mod_spec(1),
            pl.BlockSpec((1, d), lambda b, j: (0, 0)),
            pl.BlockSpec(w_grp_bf16.shape, lambda b, j: (0, 0, 0)),
            pl.BlockSpec((1, d), lambda b, j: (0, 0)),
        ],
        out_specs=[pl.BlockSpec((1, tile, d), lambda b, j: (b, j, 0)),
                   pl.BlockSpec((1, n_keep, d), lambda b, j: (b, 0, 0))],
        out_shape=[jax.ShapeDtypeStruct((bsz, t, d), F32), jax.ShapeDtypeStruct((bsz, n_keep, d), F32)],
        scratch_shapes=[pltpu.VMEM((halo + tile, d), F32)],
        compiler_params=_params("parallel", "arbitrary"),
        name="pool_mixer_prompt",
    )(x, mod, mod, g.reshape(1, d), w_grp_bf16, scale.reshape(1, d))


def _pool_sample_kernel(x_ref, buf_ref, sh_ref, sc_ref, g_ref, w_ref, scale_ref, y_ref, cache_ref,
                        *, windows, start_pos):
    t_len = x_ref.shape[0]
    n_prev = buf_ref.shape[0]
    hs = [_rms_norm(x_ref[t], g_ref[...]) * (1.0 + sc_ref[...]) + sh_ref[...] for t in range(t_len)]

    def ext(r):
        return buf_ref[r] if r < n_prev else hs[r - n_prev]

    for t in range(t_len):
        def window_sum(gi, cols):
            acc = hs[t][:, cols]
            for s in range(1, windows[gi]):
                acc = acc + ext(n_prev + t - s)[:, cols]
            return acc
        counts = [float(min(start_pos + t + 1, w)) for w in windows]
        y_ref[t] = _pool_groups(hs[t], window_sum, counts, w_ref, scale_ref[...])
    for r in range(n_prev):
        cache_ref[r] = ext(t_len + r)


def pool_mixer_sample(x_t, buf_t, mod, g, w_grp_bf16, scale, *, seq_block, windows, start_pos):
    t_len, n_seq, d = x_t.shape
    n_prev = buf_t.shape[0]
    assert start_pos >= n_prev >= max(windows) - 1
    mod_spec = lambda k: pl.BlockSpec((seq_block, d), lambda i: (i, k))
    return pl.pallas_call(
        functools.partial(_pool_sample_kernel, windows=windows, start_pos=start_pos),
        grid=(n_seq // seq_block,),
        in_specs=[
            pl.BlockSpec((t_len, seq_block, d), lambda i: (0, i, 0)),
            pl.BlockSpec((n_prev, seq_block, d), lambda i: (0, i, 0)),
            mod_spec(0), mod_spec(1),
            pl.BlockSpec((1, d), lambda i: (0, 0)),
            pl.BlockSpec(w_grp_bf16.shape, lambda i: (0, 0, 0)),
            pl.BlockSpec((1, d), lambda i: (0, 0)),
        ],
        out_specs=[pl.BlockSpec((t_len, seq_block, d), lambda i: (0, i, 0)),
                   pl.BlockSpec((n_prev, seq_block, d), lambda i: (0, i, 0))],
        out_shape=[jax.ShapeDtypeStruct((t_len, n_seq, d), F32), jax.ShapeDtypeStruct((n_prev, n_seq, d), F32)],
        compiler_params=_params("parallel"),
        name="pool_mixer_sample",
    )(x_t, buf_t, mod, mod, g.reshape(1, d), w_grp_bf16, scale.reshape(1, d))


TOP_K = 4
SWIGLU_LIMIT = 7.0
SWIGLU_ALPHA = 1.702
POOL_WINDOWS = (2, 4, 8, 16)
PAST_LEN = 16384
PROJ_TILE = 256
MOE_TOKEN_TILE = 512
MOE_ROW_TILE = 512
HGRN_TIME_BLOCK = 1024
HGRN_CHUNK = 256
HGRN_SUB = 32
SAMPLE_T_PAD = 8
SAMPLE_SEQ_BLOCK = 8


def kernel(x_prompt, x_sample, c_prompt, c_sample, state_hgrn, cache_pool, g_norm_mix, g_norm_ffn, w_ada, b_ada, w_in_hgrn, lb_logits, g_out_hgrn, w_out_hgrn, w_grp_pool, scale_pool, w_router, b_router, w_gate_up, b_gate_up, w_down, b_down, g_final):
    bp, tp, d = x_prompt.shape
    bs, ts, _ = x_sample.shape
    n_p, n_s = bp * tp, bs * ts
    n_experts = w_router.shape[-1]
    hk = w_out_hgrn.shape[1]
    assert n_s == MOE_TOKEN_TILE and n_p % MOE_TOKEN_TILE == 0
    blocks_p = n_p // MOE_TOKEN_TILE
    n_blocks = blocks_p + 1
    n_sorted = _sorted_rows(MOE_TOKEN_TILE, TOP_K, n_experts)
    n_row_tiles = -(-(n_blocks * n_sorted + n_experts * (MOE_ROW_TILE - BF16_ROWS)) // MOE_ROW_TILE)

    mod = adaln(jnp.concatenate([c_prompt, c_sample], axis=0), w_ada, b_ada)
    mod_p = [mod[l, :bp][:, None, :] for l in range(mod.shape[0])]
    mod_s = [mod[l, bp:] for l in range(mod.shape[0])]

    xp = x_prompt.reshape(n_p, d)
    xs = x_sample.transpose(1, 0, 2).reshape(n_s, d)

    def moe(layer, x_p, y_p, x_s, y_s, w_out, final_norm):
        route = functools.partial(resid_router, g=g_norm_ffn[layer], w_r=w_router[layer], b_r=b_router[layer],
                                  tile=MOE_TOKEN_TILE, top_k=TOP_K, n_blocks_total=n_blocks)
        x1_p, sorted_rows, pos_p, gate_p, cnt_p = route(
            x_p, y_p, w_out, mod_p[layer], sorted_in=None, rows_per_seq=tp, block_offset=0)
        x1_s, sorted_rows, pos_s, gate_s, cnt_s = route(
            x_s, y_s, w_out, mod_s[layer], sorted_in=sorted_rows, rows_per_seq=None, block_offset=blocks_p)
        cnt8 = jnp.concatenate([cnt_p, cnt_s], axis=0)[:, 0, :n_experts]
        tables = _expert_tables(cnt8, n_sorted, n_row_tiles, MOE_ROW_TILE)
        ys = moe_experts(sorted_rows, w_gate_up, b_gate_up[layer], w_down, b_down[layer], tables,
                         layer=layer, row_tile=MOE_ROW_TILE, limit=SWIGLU_LIMIT, alpha=SWIGLU_ALPHA)
        combine = functools.partial(moe_combine, ys=ys, g_final=g_final, tile=MOE_TOKEN_TILE,
                                    final_norm=final_norm, n_sorted=n_sorted)
        out_p = combine(x1_p, pos=pos_p, gates=gate_p, mod=mod_p[layer], rows_per_seq=tp, block_offset=0)
        out_s = combine(x1_s, pos=pos_s, gates=gate_s, mod=mod_s[layer], rows_per_seq=None,
                        block_offset=blocks_p)
        return out_p, out_s

    w_in = w_in_hgrn[0].astype(BF16)
    proj_s = norm_proj(xs, mod_s[0], g_norm_mix[0], w_in, tile=bs, rows_per_seq=None)
    o_p, state_p = hgrn_recurrence(None, lb_logits, g_out_hgrn[0], None,
                                   layer=0, seq_block=1, time_block=HGRN_TIME_BLOCK, chunk=HGRN_CHUNK,
                                   c_sub=HGRN_SUB, n_valid=HGRN_CHUNK,
                                   norm_proj_of=(x_prompt, mod_p[0], g_norm_mix[0], w_in))
    proj_sb = jnp.pad(proj_s.reshape(ts, bs, 4 * hk).transpose(1, 0, 2), ((0, 0), (0, SAMPLE_T_PAD - ts), (0, 0)))
    o_s, state_s = hgrn_recurrence(proj_sb, lb_logits, g_out_hgrn[0], state_hgrn[0],
                                   layer=0, seq_block=SAMPLE_SEQ_BLOCK, time_block=SAMPLE_T_PAD,
                                   chunk=SAMPLE_T_PAD, c_sub=SAMPLE_T_PAD, n_valid=ts)
    o_s = o_s[:, :ts].transpose(1, 0, 2).reshape(n_s, hk)
    x_p, x_s = moe(0, xp, o_p.reshape(n_p, hk), xs, o_s, w_out_hgrn[0].astype(BF16), False)

    w_grp = w_grp_pool[0].astype(BF16)
    n_keep = cache_pool.shape[2]
    y_p, cache_p = pool_mixer_prompt(x_p.reshape(bp, tp, d), mod_p[1], g_norm_mix[1], w_grp, scale_pool[0],
                                     tile=PROJ_TILE, windows=POOL_WINDOWS, n_keep=n_keep)
    y_s, cache_s = pool_mixer_sample(x_s.reshape(ts, bs, d), cache_pool[0].transpose(1, 0, 2), mod_s[1],
                                     g_norm_mix[1], w_grp, scale_pool[0],
                                     seq_block=32, windows=POOL_WINDOWS, start_pos=PAST_LEN)
    x_p, x_s = moe(1, x_p, y_p.reshape(n_p, d), x_s, y_s.reshape(n_s, d), None, True)

    return (x_p.reshape(bp, tp, d), x_s.reshape(ts, bs, d).transpose(1, 0, 2),
            state_p[None], state_s[None], cache_p[None], cache_s.transpose(1, 0, 2)[None])
```

```python
import functools

import jax
import jax.numpy as jnp
from jax import lax
from jax.experimental import pallas as pl
from jax.experimental.pallas import tpu as pltpu

F32 = jnp.float32
BF16 = jnp.bfloat16

RMS_EPS = 1e-6
LANES = 128
SUBLANES = 8
BF16_ROWS = 16
HEAD_DIM = 128
VMEM_LIMIT = 56 * 1024 * 1024

_dot = functools.partial(jnp.dot, preferred_element_type=F32)


def _params(*semantics):
    return pltpu.CompilerParams(dimension_semantics=semantics, vmem_limit_bytes=VMEM_LIMIT)


def _split_bf16(x, n):
    parts, r = [], x
    for _ in range(n):
        p = r.astype(BF16)
        parts.append(p)
        r = r - p.astype(F32)
    return parts


def _dot_hp(a, b):
    a_hi, a_lo = _split_bf16(a, 2)
    b_hi, b_lo = _split_bf16(b, 2)
    return _dot(a_hi, b_hi) + (_dot(a_hi, b_lo) + _dot(a_lo, b_hi))


def _sigmoid(x):
    return 1.0 / (1.0 + jnp.exp(-x))


def _silu(x):
    return x * _sigmoid(x)


def _rms_norm(x, g):
    ms = jnp.mean(x * x, axis=-1, keepdims=True)
    return x * lax.rsqrt(ms + RMS_EPS) * g


def _adaln_kernel(c_ref, w_ref, b_ref, o_ref):
    o_ref[0] = _dot_hp(_silu(c_ref[...]), w_ref[0]) + b_ref[0]


def adaln(c_all, w_ada, b_ada, *, col_block=1536):
    n_seq, d = c_all.shape
    n_layers, _, d6 = w_ada.shape
    return pl.pallas_call(
        _adaln_kernel,
        grid=(n_layers, d6 // col_block),
        in_specs=[
            pl.BlockSpec((n_seq, d), lambda l, j: (0, 0)),
            pl.BlockSpec((1, d, col_block), lambda l, j: (l, 0, j)),
            pl.BlockSpec((1, 1, col_block), lambda l, j: (l, 0, j)),
        ],
        out_specs=pl.BlockSpec((1, n_seq, col_block), lambda l, j: (l, 0, j)),
        out_shape=jax.ShapeDtypeStruct((n_layers, n_seq, d6), F32),
        compiler_params=_params("parallel", "parallel"),
        name="adaln",
    )(c_all, w_ada, b_ada.reshape(n_layers, 1, d6))


def _mod_spec(mod, k, d, tile, rows_per_seq, tile_of=lambda i: i):
    if rows_per_seq is None:
        return pl.BlockSpec((mod.shape[0], d), lambda i: (0, k))
    tiles_per_seq = rows_per_seq // tile
    return pl.BlockSpec((1, 1, d), lambda i: (tile_of(i) // tiles_per_seq, 0, k))


def _mod_rows(ref, tile):
    m = ref[...].reshape(-1, ref.shape[-1])
    if m.shape[0] not in (1, tile):
        m = jnp.concatenate([m] * (tile // m.shape[0]), axis=0)
    return m


def _norm_proj_kernel(x_ref, sh_ref, sc_ref, g_ref, w_ref, o_ref):
    tile = x_ref.shape[0]
    h = _rms_norm(x_ref[...], g_ref[...]) * (1.0 + _mod_rows(sc_ref, tile)) + _mod_rows(sh_ref, tile)
    o_ref[...] = _dot(h.astype(BF16), w_ref[...])


def norm_proj(x, mod, g, w_bf16, *, tile, rows_per_seq):
    n, d = x.shape
    p = w_bf16.shape[1]
    return pl.pallas_call(
        _norm_proj_kernel,
        grid=(n // tile,),
        in_specs=[
            pl.BlockSpec((tile, d), lambda i: (i, 0)),
            _mod_spec(mod, 0, d, tile, rows_per_seq),
            _mod_spec(mod, 1, d, tile, rows_per_seq),
            pl.BlockSpec((1, d), lambda i: (0, 0)),
            pl.BlockSpec((d, p), lambda i: (0, 0)),
        ],
        out_specs=pl.BlockSpec((tile, p), lambda i: (i, 0)),
        out_shape=jax.ShapeDtypeStruct((n, p), F32),
        compiler_params=_params("parallel"),
        name="hgrn_norm_proj",
    )(x, mod, mod, g.reshape(1, d), w_bf16)


def _cumsum_rows(x, tri):
    hi, mid, lo = _split_bf16(x, 3)
    return _dot(tri, hi) + (_dot(tri, mid) + _dot(tri, lo))


MAX_LOG_DECAY_RANGE = 80.0
MLP_ROW_STEPS = 4


def _hgrn_prep(proj, lb, n_valid):
    c = proj.shape[0]
    hk = proj.shape[1] // 4
    row = lax.broadcasted_iota(jnp.int32, (c, c), 0)
    col = lax.broadcasted_iota(jnp.int32, (c, c), 1)
    zf = proj[:, hk:2 * hk]
    e = jnp.exp(-jnp.abs(zf))
    r = 1.0 / (1.0 + e)
    pos = zf >= 0
    sig_p = jnp.where(pos, 1.0, e) * r
    sig_n = jnp.where(pos, e, 1.0) * r
    logf = jnp.log(lb + (1.0 - lb) * sig_p)
    k = (1.0 - lb) * sig_n
    if n_valid < c:
        live = lax.broadcasted_iota(jnp.int32, (c, 1), 0) < n_valid
        logf = jnp.where(live, logf, 0.0)
        k = jnp.where(live, k, 0.0)
    b = _cumsum_rows(logf, (row >= col).astype(BF16))
    return _silu(proj[:, :hk]), k, proj[:, 2 * hk:3 * hk], _silu(proj[:, 3 * hk:]), b


def _decay_range(b, c_sub):
    c = b.shape[0]
    worst = None
    for i in range(c // c_sub):
        span = b[i * c_sub:i * c_sub + 1, :] - b[(i + 1) * c_sub - 1:(i + 1) * c_sub, :]
        worst = span if worst is None else jnp.maximum(worst, span)
    return jnp.max(worst)


def _head_norm_gate(o, gate, gout):
    outs = []
    for h in range(o.shape[1] // HEAD_DIM):
        hs = slice(h * HEAD_DIM, (h + 1) * HEAD_DIM)
        oh = o[:, hs]
        outs.append(oh * lax.rsqrt(jnp.mean(oh * oh, axis=-1, keepdims=True) + RMS_EPS) * gout * gate[:, hs])
    return jnp.concatenate(outs, axis=-1)


def _hgrn_chunk(prep, gout, st_refs, seq, c_sub):
    q, k, v, gate, b = prep
    c = q.shape[0]
    n_heads = q.shape[1] // HEAD_DIM
    row = lax.broadcasted_iota(jnp.int32, (c, c), 0)
    col = lax.broadcasted_iota(jnp.int32, (c, c), 1)
    causal = row >= col
    n_sub = c // c_sub
    subs = [slice(i * c_sub, (i + 1) * c_sub) for i in range(n_sub)]

    intra, inter = [], []
    for h in range(n_heads):
        hs = slice(h * HEAD_DIM, (h + 1) * HEAD_DIM)
        bh, qh, kh, vh = b[:, hs], q[:, hs], k[:, hs], v[:, hs]
        vb = vh.astype(BF16)
        refs = [bh[i * c_sub + c_sub // 2:i * c_sub + c_sub // 2 + 1, :] for i in range(n_sub)]
        k_own = [kh[rs] * jnp.exp(jnp.minimum(ref - bh[rs], MAX_LOG_DECAY_RANGE))
                 for rs, ref in zip(subs, refs)]
        a_rows = []
        for i in range(n_sub):
            q_hat = (qh[subs[i]] * jnp.exp(bh[subs[i]] - refs[i])).astype(BF16)
            parts = [k_own[j] * jnp.exp(refs[i] - refs[j]) for j in range(i)] + [k_own[i]]
            parts += [jnp.zeros((c_sub, HEAD_DIM), F32)] * (n_sub - 1 - i)
            k_hat = (jnp.concatenate(parts, axis=0) if n_sub > 1 else parts[0]).astype(BF16)
            a_rows.append(lax.dot_general(q_hat, k_hat, (((1,), (1,)), ((), ())),
                                          preferred_element_type=F32))
        att = jnp.where(causal, jnp.concatenate(a_rows, axis=0) if len(a_rows) > 1 else a_rows[0], 0.0)
        intra.append(_dot(att.astype(BF16), vb))
        st = st_refs[seq, h]
        inter.append(lax.dot_general((qh * jnp.exp(bh)).astype(BF16), st.astype(BF16),
                                     (((1,), (1,)), ((), ())), preferred_element_type=F32))
        b_last = bh[c - 1:c, :]
        k_dec = (kh * jnp.exp(b_last - bh)).astype(BF16)
        st_refs[seq, h] = st * jnp.exp(b_last) + lax.dot_general(
            vb, k_dec, (((0,), (0,)), ((), ())), preferred_element_type=F32)
    inter = jnp.concatenate(inter, axis=-1)
    return _head_norm_gate(jnp.concatenate(intra, axis=-1) + inter, gate, gout), inter


def _hgrn_chunk_exact(prep, inter, gout, q_ref, b_ref, oi_ref):
    q, k, v, gate, b = prep
    c = q.shape[0]
    n_heads = q.shape[1] // HEAD_DIM
    q_ref[...] = q
    b_ref[...] = b
    key_row = lax.broadcasted_iota(jnp.int32, (c, 1), 0)

    def row_group(g, carry):
        rows = pl.ds(pl.multiple_of(g * SUBLANES, SUBLANES), SUBLANES)
        for h in range(n_heads):
            hs = slice(h * HEAD_DIM, (h + 1) * HEAD_DIM)
            q_g, b_g = q_ref[rows, hs], b_ref[rows, hs]
            o_rows = []
            for r in range(SUBLANES):
                decay = jnp.exp(jnp.minimum(b_g[r:r + 1] - b[:, hs], 0.0))
                score = jnp.sum(decay * k[:, hs] * q_g[r:r + 1], axis=-1, keepdims=True)
                score = jnp.where(key_row <= g * SUBLANES + r, score, 0.0)
                o_rows.append(jnp.sum(score * v[:, hs], axis=0, keepdims=True))
            oi_ref[rows, hs] = jnp.concatenate(o_rows, axis=0)
        return carry
    lax.fori_loop(0, c // SUBLANES, row_group, 0)
    return _head_norm_gate(oi_ref[...] + inter, gate, gout)


def _lower_bound(lb_logits, layer):
    e = jnp.exp(lb_logits - jnp.max(lb_logits, axis=0, keepdims=True))
    return jnp.sum(e[:layer + 1], axis=0, keepdims=True) / jnp.sum(e, axis=0, keepdims=True)


def _hgrn_rec_kernel(*refs, chunk, c_sub, n_valid, has_state, layer, fused_proj):
    refs = list(refs)
    if fused_proj:
        x_ref, sh_ref, sc_ref, g_ref, w_ref = refs[:5]
        del refs[:5]
        proj_ref = None
    else:
        proj_ref = refs.pop(0)
    lb_ref, gout_ref = refs[:2]
    del refs[:2]
    s0_ref = refs.pop(0) if has_state else None
    o_ref, sout_ref, st_ref, inter_ref, q_ref, b_ref, oi_ref = refs[:7]
    proj_buf = refs[7] if fused_proj else None
    bb, tb, _ = o_ref.shape
    n_heads = st_ref.shape[1]
    n_chunks = tb // chunk
    j = pl.program_id(1)

    @pl.when(j == 0)
    def _():
        if has_state:
            for s in range(bb):
                for h in range(n_heads):
                    st_ref[s, h] = s0_ref[s, h].T
        else:
            st_ref[...] = jnp.zeros_like(st_ref)

    lb = _lower_bound(lb_ref[...], layer)
    gout = gout_ref[...]

    def project(s, ci):
        rows = pl.ds(pl.multiple_of(ci * chunk, chunk), chunk)
        h = _rms_norm(x_ref[s, rows, :], g_ref[...]) * (1.0 + sc_ref[s]) + sh_ref[s]
        return _dot(h.astype(BF16), w_ref[...])

    if fused_proj:
        for s in range(bb):
            proj_buf[s, 0] = project(s, 0)

    def chunk_body(ci, carry):
        rows = pl.ds(pl.multiple_of(ci * chunk, chunk), chunk)

        def load_proj(s):
            return proj_buf[s, ci % 2] if fused_proj else proj_ref[s, rows, :]
        span = None
        for s in range(bb):
            prep = _hgrn_prep(load_proj(s), lb, n_valid)
            o_ref[s, rows, :], inter_ref[s] = _hgrn_chunk(prep, gout, st_ref, s, c_sub)
            worst = _decay_range(prep[4], c_sub)
            span = worst if span is None else jnp.maximum(span, worst)
            if fused_proj:
                proj_buf[s, (ci + 1) % 2] = project(s, jnp.minimum(ci + 1, n_chunks - 1))

        @pl.when(jnp.logical_not(span <= MAX_LOG_DECAY_RANGE))
        def _():
            for s in range(bb):
                prep = _hgrn_prep(load_proj(s), lb, n_valid)
                o_ref[s, rows, :] = _hgrn_chunk_exact(prep, inter_ref[s], gout, q_ref, b_ref, oi_ref)
        return carry
    lax.fori_loop(0, n_chunks, chunk_body, 0)

    @pl.when(j == pl.num_programs(1) - 1)
    def _():
        for s in range(bb):
            for h in range(n_heads):
                sout_ref[s, h] = st_ref[s, h].T


def hgrn_recurrence(proj, lb_logits, g_out, s0, *, layer, seq_block, time_block, chunk, c_sub, n_valid,
                    norm_proj_of=None):
    fused = norm_proj_of is not None
    if fused:
        x, mod, g, w_in = norm_proj_of
        bsz, t, d = x.shape
        p = w_in.shape[1]
        mod_spec = lambda k: pl.BlockSpec((seq_block, 1, d), lambda i, j: (i, 0, k))
        in_specs = [pl.BlockSpec((seq_block, time_block, d), lambda i, j: (i, j, 0)), mod_spec(0), mod_spec(1),
                    pl.BlockSpec((1, d), lambda i, j: (0, 0)), pl.BlockSpec((d, p), lambda i, j: (0, 0))]
        args = [x, mod, mod, g.reshape(1, d), w_in]
    else:
        bsz, t, p = proj.shape
        in_specs = [pl.BlockSpec((seq_block, time_block, p), lambda i, j: (i, j, 0))]
        args = [proj]
    hk = p // 4
    n_heads = hk // HEAD_DIM
    has_state = s0 is not None
    st_shape = (seq_block, n_heads, HEAD_DIM, HEAD_DIM)
    st_spec = pl.BlockSpec(st_shape, lambda i, j: (i, 0, 0, 0))
    in_specs += [pl.BlockSpec(lb_logits.shape, lambda i, j: (0, 0)),
                 pl.BlockSpec((1, HEAD_DIM), lambda i, j: (0, 0))]
    args += [lb_logits, g_out.reshape(1, HEAD_DIM)]
    if has_state:
        in_specs.append(st_spec)
        args.append(s0)
    scratch = [pltpu.VMEM(st_shape, F32), pltpu.VMEM((seq_block, chunk, hk), F32)]
    scratch += [pltpu.VMEM((chunk, hk), F32)] * 3
    if fused:
        scratch.append(pltpu.VMEM((seq_block, 2, chunk, p), F32))
    return pl.pallas_call(
        functools.partial(_hgrn_rec_kernel, chunk=chunk, c_sub=c_sub, n_valid=n_valid,
                          has_state=has_state, layer=layer, fused_proj=fused),
        grid=(bsz // seq_block, t // time_block),
        in_specs=in_specs,
        out_specs=[pl.BlockSpec((seq_block, time_block, hk), lambda i, j: (i, j, 0)), st_spec],
        out_shape=[jax.ShapeDtypeStruct((bsz, t, hk), F32),
                   jax.ShapeDtypeStruct((bsz, n_heads, HEAD_DIM, HEAD_DIM), F32)],
        scratch_shapes=scratch,
        compiler_params=_params("parallel", "arbitrary"),
        name="hgrn_recurrence",
    )(*args)


def _sorted_rows(tile, top_k, n_experts):
    return tile * top_k + n_experts * BF16_ROWS


def _resid_router_kernel(*refs, top_k, n_experts, has_w_out, chained, row_chunk):
    refs = list(refs)
    x_ref, y_ref = refs[:2]
    del refs[:2]
    wo_ref = refs.pop(0) if has_w_out else None
    gt_ref, sh_ref, sc_ref, g_ref, wr_ref, br_ref = refs[:6]
    del refs[:6]
    if chained:
        refs.pop(0)
    x1_ref, xs_ref, pos_ref, gate_ref, cnt_ref, hb_ref, post_ref = refs
    tile, d = x_ref.shape
    n_sorted = xs_ref.shape[0]

    @pl.when(pl.program_id(0) == 0)
    def _():
        hb_ref[...] = jnp.zeros_like(hb_ref)
        post_ref[...] = jnp.full_like(post_ref, -1.0)

    for r0 in range(0, n_sorted, row_chunk):
        slot = (lax.broadcasted_iota(jnp.int32, (row_chunk, tile), 0) + r0).astype(F32)
        sel = jnp.where(slot == post_ref[0:1, :], 1.0, 0.0)
        for k in range(1, top_k):
            sel = sel + jnp.where(slot == post_ref[k:k + 1, :], 1.0, 0.0)
        xs_ref[r0:r0 + row_chunk, :] = _dot(sel.astype(BF16), hb_ref[...]).astype(BF16)

    y = y_ref[...]
    if has_w_out:
        y = _dot(y.astype(BF16), wo_ref[...])
    x1 = x_ref[...] + _mod_rows(gt_ref, tile) * y
    x1_ref[...] = x1
    h = _rms_norm(x1, g_ref[...]) * (1.0 + _mod_rows(sc_ref, tile)) + _mod_rows(sh_ref, tile)

    lane = lax.broadcasted_iota(jnp.int32, (tile, LANES), 1).astype(F32)
    logits = jnp.where(lane < n_experts, _dot_hp(h, wr_ref[...]) + br_ref[...], -jnp.inf)
    picks, vals = [], []
    for _ in range(top_k):
        m = jnp.max(logits, axis=-1, keepdims=True)
        pick = jnp.min(jnp.where(logits == m, lane, float(LANES)), axis=-1, keepdims=True)
        picks.append(pick)
        vals.append(m)
        logits = jnp.where(lane == pick, -jnp.inf, logits)
    exps = [jnp.exp(v - vals[0]) for v in vals]
    denom = exps[0]
    for e in exps[1:]:
        denom = denom + e

    onehots = [(lane == p).astype(F32) for p in picks]
    oh_sum = onehots[0]
    for oh in onehots[1:]:
        oh_sum = oh_sum + oh
    row = lax.broadcasted_iota(jnp.int32, (tile, tile), 0)
    col = lax.broadcasted_iota(jnp.int32, (tile, tile), 1)
    before = _dot((row > col).astype(BF16), oh_sum.astype(BF16))
    count = jnp.sum(oh_sum, axis=0, keepdims=True)
    cnt_pad = jnp.floor((count + (BF16_ROWS - 1.0)) * (1.0 / BF16_ROWS)) * BF16_ROWS
    lane8 = lax.broadcasted_iota(jnp.int32, (SUBLANES, LANES), 1)
    run = jnp.broadcast_to(cnt_pad, (SUBLANES, LANES))
    shift = 1
    while shift < n_experts:
        run = run + jnp.where(lane8 >= shift, pltpu.roll(run, shift, 1), 0.0)
        shift *= 2
    pos = before + (run[0:1] - cnt_pad)
    pos_out = jnp.zeros((tile, LANES), F32)
    gate_out = jnp.zeros((tile, LANES), F32)
    for k in range(top_k):
        pos_k = jnp.sum(onehots[k] * pos, axis=-1, keepdims=True)
        pos_out = jnp.where(lane == k, pos_k, pos_out)
        gate_out = jnp.where(lane == k, exps[k] / denom, gate_out)
    pos_ref[...] = pos_out.astype(jnp.int32)
    gate_ref[...] = gate_out
    cnt_ref[0] = cnt_pad.astype(jnp.int32)
    post_ref[...] = pos_out.T[:SUBLANES]
    hb_ref[...] = h.astype(BF16)


def resid_router(x, y, w_out_bf16, mod, g, w_r, b_r, sorted_in, *, tile, rows_per_seq, top_k,
                 block_offset, n_blocks_total):
    n, d = x.shape
    n_experts = w_r.shape[1]
    n_sorted = _sorted_rows(tile, top_k, n_experts)
    w_r_pad = jnp.pad(w_r, ((0, 0), (0, LANES - n_experts)))
    b_r_pad = jnp.pad(b_r, (0, LANES - n_experts)).reshape(1, LANES)
    has_w_out = w_out_bf16 is not None
    chained = sorted_in is not None
    n_tiles = n // tile
    routed = lambda i: jnp.minimum(i, n_tiles - 1)
    row_spec = pl.BlockSpec((tile, d), lambda i: (routed(i), 0))
    lane_spec = pl.BlockSpec((tile, LANES), lambda i: (routed(i), 0))
    full = lambda a: pl.BlockSpec(a.shape, lambda i: (0,) * a.ndim)
    in_specs = [row_spec, pl.BlockSpec((tile, y.shape[1]), lambda i: (routed(i), 0))]
    args = [x, y]
    if has_w_out:
        in_specs.append(full(w_out_bf16))
        args.append(w_out_bf16)
    in_specs += [_mod_spec(mod, 2, d, tile, rows_per_seq, routed), _mod_spec(mod, 3, d, tile, rows_per_seq, routed),
                 _mod_spec(mod, 4, d, tile, rows_per_seq, routed), pl.BlockSpec((1, d), lambda i: (0, 0)),
                 full(w_r_pad), full(b_r_pad)]
    args += [mod, mod, mod, g.reshape(1, d), w_r_pad, b_r_pad]
    aliases = {}
    if chained:
        aliases = {len(args): 1}
        in_specs.append(pl.BlockSpec(memory_space=pl.ANY))
        args.append(sorted_in)
    return pl.pallas_call(
        functools.partial(_resid_router_kernel, top_k=top_k, n_experts=n_experts, has_w_out=has_w_out,
                          chained=chained, row_chunk=256),
        grid=(n_tiles + 1,),
        in_specs=in_specs,
        out_specs=[row_spec,
                   pl.BlockSpec((n_sorted, d), lambda i: (jnp.maximum(i - 1, 0) + block_offset, 0)),
                   lane_spec, lane_spec,
                   pl.BlockSpec((1, 1, LANES), lambda i: (routed(i), 0, 0))],
        out_shape=[jax.ShapeDtypeStruct((n, d), F32),
                   jax.ShapeDtypeStruct((n_blocks_total * n_sorted, d), BF16),
                   jax.ShapeDtypeStruct((n, LANES), jnp.int32), jax.ShapeDtypeStruct((n, LANES), F32),
                   jax.ShapeDtypeStruct((n_tiles, 1, LANES), jnp.int32)],
        scratch_shapes=[pltpu.VMEM((tile, d), BF16), pltpu.VMEM((SUBLANES, tile), F32)],
        input_output_aliases=aliases,
        compiler_params=_params("arbitrary"),
        name="resid_router",
    )(*args)


def _experts_kernel(te_ref, first_ref, rows_ref, base_ref, slo_ref, shi_ref, wslot_ref, nexte_ref, used_ref,
                    sstart_ref, slen_ref, ssrc_ref,
                    xs_hbm, wgu_hbm, bgu_ref, wdn_hbm, bdn_ref, ys_hbm,
                    xbuf, ybuf, wgu_f, wdn_f, wgu_b, wdn_b, in_sem, out_sem, w_sem,
                    *, layer, limit, alpha, col_chunk):
    del ys_hbm
    i = pl.program_id(0)
    used = used_ref[0]
    tm = xbuf.shape[1]

    def copy(src_rows, dst_rows, slot, inbound):
        if inbound:
            return pltpu.make_async_copy(xs_hbm.at[src_rows, :], xbuf.at[slot, dst_rows, :], in_sem.at[slot])
        return pltpu.make_async_copy(ybuf.at[slot, dst_rows, :], xs_hbm.at[src_rows, :], out_sem.at[slot])

    def piece_copies(tile_idx, slot, inbound, wait):
        if wait:
            rows = pl.ds(0, pl.multiple_of(rows_ref[tile_idx], BF16_ROWS))
            copy(rows, rows, slot, inbound).wait()
            return
        base = base_ref[tile_idx]

        def piece(s, c):
            first = sstart_ref[s] - base
            lo = jnp.maximum(first, 0)
            n_rows = pl.multiple_of(jnp.minimum(first + slen_ref[s], tm) - lo, BF16_ROWS)

            @pl.when(n_rows > 0)
            def _():
                src = pl.multiple_of(ssrc_ref[s] + (lo - first), BF16_ROWS)
                copy(pl.ds(src, n_rows), pl.ds(pl.multiple_of(lo, BF16_ROWS), n_rows), slot, inbound).start()
            return c
        lax.fori_loop(slo_ref[tile_idx], shi_ref[tile_idx], piece, 0)

    def weight_copies(e, slot):
        return (pltpu.make_async_copy(wgu_hbm.at[layer, e], wgu_f.at[slot], w_sem.at[slot]),
                pltpu.make_async_copy(wdn_hbm.at[layer, e], wdn_f.at[slot], w_sem.at[slot]))

    @pl.when(i == 0)
    def _():
        xbuf[...] = jnp.zeros_like(xbuf)
        for cp in weight_copies(te_ref[0], 0):
            cp.start()
        piece_copies(0, 0, True, False)

    @pl.when(i < used)
    def _():
        slot = i % 2

        @pl.when(i + 1 < used)
        def _():
            piece_copies(i + 1, 1 - slot, True, False)

        @pl.when(first_ref[i] == 1)
        def _():
            ws = wslot_ref[i]
            for cp in weight_copies(te_ref[i], ws):
                cp.wait()

            @pl.when(nexte_ref[i] >= 0)
            def _():
                for cp in weight_copies(nexte_ref[i], 1 - ws):
                    cp.start()
            wgu_b[...] = wgu_f[ws].astype(BF16)
            wdn_b[...] = wdn_f[ws].astype(BF16)

        piece_copies(i, slot, True, True)

        @pl.when(i >= 2)
        def _():
            piece_copies(i - 2, slot, False, True)

        d_ff = wdn_b.shape[0]
        tm = xbuf.shape[1]
        e = te_ref[i]
        b_gu = bgu_ref[pl.ds(e, 1), :]
        b_dn = bdn_ref[pl.ds(e, 1), :]

        def mlp(n_rows):
            x = xbuf[slot, :n_rows, :]
            y = None
            for c0 in range(0, d_ff, col_chunk):
                cs = slice(c0, c0 + col_chunk)
                us = slice(d_ff + c0, d_ff + c0 + col_chunk)
                gate = jnp.minimum(_dot(x, wgu_b[:, cs]) + b_gu[:, cs], limit)
                up = jnp.clip(_dot(x, wgu_b[:, us]) + b_gu[:, us], -limit, limit)
                act = ((up + 1.0) * (gate * _sigmoid(alpha * gate))).astype(BF16)
                part = _dot(act, wdn_b[cs, :])
                y = part if y is None else y + part
            ybuf[slot, :n_rows, :] = (y + b_dn).astype(BF16)

        step = tm // MLP_ROW_STEPS
        for part in range(1, MLP_ROW_STEPS + 1):
            @pl.when((rows_ref[i] > (part - 1) * step) & (rows_ref[i] <= part * step))
            def _():
                mlp(part * step)
        piece_copies(i, slot, False, False)

    @pl.when(i == pl.num_programs(0) - 1)
    def _():
        @pl.when(used >= 2)
        def _():
            piece_copies(used - 2, used % 2, False, True)
        piece_copies(used - 1, (used - 1) % 2, False, True)


def moe_experts(sorted_rows, w_gu, b_gu, w_dn, b_dn, tables, *, layer, row_tile, limit, alpha, col_chunk=512):
    n_rows, d = sorted_rows.shape
    d_gu = w_gu.shape[-1]
    d_ff = w_dn.shape[-2]
    n_tiles = tables[0].shape[0]
    vmem = lambda a: pl.BlockSpec(a.shape, lambda i, *_: (0,) * a.ndim)
    any_spec = pl.BlockSpec(memory_space=pl.ANY)
    return pl.pallas_call(
        functools.partial(_experts_kernel, layer=layer, limit=limit, alpha=alpha,
                          col_chunk=min(col_chunk, d_ff)),
        grid_spec=pltpu.PrefetchScalarGridSpec(
            num_scalar_prefetch=len(tables),
            grid=(n_tiles,),
            in_specs=[any_spec, any_spec, vmem(b_gu), any_spec, vmem(b_dn)],
            out_specs=any_spec,
            scratch_shapes=[
                pltpu.VMEM((2, row_tile, d), BF16), pltpu.VMEM((2, row_tile, d), BF16),
                pltpu.VMEM((2, d, d_gu), F32), pltpu.VMEM((2, d_ff, d), F32),
                pltpu.VMEM((d, d_gu), BF16), pltpu.VMEM((d_ff, d), BF16),
                pltpu.SemaphoreType.DMA((2,)), pltpu.SemaphoreType.DMA((2,)), pltpu.SemaphoreType.DMA((2,)),
            ],
        ),
        out_shape=jax.ShapeDtypeStruct((n_rows, d), BF16),
        input_output_aliases={len(tables): 0},
        compiler_params=_params("arbitrary"),
        name="moe_experts",
    )(*tables, sorted_rows, w_gu, b_gu, w_dn, b_dn)


def _expert_tables(cnt, n_sorted, n_tiles, row_tile):
    n_blocks, n_experts = cnt.shape

    def prefix_sum(a):
        n = a.shape[-1]
        upto = jnp.arange(n)[:, None] <= jnp.arange(n)[None, :]
        return jnp.sum(jnp.where(upto, a[..., :, None], 0), axis=-2)

    local_off = prefix_sum(cnt) - cnt
    seg_end = prefix_sum(cnt.T)
    seg_start = seg_end - cnt.T
    seg_src = jnp.arange(n_blocks, dtype=jnp.int32)[None, :] * n_sorted + local_off.T
    total = seg_end[:, -1]
    padded = (total + row_tile - 1) // row_tile * row_tile
    pad_end = prefix_sum(padded)
    pad_start = pad_end - padded
    tiles = jnp.arange(n_tiles, dtype=jnp.int32)
    experts = jnp.arange(n_experts, dtype=jnp.int32)
    n_used = pad_end[-1] // row_tile
    tile_expert = jnp.minimum(jnp.sum(tiles[:, None] * row_tile >= pad_end[None, :], axis=1), n_experts - 1)
    is_expert = tile_expert[:, None] == experts[None, :]

    def of_expert(a):
        if a.ndim == 1:
            return jnp.sum(jnp.where(is_expert, a[None, :], 0), axis=1)
        return jnp.sum(jnp.where(is_expert[:, :, None], a[None, :, :], 0), axis=1)

    live = tiles < n_used
    tile_base = tiles * row_tile - of_expert(pad_start)
    tile_rows = jnp.where(live, jnp.clip(of_expert(total) - tile_base, 0, row_tile), 0)
    tile_first = (live & (tile_base == 0)).astype(jnp.int32)
    seg_lo = jnp.sum(of_expert(seg_end) <= tile_base[:, None], axis=1)
    seg_hi = jnp.sum(of_expert(seg_start) < tile_base[:, None] + row_tile, axis=1)
    seg_lo = tile_expert * n_blocks + jnp.minimum(seg_lo, seg_hi)
    seg_hi = tile_expert * n_blocks + seg_hi
    owns = total > 0
    order = prefix_sum(owns.astype(jnp.int32)) - 1
    experts = jnp.arange(n_experts, dtype=jnp.int32)
    later = (experts[None, :] > experts[:, None]) & owns[None, :]
    next_expert = jnp.min(jnp.where(later, experts[None, :], n_experts), axis=1)
    next_expert = jnp.where(next_expert < n_experts, next_expert, -1)
    i32 = lambda a: a.astype(jnp.int32)
    return (i32(tile_expert), tile_first, i32(tile_rows), i32(tile_base), i32(seg_lo), i32(seg_hi),
            i32(of_expert(order) % 2), i32(of_expert(next_expert)), i32(n_used).reshape(1),
            i32(seg_start.reshape(-1)), i32(cnt.T.reshape(-1)), i32(seg_src.reshape(-1)))


def _combine_kernel(x_ref, pos_ref, gate_ref, gt_ref, gfin_ref, ys_ref, o_ref, *, top_k, final_norm, k_chunk):
    tile, d = x_ref.shape
    n_sorted = ys_ref.shape[0]
    pos = pos_ref[...].astype(F32)
    gates = gate_ref[...]
    acc = None
    for r0 in range(0, n_sorted, k_chunk):
        slot = (lax.broadcasted_iota(jnp.int32, (tile, k_chunk), 1) + r0).astype(F32)
        w = jnp.where(slot == pos[:, 0:1], gates[:, 0:1], 0.0)
        for k in range(1, top_k):
            w = w + jnp.where(slot == pos[:, k:k + 1], gates[:, k:k + 1], 0.0)
        part = _dot(w.astype(BF16), ys_ref[r0:r0 + k_chunk, :])
        acc = part if acc is None else acc + part
    out = x_ref[...] + _mod_rows(gt_ref, tile) * acc
    if final_norm:
        out = _rms_norm(out, gfin_ref[...])
    o_ref[...] = out


def moe_combine(x, ys, pos, gates, mod, g_final, *, tile, rows_per_seq, final_norm, block_offset, n_sorted):
    n, d = x.shape
    return pl.pallas_call(
        functools.partial(_combine_kernel, top_k=TOP_K, final_norm=final_norm, k_chunk=512),
        grid=(n // tile,),
        in_specs=[
            pl.BlockSpec((tile, d), lambda i: (i, 0)),
            pl.BlockSpec((tile, LANES), lambda i: (i, 0)),
            pl.BlockSpec((tile, LANES), lambda i: (i, 0)),
            _mod_spec(mod, 5, d, tile, rows_per_seq),
            pl.BlockSpec((1, d), lambda i: (0, 0)),
            pl.BlockSpec((n_sorted, d), lambda i: (i + block_offset, 0)),
        ],
        out_specs=pl.BlockSpec((tile, d), lambda i: (i, 0)),
        out_shape=jax.ShapeDtypeStruct((n, d), F32),
        compiler_params=_params("parallel"),
        name="moe_combine",
    )(x, pos, gates, mod, g_final.reshape(1, d), ys)


def _pool_groups(h, window_sum, counts, w_ref, scale):
    n_groups = w_ref.shape[0]
    dg = h.shape[-1] // n_groups
    outs = []
    for gi in range(n_groups):
        cols = slice(gi * dg, (gi + 1) * dg)
        pooled = window_sum(gi, cols) / counts[gi] - h[:, cols]
        outs.append(_dot(pooled.astype(BF16), w_ref[gi]))
    return jnp.concatenate(outs, axis=-1) * scale


def _pool_prompt_kernel(x_ref, sh_ref, sc_ref, g_ref, w_ref, scale_ref, y_ref, cache_ref, ext_ref,
                        *, windows, halo):
    tile, d = x_ref.shape[1:]
    j = pl.program_id(1)

    @pl.when(j == 0)
    def _():
        ext_ref[0:halo, :] = jnp.zeros((halo, d), F32)

    h = _rms_norm(x_ref[0], g_ref[...]) * (1.0 + sc_ref[...].reshape(-1, d)) + sh_ref[...].reshape(-1, d)
    ext_ref[halo:halo + tile, :] = h
    pos = j * tile + lax.broadcasted_iota(jnp.int32, (tile, 1), 0)

    def window_sum(gi, cols):
        acc = h[:, cols]
        for s in range(1, windows[gi]):
            acc = acc + ext_ref[halo - s:halo - s + tile, cols]
        return acc
    counts = [jnp.minimum(pos + 1, w).astype(F32) for w in windows]
    y_ref[0] = _pool_groups(h, window_sum, counts, w_ref, scale_ref[...])

    n_keep = cache_ref.shape[1]
    @pl.when(j == pl.num_programs(1) - 1)
    def _():
        cache_ref[0] = ext_ref[halo + tile - n_keep:halo + tile, :]
    ext_ref[0:halo, :] = ext_ref[tile:tile + halo, :]


def pool_mixer_prompt(x, mod, g, w_grp_bf16, scale, *, tile, windows, n_keep):
    bsz, t, d = x.shape
    halo = 16
    assert max(windows) <= halo <= tile and n_keep <= tile
    mod_spec = lambda k: pl.BlockSpec((1, 1, d), lambda b, j: (b, 0, k))
    return pl.pallas_call(
        functools.partial(_pool_prompt_kernel, windows=windows, halo=halo),
        grid=(bsz, t // tile),
        in_specs=[
            pl.BlockSpec((1, tile, d), lambda b, j: (b, j, 0)),
            mod_spec(0), mod_spec(1),
            pl.BlockSpec((1, d), lambda b, j: (0, 0)),
            pl.BlockSpec(w_grp_bf16.shape, lambda b, j: (0, 0, 0)),
            pl.BlockSpec((1, d), lambda b, j: (0, 0)),
        ],
        out_specs=[pl.BlockSpec((1, tile, d), lambda b, j: (b, j, 0)),
                   pl.BlockSpec((1, n_keep, d), lambda b, j: (b, 0, 0))],
        out_shape=[jax.ShapeDtypeStruct((bsz, t, d), F32), jax.ShapeDtypeStruct((bsz, n_keep, d), F32)],
        scratch_shapes=[pltpu.VMEM((halo + tile, d), F32)],
        compiler_params=_params("parallel", "arbitrary"),
        name="pool_mixer_prompt",
    )(x, mod, mod, g.reshape(1, d), w_grp_bf16, scale.reshape(1, d))


def _pool_sample_kernel(x_ref, buf_ref, sh_ref, sc_ref, g_ref, w_ref, scale_ref, y_ref, cache_ref,
                        *, windows, start_pos):
    t_len = x_ref.shape[0]
    n_prev = buf_ref.shape[0]
    hs = [_rms_norm(x_ref[t], g_ref[...]) * (1.0 + sc_ref[...]) + sh_ref[...] for t in range(t_len)]

    def ext(r):
        return buf_ref[r] if r < n_prev else hs[r - n_prev]

    for t in range(t_len):
        def window_sum(gi, cols):
            acc = hs[t][:, cols]
            for s in range(1, windows[gi]):
                acc = acc + ext(n_prev + t - s)[:, cols]
            return acc
        counts = [float(min(start_pos + t + 1, w)) for w in windows]
        y_ref[t] = _pool_groups(hs[t], window_sum, counts, w_ref, scale_ref[...])
    for r in range(n_prev):
        cache_ref[r] = ext(t_len + r)


def pool_mixer_sample(x_t, buf_t, mod, g, w_grp_bf16, scale, *, seq_block, windows, start_pos):
    t_len, n_seq, d = x_t.shape
    n_prev = buf_t.shape[0]
    assert start_pos >= n_prev >= max(windows) - 1
    mod_spec = lambda k: pl.BlockSpec((seq_block, d), lambda i: (i, k))
    return pl.pallas_call(
        functools.partial(_pool_sample_kernel, windows=windows, start_pos=start_pos),
        grid=(n_seq // seq_block,),
        in_specs=[
            pl.BlockSpec((t_len, seq_block, d), lambda i: (0, i, 0)),
            pl.BlockSpec((n_prev, seq_block, d), lambda i: (0, i, 0)),
            mod_spec(0), mod_spec(1),
            pl.BlockSpec((1, d), lambda i: (0, 0)),
            pl.BlockSpec(w_grp_bf16.shape, lambda i: (0, 0, 0)),
            pl.BlockSpec((1, d), lambda i: (0, 0)),
        ],
        out_specs=[pl.BlockSpec((t_len, seq_block, d), lambda i: (0, i, 0)),
                   pl.BlockSpec((n_prev, seq_block, d), lambda i: (0, i, 0))],
        out_shape=[jax.ShapeDtypeStruct((t_len, n_seq, d), F32), jax.ShapeDtypeStruct((n_prev, n_seq, d), F32)],
        compiler_params=_params("parallel"),
        name="pool_mixer_sample",
    )(x_t, buf_t, mod, mod, g.reshape(1, d), w_grp_bf16, scale.reshape(1, d))


TOP_K = 4
SWIGLU_LIMIT = 7.0
SWIGLU_ALPHA = 1.702
POOL_WINDOWS = (2, 4, 8, 16)
PAST_LEN = 16384
PROJ_TILE = 256
MOE_TOKEN_TILE = 512
MOE_ROW_TILE = 512
HGRN_TIME_BLOCK = 1024
HGRN_CHUNK = 256
HGRN_SUB = 32
SAMPLE_T_PAD = 8
SAMPLE_SEQ_BLOCK = 8


def kernel(x_prompt, x_sample, c_prompt, c_sample, state_hgrn, cache_pool, g_norm_mix, g_norm_ffn, w_ada, b_ada, w_in_hgrn, lb_logits, g_out_hgrn, w_out_hgrn, w_grp_pool, scale_pool, w_router, b_router, w_gate_up, b_gate_up, w_down, b_down, g_final):
    bp, tp, d = x_prompt.shape
    bs, ts, _ = x_sample.shape
    n_p, n_s = bp * tp, bs * ts
    n_experts = w_router.shape[-1]
    hk = w_out_hgrn.shape[1]
    assert n_s == MOE_TOKEN_TILE and n_p % MOE_TOKEN_TILE == 0
    blocks_p = n_p // MOE_TOKEN_TILE
    n_blocks = blocks_p + 1
    n_sorted = _sorted_rows(MOE_TOKEN_TILE, TOP_K, n_experts)
    n_row_tiles = -(-(n_blocks * n_sorted + n_experts * (MOE_ROW_TILE - BF16_ROWS)) // MOE_ROW_TILE)

    mod = adaln(jnp.concatenate([c_prompt, c_sample], axis=0), w_ada, b_ada)
    mod_p = [mod[l, :bp][:, None, :] for l in range(mod.shape[0])]
    mod_s = [mod[l, bp:] for l in range(mod.shape[0])]

    xp = x_prompt.reshape(n_p, d)
    xs = x_sample.transpose(1, 0, 2).reshape(n_s, d)

    def moe(layer, x_p, y_p, x_s, y_s, w_out, final_norm):
        route = functools.partial(resid_router, g=g_norm_ffn[layer], w_r=w_router[layer], b_r=b_router[layer],
                                  tile=MOE_TOKEN_TILE, top_k=TOP_K, n_blocks_total=n_blocks)
        x1_p, sorted_rows, pos_p, gate_p, cnt_p = route(
            x_p, y_p, w_out, mod_p[layer], sorted_in=None, rows_per_seq=tp, block_offset=0)
        x1_s, sorted_rows, pos_s, gate_s, cnt_s = route(
            x_s, y_s, w_out, mod_s[layer], sorted_in=sorted_rows, rows_per_seq=None, block_offset=blocks_p)
        cnt8 = jnp.concatenate([cnt_p, cnt_s], axis=0)[:, 0, :n_experts]
        tables = _expert_tables(cnt8, n_sorted, n_row_tiles, MOE_ROW_TILE)
        ys = moe_experts(sorted_rows, w_gate_up, b_gate_up[layer], w_down, b_down[layer], tables,
                         layer=layer, row_tile=MOE_ROW_TILE, limit=SWIGLU_LIMIT, alpha=SWIGLU_ALPHA)
        combine = functools.partial(moe_combine, ys=ys, g_final=g_final, tile=MOE_TOKEN_TILE,
                                    final_norm=final_norm, n_sorted=n_sorted)
        out_p = combine(x1_p, pos=pos_p, gates=gate_p, mod=mod_p[layer], rows_per_seq=tp, block_offset=0)
        out_s = combine(x1_s, pos=pos_s, gates=gate_s, mod=mod_s[layer], rows_per_seq=None,
                        block_offset=blocks_p)
        return out_p, out_s

    w_in = w_in_hgrn[0].astype(BF16)
    proj_s = norm_proj(xs, mod_s[0], g_norm_mix[0], w_in, tile=bs, rows_per_seq=None)
    o_p, state_p = hgrn_recurrence(None, lb_logits, g_out_hgrn[0], None,
                                   layer=0, seq_block=1, time_block=HGRN_TIME_BLOCK, chunk=HGRN_CHUNK,
                                   c_sub=HGRN_SUB, n_valid=HGRN_CHUNK,
                                   norm_proj_of=(x_prompt, mod_p[0], g_norm_mix[0], w_in))
    proj_sb = jnp.pad(proj_s.reshape(ts, bs, 4 * hk).transpose(1, 0, 2), ((0, 0), (0, SAMPLE_T_PAD - ts), (0, 0)))
    o_s, state_s = hgrn_recurrence(proj_sb, lb_logits, g_out_hgrn[0], state_hgrn[0],
                                   layer=0, seq_block=SAMPLE_SEQ_BLOCK, time_block=SAMPLE_T_PAD,
                                   chunk=SAMPLE_T_PAD, c_sub=SAMPLE_T_PAD, n_valid=ts)
    o_s = o_s[:, :ts].transpose(1, 0, 2).reshape(n_s, hk)
    x_p, x_s = moe(0, xp, o_p.reshape(n_p, hk), xs, o_s, w_out_hgrn[0].astype(BF16), False)

    w_grp = w_grp_pool[0].astype(BF16)
    n_keep = cache_pool.shape[2]
    y_p, cache_p = pool_mixer_prompt(x_p.reshape(bp, tp, d), mod_p[1], g_norm_mix[1], w_grp, scale_pool[0],
                                     tile=PROJ_TILE, windows=POOL_WINDOWS, n_keep=n_keep)
    y_s, cache_s = pool_mixer_sample(x_s.reshape(ts, bs, d), cache_pool[0].transpose(1, 0, 2), mod_s[1],
                                     g_norm_mix[1], w_grp, scale_pool[0],
                                     seq_block=32, windows=POOL_WINDOWS, start_pos=PAST_LEN)
    x_p, x_s = moe(1, x_p, y_p.reshape(n_p, d), x_s, y_s.reshape(n_s, d), None, True)

    return (x_p.reshape(bp, tp, d), x_s.reshape(ts, bs, d).transpose(1, 0, 2),
            state_p[None], state_s[None], cache_p[None], cache_s.transpose(1, 0, 2)[None])
```

```python
import functools

import jax
import jax.numpy as jnp
from jax import lax
from jax.experimental import pallas as pl
from jax.experimental.pallas import tpu as pltpu

F32 = jnp.float32
BF16 = jnp.bfloat16

RMS_EPS = 1e-6
LANES = 128
SUBLANES = 8
BF16_ROWS = 16
HEAD_DIM = 128
VMEM_LIMIT = 56 * 1024 * 1024

_dot = functools.partial(jnp.dot, preferred_element_type=F32)


def _params(*semantics):
    return pltpu.CompilerParams(dimension_semantics=semantics, vmem_limit_bytes=VMEM_LIMIT)


def _split_bf16(x, n):
    parts, r = [], x
    for _ in range(n):
        p = r.astype(BF16)
        parts.append(p)
        r = r - p.astype(F32)
    return parts


def _dot_hp(a, b):
    a_hi, a_lo = _split_bf16(a, 2)
    b_hi, b_lo = _split_bf16(b, 2)
    return _dot(a_hi, b_hi) + (_dot(a_hi, b_lo) + _dot(a_lo, b_hi))


def _sigmoid(x):
    return 1.0 / (1.0 + jnp.exp(-x))


def _silu(x):
    return x * _sigmoid(x)


def _rms_norm(x, g):
    ms = jnp.mean(x * x, axis=-1, keepdims=True)
    return x * lax.rsqrt(ms + RMS_EPS) * g


def _adaln_kernel(c_ref, w_ref, b_ref, o_ref):
    o_ref[0] = _dot_hp(_silu(c_ref[...]), w_ref[0]) + b_ref[0]


def adaln(c_all, w_ada, b_ada, *, col_block=1536):
    n_seq, d = c_all.shape
    n_layers, _, d6 = w_ada.shape
    return pl.pallas_call(
        _adaln_kernel,
        grid=(n_layers, d6 // col_block),
        in_specs=[
            pl.BlockSpec((n_seq, d), lambda l, j: (0, 0)),
            pl.BlockSpec((1, d, col_block), lambda l, j: (l, 0, j)),
            pl.BlockSpec((1, 1, col_block), lambda l, j: (l, 0, j)),
        ],
        out_specs=pl.BlockSpec((1, n_seq, col_block), lambda l, j: (l, 0, j)),
        out_shape=jax.ShapeDtypeStruct((n_layers, n_seq, d6), F32),
        compiler_params=_params("parallel", "parallel"),
        name="adaln",
    )(c_all, w_ada, b_ada.reshape(n_layers, 1, d6))


def _mod_spec(mod, k, d, tile, rows_per_seq):
    if rows_per_seq is None:
        return pl.BlockSpec((mod.shape[0], d), lambda i: (0, k))
    tiles_per_seq = rows_per_seq // tile
    return pl.BlockSpec((1, 1, d), lambda i: (i // tiles_per_seq, 0, k))


def _mod_rows(ref, tile):
    m = ref[...].reshape(-1, ref.shape[-1])
    if m.shape[0] not in (1, tile):
        m = jnp.concatenate([m] * (tile // m.shape[0]), axis=0)
    return m


def _norm_proj_kernel(x_ref, sh_ref, sc_ref, g_ref, w_ref, o_ref):
    tile = x_ref.shape[0]
    h = _rms_norm(x_ref[...], g_ref[...]) * (1.0 + _mod_rows(sc_ref, tile)) + _mod_rows(sh_ref, tile)
    o_ref[...] = _dot(h.astype(BF16), w_ref[...])


def norm_proj(x, mod, g, w_bf16, *, tile, rows_per_seq):
    n, d = x.shape
    p = w_bf16.shape[1]
    return pl.pallas_call(
        _norm_proj_kernel,
        grid=(n // tile,),
        in_specs=[
            pl.BlockSpec((tile, d), lambda i: (i, 0)),
            _mod_spec(mod, 0, d, tile, rows_per_seq),
            _mod_spec(mod, 1, d, tile, rows_per_seq),
            pl.BlockSpec((1, d), lambda i: (0, 0)),
            pl.BlockSpec((d, p), lambda i: (0, 0)),
        ],
        out_specs=pl.BlockSpec((tile, p), lambda i: (i, 0)),
        out_shape=jax.ShapeDtypeStruct((n, p), F32),
        compiler_params=_params("parallel"),
        name="hgrn_norm_proj",
    )(x, mod, mod, g.reshape(1, d), w_bf16)


def _cumsum_rows(x, tri):
    hi, mid, lo = _split_bf16(x, 3)
    return _dot(tri, hi) + (_dot(tri, mid) + _dot(tri, lo))


MAX_LOG_DECAY_RANGE = 80.0
MLP_ROW_STEPS = 4


def _hgrn_prep(proj, lb, n_valid):
    c = proj.shape[0]
    hk = proj.shape[1] // 4
    row = lax.broadcasted_iota(jnp.int32, (c, c), 0)
    col = lax.broadcasted_iota(jnp.int32, (c, c), 1)
    zf = proj[:, hk:2 * hk]
    e = jnp.exp(-jnp.abs(zf))
    r = 1.0 / (1.0 + e)
    pos = zf >= 0
    sig_p = jnp.where(pos, 1.0, e) * r
    sig_n = jnp.where(pos, e, 1.0) * r
    logf = jnp.log(lb + (1.0 - lb) * sig_p)
    k = (1.0 - lb) * sig_n
    if n_valid < c:
        live = lax.broadcasted_iota(jnp.int32, (c, 1), 0) < n_valid
        logf = jnp.where(live, logf, 0.0)
        k = jnp.where(live, k, 0.0)
    b = _cumsum_rows(logf, (row >= col).astype(BF16))
    return _silu(proj[:, :hk]), k, proj[:, 2 * hk:3 * hk], _silu(proj[:, 3 * hk:]), b


def _decay_range(b, c_sub):
    c = b.shape[0]
    worst = None
    for i in range(c // c_sub):
        span = b[i * c_sub:i * c_sub + 1, :] - b[(i + 1) * c_sub - 1:(i + 1) * c_sub, :]
        worst = span if worst is None else jnp.maximum(worst, span)
    return jnp.max(worst)


def _head_norm_gate(o, gate, gout):
    outs = []
    for h in range(o.shape[1] // HEAD_DIM):
        hs = slice(h * HEAD_DIM, (h + 1) * HEAD_DIM)
        oh = o[:, hs]
        outs.append(oh * lax.rsqrt(jnp.mean(oh * oh, axis=-1, keepdims=True) + RMS_EPS) * gout * gate[:, hs])
    return jnp.concatenate(outs, axis=-1)


def _hgrn_chunk(prep, gout, st_refs, seq, c_sub):
    q, k, v, gate, b = prep
    c = q.shape[0]
    n_heads = q.shape[1] // HEAD_DIM
    row = lax.broadcasted_iota(jnp.int32, (c, c), 0)
    col = lax.broadcasted_iota(jnp.int32, (c, c), 1)
    causal = row >= col
    n_sub = c // c_sub
    subs = [slice(i * c_sub, (i + 1) * c_sub) for i in range(n_sub)]

    intra, inter = [], []
    for h in range(n_heads):
        hs = slice(h * HEAD_DIM, (h + 1) * HEAD_DIM)
        bh, qh, kh, vh = b[:, hs], q[:, hs], k[:, hs], v[:, hs]
        vb = vh.astype(BF16)
        refs = [bh[i * c_sub + c_sub // 2:i * c_sub + c_sub // 2 + 1, :] for i in range(n_sub)]
        k_own = [kh[rs] * jnp.exp(jnp.minimum(ref - bh[rs], MAX_LOG_DECAY_RANGE))
                 for rs, ref in zip(subs, refs)]
        a_rows = []
        for i in range(n_sub):
            q_hat = (qh[subs[i]] * jnp.exp(bh[subs[i]] - refs[i])).astype(BF16)
            parts = [k_own[j] * jnp.exp(refs[i] - refs[j]) for j in range(i)] + [k_own[i]]
            parts += [jnp.zeros((c_sub, HEAD_DIM), F32)] * (n_sub - 1 - i)
            k_hat = (jnp.concatenate(parts, axis=0) if n_sub > 1 else parts[0]).astype(BF16)
            a_rows.append(lax.dot_general(q_hat, k_hat, (((1,), (1,)), ((), ())),
                                          preferred_element_type=F32))
        att = jnp.where(causal, jnp.concatenate(a_rows, axis=0) if len(a_rows) > 1 else a_rows[0], 0.0)
        intra.append(_dot(att.astype(BF16), vb))
        st = st_refs[seq, h]
        inter.append(lax.dot_general((qh * jnp.exp(bh)).astype(BF16), st.astype(BF16),
                                     (((1,), (1,)), ((), ())), preferred_element_type=F32))
        b_last = bh[c - 1:c, :]
        k_dec = (kh * jnp.exp(b_last - bh)).astype(BF16)
        st_refs[seq, h] = st * jnp.exp(b_last) + lax.dot_general(
            vb, k_dec, (((0,), (0,)), ((), ())), preferred_element_type=F32)
    inter = jnp.concatenate(inter, axis=-1)
    return _head_norm_gate(jnp.concatenate(intra, axis=-1) + inter, gate, gout), inter


def _hgrn_chunk_exact(prep, inter, gout, q_ref, b_ref, oi_ref):
    q, k, v, gate, b = prep
    c = q.shape[0]
    n_heads = q.shape[1] // HEAD_DIM
    q_ref[...] = q
    b_ref[...] = b
    key_row = lax.broadcasted_iota(jnp.int32, (c, 1), 0)

    def row_group(g, carry):
        rows = pl.ds(pl.multiple_of(g * SUBLANES, SUBLANES), SUBLANES)
        for h in range(n_heads):
            hs = slice(h * HEAD_DIM, (h + 1) * HEAD_DIM)
            q_g, b_g = q_ref[rows, hs], b_ref[rows, hs]
            o_rows = []
            for r in range(SUBLANES):
                decay = jnp.exp(jnp.minimum(b_g[r:r + 1] - b[:, hs], 0.0))
                score = jnp.sum(decay * k[:, hs] * q_g[r:r + 1], axis=-1, keepdims=True)
                score = jnp.where(key_row <= g * SUBLANES + r, score, 0.0)
                o_rows.append(jnp.sum(score * v[:, hs], axis=0, keepdims=True))
            oi_ref[rows, hs] = jnp.concatenate(o_rows, axis=0)
        return carry
    lax.fori_loop(0, c // SUBLANES, row_group, 0)
    return _head_norm_gate(oi_ref[...] + inter, gate, gout)


def _lower_bound(lb_logits, layer):
    e = jnp.exp(lb_logits - jnp.max(lb_logits, axis=0, keepdims=True))
    return jnp.sum(e[:layer + 1], axis=0, keepdims=True) / jnp.sum(e, axis=0, keepdims=True)


def _hgrn_rec_kernel(*refs, chunk, c_sub, n_valid, has_state, layer, fused_proj):
    refs = list(refs)
    if fused_proj:
        x_ref, sh_ref, sc_ref, g_ref, w_ref = refs[:5]
        del refs[:5]
        proj_ref = None
    else:
        proj_ref = refs.pop(0)
    lb_ref, gout_ref = refs[:2]
    del refs[:2]
    s0_ref = refs.pop(0) if has_state else None
    o_ref, sout_ref, st_ref, inter_ref, q_ref, b_ref, oi_ref = refs[:7]
    proj_buf = refs[7] if fused_proj else None
    bb, tb, _ = o_ref.shape
    n_heads = st_ref.shape[1]
    n_chunks = tb // chunk
    j = pl.program_id(1)

    @pl.when(j == 0)
    def _():
        if has_state:
            for s in range(bb):
                for h in range(n_heads):
                    st_ref[s, h] = s0_ref[s, h].T
        else:
            st_ref[...] = jnp.zeros_like(st_ref)

    lb = _lower_bound(lb_ref[...], layer)
    gout = gout_ref[...]

    def project(s, ci):
        rows = pl.ds(pl.multiple_of(ci * chunk, chunk), chunk)
        h = _rms_norm(x_ref[s, rows, :], g_ref[...]) * (1.0 + sc_ref[s]) + sh_ref[s]
        return _dot(h.astype(BF16), w_ref[...])

    if fused_proj:
        for s in range(bb):
            proj_buf[s, 0] = project(s, 0)

    def chunk_body(ci, carry):
        rows = pl.ds(pl.multiple_of(ci * chunk, chunk), chunk)

        def load_proj(s):
            return proj_buf[s, ci % 2] if fused_proj else proj_ref[s, rows, :]
        span = None
        preps = [_hgrn_prep(load_proj(s), lb, n_valid) for s in range(bb)]
        for s, prep in enumerate(preps):
            o_ref[s, rows, :], inter_ref[s] = _hgrn_chunk(prep, gout, st_ref, s, c_sub)
            worst = _decay_range(prep[4], c_sub)
            span = worst if span is None else jnp.maximum(span, worst)
            if fused_proj:
                proj_buf[s, (ci + 1) % 2] = project(s, jnp.minimum(ci + 1, n_chunks - 1))

        @pl.when(jnp.logical_not(span <= MAX_LOG_DECAY_RANGE))
        def _():
            for s in range(bb):
                prep = _hgrn_prep(load_proj(s), lb, n_valid)
                o_ref[s, rows, :] = _hgrn_chunk_exact(prep, inter_ref[s], gout, q_ref, b_ref, oi_ref)
        return carry
    lax.fori_loop(0, n_chunks, chunk_body, 0)

    @pl.when(j == pl.num_programs(1) - 1)
    def _():
        for s in range(bb):
            for h in range(n_heads):
                sout_ref[s, h] = st_ref[s, h].T


def hgrn_recurrence(proj, lb_logits, g_out, s0, *, layer, seq_block, time_block, chunk, c_sub, n_valid,
                    norm_proj_of=None):
    fused = norm_proj_of is not None
    if fused:
        x, mod, g, w_in = norm_proj_of
        bsz, t, d = x.shape
        p = w_in.shape[1]
        mod_spec = lambda k: pl.BlockSpec((seq_block, 1, d), lambda i, j: (i, 0, k))
        in_specs = [pl.BlockSpec((seq_block, time_block, d), lambda i, j: (i, j, 0)), mod_spec(0), mod_spec(1),
                    pl.BlockSpec((1, d), lambda i, j: (0, 0)), pl.BlockSpec((d, p), lambda i, j: (0, 0))]
        args = [x, mod, mod, g.reshape(1, d), w_in]
    else:
        bsz, t, p = proj.shape
        in_specs = [pl.BlockSpec((seq_block, time_block, p), lambda i, j: (i, j, 0))]
        args = [proj]
    hk = p // 4
    n_heads = hk // HEAD_DIM
    has_state = s0 is not None
    st_shape = (seq_block, n_heads, HEAD_DIM, HEAD_DIM)
    st_spec = pl.BlockSpec(st_shape, lambda i, j: (i, 0, 0, 0))
    in_specs += [pl.BlockSpec(lb_logits.shape, lambda i, j: (0, 0)),
                 pl.BlockSpec((1, HEAD_DIM), lambda i, j: (0, 0))]
    args += [lb_logits, g_out.reshape(1, HEAD_DIM)]
    if has_state:
        in_specs.append(st_spec)
        args.append(s0)
    scratch = [pltpu.VMEM(st_shape, F32), pltpu.VMEM((seq_block, chunk, hk), F32)]
    scratch += [pltpu.VMEM((chunk, hk), F32)] * 3
    if fused:
        scratch.append(pltpu.VMEM((seq_block, 2, chunk, p), F32))
    return pl.pallas_call(
        functools.partial(_hgrn_rec_kernel, chunk=chunk, c_sub=c_sub, n_valid=n_valid,
                          has_state=has_state, layer=layer, fused_proj=fused),
        grid=(bsz // seq_block, t // time_block),
        in_specs=in_specs,
        out_specs=[pl.BlockSpec((seq_block, time_block, hk), lambda i, j: (i, j, 0)), st_spec],
        out_shape=[jax.ShapeDtypeStruct((bsz, t, hk), F32),
                   jax.ShapeDtypeStruct((bsz, n_heads, HEAD_DIM, HEAD_DIM), F32)],
        scratch_shapes=scratch,
        compiler_params=_params("parallel", "arbitrary"),
        name="hgrn_recurrence",
    )(*args)


def _sorted_rows(tile, top_k, n_experts):
    return tile * top_k + n_experts * BF16_ROWS


def _resid_router_kernel(*refs, top_k, n_experts, has_w_out, chained, row_chunk):
    refs = list(refs)
    x_ref, y_ref = refs[:2]
    del refs[:2]
    wo_ref = refs.pop(0) if has_w_out else None
    gt_ref, sh_ref, sc_ref, g_ref, wr_ref, br_ref = refs[:6]
    del refs[:6]
    if chained:
        refs.pop(0)
    x1_ref, xs_ref, pos_ref, gate_ref, cnt_ref = refs
    tile, d = x_ref.shape
    n_sorted = xs_ref.shape[0]

    y = y_ref[...]
    if has_w_out:
        y = _dot(y.astype(BF16), wo_ref[...])
    x1 = x_ref[...] + _mod_rows(gt_ref, tile) * y
    x1_ref[...] = x1
    h = _rms_norm(x1, g_ref[...]) * (1.0 + _mod_rows(sc_ref, tile)) + _mod_rows(sh_ref, tile)

    lane = lax.broadcasted_iota(jnp.int32, (tile, LANES), 1).astype(F32)
    logits = jnp.where(lane < n_experts, _dot_hp(h, wr_ref[...]) + br_ref[...], -jnp.inf)
    picks, vals = [], []
    for _ in range(top_k):
        m = jnp.max(logits, axis=-1, keepdims=True)
        pick = jnp.min(jnp.where(logits == m, lane, float(LANES)), axis=-1, keepdims=True)
        picks.append(pick)
        vals.append(m)
        logits = jnp.where(lane == pick, -jnp.inf, logits)
    exps = [jnp.exp(v - vals[0]) for v in vals]
    denom = exps[0]
    for e in exps[1:]:
        denom = denom + e

    onehots = [(lane == p).astype(F32) for p in picks]
    oh_sum = onehots[0]
    for oh in onehots[1:]:
        oh_sum = oh_sum + oh
    row = lax.broadcasted_iota(jnp.int32, (tile, tile), 0)
    col = lax.broadcasted_iota(jnp.int32, (tile, tile), 1)
    before = _dot((row > col).astype(BF16), oh_sum.astype(BF16))
    count = jnp.sum(oh_sum, axis=0, keepdims=True)
    cnt_pad = jnp.floor((count + (BF16_ROWS - 1.0)) * (1.0 / BF16_ROWS)) * BF16_ROWS
    lane8 = lax.broadcasted_iota(jnp.int32, (SUBLANES, LANES), 1)
    run = jnp.broadcast_to(cnt_pad, (SUBLANES, LANES))
    shift = 1
    while shift < n_experts:
        run = run + jnp.where(lane8 >= shift, pltpu.roll(run, shift, 1), 0.0)
        shift *= 2
    pos = before + (run[0:1] - cnt_pad)
    pos_out = jnp.zeros((tile, LANES), F32)
    gate_out = jnp.zeros((tile, LANES), F32)
    for k in range(top_k):
        pos_k = jnp.sum(onehots[k] * pos, axis=-1, keepdims=True)
        pos_out = jnp.where(lane == k, pos_k, pos_out)
        gate_out = jnp.where(lane == k, exps[k] / denom, gate_out)
    pos_ref[...] = pos_out.astype(jnp.int32)
    gate_ref[...] = gate_out
    cnt_ref[0] = cnt_pad.astype(jnp.int32)

    pos_t = pos_out.T
    hb = h.astype(BF16)
    for r0 in range(0, n_sorted, row_chunk):
        slot = (lax.broadcasted_iota(jnp.int32, (row_chunk, tile), 0) + r0).astype(F32)
        sel = jnp.where(slot == pos_t[0:1], 1.0, 0.0)
        for k in range(1, top_k):
            sel = sel + jnp.where(slot == pos_t[k:k + 1], 1.0, 0.0)
        xs_ref[r0:r0 + row_chunk, :] = _dot(sel.astype(BF16), hb).astype(BF16)


def resid_router(x, y, w_out_bf16, mod, g, w_r, b_r, sorted_in, *, tile, rows_per_seq, top_k,
                 block_offset, n_blocks_total):
    n, d = x.shape
    n_experts = w_r.shape[1]
    n_sorted = _sorted_rows(tile, top_k, n_experts)
    w_r_pad = jnp.pad(w_r, ((0, 0), (0, LANES - n_experts)))
    b_r_pad = jnp.pad(b_r, (0, LANES - n_experts)).reshape(1, LANES)
    has_w_out = w_out_bf16 is not None
    chained = sorted_in is not None
    row_spec = pl.BlockSpec((tile, d), lambda i: (i, 0))
    lane_spec = pl.BlockSpec((tile, LANES), lambda i: (i, 0))
    full = lambda a: pl.BlockSpec(a.shape, lambda i: (0,) * a.ndim)
    in_specs = [row_spec, pl.BlockSpec((tile, y.shape[1]), lambda i: (i, 0))]
    args = [x, y]
    if has_w_out:
        in_specs.append(full(w_out_bf16))
        args.append(w_out_bf16)
    in_specs += [_mod_spec(mod, 2, d, tile, rows_per_seq), _mod_spec(mod, 3, d, tile, rows_per_seq),
                 _mod_spec(mod, 4, d, tile, rows_per_seq), pl.BlockSpec((1, d), lambda i: (0, 0)),
                 full(w_r_pad), full(b_r_pad)]
    args += [mod, mod, mod, g.reshape(1, d), w_r_pad, b_r_pad]
    aliases = {}
    if chained:
        aliases = {len(args): 1}
        in_specs.append(pl.BlockSpec(memory_space=pl.ANY))
        args.append(sorted_in)
    n_tiles = n // tile
    return pl.pallas_call(
        functools.partial(_resid_router_kernel, top_k=top_k, n_experts=n_experts, has_w_out=has_w_out,
                          chained=chained, row_chunk=ROUTER_ROW_CHUNK),
        grid=(n_tiles,),
        in_specs=in_specs,
        out_specs=[row_spec,
                   pl.BlockSpec((n_sorted, d), lambda i: (i + block_offset, 0)),
                   lane_spec, lane_spec,
                   pl.BlockSpec((1, 1, LANES), lambda i: (i, 0, 0))],
        out_shape=[jax.ShapeDtypeStruct((n, d), F32),
                   jax.ShapeDtypeStruct((n_blocks_total * n_sorted, d), BF16),
                   jax.ShapeDtypeStruct((n, LANES), jnp.int32), jax.ShapeDtypeStruct((n, LANES), F32),
                   jax.ShapeDtypeStruct((n_tiles, 1, LANES), jnp.int32)],
        input_output_aliases=aliases,
        compiler_params=_params("parallel"),
        name="resid_router",
    )(*args)


def _experts_kernel(te_ref, first_ref, rows_ref, base_ref, slo_ref, shi_ref, wslot_ref, nexte_ref, used_ref,
                    sstart_ref, slen_ref, ssrc_ref,
                    xs_hbm, wgu_hbm, bgu_ref, wdn_hbm, bdn_ref, ys_hbm,
                    xbuf, ybuf, wgu_f, wdn_f, wgu_b, wdn_b, in_sem, out_sem, w_sem,
                    *, layer, limit, alpha, col_chunk):
    del ys_hbm
    i = pl.program_id(0)
    used = used_ref[0]
    tm = xbuf.shape[1]

    def copy(src_rows, dst_rows, slot, inbound):
        if inbound:
            return pltpu.make_async_copy(xs_hbm.at[src_rows, :], xbuf.at[slot, dst_rows, :], in_sem.at[slot])
        return pltpu.make_async_copy(ybuf.at[slot, dst_rows, :], xs_hbm.at[src_rows, :], out_sem.at[slot])

    def piece_copies(tile_idx, slot, inbound, wait):
        if wait:
            rows = pl.ds(0, pl.multiple_of(rows_ref[tile_idx], BF16_ROWS))
            copy(rows, rows, slot, inbound).wait()
            return
        base = base_ref[tile_idx]

        def piece(s, c):
            first = sstart_ref[s] - base
            lo = jnp.maximum(first, 0)
            n_rows = pl.multiple_of(jnp.minimum(first + slen_ref[s], tm) - lo, BF16_ROWS)

            @pl.when(n_rows > 0)
            def _():
                src = pl.multiple_of(ssrc_ref[s] + (lo - first), BF16_ROWS)
                copy(pl.ds(src, n_rows), pl.ds(pl.multiple_of(lo, BF16_ROWS), n_rows), slot, inbound).start()
            return c
        lax.fori_loop(slo_ref[tile_idx], shi_ref[tile_idx], piece, 0)

    def weight_copies(e, slot):
        return (pltpu.make_async_copy(wgu_hbm.at[layer, e], wgu_f.at[slot], w_sem.at[slot]),
                pltpu.make_async_copy(wdn_hbm.at[layer, e], wdn_f.at[slot], w_sem.at[slot]))

    @pl.when(i == 0)
    def _():
        xbuf[...] = jnp.zeros_like(xbuf)
        for cp in weight_copies(te_ref[0], 0):
            cp.start()
        piece_copies(0, 0, True, False)

    @pl.when(i < used)
    def _():
        slot = i % 2

        @pl.when(i + 1 < used)
        def _():
            piece_copies(i + 1, 1 - slot, True, False)

        @pl.when(first_ref[i] == 1)
        def _():
            ws = wslot_ref[i]
            for cp in weight_copies(te_ref[i], ws):
                cp.wait()

            @pl.when(nexte_ref[i] >= 0)
            def _():
                for cp in weight_copies(nexte_ref[i], 1 - ws):
                    cp.start()
            wgu_b[...] = wgu_f[ws].astype(BF16)
            wdn_b[...] = wdn_f[ws].astype(BF16)

        piece_copies(i, slot, True, True)

        @pl.when(i >= 2)
        def _():
            piece_copies(i - 2, slot, False, True)

        d_ff = wdn_b.shape[0]
        tm = xbuf.shape[1]
        e = te_ref[i]
        b_gu = bgu_ref[pl.ds(e, 1), :]
        b_dn = bdn_ref[pl.ds(e, 1), :]

        def mlp(n_rows):
            x = xbuf[slot, :n_rows, :]
            y = None
            for c0 in range(0, d_ff, col_chunk):
                cs = slice(c0, c0 + col_chunk)
                us = slice(d_ff + c0, d_ff + c0 + col_chunk)
                gate = jnp.minimum(_dot(x, wgu_b[:, cs]) + b_gu[:, cs], limit)
                up = jnp.clip(_dot(x, wgu_b[:, us]) + b_gu[:, us], -limit, limit)
                act = ((up + 1.0) * (gate * _sigmoid(alpha * gate))).astype(BF16)
                part = _dot(act, wdn_b[cs, :])
                y = part if y is None else y + part
            ybuf[slot, :n_rows, :] = (y + b_dn).astype(BF16)

        step = tm // MLP_ROW_STEPS
        for part in range(1, MLP_ROW_STEPS + 1):
            @pl.when((rows_ref[i] > (part - 1) * step) & (rows_ref[i] <= part * step))
            def _():
                mlp(part * step)
        piece_copies(i, slot, False, False)

    @pl.when(i == pl.num_programs(0) - 1)
    def _():
        @pl.when(used >= 2)
        def _():
            piece_copies(used - 2, used % 2, False, True)
        piece_copies(used - 1, (used - 1) % 2, False, True)


def moe_experts(sorted_rows, w_gu, b_gu, w_dn, b_dn, tables, *, layer, row_tile, limit, alpha, col_chunk=512):
    n_rows, d = sorted_rows.shape
    d_gu = w_gu.shape[-1]
    d_ff = w_dn.shape[-2]
    n_tiles = tables[0].shape[0]
    vmem = lambda a: pl.BlockSpec(a.shape, lambda i, *_: (0,) * a.ndim)
    any_spec = pl.BlockSpec(memory_space=pl.ANY)
    return pl.pallas_call(
        functools.partial(_experts_kernel, layer=layer, limit=limit, alpha=alpha,
                          col_chunk=min(col_chunk, d_ff)),
        grid_spec=pltpu.PrefetchScalarGridSpec(
            num_scalar_prefetch=len(tables),
            grid=(n_tiles,),
            in_specs=[any_spec, any_spec, vmem(b_gu), any_spec, vmem(b_dn)],
            out_specs=any_spec,
            scratch_shapes=[
                pltpu.VMEM((2, row_tile, d), BF16), pltpu.VMEM((2, row_tile, d), BF16),
                pltpu.VMEM((2, d, d_gu), F32), pltpu.VMEM((2, d_ff, d), F32),
                pltpu.VMEM((d, d_gu), BF16), pltpu.VMEM((d_ff, d), BF16),
                pltpu.SemaphoreType.DMA((2,)), pltpu.SemaphoreType.DMA((2,)), pltpu.SemaphoreType.DMA((2,)),
            ],
        ),
        out_shape=jax.ShapeDtypeStruct((n_rows, d), BF16),
        input_output_aliases={len(tables): 0},
        compiler_params=_params("arbitrary"),
        name="moe_experts",
    )(*tables, sorted_rows, w_gu, b_gu, w_dn, b_dn)


def _expert_tables(cnt, n_sorted, n_tiles, row_tile):
    n_blocks, n_experts = cnt.shape

    def prefix_sum(a):
        n = a.shape[-1]
        upto = jnp.arange(n)[:, None] <= jnp.arange(n)[None, :]
        return jnp.sum(jnp.where(upto, a[..., :, None], 0), axis=-2)

    local_off = prefix_sum(cnt) - cnt
    seg_end = prefix_sum(cnt.T)
    seg_start = seg_end - cnt.T
    seg_src = jnp.arange(n_blocks, dtype=jnp.int32)[None, :] * n_sorted + local_off.T
    total = seg_end[:, -1]
    padded = (total + row_tile - 1) // row_tile * row_tile
    pad_end = prefix_sum(padded)
    pad_start = pad_end - padded
    tiles = jnp.arange(n_tiles, dtype=jnp.int32)
    experts = jnp.arange(n_experts, dtype=jnp.int32)
    n_used = pad_end[-1] // row_tile
    tile_expert = jnp.minimum(jnp.sum(tiles[:, None] * row_tile >= pad_end[None, :], axis=1), n_experts - 1)
    is_expert = tile_expert[:, None] == experts[None, :]

    def of_expert(a):
        if a.ndim == 1:
            return jnp.sum(jnp.where(is_expert, a[None, :], 0), axis=1)
        return jnp.sum(jnp.where(is_expert[:, :, None], a[None, :, :], 0), axis=1)

    live = tiles < n_used
    tile_base = tiles * row_tile - of_expert(pad_start)
    tile_rows = jnp.where(live, jnp.clip(of_expert(total) - tile_base, 0, row_tile), 0)
    tile_first = (live & (tile_base == 0)).astype(jnp.int32)
    seg_lo = jnp.sum(of_expert(seg_end) <= tile_base[:, None], axis=1)
    seg_hi = jnp.sum(of_expert(seg_start) < tile_base[:, None] + row_tile, axis=1)
    seg_lo = tile_expert * n_blocks + jnp.minimum(seg_lo, seg_hi)
    seg_hi = tile_expert * n_blocks + seg_hi
    owns = total > 0
    order = prefix_sum(owns.astype(jnp.int32)) - 1
    experts = jnp.arange(n_experts, dtype=jnp.int32)
    later = (experts[None, :] > experts[:, None]) & owns[None, :]
    next_expert = jnp.min(jnp.where(later, experts[None, :], n_experts), axis=1)
    next_expert = jnp.where(next_expert < n_experts, next_expert, -1)
    i32 = lambda a: a.astype(jnp.int32)
    return (i32(tile_expert), tile_first, i32(tile_rows), i32(tile_base), i32(seg_lo), i32(seg_hi),
            i32(of_expert(order) % 2), i32(of_expert(next_expert)), i32(n_used).reshape(1),
            i32(seg_start.reshape(-1)), i32(cnt.T.reshape(-1)), i32(seg_src.reshape(-1)))


def _combine_kernel(x_ref, pos_ref, gate_ref, gt_ref, gfin_ref, ys_ref, o_ref, *, top_k, final_norm, k_chunk):
    tile, d = x_ref.shape
    n_sorted = ys_ref.shape[0]
    pos = pos_ref[...].astype(F32)
    gates = gate_ref[...]
    acc = None
    for r0 in range(0, n_sorted, k_chunk):
        slot = (lax.broadcasted_iota(jnp.int32, (tile, k_chunk), 1) + r0).astype(F32)
        w = jnp.where(slot == pos[:, 0:1], gates[:, 0:1], 0.0)
        for k in range(1, top_k):
            w = w + jnp.where(slot == pos[:, k:k + 1], gates[:, k:k + 1], 0.0)
        part = _dot(w.astype(BF16), ys_ref[r0:r0 + k_chunk, :])
        acc = part if acc is None else acc + part
    out = x_ref[...] + _mod_rows(gt_ref, tile) * acc
    if final_norm:
        out = _rms_norm(out, gfin_ref[...])
    o_ref[...] = out


def moe_combine(x, ys, pos, gates, mod, g_final, *, tile, rows_per_seq, final_norm, block_offset, n_sorted):
    n, d = x.shape
    return pl.pallas_call(
        functools.partial(_combine_kernel, top_k=TOP_K, final_norm=final_norm, k_chunk=COMBINE_K_CHUNK),
        grid=(n // tile,),
        in_specs=[
            pl.BlockSpec((tile, d), lambda i: (i, 0)),
            pl.BlockSpec((tile, LANES), lambda i: (i, 0)),
            pl.BlockSpec((tile, LANES), lambda i: (i, 0)),
            _mod_spec(mod, 5, d, tile, rows_per_seq),
            pl.BlockSpec((1, d), lambda i: (0, 0)),
            pl.BlockSpec((n_sorted, d), lambda i: (i + block_offset, 0)),
        ],
        out_specs=pl.BlockSpec((tile, d), lambda i: (i, 0)),
        out_shape=jax.ShapeDtypeStruct((n, d), F32),
        compiler_params=_params("parallel"),
        name="moe_combine",
    )(x, pos, gates, mod, g_final.reshape(1, d), ys)


def _pool_groups(h, window_sum, counts, w_ref, scale):
    n_groups = w_ref.shape[0]
    dg = h.shape[-1] // n_groups
    outs = []
    for gi in range(n_groups):
        cols = slice(gi * dg, (gi + 1) * dg)
        pooled = window_sum(gi, cols) / counts[gi] - h[:, cols]
        outs.append(_dot(pooled.astype(BF16), w_ref[gi]))
    return jnp.concatenate(outs, axis=-1) * scale


def _pool_prompt_kernel(x_ref, sh_ref, sc_ref, g_ref, w_ref, scale_ref, y_ref, cache_ref, ext_ref,
                        *, windows, halo):
    tile, d = x_ref.shape[1:]
    j = pl.program_id(1)

    @pl.when(j == 0)
    def _():
        ext_ref[0:halo, :] = jnp.zeros((halo, d), F32)

    h = _rms_norm(x_ref[0], g_ref[...]) * (1.0 + sc_ref[...].reshape(-1, d)) + sh_ref[...].reshape(-1, d)
    ext_ref[halo:halo + tile, :] = h
    pos = j * tile + lax.broadcasted_iota(jnp.int32, (tile, 1), 0)

    def window_sum(gi, cols):
        acc = h[:, cols]
        for s in range(1, windows[gi]):
            acc = acc + ext_ref[halo - s:halo - s + tile, cols]
        return acc
    counts = [jnp.minimum(pos + 1, w).astype(F32) for w in windows]
    y_ref[0] = _pool_groups(h, window_sum, counts, w_ref, scale_ref[...])

    n_keep = cache_ref.shape[1]
    @pl.when(j == pl.num_programs(1) - 1)
    def _():
        cache_ref[0] = ext_ref[halo + tile - n_keep:halo + tile, :]
    ext_ref[0:halo, :] = ext_ref[tile:tile + halo, :]


def pool_mixer_prompt(x, mod, g, w_grp_bf16, scale, *, tile, windows, n_keep):
    bsz, t, d = x.shape
    halo = 16
    assert max(windows) <= halo <= tile and n_keep <= tile
    mod_spec = lambda k: pl.BlockSpec((1, 1, d), lambda b, j: (b, 0, k))
    return pl.pallas_call(
        functools.partial(_pool_prompt_kernel, windows=windows, halo=halo),
        grid=(bsz, t // tile),
        in_specs=[
            pl.BlockSpec((1, tile, d), lambda b, j: (b, j, 0)),
            mod_spec(0), mod_spec(1),
            pl.BlockSpec((1, d), lambda b, j: (0, 0)),
            pl.BlockSpec(w_grp_bf16.shape, lambda b, j: (0, 0, 0)),
            pl.BlockSpec((1, d), lambda b, j: (0, 0)),
        ],
        out_specs=[pl.BlockSpec((1, tile, d), lambda b, j: (b, j, 0)),
                   pl.BlockSpec((1, n_keep, d), lambda b, j: (b, 0, 0))],
        out_shape=[jax.ShapeDtypeStruct((bsz, t, d), F32), jax.ShapeDtypeStruct((bsz, n_keep, d), F32)],
        scratch_shapes=[pltpu.VMEM((halo + tile, d), F32)],
        compiler_params=_params("parallel", "arbitrary"),
        name="pool_mixer_prompt",
    )(x, mod, mod, g.reshape(1, d), w_grp_bf16, scale.reshape(1, d))


def _pool_sample_kernel(x_ref, buf_ref, sh_ref, sc_ref, g_ref, w_ref, scale_ref, y_ref, cache_ref,
                        *, windows, start_pos):
    t_len = x_ref.shape[0]
    n_prev = buf_ref.shape[0]
    hs = [_rms_norm(x_ref[t], g_ref[...]) * (1.0 + sc_ref[...]) + sh_ref[...] for t in range(t_len)]

    def ext(r):
        return buf_ref[r] if r < n_prev else hs[r - n_prev]

    for t in range(t_len):
        def window_sum(gi, cols):
            acc = hs[t][:, cols]
            for s in range(1, windows[gi]):
                acc = acc + ext(n_prev + t - s)[:, cols]
            return acc
        counts = [float(min(start_pos + t + 1, w)) for w in windows]
        y_ref[t] = _pool_groups(hs[t], window_sum, counts, w_ref, scale_ref[...])
    for r in range(n_prev):
        cache_ref[r] = ext(t_len + r)


def pool_mixer_sample(x_t, buf_t, mod, g, w_grp_bf16, scale, *, seq_block, windows, start_pos):
    t_len, n_seq, d = x_t.shape
    n_prev = buf_t.shape[0]
    assert start_pos >= n_prev >= max(windows) - 1
    mod_spec = lambda k: pl.BlockSpec((seq_block, d), lambda i: (i, k))
    return pl.pallas_call(
        functools.partial(_pool_sample_kernel, windows=windows, start_pos=start_pos),
        grid=(n_seq // seq_block,),
        in_specs=[
            pl.BlockSpec((t_len, seq_block, d), lambda i: (0, i, 0)),
            pl.BlockSpec((n_prev, seq_block, d), lambda i: (0, i, 0)),
            mod_spec(0), mod_spec(1),
            pl.BlockSpec((1, d), lambda i: (0, 0)),
            pl.BlockSpec(w_grp_bf16.shape, lambda i: (0, 0, 0)),
            pl.BlockSpec((1, d), lambda i: (0, 0)),
        ],
        out_specs=[pl.BlockSpec((t_len, seq_block, d), lambda i: (0, i, 0)),
                   pl.BlockSpec((n_prev, seq_block, d), lambda i: (0, i, 0))],
        out_shape=[jax.ShapeDtypeStruct((t_len, n_seq, d), F32), jax.ShapeDtypeStruct((n_prev, n_seq, d), F32)],
        compiler_params=_params("parallel"),
        name="pool_mixer_sample",
    )(x_t, buf_t, mod, mod, g.reshape(1, d), w_grp_bf16, scale.reshape(1, d))


TOP_K = 4
SWIGLU_LIMIT = 7.0
SWIGLU_ALPHA = 1.702
POOL_WINDOWS = (2, 4, 8, 16)
PAST_LEN = 16384
PROJ_TILE = 256
MOE_TOKEN_TILE = 512
MOE_ROW_TILE = 512
ROUTER_ROW_CHUNK = 256
COMBINE_K_CHUNK = 512
HGRN_TIME_BLOCK = 1024
HGRN_CHUNK = 256
HGRN_SUB = 32
SAMPLE_T_PAD = 8
SAMPLE_SEQ_BLOCK = 8


def kernel(x_prompt, x_sample, c_prompt, c_sample, state_hgrn, cache_pool, g_norm_mix, g_norm_ffn, w_ada, b_ada, w_in_hgrn, lb_logits, g_out_hgrn, w_out_hgrn, w_grp_pool, scale_pool, w_router, b_router, w_gate_up, b_gate_up, w_down, b_down, g_final):
    bp, tp, d = x_prompt.shape
    bs, ts, _ = x_sample.shape
    n_p, n_s = bp * tp, bs * ts
    n_experts = w_router.shape[-1]
    hk = w_out_hgrn.shape[1]
    assert n_s == MOE_TOKEN_TILE and n_p % MOE_TOKEN_TILE == 0
    blocks_p = n_p // MOE_TOKEN_TILE
    n_blocks = blocks_p + 1
    n_sorted = _sorted_rows(MOE_TOKEN_TILE, TOP_K, n_experts)
    n_row_tiles = -(-(n_blocks * n_sorted + n_experts * (MOE_ROW_TILE - BF16_ROWS)) // MOE_ROW_TILE)

    mod = adaln(jnp.concatenate([c_prompt, c_sample], axis=0), w_ada, b_ada)
    mod_p = [mod[l, :bp][:, None, :] for l in range(mod.shape[0])]
    mod_s = [mod[l, bp:] for l in range(mod.shape[0])]

    xp = x_prompt.reshape(n_p, d)
    xs = x_sample.transpose(1, 0, 2).reshape(n_s, d)

    def moe(layer, x_p, y_p, x_s, y_s, w_out, final_norm):
        route = functools.partial(resid_router, g=g_norm_ffn[layer], w_r=w_router[layer], b_r=b_router[layer],
                                  tile=MOE_TOKEN_TILE, top_k=TOP_K, n_blocks_total=n_blocks)
        x1_p, sorted_rows, pos_p, gate_p, cnt_p = route(
            x_p, y_p, w_out, mod_p[layer], sorted_in=None, rows_per_seq=tp, block_offset=0)
        x1_s, sorted_rows, pos_s, gate_s, cnt_s = route(
            x_s, y_s, w_out, mod_s[layer], sorted_in=sorted_rows, rows_per_seq=None, block_offset=blocks_p)
        cnt8 = jnp.concatenate([cnt_p, cnt_s], axis=0)[:, 0, :n_experts]
        tables = _expert_tables(cnt8, n_sorted, n_row_tiles, MOE_ROW_TILE)
        ys = moe_experts(sorted_rows, w_gate_up, b_gate_up[layer], w_down, b_down[layer], tables,
                         layer=layer, row_tile=MOE_ROW_TILE, limit=SWIGLU_LIMIT, alpha=SWIGLU_ALPHA)
        combine = functools.partial(moe_combine, ys=ys, g_final=g_final, tile=MOE_TOKEN_TILE,
                                    final_norm=final_norm, n_sorted=n_sorted)
        out_p = combine(x1_p, pos=pos_p, gates=gate_p, mod=mod_p[layer], rows_per_seq=tp, block_offset=0)
        out_s = combine(x1_s, pos=pos_s, gates=gate_s, mod=mod_s[layer], rows_per_seq=None,
                        block_offset=blocks_p)
        return out_p, out_s

    w_in = w_in_hgrn[0].astype(BF16)
    proj_s = norm_proj(xs, mod_s[0], g_norm_mix[0], w_in, tile=bs, rows_per_seq=None)
    o_p, state_p = hgrn_recurrence(None, lb_logits, g_out_hgrn[0], None,
                                   layer=0, seq_block=1, time_block=HGRN_TIME_BLOCK, chunk=HGRN_CHUNK,
                                   c_sub=HGRN_SUB, n_valid=HGRN_CHUNK,
                                   norm_proj_of=(x_prompt, mod_p[0], g_norm_mix[0], w_in))
    proj_sb = jnp.pad(proj_s.reshape(ts, bs, 4 * hk).transpose(1, 0, 2), ((0, 0), (0, SAMPLE_T_PAD - ts), (0, 0)))
    o_s, state_s = hgrn_recurrence(proj_sb, lb_logits, g_out_hgrn[0], state_hgrn[0],
                                   layer=0, seq_block=SAMPLE_SEQ_BLOCK, time_block=SAMPLE_T_PAD,
                                   chunk=SAMPLE_T_PAD, c_sub=SAMPLE_T_PAD, n_valid=ts)
    o_s = o_s[:, :ts].transpose(1, 0, 2).reshape(n_s, hk)
    x_p, x_s = moe(0, xp, o_p.reshape(n_p, hk), xs, o_s, w_out_hgrn[0].astype(BF16), False)

    w_grp = w_grp_pool[0].astype(BF16)
    n_keep = cache_pool.shape[2]
    y_p, cache_p = pool_mixer_prompt(x_p.reshape(bp, tp, d), mod_p[1], g_norm_mix[1], w_grp, scale_pool[0],
                                     tile=PROJ_TILE, windows=POOL_WINDOWS, n_keep=n_keep)
    y_s, cache_s = pool_mixer_sample(x_s.reshape(ts, bs, d), cache_pool[0].transpose(1, 0, 2), mod_s[1],
                                     g_norm_mix[1], w_grp, scale_pool[0],
                                     seq_block=32, windows=POOL_WINDOWS, start_pos=PAST_LEN)
    x_p, x_s = moe(1, x_p, y_p.reshape(n_p, d), x_s, y_s.reshape(n_s, d), None, True)

    return (x_p.reshape(bp, tp, d), x_s.reshape(ts, bs, d).transpose(1, 0, 2),
            state_p[None], state_s[None], cache_p[None], cache_s.transpose(1, 0, 2)[None])
```

```python
import functools

import jax
import jax.numpy as jnp
from jax import lax
from jax.experimental import pallas as pl
from jax.experimental.pallas import tpu as pltpu

F32 = jnp.float32
BF16 = jnp.bfloat16

RMS_EPS = 1e-6
LANES = 128
SUBLANES = 8
BF16_ROWS = 16
HEAD_DIM = 128
VMEM_LIMIT = 56 * 1024 * 1024

_dot = functools.partial(jnp.dot, preferred_element_type=F32)


def _params(*semantics):
    return pltpu.CompilerParams(dimension_semantics=semantics, vmem_limit_bytes=VMEM_LIMIT)


def _split_bf16(x, n):
    parts, r = [], x
    for _ in range(n):
        p = r.astype(BF16)
        parts.append(p)
        r = r - p.astype(F32)
    return parts


def _dot_hp(a, b):
    a_hi, a_lo = _split_bf16(a, 2)
    b_hi, b_lo = _split_bf16(b, 2)
    return _dot(a_hi, b_hi) + (_dot(a_hi, b_lo) + _dot(a_lo, b_hi))


def _sigmoid(x):
    return 1.0 / (1.0 + jnp.exp(-x))


def _silu(x):
    return x * _sigmoid(x)


def _rms_norm(x, g):
    ms = jnp.mean(x * x, axis=-1, keepdims=True)
    return x * lax.rsqrt(ms + RMS_EPS) * g


def _adaln_kernel(c_ref, w_ref, b_ref, o_ref):
    o_ref[0] = _dot_hp(_silu(c_ref[...]), w_ref[0]) + b_ref[0]


def adaln(c_all, w_ada, b_ada, *, col_block=1536):
    n_seq, d = c_all.shape
    n_layers, _, d6 = w_ada.shape
    return pl.pallas_call(
        _adaln_kernel,
        grid=(n_layers, d6 // col_block),
        in_specs=[
            pl.BlockSpec((n_seq, d), lambda l, j: (0, 0)),
            pl.BlockSpec((1, d, col_block), lambda l, j: (l, 0, j)),
            pl.BlockSpec((1, 1, col_block), lambda l, j: (l, 0, j)),
        ],
        out_specs=pl.BlockSpec((1, n_seq, col_block), lambda l, j: (l, 0, j)),
        out_shape=jax.ShapeDtypeStruct((n_layers, n_seq, d6), F32),
        compiler_params=_params("parallel", "parallel"),
        name="adaln",
    )(c_all, w_ada, b_ada.reshape(n_layers, 1, d6))


def _mod_spec(mod, k, d, tile, rows_per_seq):
    if rows_per_seq is None:
        return pl.BlockSpec((mod.shape[0], d), lambda i: (0, k))
    tiles_per_seq = rows_per_seq // tile
    return pl.BlockSpec((1, 1, d), lambda i: (i // tiles_per_seq, 0, k))


def _mod_rows(ref, tile):
    m = ref[...].reshape(-1, ref.shape[-1])
    if m.shape[0] not in (1, tile):
        m = jnp.concatenate([m] * (tile // m.shape[0]), axis=0)
    return m


def _norm_proj_kernel(x_ref, sh_ref, sc_ref, g_ref, w_ref, o_ref):
    tile = x_ref.shape[0]
    h = _rms_norm(x_ref[...], g_ref[...]) * (1.0 + _mod_rows(sc_ref, tile)) + _mod_rows(sh_ref, tile)
    o_ref[...] = _dot(h.astype(BF16), w_ref[...])


def norm_proj(x, mod, g, w_bf16, *, tile, rows_per_seq):
    n, d = x.shape
    p = w_bf16.shape[1]
    return pl.pallas_call(
        _norm_proj_kernel,
        grid=(n // tile,),
        in_specs=[
            pl.BlockSpec((tile, d), lambda i: (i, 0)),
            _mod_spec(mod, 0, d, tile, rows_per_seq),
            _mod_spec(mod, 1, d, tile, rows_per_seq),
            pl.BlockSpec((1, d), lambda i: (0, 0)),
            pl.BlockSpec((d, p), lambda i: (0, 0)),
        ],
        out_specs=pl.BlockSpec((tile, p), lambda i: (i, 0)),
        out_shape=jax.ShapeDtypeStruct((n, p), F32),
        compiler_params=_params("parallel"),
        name="hgrn_norm_proj",
    )(x, mod, mod, g.reshape(1, d), w_bf16)


def _cumsum_rows(x, tri):
    hi, mid, lo = _split_bf16(x, 3)
    return _dot(tri, hi) + (_dot(tri, mid) + _dot(tri, lo))


MAX_LOG_DECAY_RANGE = 80.0
MLP_ROW_STEPS = 4


def _hgrn_prep(proj, lb, n_valid):
    c = proj.shape[0]
    hk = proj.shape[1] // 4
    row = lax.broadcasted_iota(jnp.int32, (c, c), 0)
    col = lax.broadcasted_iota(jnp.int32, (c, c), 1)
    zf = proj[:, hk:2 * hk]
    e = jnp.exp(-jnp.abs(zf))
    r = 1.0 / (1.0 + e)
    pos = zf >= 0
    sig_p = jnp.where(pos, 1.0, e) * r
    sig_n = jnp.where(pos, e, 1.0) * r
    logf = jnp.log(lb + (1.0 - lb) * sig_p)
    k = (1.0 - lb) * sig_n
    if n_valid < c:
        live = lax.broadcasted_iota(jnp.int32, (c, 1), 0) < n_valid
        logf = jnp.where(live, logf, 0.0)
        k = jnp.where(live, k, 0.0)
    b = _cumsum_rows(logf, (row >= col).astype(BF16))
    return _silu(proj[:, :hk]), k, proj[:, 2 * hk:3 * hk], _silu(proj[:, 3 * hk:]), b


def _decay_range(b, c_sub):
    c = b.shape[0]
    worst = None
    for i in range(c // c_sub):
        span = b[i * c_sub:i * c_sub + 1, :] - b[(i + 1) * c_sub - 1:(i + 1) * c_sub, :]
        worst = span if worst is None else jnp.maximum(worst, span)
    return jnp.max(worst)


def _head_norm_gate(o, gate, gout):
    outs = []
    for h in range(o.shape[1] // HEAD_DIM):
        hs = slice(h * HEAD_DIM, (h + 1) * HEAD_DIM)
        oh = o[:, hs]
        outs.append(oh * lax.rsqrt(jnp.mean(oh * oh, axis=-1, keepdims=True) + RMS_EPS) * gout * gate[:, hs])
    return jnp.concatenate(outs, axis=-1)


def _hgrn_chunk(prep, gout, st_refs, seq, c_sub):
    q, k, v, gate, b = prep
    c = q.shape[0]
    n_heads = q.shape[1] // HEAD_DIM
    row = lax.broadcasted_iota(jnp.int32, (c, c), 0)
    col = lax.broadcasted_iota(jnp.int32, (c, c), 1)
    causal = row >= col
    n_sub = c // c_sub
    subs = [slice(i * c_sub, (i + 1) * c_sub) for i in range(n_sub)]

    intra, inter = [], []
    for h in range(n_heads):
        hs = slice(h * HEAD_DIM, (h + 1) * HEAD_DIM)
        bh, qh, kh, vh = b[:, hs], q[:, hs], k[:, hs], v[:, hs]
        vb = vh.astype(BF16)
        refs = [bh[i * c_sub + c_sub // 2:i * c_sub + c_sub // 2 + 1, :] for i in range(n_sub)]
        k_own = [kh[rs] * jnp.exp(jnp.minimum(ref - bh[rs], MAX_LOG_DECAY_RANGE))
                 for rs, ref in zip(subs, refs)]
        a_rows = []
        for i in range(n_sub):
            q_hat = (qh[subs[i]] * jnp.exp(bh[subs[i]] - refs[i])).astype(BF16)
            parts = [k_own[j] * jnp.exp(refs[i] - refs[j]) for j in range(i)] + [k_own[i]]
            parts += [jnp.zeros((c_sub, HEAD_DIM), F32)] * (n_sub - 1 - i)
            k_hat = (jnp.concatenate(parts, axis=0) if n_sub > 1 else parts[0]).astype(BF16)
            a_rows.append(lax.dot_general(q_hat, k_hat, (((1,), (1,)), ((), ())),
                                          preferred_element_type=F32))
        att = jnp.where(causal, jnp.concatenate(a_rows, axis=0) if len(a_rows) > 1 else a_rows[0], 0.0)
        intra.append(_dot(att.astype(BF16), vb))
        st = st_refs[seq, h]
        inter.append(lax.dot_general((qh * jnp.exp(bh)).astype(BF16), st.astype(BF16),
                                     (((1,), (1,)), ((), ())), preferred_element_type=F32))
        b_last = bh[c - 1:c, :]
        k_dec = (kh * jnp.exp(b_last - bh)).astype(BF16)
        st_refs[seq, h] = st * jnp.exp(b_last) + lax.dot_general(
            vb, k_dec, (((0,), (0,)), ((), ())), preferred_element_type=F32)
    inter = jnp.concatenate(inter, axis=-1)
    return _head_norm_gate(jnp.concatenate(intra, axis=-1) + inter, gate, gout), inter


def _hgrn_chunk_exact(prep, inter, gout, q_ref, b_ref, oi_ref):
    q, k, v, gate, b = prep
    c = q.shape[0]
    n_heads = q.shape[1] // HEAD_DIM
    q_ref[...] = q
    b_ref[...] = b
    key_row = lax.broadcasted_iota(jnp.int32, (c, 1), 0)

    def row_group(g, carry):
        rows = pl.ds(pl.multiple_of(g * SUBLANES, SUBLANES), SUBLANES)
        for h in range(n_heads):
            hs = slice(h * HEAD_DIM, (h + 1) * HEAD_DIM)
            q_g, b_g = q_ref[rows, hs], b_ref[rows, hs]
            o_rows = []
            for r in range(SUBLANES):
                decay = jnp.exp(jnp.minimum(b_g[r:r + 1] - b[:, hs], 0.0))
                score = jnp.sum(decay * k[:, hs] * q_g[r:r + 1], axis=-1, keepdims=True)
                score = jnp.where(key_row <= g * SUBLANES + r, score, 0.0)
                o_rows.append(jnp.sum(score * v[:, hs], axis=0, keepdims=True))
            oi_ref[rows, hs] = jnp.concatenate(o_rows, axis=0)
        return carry
    lax.fori_loop(0, c // SUBLANES, row_group, 0)
    return _head_norm_gate(oi_ref[...] + inter, gate, gout)


def _lower_bound(lb_logits, layer):
    e = jnp.exp(lb_logits - jnp.max(lb_logits, axis=0, keepdims=True))
    return jnp.sum(e[:layer + 1], axis=0, keepdims=True) / jnp.sum(e, axis=0, keepdims=True)


def _hgrn_rec_kernel(*refs, chunk, c_sub, n_valid, has_state, layer, fused_proj):
    refs = list(refs)
    if fused_proj:
        x_ref, sh_ref, sc_ref, g_ref, w_ref = refs[:5]
        del refs[:5]
        proj_ref = None
    else:
        proj_ref = refs.pop(0)
    lb_ref, gout_ref = refs[:2]
    del refs[:2]
    s0_ref = refs.pop(0) if has_state else None
    o_ref, sout_ref, st_ref, inter_ref, q_ref, b_ref, oi_ref = refs[:7]
    proj_buf = refs[7] if fused_proj else None
    bb, tb, _ = o_ref.shape
    n_heads = st_ref.shape[1]
    n_chunks = tb // chunk
    j = pl.program_id(1)

    @pl.when(j == 0)
    def _():
        if has_state:
            for s in range(bb):
                for h in range(n_heads):
                    st_ref[s, h] = s0_ref[s, h].T
        else:
            st_ref[...] = jnp.zeros_like(st_ref)

    lb = _lower_bound(lb_ref[...], layer)
    gout = gout_ref[...]

    def project(s, ci):
        rows = pl.ds(pl.multiple_of(ci * chunk, chunk), chunk)
        h = _rms_norm(x_ref[s, rows, :], g_ref[...]) * (1.0 + sc_ref[s]) + sh_ref[s]
        return _dot(h.astype(BF16), w_ref[...])

    if fused_proj:
        for s in range(bb):
            proj_buf[s, 0] = project(s, 0)

    def chunk_body(ci, carry):
        rows = pl.ds(pl.multiple_of(ci * chunk, chunk), chunk)

        def load_proj(s):
            return proj_buf[s, ci % 2] if fused_proj else proj_ref[s, rows, :]
        span = None
        preps = [_hgrn_prep(load_proj(s), lb, n_valid) for s in range(bb)]
        for s, prep in enumerate(preps):
            o_ref[s, rows, :], inter_ref[s] = _hgrn_chunk(prep, gout, st_ref, s, c_sub)
            worst = _decay_range(prep[4], c_sub)
            span = worst if span is None else jnp.maximum(span, worst)
            if fused_proj:
                proj_buf[s, (ci + 1) % 2] = project(s, jnp.minimum(ci + 1, n_chunks - 1))

        @pl.when(jnp.logical_not(span <= MAX_LOG_DECAY_RANGE))
        def _():
            for s in range(bb):
                prep = _hgrn_prep(load_proj(s), lb, n_valid)
                o_ref[s, rows, :] = _hgrn_chunk_exact(prep, inter_ref[s], gout, q_ref, b_ref, oi_ref)
        return carry
    lax.fori_loop(0, n_chunks, chunk_body, 0)

    @pl.when(j == pl.num_programs(1) - 1)
    def _():
        for s in range(bb):
            for h in range(n_heads):
                sout_ref[s, h] = st_ref[s, h].T


def hgrn_recurrence(proj, lb_logits, g_out, s0, *, layer, seq_block, time_block, chunk, c_sub, n_valid,
                    norm_proj_of=None):
    fused = norm_proj_of is not None
    if fused:
        x, mod, g, w_in = norm_proj_of
        bsz, t, d = x.shape
        p = w_in.shape[1]
        mod_spec = lambda k: pl.BlockSpec((seq_block, 1, d), lambda i, j: (i, 0, k))
        in_specs = [pl.BlockSpec((seq_block, time_block, d), lambda i, j: (i, j, 0)), mod_spec(0), mod_spec(1),
                    pl.BlockSpec((1, d), lambda i, j: (0, 0)), pl.BlockSpec((d, p), lambda i, j: (0, 0))]
        args = [x, mod, mod, g.reshape(1, d), w_in]
    else:
        bsz, t, p = proj.shape
        in_specs = [pl.BlockSpec((seq_block, time_block, p), lambda i, j: (i, j, 0))]
        args = [proj]
    hk = p // 4
    n_heads = hk // HEAD_DIM
    has_state = s0 is not None
    st_shape = (seq_block, n_heads, HEAD_DIM, HEAD_DIM)
    st_spec = pl.BlockSpec(st_shape, lambda i, j: (i, 0, 0, 0))
    in_specs += [pl.BlockSpec(lb_logits.shape, lambda i, j: (0, 0)),
                 pl.BlockSpec((1, HEAD_DIM), lambda i, j: (0, 0))]
    args += [lb_logits, g_out.reshape(1, HEAD_DIM)]
    if has_state:
        in_specs.append(st_spec)
        args.append(s0)
    scratch = [pltpu.VMEM(st_shape, F32), pltpu.VMEM((seq_block, chunk, hk), F32)]
    scratch += [pltpu.VMEM((chunk, hk), F32)] * 3
    if fused:
        scratch.append(pltpu.VMEM((seq_block, 2, chunk, p), F32))
    return pl.pallas_call(
        functools.partial(_hgrn_rec_kernel, chunk=chunk, c_sub=c_sub, n_valid=n_valid,
                          has_state=has_state, layer=layer, fused_proj=fused),
        grid=(bsz // seq_block, t // time_block),
        in_specs=in_specs,
        out_specs=[pl.BlockSpec((seq_block, time_block, hk), lambda i, j: (i, j, 0)), st_spec],
        out_shape=[jax.ShapeDtypeStruct((bsz, t, hk), F32),
                   jax.ShapeDtypeStruct((bsz, n_heads, HEAD_DIM, HEAD_DIM), F32)],
        scratch_shapes=scratch,
        compiler_params=_params("parallel", "arbitrary"),
        name="hgrn_recurrence",
    )(*args)


def _sorted_rows(tile, top_k, n_experts):
    return tile * top_k + n_experts * BF16_ROWS


def _resid_router_kernel(*refs, top_k, n_experts, has_w_out, chained, row_chunk, pool):
    refs = list(refs)
    x_ref = refs.pop(0)
    if pool is None:
        y_ref = refs.pop(0)
        wo_ref = refs.pop(0) if has_w_out else None
    else:
        msh_ref, msc_ref, mg_ref, wgrp_ref, mscale_ref = refs[:5]
        del refs[:5]
    gt_ref, sh_ref, sc_ref, g_ref, wr_ref, br_ref = refs[:6]
    del refs[:6]
    if chained:
        refs.pop(0)
    x1_ref, xs_ref, pos_ref, gate_ref, cnt_ref = refs[:5]
    tile, d = x_ref.shape
    n_sorted = xs_ref.shape[0]

    if pool is None:
        y = y_ref[...]
        if has_w_out:
            y = _dot(y.astype(BF16), wo_ref[...])
    else:
        windows, halo, tiles_per_seq = pool
        cache_ref, ext_ref = refs[5:7]
        part = pl.program_id(0) % tiles_per_seq

        @pl.when(part == 0)
        def _():
            ext_ref[0:halo, :] = jnp.zeros((halo, d), F32)

        hm = _rms_norm(x_ref[...], mg_ref[...]) * (1.0 + _mod_rows(msc_ref, tile)) + _mod_rows(msh_ref, tile)
        ext_ref[halo:halo + tile, :] = hm
        token = part * tile + lax.broadcasted_iota(jnp.int32, (tile, 1), 0)

        def window_sum(gi, cols):
            acc = hm[:, cols]
            for s in range(1, windows[gi]):
                acc = acc + ext_ref[halo - s:halo - s + tile, cols]
            return acc
        counts = [jnp.minimum(token + 1, w).astype(F32) for w in windows]
        y = _pool_groups(hm, window_sum, counts, wgrp_ref, mscale_ref[...])

        n_keep = cache_ref.shape[1]
        @pl.when(part == tiles_per_seq - 1)
        def _():
            cache_ref[0] = ext_ref[halo + tile - n_keep:halo + tile, :]
        ext_ref[0:halo, :] = ext_ref[tile:tile + halo, :]
    x1 = x_ref[...] + _mod_rows(gt_ref, tile) * y
    x1_ref[...] = x1
    h = _rms_norm(x1, g_ref[...]) * (1.0 + _mod_rows(sc_ref, tile)) + _mod_rows(sh_ref, tile)

    lane = lax.broadcasted_iota(jnp.int32, (tile, LANES), 1).astype(F32)
    logits = jnp.where(lane < n_experts, _dot_hp(h, wr_ref[...]) + br_ref[...], -jnp.inf)
    picks, vals = [], []
    for _ in range(top_k):
        m = jnp.max(logits, axis=-1, keepdims=True)
        pick = jnp.min(jnp.where(logits == m, lane, float(LANES)), axis=-1, keepdims=True)
        picks.append(pick)
        vals.append(m)
        logits = jnp.where(lane == pick, -jnp.inf, logits)
    exps = [jnp.exp(v - vals[0]) for v in vals]
    denom = exps[0]
    for e in exps[1:]:
        denom = denom + e

    onehots = [(lane == p).astype(F32) for p in picks]
    oh_sum = onehots[0]
    for oh in onehots[1:]:
        oh_sum = oh_sum + oh
    row = lax.broadcasted_iota(jnp.int32, (tile, tile), 0)
    col = lax.broadcasted_iota(jnp.int32, (tile, tile), 1)
    before = _dot((row > col).astype(BF16), oh_sum.astype(BF16))
    count = jnp.sum(oh_sum, axis=0, keepdims=True)
    cnt_pad = jnp.floor((count + (BF16_ROWS - 1.0)) * (1.0 / BF16_ROWS)) * BF16_ROWS
    lane8 = lax.broadcasted_iota(jnp.int32, (SUBLANES, LANES), 1)
    run = jnp.broadcast_to(cnt_pad, (SUBLANES, LANES))
    shift = 1
    while shift < n_experts:
        run = run + jnp.where(lane8 >= shift, pltpu.roll(run, shift, 1), 0.0)
        shift *= 2
    pos = before + (run[0:1] - cnt_pad)
    pos_out = jnp.zeros((tile, LANES), F32)
    gate_out = jnp.zeros((tile, LANES), F32)
    for k in range(top_k):
        pos_k = jnp.sum(onehots[k] * pos, axis=-1, keepdims=True)
        pos_out = jnp.where(lane == k, pos_k, pos_out)
        gate_out = jnp.where(lane == k, exps[k] / denom, gate_out)
    pos_ref[...] = pos_out.astype(jnp.int32)
    gate_ref[...] = gate_out
    cnt_ref[0] = cnt_pad.astype(jnp.int32)

    pos_t = pos_out.T
    hb = h.astype(BF16)
    for r0 in range(0, n_sorted, row_chunk):
        slot = (lax.broadcasted_iota(jnp.int32, (row_chunk, tile), 0) + r0).astype(F32)
        sel = jnp.where(slot == pos_t[0:1], 1.0, 0.0)
        for k in range(1, top_k):
            sel = sel + jnp.where(slot == pos_t[k:k + 1], 1.0, 0.0)
        xs_ref[r0:r0 + row_chunk, :] = _dot(sel.astype(BF16), hb).astype(BF16)


def resid_router(x, y, w_out_bf16, mod, g, w_r, b_r, sorted_in, *, tile, rows_per_seq, top_k,
                 block_offset, n_blocks_total, pool=None):
    n, d = x.shape
    n_experts = w_r.shape[1]
    n_sorted = _sorted_rows(tile, top_k, n_experts)
    w_r_pad = jnp.pad(w_r, ((0, 0), (0, LANES - n_experts)))
    b_r_pad = jnp.pad(b_r, (0, LANES - n_experts)).reshape(1, LANES)
    has_w_out = w_out_bf16 is not None
    chained = sorted_in is not None
    row_spec = pl.BlockSpec((tile, d), lambda i: (i, 0))
    lane_spec = pl.BlockSpec((tile, LANES), lambda i: (i, 0))
    full = lambda a: pl.BlockSpec(a.shape, lambda i: (0,) * a.ndim)
    extra_out_specs, extra_out_shape, scratch, pool_static = [], [], [], None
    if pool is None:
        in_specs = [row_spec, pl.BlockSpec((tile, y.shape[1]), lambda i: (i, 0))]
        args = [x, y]
        if has_w_out:
            in_specs.append(full(w_out_bf16))
            args.append(w_out_bf16)
    else:
        g_mix, w_grp, scale, windows, n_keep = pool
        halo = 2 * SUBLANES
        tiles_per_seq = rows_per_seq // tile
        assert max(windows) <= halo <= tile and n_keep <= tile
        in_specs = [row_spec, _mod_spec(mod, 0, d, tile, rows_per_seq), _mod_spec(mod, 1, d, tile, rows_per_seq),
                    pl.BlockSpec((1, d), lambda i: (0, 0)), full(w_grp), pl.BlockSpec((1, d), lambda i: (0, 0))]
        args = [x, mod, mod, g_mix.reshape(1, d), w_grp, scale.reshape(1, d)]
        extra_out_specs = [pl.BlockSpec((1, n_keep, d), lambda i: (i // tiles_per_seq, 0, 0))]
        extra_out_shape = [jax.ShapeDtypeStruct((n // rows_per_seq, n_keep, d), F32)]
        scratch = [pltpu.VMEM((halo + tile, d), F32)]
        pool_static = (windows, halo, tiles_per_seq)
    in_specs += [_mod_spec(mod, 2, d, tile, rows_per_seq), _mod_spec(mod, 3, d, tile, rows_per_seq),
                 _mod_spec(mod, 4, d, tile, rows_per_seq), pl.BlockSpec((1, d), lambda i: (0, 0)),
                 full(w_r_pad), full(b_r_pad)]
    args += [mod, mod, mod, g.reshape(1, d), w_r_pad, b_r_pad]
    aliases = {}
    if chained:
        aliases = {len(args): 1}
        in_specs.append(pl.BlockSpec(memory_space=pl.ANY))
        args.append(sorted_in)
    n_tiles = n // tile
    return pl.pallas_call(
        functools.partial(_resid_router_kernel, top_k=top_k, n_experts=n_experts, has_w_out=has_w_out,
                          chained=chained, row_chunk=ROUTER_ROW_CHUNK, pool=pool_static),
        grid=(n_tiles,),
        in_specs=in_specs,
        out_specs=[row_spec,
                   pl.BlockSpec((n_sorted, d), lambda i: (i + block_offset, 0)),
                   lane_spec, lane_spec,
                   pl.BlockSpec((1, 1, LANES), lambda i: (i, 0, 0))] + extra_out_specs,
        out_shape=[jax.ShapeDtypeStruct((n, d), F32),
                   jax.ShapeDtypeStruct((n_blocks_total * n_sorted, d), BF16),
                   jax.ShapeDtypeStruct((n, LANES), jnp.int32), jax.ShapeDtypeStruct((n, LANES), F32),
                   jax.ShapeDtypeStruct((n_tiles, 1, LANES), jnp.int32)] + extra_out_shape,
        scratch_shapes=scratch,
        input_output_aliases=aliases,
        compiler_params=_params("arbitrary" if pool is not None else "parallel"),
        name="resid_router",
    )(*args)


def _experts_kernel(te_ref, first_ref, rows_ref, base_ref, slo_ref, shi_ref, wslot_ref, nexte_ref, used_ref,
                    sstart_ref, slen_ref, ssrc_ref,
                    xs_hbm, wgu_hbm, bgu_ref, wdn_hbm, bdn_ref, ys_hbm,
                    xbuf, ybuf, wgu_f, wdn_f, wgu_b, wdn_b, in_sem, out_sem, w_sem,
                    *, layer, limit, alpha, col_chunk):
    del ys_hbm
    i = pl.program_id(0)
    used = used_ref[0]
    tm = xbuf.shape[1]

    def copy(src_rows, dst_rows, slot, inbound):
        if inbound:
            return pltpu.make_async_copy(xs_hbm.at[src_rows, :], xbuf.at[slot, dst_rows, :], in_sem.at[slot])
        return pltpu.make_async_copy(ybuf.at[slot, dst_rows, :], xs_hbm.at[src_rows, :], out_sem.at[slot])

    def piece_copies(tile_idx, slot, inbound, wait):
        if wait:
            rows = pl.ds(0, pl.multiple_of(rows_ref[tile_idx], BF16_ROWS))
            copy(rows, rows, slot, inbound).wait()
            return
        base = base_ref[tile_idx]

        def piece(s, c):
            first = sstart_ref[s] - base
            lo = jnp.maximum(first, 0)
            n_rows = pl.multiple_of(jnp.minimum(first + slen_ref[s], tm) - lo, BF16_ROWS)

            @pl.when(n_rows > 0)
            def _():
                src = pl.multiple_of(ssrc_ref[s] + (lo - first), BF16_ROWS)
                copy(pl.ds(src, n_rows), pl.ds(pl.multiple_of(lo, BF16_ROWS), n_rows), slot, inbound).start()
            return c
        lax.fori_loop(slo_ref[tile_idx], shi_ref[tile_idx], piece, 0)

    def weight_copies(e, slot):
        return (pltpu.make_async_copy(wgu_hbm.at[layer, e], wgu_f.at[slot], w_sem.at[slot]),
                pltpu.make_async_copy(wdn_hbm.at[layer, e], wdn_f.at[slot], w_sem.at[slot]))

    @pl.when(i == 0)
    def _():
        xbuf[...] = jnp.zeros_like(xbuf)
        for cp in weight_copies(te_ref[0], 0):
            cp.start()
        piece_copies(0, 0, True, False)

    @pl.when(i < used)
    def _():
        slot = i % 2

        @pl.when(i + 1 < used)
        def _():
            piece_copies(i + 1, 1 - slot, True, False)

        @pl.when(first_ref[i] == 1)
        def _():
            ws = wslot_ref[i]
            for cp in weight_copies(te_ref[i], ws):
                cp.wait()

            @pl.when(nexte_ref[i] >= 0)
            def _():
                for cp in weight_copies(nexte_ref[i], 1 - ws):
                    cp.start()
            wgu_b[...] = wgu_f[ws].astype(BF16)
            wdn_b[...] = wdn_f[ws].astype(BF16)

        piece_copies(i, slot, True, True)

        @pl.when(i >= 2)
        def _():
            piece_copies(i - 2, slot, False, True)

        d_ff = wdn_b.shape[0]
        tm = xbuf.shape[1]
        e = te_ref[i]
        b_gu = bgu_ref[pl.ds(e, 1), :]
        b_dn = bdn_ref[pl.ds(e, 1), :]

        def mlp(n_rows):
            x = xbuf[slot, :n_rows, :]
            y = None
            for c0 in range(0, d_ff, col_chunk):
                cs = slice(c0, c0 + col_chunk)
                us = slice(d_ff + c0, d_ff + c0 + col_chunk)
                gate = jnp.minimum(_dot(x, wgu_b[:, cs]) + b_gu[:, cs], limit)
                up = jnp.clip(_dot(x, wgu_b[:, us]) + b_gu[:, us], -limit, limit)
                act = ((up + 1.0) * (gate * _sigmoid(alpha * gate))).astype(BF16)
                part = _dot(act, wdn_b[cs, :])
                y = part if y is None else y + part
            ybuf[slot, :n_rows, :] = (y + b_dn).astype(BF16)

        step = tm // MLP_ROW_STEPS
        for part in range(1, MLP_ROW_STEPS + 1):
            @pl.when((rows_ref[i] > (part - 1) * step) & (rows_ref[i] <= part * step))
            def _():
                mlp(part * step)
        piece_copies(i, slot, False, False)

    @pl.when(i == pl.num_programs(0) - 1)
    def _():
        @pl.when(used >= 2)
        def _():
            piece_copies(used - 2, used % 2, False, True)
        piece_copies(used - 1, (used - 1) % 2, False, True)


def moe_experts(sorted_rows, w_gu, b_gu, w_dn, b_dn, tables, *, layer, row_tile, limit, alpha, col_chunk=512):
    n_rows, d = sorted_rows.shape
    d_gu = w_gu.shape[-1]
    d_ff = w_dn.shape[-2]
    n_tiles = tables[0].shape[0]
    vmem = lambda a: pl.BlockSpec(a.shape, lambda i, *_: (0,) * a.ndim)
    any_spec = pl.BlockSpec(memory_space=pl.ANY)
    return pl.pallas_call(
        functools.partial(_experts_kernel, layer=layer, limit=limit, alpha=alpha,
                          col_chunk=min(col_chunk, d_ff)),
        grid_spec=pltpu.PrefetchScalarGridSpec(
            num_scalar_prefetch=len(tables),
            grid=(n_tiles,),
            in_specs=[any_spec, any_spec, vmem(b_gu), any_spec, vmem(b_dn)],
            out_specs=any_spec,
            scratch_shapes=[
                pltpu.VMEM((2, row_tile, d), BF16), pltpu.VMEM((2, row_tile, d), BF16),
                pltpu.VMEM((2, d, d_gu), F32), pltpu.VMEM((2, d_ff, d), F32),
                pltpu.VMEM((d, d_gu), BF16), pltpu.VMEM((d_ff, d), BF16),
                pltpu.SemaphoreType.DMA((2,)), pltpu.SemaphoreType.DMA((2,)), pltpu.SemaphoreType.DMA((2,)),
            ],
        ),
        out_shape=jax.ShapeDtypeStruct((n_rows, d), BF16),
        input_output_aliases={len(tables): 0},
        compiler_params=_params("arbitrary"),
        name="moe_experts",
    )(*tables, sorted_rows, w_gu, b_gu, w_dn, b_dn)


def _expert_tables(cnt, n_sorted, n_tiles, row_tile):
    n_blocks, n_experts = cnt.shape

    def prefix_sum(a):
        n = a.shape[-1]
        upto = jnp.arange(n)[:, None] <= jnp.arange(n)[None, :]
        return jnp.sum(jnp.where(upto, a[..., :, None], 0), axis=-2)

    local_off = prefix_sum(cnt) - cnt
    seg_end = prefix_sum(cnt.T)
    seg_start = seg_end - cnt.T
    seg_src = jnp.arange(n_blocks, dtype=jnp.int32)[None, :] * n_sorted + local_off.T
    total = seg_end[:, -1]
    padded = (total + row_tile - 1) // row_tile * row_tile
    pad_end = prefix_sum(padded)
    pad_start = pad_end - padded
    tiles = jnp.arange(n_tiles, dtype=jnp.int32)
    experts = jnp.arange(n_experts, dtype=jnp.int32)
    n_used = pad_end[-1] // row_tile
    tile_expert = jnp.minimum(jnp.sum(tiles[:, None] * row_tile >= pad_end[None, :], axis=1), n_experts - 1)
    is_expert = tile_expert[:, None] == experts[None, :]

    def of_expert(a):
        if a.ndim == 1:
            return jnp.sum(jnp.where(is_expert, a[None, :], 0), axis=1)
        return jnp.sum(jnp.where(is_expert[:, :, None], a[None, :, :], 0), axis=1)

    live = tiles < n_used
    tile_base = tiles * row_tile - of_expert(pad_start)
    tile_rows = jnp.where(live, jnp.clip(of_expert(total) - tile_base, 0, row_tile), 0)
    tile_first = (live & (tile_base == 0)).astype(jnp.int32)
    seg_lo = jnp.sum(of_expert(seg_end) <= tile_base[:, None], axis=1)
    seg_hi = jnp.sum(of_expert(seg_start) < tile_base[:, None] + row_tile, axis=1)
    seg_lo = tile_expert * n_blocks + jnp.minimum(seg_lo, seg_hi)
    seg_hi = tile_expert * n_blocks + seg_hi
    owns = total > 0
    order = prefix_sum(owns.astype(jnp.int32)) - 1
    experts = jnp.arange(n_experts, dtype=jnp.int32)
    later = (experts[None, :] > experts[:, None]) & owns[None, :]
    next_expert = jnp.min(jnp.where(later, experts[None, :], n_experts), axis=1)
    next_expert = jnp.where(next_expert < n_experts, next_expert, -1)
    i32 = lambda a: a.astype(jnp.int32)
    return (i32(tile_expert), tile_first, i32(tile_rows), i32(tile_base), i32(seg_lo), i32(seg_hi),
            i32(of_expert(order) % 2), i32(of_expert(next_expert)), i32(n_used).reshape(1),
            i32(seg_start.reshape(-1)), i32(cnt.T.reshape(-1)), i32(seg_src.reshape(-1)))


def _combine_kernel(x_ref, pos_ref, gate_ref, gt_ref, gfin_ref, ys_ref, o_ref, *, top_k, final_norm, k_chunk):
    tile, d = x_ref.shape
    n_sorted = ys_ref.shape[0]
    pos = pos_ref[...].astype(F32)
    gates = gate_ref[...]
    acc = None
    for r0 in range(0, n_sorted, k_chunk):
        slot = (lax.broadcasted_iota(jnp.int32, (tile, k_chunk), 1) + r0).astype(F32)
        w = jnp.where(slot == pos[:, 0:1], gates[:, 0:1], 0.0)
        for k in range(1, top_k):
            w = w + jnp.where(slot == pos[:, k:k + 1], gates[:, k:k + 1], 0.0)
        part = _dot(w.astype(BF16), ys_ref[r0:r0 + k_chunk, :])
        acc = part if acc is None else acc + part
    out = x_ref[...] + _mod_rows(gt_ref, tile) * acc
    if final_norm:
        out = _rms_norm(out, gfin_ref[...])
    o_ref[...] = out


def moe_combine(x, ys, pos, gates, mod, g_final, *, tile, rows_per_seq, final_norm, block_offset, n_sorted):
    n, d = x.shape
    return pl.pallas_call(
        functools.partial(_combine_kernel, top_k=TOP_K, final_norm=final_norm, k_chunk=COMBINE_K_CHUNK),
        grid=(n // tile,),
        in_specs=[
            pl.BlockSpec((tile, d), lambda i: (i, 0)),
            pl.BlockSpec((tile, LANES), lambda i: (i, 0)),
            pl.BlockSpec((tile, LANES), lambda i: (i, 0)),
            _mod_spec(mod, 5, d, tile, rows_per_seq),
            pl.BlockSpec((1, d), lambda i: (0, 0)),
            pl.BlockSpec((n_sorted, d), lambda i: (i + block_offset, 0)),
        ],
        out_specs=pl.BlockSpec((tile, d), lambda i: (i, 0)),
        out_shape=jax.ShapeDtypeStruct((n, d), F32),
        compiler_params=_params("parallel"),
        name="moe_combine",
    )(x, pos, gates, mod, g_final.reshape(1, d), ys)


def _pool_groups(h, window_sum, counts, w_ref, scale):
    n_groups = w_ref.shape[0]
    dg = h.shape[-1] // n_groups
    outs = []
    for gi in range(n_groups):
        cols = slice(gi * dg, (gi + 1) * dg)
        pooled = window_sum(gi, cols) / counts[gi] - h[:, cols]
        outs.append(_dot(pooled.astype(BF16), w_ref[gi]))
    return jnp.concatenate(outs, axis=-1) * scale


def _pool_sample_kernel(x_ref, buf_ref, sh_ref, sc_ref, g_ref, w_ref, scale_ref, y_ref, cache_ref,
                        *, windows, start_pos):
    t_len = x_ref.shape[0]
    n_prev = buf_ref.shape[0]
    hs = [_rms_norm(x_ref[t], g_ref[...]) * (1.0 + sc_ref[...]) + sh_ref[...] for t in range(t_len)]

    def ext(r):
        return buf_ref[r] if r < n_prev else hs[r - n_prev]

    for t in range(t_len):
        def window_sum(gi, cols):
            acc = hs[t][:, cols]
            for s in range(1, windows[gi]):
                acc = acc + ext(n_prev + t - s)[:, cols]
            return acc
        counts = [float(min(start_pos + t + 1, w)) for w in windows]
        y_ref[t] = _pool_groups(hs[t], window_sum, counts, w_ref, scale_ref[...])
    for r in range(n_prev):
        cache_ref[r] = ext(t_len + r)


def pool_mixer_sample(x_t, buf_t, mod, g, w_grp_bf16, scale, *, seq_block, windows, start_pos):
    t_len, n_seq, d = x_t.shape
    n_prev = buf_t.shape[0]
    assert start_pos >= n_prev >= max(windows) - 1
    mod_spec = lambda k: pl.BlockSpec((seq_block, d), lambda i: (i, k))
    return pl.pallas_call(
        functools.partial(_pool_sample_kernel, windows=windows, start_pos=start_pos),
        grid=(n_seq // seq_block,),
        in_specs=[
            pl.BlockSpec((t_len, seq_block, d), lambda i: (0, i, 0)),
            pl.BlockSpec((n_prev, seq_block, d), lambda i: (0, i, 0)),
            mod_spec(0), mod_spec(1),
            pl.BlockSpec((1, d), lambda i: (0, 0)),
            pl.BlockSpec(w_grp_bf16.shape, lambda i: (0, 0, 0)),
            pl.BlockSpec((1, d), lambda i: (0, 0)),
        ],
        out_specs=[pl.BlockSpec((t_len, seq_block, d), lambda i: (0, i, 0)),
                   pl.BlockSpec((n_prev, seq_block, d), lambda i: (0, i, 0))],
        out_shape=[jax.ShapeDtypeStruct((t_len, n_seq, d), F32), jax.ShapeDtypeStruct((n_prev, n_seq, d), F32)],
        compiler_params=_params("parallel"),
        name="pool_mixer_sample",
    )(x_t, buf_t, mod, mod, g.reshape(1, d), w_grp_bf16, scale.reshape(1, d))


TOP_K = 4
SWIGLU_LIMIT = 7.0
SWIGLU_ALPHA = 1.702
POOL_WINDOWS = (2, 4, 8, 16)
PAST_LEN = 16384
MOE_TOKEN_TILE = 512
MOE_ROW_TILE = 512
ROUTER_ROW_CHUNK = 256
COMBINE_K_CHUNK = 512
HGRN_TIME_BLOCK = 1024
HGRN_CHUNK = 256
HGRN_SUB = 32
SAMPLE_T_PAD = 8
SAMPLE_SEQ_BLOCK = 8


def kernel(x_prompt, x_sample, c_prompt, c_sample, state_hgrn, cache_pool, g_norm_mix, g_norm_ffn, w_ada, b_ada, w_in_hgrn, lb_logits, g_out_hgrn, w_out_hgrn, w_grp_pool, scale_pool, w_router, b_router, w_gate_up, b_gate_up, w_down, b_down, g_final):
    bp, tp, d = x_prompt.shape
    bs, ts, _ = x_sample.shape
    n_p, n_s = bp * tp, bs * ts
    n_experts = w_router.shape[-1]
    hk = w_out_hgrn.shape[1]
    assert n_s == MOE_TOKEN_TILE and n_p % MOE_TOKEN_TILE == 0
    blocks_p = n_p // MOE_TOKEN_TILE
    n_blocks = blocks_p + 1
    n_sorted = _sorted_rows(MOE_TOKEN_TILE, TOP_K, n_experts)
    n_row_tiles = -(-(n_blocks * n_sorted + n_experts * (MOE_ROW_TILE - BF16_ROWS)) // MOE_ROW_TILE)

    mod = adaln(jnp.concatenate([c_prompt, c_sample], axis=0), w_ada, b_ada)
    mod_p = [mod[l, :bp][:, None, :] for l in range(mod.shape[0])]
    mod_s = [mod[l, bp:] for l in range(mod.shape[0])]

    xp = x_prompt.reshape(n_p, d)
    xs = x_sample.transpose(1, 0, 2).reshape(n_s, d)

    def moe(layer, x_p, y_p, x_s, y_s, w_out, final_norm, pool_p=None):
        route = functools.partial(resid_router, g=g_norm_ffn[layer], w_r=w_router[layer], b_r=b_router[layer],
                                  tile=MOE_TOKEN_TILE, top_k=TOP_K, n_blocks_total=n_blocks)
        x1_p, sorted_rows, pos_p, gate_p, cnt_p, *cache = route(
            x_p, y_p, w_out, mod_p[layer], sorted_in=None, rows_per_seq=tp, block_offset=0, pool=pool_p)
        x1_s, sorted_rows, pos_s, gate_s, cnt_s = route(
            x_s, y_s, w_out, mod_s[layer], sorted_in=sorted_rows, rows_per_seq=None, block_offset=blocks_p)
        cnt8 = jnp.concatenate([cnt_p, cnt_s], axis=0)[:, 0, :n_experts]
        tables = _expert_tables(cnt8, n_sorted, n_row_tiles, MOE_ROW_TILE)
        ys = moe_experts(sorted_rows, w_gate_up, b_gate_up[layer], w_down, b_down[layer], tables,
                         layer=layer, row_tile=MOE_ROW_TILE, limit=SWIGLU_LIMIT, alpha=SWIGLU_ALPHA)
        combine = functools.partial(moe_combine, ys=ys, g_final=g_final, tile=MOE_TOKEN_TILE,
                                    final_norm=final_norm, n_sorted=n_sorted)
        out_p = combine(x1_p, pos=pos_p, gates=gate_p, mod=mod_p[layer], rows_per_seq=tp, block_offset=0)
        out_s = combine(x1_s, pos=pos_s, gates=gate_s, mod=mod_s[layer], rows_per_seq=None,
                        block_offset=blocks_p)
        return (out_p, out_s, *cache)

    w_in = w_in_hgrn[0].astype(BF16)
    proj_s = norm_proj(xs, mod_s[0], g_norm_mix[0], w_in, tile=bs, rows_per_seq=None)
    o_p, state_p = hgrn_recurrence(None, lb_logits, g_out_hgrn[0], None,
                                   layer=0, seq_block=1, time_block=HGRN_TIME_BLOCK, chunk=HGRN_CHUNK,
                                   c_sub=HGRN_SUB, n_valid=HGRN_CHUNK,
                                   norm_proj_of=(x_prompt, mod_p[0], g_norm_mix[0], w_in))
    proj_sb = jnp.pad(proj_s.reshape(ts, bs, 4 * hk).transpose(1, 0, 2), ((0, 0), (0, SAMPLE_T_PAD - ts), (0, 0)))
    o_s, state_s = hgrn_recurrence(proj_sb, lb_logits, g_out_hgrn[0], state_hgrn[0],
                                   layer=0, seq_block=SAMPLE_SEQ_BLOCK, time_block=SAMPLE_T_PAD,
                                   chunk=SAMPLE_T_PAD, c_sub=SAMPLE_T_PAD, n_valid=ts)
    o_s = o_s[:, :ts].transpose(1, 0, 2).reshape(n_s, hk)
    x_p, x_s = moe(0, xp, o_p.reshape(n_p, hk), xs, o_s, w_out_hgrn[0].astype(BF16), False)

    w_grp = w_grp_pool[0].astype(BF16)
    n_keep = cache_pool.shape[2]
    y_s, cache_s = pool_mixer_sample(x_s.reshape(ts, bs, d), cache_pool[0].transpose(1, 0, 2), mod_s[1],
                                     g_norm_mix[1], w_grp, scale_pool[0],
                                     seq_block=32, windows=POOL_WINDOWS, start_pos=PAST_LEN)
    x_p, x_s, cache_p = moe(1, x_p, None, x_s, y_s.reshape(n_s, d), None, True,
                            pool_p=(g_norm_mix[1], w_grp, scale_pool[0], POOL_WINDOWS, n_keep))

    return (x_p.reshape(bp, tp, d), x_s.reshape(ts, bs, d).transpose(1, 0, 2),
            state_p[None], state_s[None], cache_p[None], cache_s.transpose(1, 0, 2)[None])
```

```python
import functools

import jax
import jax.numpy as jnp
from jax import lax
from jax.experimental import pallas as pl
from jax.experimental.pallas import tpu as pltpu

F32 = jnp.float32
BF16 = jnp.bfloat16

RMS_EPS = 1e-6
LANES = 128
SUBLANES = 8
BF16_ROWS = 16
HEAD_DIM = 128
VMEM_LIMIT = 56 * 1024 * 1024

_dot = functools.partial(jnp.dot, preferred_element_type=F32)


def _params(*semantics):
    return pltpu.CompilerParams(dimension_semantics=semantics, vmem_limit_bytes=VMEM_LIMIT)


def _split_bf16(x, n):
    parts, r = [], x
    for _ in range(n):
        p = r.astype(BF16)
        parts.append(p)
        r = r - p.astype(F32)
    return parts


def _dot_hp(a, b):
    a_hi, a_lo = _split_bf16(a, 2)
    b_hi, b_lo = _split_bf16(b, 2)
    return _dot(a_hi, b_hi) + (_dot(a_hi, b_lo) + _dot(a_lo, b_hi))


def _sigmoid(x):
    return 1.0 / (1.0 + jnp.exp(-x))


def _silu(x):
    return x * _sigmoid(x)


def _rms_norm(x, g):
    ms = jnp.mean(x * x, axis=-1, keepdims=True)
    return x * lax.rsqrt(ms + RMS_EPS) * g


def _adaln_kernel(c_ref, w_ref, b_ref, o_ref):
    o_ref[0] = _dot_hp(_silu(c_ref[...]), w_ref[0]) + b_ref[0]


def adaln(c_all, w_ada, b_ada, *, col_block=1536):
    n_seq, d = c_all.shape
    n_layers, _, d6 = w_ada.shape
    return pl.pallas_call(
        _adaln_kernel,
        grid=(n_layers, d6 // col_block),
        in_specs=[
            pl.BlockSpec((n_seq, d), lambda l, j: (0, 0)),
            pl.BlockSpec((1, d, col_block), lambda l, j: (l, 0, j)),
            pl.BlockSpec((1, 1, col_block), lambda l, j: (l, 0, j)),
        ],
        out_specs=pl.BlockSpec((1, n_seq, col_block), lambda l, j: (l, 0, j)),
        out_shape=jax.ShapeDtypeStruct((n_layers, n_seq, d6), F32),
        compiler_params=_params("parallel", "parallel"),
        name="adaln",
    )(c_all, w_ada, b_ada.reshape(n_layers, 1, d6))


def _mod_spec(mod, k, d, tile, rows_per_seq):
    if rows_per_seq is None:
        return pl.BlockSpec((mod.shape[0], d), lambda i: (0, k))
    tiles_per_seq = rows_per_seq // tile
    return pl.BlockSpec((1, 1, d), lambda i: (i // tiles_per_seq, 0, k))


def _mod_rows(ref, tile):
    m = ref[...].reshape(-1, ref.shape[-1])
    if m.shape[0] not in (1, tile):
        m = jnp.concatenate([m] * (tile // m.shape[0]), axis=0)
    return m


def _norm_proj_kernel(x_ref, sh_ref, sc_ref, g_ref, w_ref, o_ref):
    tile = x_ref.shape[0]
    h = _rms_norm(x_ref[...], g_ref[...]) * (1.0 + _mod_rows(sc_ref, tile)) + _mod_rows(sh_ref, tile)
    o_ref[...] = _dot(h.astype(BF16), w_ref[...])


def norm_proj(x, mod, g, w_bf16, *, tile, rows_per_seq):
    n, d = x.shape
    p = w_bf16.shape[1]
    return pl.pallas_call(
        _norm_proj_kernel,
        grid=(n // tile,),
        in_specs=[
            pl.BlockSpec((tile, d), lambda i: (i, 0)),
            _mod_spec(mod, 0, d, tile, rows_per_seq),
            _mod_spec(mod, 1, d, tile, rows_per_seq),
            pl.BlockSpec((1, d), lambda i: (0, 0)),
            pl.BlockSpec((d, p), lambda i: (0, 0)),
        ],
        out_specs=pl.BlockSpec((tile, p), lambda i: (i, 0)),
        out_shape=jax.ShapeDtypeStruct((n, p), F32),
        compiler_params=_params("parallel"),
        name="hgrn_norm_proj",
    )(x, mod, mod, g.reshape(1, d), w_bf16)


def _cumsum_rows(x, tri):
    hi, mid, lo = _split_bf16(x, 3)
    return _dot(tri, hi) + (_dot(tri, mid) + _dot(tri, lo))


MAX_LOG_DECAY_RANGE = 80.0
MLP_ROW_STEPS = 4


def _hgrn_prep(proj, lb, n_valid):
    c = proj.shape[0]
    hk = proj.shape[1] // 4
    row = lax.broadcasted_iota(jnp.int32, (c, c), 0)
    col = lax.broadcasted_iota(jnp.int32, (c, c), 1)
    zf = proj[:, hk:2 * hk]
    e = jnp.exp(-jnp.abs(zf))
    r = 1.0 / (1.0 + e)
    pos = zf >= 0
    sig_p = jnp.where(pos, 1.0, e) * r
    sig_n = jnp.where(pos, e, 1.0) * r
    logf = jnp.log(lb + (1.0 - lb) * sig_p)
    k = (1.0 - lb) * sig_n
    if n_valid < c:
        live = lax.broadcasted_iota(jnp.int32, (c, 1), 0) < n_valid
        logf = jnp.where(live, logf, 0.0)
        k = jnp.where(live, k, 0.0)
    b = _cumsum_rows(logf, (row >= col).astype(BF16))
    return _silu(proj[:, :hk]), k, proj[:, 2 * hk:3 * hk], _silu(proj[:, 3 * hk:]), b


def _decay_range(b, c_sub):
    c = b.shape[0]
    worst = None
    for i in range(c // c_sub):
        span = b[i * c_sub:i * c_sub + 1, :] - b[(i + 1) * c_sub - 1:(i + 1) * c_sub, :]
        worst = span if worst is None else jnp.maximum(worst, span)
    return jnp.max(worst)


def _head_norm_gate(o, gate, gout):
    outs = []
    for h in range(o.shape[1] // HEAD_DIM):
        hs = slice(h * HEAD_DIM, (h + 1) * HEAD_DIM)
        oh = o[:, hs]
        outs.append(oh * lax.rsqrt(jnp.mean(oh * oh, axis=-1, keepdims=True) + RMS_EPS) * gout * gate[:, hs])
    return jnp.concatenate(outs, axis=-1)


def _hgrn_chunk(prep, gout, st_refs, seq, c_sub):
    q, k, v, gate, b = prep
    c = q.shape[0]
    n_heads = q.shape[1] // HEAD_DIM
    row = lax.broadcasted_iota(jnp.int32, (c, c), 0)
    col = lax.broadcasted_iota(jnp.int32, (c, c), 1)
    causal = row >= col
    n_sub = c // c_sub
    subs = [slice(i * c_sub, (i + 1) * c_sub) for i in range(n_sub)]

    intra, inter = [], []
    for h in range(n_heads):
        hs = slice(h * HEAD_DIM, (h + 1) * HEAD_DIM)
        bh, qh, kh, vh = b[:, hs], q[:, hs], k[:, hs], v[:, hs]
        vb = vh.astype(BF16)
        refs = [bh[i * c_sub + c_sub // 2:i * c_sub + c_sub // 2 + 1, :] for i in range(n_sub)]
        k_own = [kh[rs] * jnp.exp(jnp.minimum(ref - bh[rs], MAX_LOG_DECAY_RANGE))
                 for rs, ref in zip(subs, refs)]
        a_rows = []
        for i in range(n_sub):
            q_hat = (qh[subs[i]] * jnp.exp(bh[subs[i]] - refs[i])).astype(BF16)
            parts = [k_own[j] * jnp.exp(refs[i] - refs[j]) for j in range(i)] + [k_own[i]]
            parts += [jnp.zeros((c_sub, HEAD_DIM), F32)] * (n_sub - 1 - i)
            k_hat = (jnp.concatenate(parts, axis=0) if n_sub > 1 else parts[0]).astype(BF16)
            a_rows.append(lax.dot_general(q_hat, k_hat, (((1,), (1,)), ((), ())),
                                          preferred_element_type=F32))
        att = jnp.where(causal, jnp.concatenate(a_rows, axis=0) if len(a_rows) > 1 else a_rows[0], 0.0)
        intra.append(_dot(att.astype(BF16), vb))
        st = st_refs[seq, h]
        inter.append(lax.dot_general((qh * jnp.exp(bh)).astype(BF16), st.astype(BF16),
                                     (((1,), (1,)), ((), ())), preferred_element_type=F32))
        b_last = bh[c - 1:c, :]
        k_dec = (kh * jnp.exp(b_last - bh)).astype(BF16)
        st_refs[seq, h] = st * jnp.exp(b_last) + lax.dot_general(
            vb, k_dec, (((0,), (0,)), ((), ())), preferred_element_type=F32)
    inter = jnp.concatenate(inter, axis=-1)
    return _head_norm_gate(jnp.concatenate(intra, axis=-1) + inter, gate, gout), inter


def _hgrn_chunk_exact(prep, inter, gout, q_ref, b_ref, oi_ref):
    q, k, v, gate, b = prep
    c = q.shape[0]
    n_heads = q.shape[1] // HEAD_DIM
    q_ref[...] = q
    b_ref[...] = b
    key_row = lax.broadcasted_iota(jnp.int32, (c, 1), 0)

    def row_group(g, carry):
        rows = pl.ds(pl.multiple_of(g * SUBLANES, SUBLANES), SUBLANES)
        for h in range(n_heads):
            hs = slice(h * HEAD_DIM, (h + 1) * HEAD_DIM)
            q_g, b_g = q_ref[rows, hs], b_ref[rows, hs]
            o_rows = []
            for r in range(SUBLANES):
                decay = jnp.exp(jnp.minimum(b_g[r:r + 1] - b[:, hs], 0.0))
                score = jnp.sum(decay * k[:, hs] * q_g[r:r + 1], axis=-1, keepdims=True)
                score = jnp.where(key_row <= g * SUBLANES + r, score, 0.0)
                o_rows.append(jnp.sum(score * v[:, hs], axis=0, keepdims=True))
            oi_ref[rows, hs] = jnp.concatenate(o_rows, axis=0)
        return carry
    lax.fori_loop(0, c // SUBLANES, row_group, 0)
    return _head_norm_gate(oi_ref[...] + inter, gate, gout)


def _lower_bound(lb_logits, layer):
    e = jnp.exp(lb_logits - jnp.max(lb_logits, axis=0, keepdims=True))
    return jnp.sum(e[:layer + 1], axis=0, keepdims=True) / jnp.sum(e, axis=0, keepdims=True)


def _hgrn_rec_kernel(*refs, chunk, c_sub, n_valid, has_state, layer, fused_proj):
    refs = list(refs)
    if fused_proj:
        x_ref, sh_ref, sc_ref, g_ref, w_ref = refs[:5]
        del refs[:5]
        proj_ref = None
    else:
        proj_ref = refs.pop(0)
    lb_ref, gout_ref = refs[:2]
    del refs[:2]
    s0_ref = refs.pop(0) if has_state else None
    o_ref, sout_ref, st_ref, inter_ref, q_ref, b_ref, oi_ref = refs[:7]
    proj_buf = refs[7] if fused_proj else None
    bb, tb, _ = o_ref.shape
    n_heads = st_ref.shape[1]
    n_chunks = tb // chunk
    j = pl.program_id(1)

    @pl.when(j == 0)
    def _():
        if has_state:
            for s in range(bb):
                for h in range(n_heads):
                    st_ref[s, h] = s0_ref[s, h].T
        else:
            st_ref[...] = jnp.zeros_like(st_ref)

    lb = _lower_bound(lb_ref[...], layer)
    gout = gout_ref[...]

    def project(s, ci):
        rows = pl.ds(pl.multiple_of(ci * chunk, chunk), chunk)
        h = _rms_norm(x_ref[s, rows, :], g_ref[...]) * (1.0 + sc_ref[s]) + sh_ref[s]
        return _dot(h.astype(BF16), w_ref[...])

    if fused_proj:
        for s in range(bb):
            proj_buf[s, 0] = project(s, 0)

    def chunk_body(ci, carry):
        rows = pl.ds(pl.multiple_of(ci * chunk, chunk), chunk)

        def load_proj(s):
            return proj_buf[s, ci % 2] if fused_proj else proj_ref[s, rows, :]
        span = None
        preps = [_hgrn_prep(load_proj(s), lb, n_valid) for s in range(bb)]
        for s, prep in enumerate(preps):
            o, inter_ref[s] = _hgrn_chunk(prep, gout, st_ref, s, c_sub)
            o_ref[s, rows, :] = o.astype(o_ref.dtype)
            worst = _decay_range(prep[4], c_sub)
            span = worst if span is None else jnp.maximum(span, worst)
            if fused_proj:
                proj_buf[s, (ci + 1) % 2] = project(s, jnp.minimum(ci + 1, n_chunks - 1))

        @pl.when(jnp.logical_not(span <= MAX_LOG_DECAY_RANGE))
        def _():
            for s in range(bb):
                prep = _hgrn_prep(load_proj(s), lb, n_valid)
                o_ref[s, rows, :] = _hgrn_chunk_exact(prep, inter_ref[s], gout, q_ref, b_ref,
                                                      oi_ref).astype(o_ref.dtype)
        return carry
    lax.fori_loop(0, n_chunks, chunk_body, 0)

    @pl.when(j == pl.num_programs(1) - 1)
    def _():
        for s in range(bb):
            for h in range(n_heads):
                sout_ref[s, h] = st_ref[s, h].T


def hgrn_recurrence(proj, lb_logits, g_out, s0, *, layer, seq_block, time_block, chunk, c_sub, n_valid,
                    norm_proj_of=None, out_dtype=F32):
    fused = norm_proj_of is not None
    if fused:
        x, mod, g, w_in = norm_proj_of
        bsz, t, d = x.shape
        p = w_in.shape[1]
        mod_spec = lambda k: pl.BlockSpec((seq_block, 1, d), lambda i, j: (i, 0, k))
        in_specs = [pl.BlockSpec((seq_block, time_block, d), lambda i, j: (i, j, 0)), mod_spec(0), mod_spec(1),
                    pl.BlockSpec((1, d), lambda i, j: (0, 0)), pl.BlockSpec((d, p), lambda i, j: (0, 0))]
        args = [x, mod, mod, g.reshape(1, d), w_in]
    else:
        bsz, t, p = proj.shape
        in_specs = [pl.BlockSpec((seq_block, time_block, p), lambda i, j: (i, j, 0))]
        args = [proj]
    hk = p // 4
    n_heads = hk // HEAD_DIM
    has_state = s0 is not None
    st_shape = (seq_block, n_heads, HEAD_DIM, HEAD_DIM)
    st_spec = pl.BlockSpec(st_shape, lambda i, j: (i, 0, 0, 0))
    in_specs += [pl.BlockSpec(lb_logits.shape, lambda i, j: (0, 0)),
                 pl.BlockSpec((1, HEAD_DIM), lambda i, j: (0, 0))]
    args += [lb_logits, g_out.reshape(1, HEAD_DIM)]
    if has_state:
        in_specs.append(st_spec)
        args.append(s0)
    scratch = [pltpu.VMEM(st_shape, F32), pltpu.VMEM((seq_block, chunk, hk), F32)]
    scratch += [pltpu.VMEM((chunk, hk), F32)] * 3
    if fused:
        scratch.append(pltpu.VMEM((seq_block, 2, chunk, p), F32))
    return pl.pallas_call(
        functools.partial(_hgrn_rec_kernel, chunk=chunk, c_sub=c_sub, n_valid=n_valid,
                          has_state=has_state, layer=layer, fused_proj=fused),
        grid=(bsz // seq_block, t // time_block),
        in_specs=in_specs,
        out_specs=[pl.BlockSpec((seq_block, time_block, hk), lambda i, j: (i, j, 0)), st_spec],
        out_shape=[jax.ShapeDtypeStruct((bsz, t, hk), out_dtype),
                   jax.ShapeDtypeStruct((bsz, n_heads, HEAD_DIM, HEAD_DIM), F32)],
        scratch_shapes=scratch,
        compiler_params=_params("parallel", "arbitrary"),
        name="hgrn_recurrence",
    )(*args)


def _sorted_rows(tile, top_k, n_experts):
    return tile * top_k + n_experts * BF16_ROWS


def _resid_router_kernel(*refs, top_k, n_experts, has_w_out, chained, row_chunk, pool):
    refs = list(refs)
    x_ref = refs.pop(0)
    if pool is None:
        y_ref = refs.pop(0)
        wo_ref = refs.pop(0) if has_w_out else None
    else:
        msh_ref, msc_ref, mg_ref, wgrp_ref, mscale_ref = refs[:5]
        del refs[:5]
    gt_ref, sh_ref, sc_ref, g_ref, wr_ref, br_ref = refs[:6]
    del refs[:6]
    if chained:
        refs.pop(0)
    x1_ref, xs_ref, pos_ref, gate_ref, cnt_ref = refs[:5]
    tile, d = x_ref.shape
    n_sorted = xs_ref.shape[0]

    if pool is None:
        y = y_ref[...]
        if has_w_out:
            y = _dot(y.astype(BF16), wo_ref[...])
    else:
        windows, halo, tiles_per_seq = pool
        cache_ref, ext_ref = refs[5:7]
        part = pl.program_id(0) % tiles_per_seq

        @pl.when(part == 0)
        def _():
            ext_ref[0:halo, :] = jnp.zeros((halo, d), F32)

        hm = _rms_norm(x_ref[...], mg_ref[...]) * (1.0 + _mod_rows(msc_ref, tile)) + _mod_rows(msh_ref, tile)
        ext_ref[halo:halo + tile, :] = hm
        token = part * tile + lax.broadcasted_iota(jnp.int32, (tile, 1), 0)

        def window_sum(gi, cols):
            acc = hm[:, cols]
            for s in range(1, windows[gi]):
                acc = acc + ext_ref[halo - s:halo - s + tile, cols]
            return acc
        counts = [jnp.minimum(token + 1, w).astype(F32) for w in windows]
        y = _pool_groups(hm, window_sum, counts, wgrp_ref, mscale_ref[...])

        n_keep = cache_ref.shape[1]
        @pl.when(part == tiles_per_seq - 1)
        def _():
            cache_ref[0] = ext_ref[halo + tile - n_keep:halo + tile, :]
        ext_ref[0:halo, :] = ext_ref[tile:tile + halo, :]
    x1 = x_ref[...] + _mod_rows(gt_ref, tile) * y
    x1_ref[...] = x1
    h = _rms_norm(x1, g_ref[...]) * (1.0 + _mod_rows(sc_ref, tile)) + _mod_rows(sh_ref, tile)

    lane = lax.broadcasted_iota(jnp.int32, (tile, LANES), 1).astype(F32)
    logits = jnp.where(lane < n_experts, _dot_hp(h, wr_ref[...]) + br_ref[...], -jnp.inf)
    picks, vals = [], []
    for _ in range(top_k):
        m = jnp.max(logits, axis=-1, keepdims=True)
        pick = jnp.min(jnp.where(logits == m, lane, float(LANES)), axis=-1, keepdims=True)
        picks.append(pick)
        vals.append(m)
        logits = jnp.where(lane == pick, -jnp.inf, logits)
    exps = [jnp.exp(v - vals[0]) for v in vals]
    denom = exps[0]
    for e in exps[1:]:
        denom = denom + e

    onehots = [(lane == p).astype(F32) for p in picks]
    oh_sum = onehots[0]
    for oh in onehots[1:]:
        oh_sum = oh_sum + oh
    row = lax.broadcasted_iota(jnp.int32, (tile, tile), 0)
    col = lax.broadcasted_iota(jnp.int32, (tile, tile), 1)
    before = _dot((row > col).astype(BF16), oh_sum.astype(BF16))
    count = jnp.sum(oh_sum, axis=0, keepdims=True)
    cnt_pad = jnp.floor((count + (BF16_ROWS - 1.0)) * (1.0 / BF16_ROWS)) * BF16_ROWS
    lane8 = lax.broadcasted_iota(jnp.int32, (SUBLANES, LANES), 1)
    run = jnp.broadcast_to(cnt_pad, (SUBLANES, LANES))
    shift = 1
    while shift < n_experts:
        run = run + jnp.where(lane8 >= shift, pltpu.roll(run, shift, 1), 0.0)
        shift *= 2
    pos = before + (run[0:1] - cnt_pad)
    pos_out = jnp.zeros((tile, LANES), F32)
    gate_out = jnp.zeros((tile, LANES), F32)
    for k in range(top_k):
        pos_k = jnp.sum(onehots[k] * pos, axis=-1, keepdims=True)
        pos_out = jnp.where(lane == k, pos_k, pos_out)
        gate_out = jnp.where(lane == k, exps[k] / denom, gate_out)
    pos_ref[...] = pos_out.astype(jnp.int32)
    gate_ref[...] = gate_out
    cnt_ref[0] = cnt_pad.astype(jnp.int32)

    pos_t = pos_out.T
    hb = h.astype(BF16)
    for r0 in range(0, n_sorted, row_chunk):
        slot = (lax.broadcasted_iota(jnp.int32, (row_chunk, tile), 0) + r0).astype(F32)
        sel = jnp.where(slot == pos_t[0:1], 1.0, 0.0)
        for k in range(1, top_k):
            sel = sel + jnp.where(slot == pos_t[k:k + 1], 1.0, 0.0)
        xs_ref[r0:r0 + row_chunk, :] = _dot(sel.astype(BF16), hb).astype(BF16)


def resid_router(x, y, w_out_bf16, mod, g, w_r, b_r, sorted_in, *, tile, rows_per_seq, top_k,
                 block_offset, n_blocks_total, pool=None):
    n, d = x.shape
    n_experts = w_r.shape[1]
    n_sorted = _sorted_rows(tile, top_k, n_experts)
    w_r_pad = jnp.pad(w_r, ((0, 0), (0, LANES - n_experts)))
    b_r_pad = jnp.pad(b_r, (0, LANES - n_experts)).reshape(1, LANES)
    has_w_out = w_out_bf16 is not None
    chained = sorted_in is not None
    row_spec = pl.BlockSpec((tile, d), lambda i: (i, 0))
    lane_spec = pl.BlockSpec((tile, LANES), lambda i: (i, 0))
    full = lambda a: pl.BlockSpec(a.shape, lambda i: (0,) * a.ndim)
    extra_out_specs, extra_out_shape, scratch, pool_static = [], [], [], None
    if pool is None:
        in_specs = [row_spec, pl.BlockSpec((tile, y.shape[1]), lambda i: (i, 0))]
        args = [x, y]
        if has_w_out:
            in_specs.append(full(w_out_bf16))
            args.append(w_out_bf16)
    else:
        g_mix, w_grp, scale, windows, n_keep = pool
        halo = 2 * SUBLANES
        tiles_per_seq = rows_per_seq // tile
        assert max(windows) <= halo <= tile and n_keep <= tile
        in_specs = [row_spec, _mod_spec(mod, 0, d, tile, rows_per_seq), _mod_spec(mod, 1, d, tile, rows_per_seq),
                    pl.BlockSpec((1, d), lambda i: (0, 0)), full(w_grp), pl.BlockSpec((1, d), lambda i: (0, 0))]
        args = [x, mod, mod, g_mix.reshape(1, d), w_grp, scale.reshape(1, d)]
        extra_out_specs = [pl.BlockSpec((1, n_keep, d), lambda i: (i // tiles_per_seq, 0, 0))]
        extra_out_shape = [jax.ShapeDtypeStruct((n // rows_per_seq, n_keep, d), F32)]
        scratch = [pltpu.VMEM((halo + tile, d), F32)]
        pool_static = (windows, halo, tiles_per_seq)
    in_specs += [_mod_spec(mod, 2, d, tile, rows_per_seq), _mod_spec(mod, 3, d, tile, rows_per_seq),
                 _mod_spec(mod, 4, d, tile, rows_per_seq), pl.BlockSpec((1, d), lambda i: (0, 0)),
                 full(w_r_pad), full(b_r_pad)]
    args += [mod, mod, mod, g.reshape(1, d), w_r_pad, b_r_pad]
    aliases = {}
    if chained:
        aliases = {len(args): 1}
        in_specs.append(pl.BlockSpec(memory_space=pl.ANY))
        args.append(sorted_in)
    n_tiles = n // tile
    return pl.pallas_call(
        functools.partial(_resid_router_kernel, top_k=top_k, n_experts=n_experts, has_w_out=has_w_out,
                          chained=chained, row_chunk=ROUTER_ROW_CHUNK, pool=pool_static),
        grid=(n_tiles,),
        in_specs=in_specs,
        out_specs=[row_spec,
                   pl.BlockSpec((n_sorted, d), lambda i: (i + block_offset, 0)),
                   lane_spec, lane_spec,
                   pl.BlockSpec((1, 1, LANES), lambda i: (i, 0, 0))] + extra_out_specs,
        out_shape=[jax.ShapeDtypeStruct((n, d), F32),
                   jax.ShapeDtypeStruct((n_blocks_total * n_sorted, d), BF16),
                   jax.ShapeDtypeStruct((n, LANES), jnp.int32), jax.ShapeDtypeStruct((n, LANES), F32),
                   jax.ShapeDtypeStruct((n_tiles, 1, LANES), jnp.int32)] + extra_out_shape,
        scratch_shapes=scratch,
        input_output_aliases=aliases,
        compiler_params=_params("arbitrary" if pool is not None else "parallel"),
        name="resid_router",
    )(*args)


def _experts_kernel(te_ref, first_ref, rows_ref, base_ref, slo_ref, shi_ref, wslot_ref, nexte_ref, used_ref,
                    sstart_ref, slen_ref, ssrc_ref,
                    xs_hbm, wgu_hbm, bgu_ref, wdn_hbm, bdn_ref, ys_hbm,
                    xbuf, ybuf, wgu_f, wdn_f, wgu_b, wdn_b, in_sem, out_sem, w_sem,
                    *, layer, limit, alpha, col_chunk):
    del ys_hbm
    i = pl.program_id(0)
    used = used_ref[0]
    tm = xbuf.shape[1]

    def copy(src_rows, dst_rows, slot, inbound):
        if inbound:
            return pltpu.make_async_copy(xs_hbm.at[src_rows, :], xbuf.at[slot, dst_rows, :], in_sem.at[slot])
        return pltpu.make_async_copy(ybuf.at[slot, dst_rows, :], xs_hbm.at[src_rows, :], out_sem.at[slot])

    def piece_copies(tile_idx, slot, inbound, wait):
        if wait:
            rows = pl.ds(0, pl.multiple_of(rows_ref[tile_idx], BF16_ROWS))
            copy(rows, rows, slot, inbound).wait()
            return
        base = base_ref[tile_idx]

        def piece(s, c):
            first = sstart_ref[s] - base
            lo = jnp.maximum(first, 0)
            n_rows = pl.multiple_of(jnp.minimum(first + slen_ref[s], tm) - lo, BF16_ROWS)

            @pl.when(n_rows > 0)
            def _():
                src = pl.multiple_of(ssrc_ref[s] + (lo - first), BF16_ROWS)
                copy(pl.ds(src, n_rows), pl.ds(pl.multiple_of(lo, BF16_ROWS), n_rows), slot, inbound).start()
            return c
        lax.fori_loop(slo_ref[tile_idx], shi_ref[tile_idx], piece, 0)

    def weight_copies(e, slot):
        return (pltpu.make_async_copy(wgu_hbm.at[layer, e], wgu_f.at[slot], w_sem.at[slot]),
                pltpu.make_async_copy(wdn_hbm.at[layer, e], wdn_f.at[slot], w_sem.at[slot]))

    @pl.when(i == 0)
    def _():
        xbuf[...] = jnp.zeros_like(xbuf)
        for cp in weight_copies(te_ref[0], 0):
            cp.start()
        piece_copies(0, 0, True, False)

    @pl.when(i < used)
    def _():
        slot = i % 2

        @pl.when(i + 1 < used)
        def _():
            piece_copies(i + 1, 1 - slot, True, False)

        @pl.when(first_ref[i] == 1)
        def _():
            ws = wslot_ref[i]
            for cp in weight_copies(te_ref[i], ws):
                cp.wait()

            @pl.when(nexte_ref[i] >= 0)
            def _():
                for cp in weight_copies(nexte_ref[i], 1 - ws):
                    cp.start()
            wgu_b[...] = wgu_f[ws].astype(BF16)
            wdn_b[...] = wdn_f[ws].astype(BF16)

        piece_copies(i, slot, True, True)

        @pl.when(i >= 2)
        def _():
            piece_copies(i - 2, slot, False, True)

        d_ff = wdn_b.shape[0]
        tm = xbuf.shape[1]
        e = te_ref[i]
        b_gu = bgu_ref[pl.ds(e, 1), :]
        b_dn = bdn_ref[pl.ds(e, 1), :]

        def mlp(n_rows):
            x = xbuf[slot, :n_rows, :]
            y = None
            for c0 in range(0, d_ff, col_chunk):
                cs = slice(c0, c0 + col_chunk)
                us = slice(d_ff + c0, d_ff + c0 + col_chunk)
                gate = jnp.minimum(_dot(x, wgu_b[:, cs]) + b_gu[:, cs], limit)
                up = jnp.clip(_dot(x, wgu_b[:, us]) + b_gu[:, us], -limit, limit)
                act = ((up + 1.0) * (gate * _sigmoid(alpha * gate))).astype(BF16)
                part = _dot(act, wdn_b[cs, :])
                y = part if y is None else y + part
            ybuf[slot, :n_rows, :] = (y + b_dn).astype(BF16)

        step = tm // MLP_ROW_STEPS
        for part in range(1, MLP_ROW_STEPS + 1):
            @pl.when((rows_ref[i] > (part - 1) * step) & (rows_ref[i] <= part * step))
            def _():
                mlp(part * step)
        piece_copies(i, slot, False, False)

    @pl.when(i == pl.num_programs(0) - 1)
    def _():
        @pl.when(used >= 2)
        def _():
            piece_copies(used - 2, used % 2, False, True)
        piece_copies(used - 1, (used - 1) % 2, False, True)


def moe_experts(sorted_rows, w_gu, b_gu, w_dn, b_dn, tables, *, layer, row_tile, limit, alpha, col_chunk=512):
    n_rows, d = sorted_rows.shape
    d_gu = w_gu.shape[-1]
    d_ff = w_dn.shape[-2]
    n_tiles = tables[0].shape[0]
    vmem = lambda a: pl.BlockSpec(a.shape, lambda i, *_: (0,) * a.ndim)
    any_spec = pl.BlockSpec(memory_space=pl.ANY)
    return pl.pallas_call(
        functools.partial(_experts_kernel, layer=layer, limit=limit, alpha=alpha,
                          col_chunk=min(col_chunk, d_ff)),
        grid_spec=pltpu.PrefetchScalarGridSpec(
            num_scalar_prefetch=len(tables),
            grid=(n_tiles,),
            in_specs=[any_spec, any_spec, vmem(b_gu), any_spec, vmem(b_dn)],
            out_specs=any_spec,
            scratch_shapes=[
                pltpu.VMEM((2, row_tile, d), BF16), pltpu.VMEM((2, row_tile, d), BF16),
                pltpu.VMEM((2, d, d_gu), F32), pltpu.VMEM((2, d_ff, d), F32),
                pltpu.VMEM((d, d_gu), BF16), pltpu.VMEM((d_ff, d), BF16),
                pltpu.SemaphoreType.DMA((2,)), pltpu.SemaphoreType.DMA((2,)), pltpu.SemaphoreType.DMA((2,)),
            ],
        ),
        out_shape=jax.ShapeDtypeStruct((n_rows, d), BF16),
        input_output_aliases={len(tables): 0},
        compiler_params=_params("arbitrary"),
        name="moe_experts",
    )(*tables, sorted_rows, w_gu, b_gu, w_dn, b_dn)


def _expert_tables(cnt, n_sorted, n_tiles, row_tile):
    n_blocks, n_experts = cnt.shape

    def prefix_sum(a):
        n = a.shape[-1]
        upto = jnp.arange(n)[:, None] <= jnp.arange(n)[None, :]
        return jnp.sum(jnp.where(upto, a[..., :, None], 0), axis=-2)

    local_off = prefix_sum(cnt) - cnt
    seg_end = prefix_sum(cnt.T)
    seg_start = seg_end - cnt.T
    seg_src = jnp.arange(n_blocks, dtype=jnp.int32)[None, :] * n_sorted + local_off.T
    total = seg_end[:, -1]
    padded = (total + row_tile - 1) // row_tile * row_tile
    pad_end = prefix_sum(padded)
    pad_start = pad_end - padded
    tiles = jnp.arange(n_tiles, dtype=jnp.int32)
    experts = jnp.arange(n_experts, dtype=jnp.int32)
    n_used = pad_end[-1] // row_tile
    tile_expert = jnp.minimum(jnp.sum(tiles[:, None] * row_tile >= pad_end[None, :], axis=1), n_experts - 1)
    is_expert = tile_expert[:, None] == experts[None, :]

    def of_expert(a):
        if a.ndim == 1:
            return jnp.sum(jnp.where(is_expert, a[None, :], 0), axis=1)
        return jnp.sum(jnp.where(is_expert[:, :, None], a[None, :, :], 0), axis=1)

    live = tiles < n_used
    tile_base = tiles * row_tile - of_expert(pad_start)
    tile_rows = jnp.where(live, jnp.clip(of_expert(total) - tile_base, 0, row_tile), 0)
    tile_first = (live & (tile_base == 0)).astype(jnp.int32)
    seg_lo = jnp.sum(of_expert(seg_end) <= tile_base[:, None], axis=1)
    seg_hi = jnp.sum(of_expert(seg_start) < tile_base[:, None] + row_tile, axis=1)
    seg_lo = tile_expert * n_blocks + jnp.minimum(seg_lo, seg_hi)
    seg_hi = tile_expert * n_blocks + seg_hi
    owns = total > 0
    order = prefix_sum(owns.astype(jnp.int32)) - 1
    experts = jnp.arange(n_experts, dtype=jnp.int32)
    later = (experts[None, :] > experts[:, None]) & owns[None, :]
    next_expert = jnp.min(jnp.where(later, experts[None, :], n_experts), axis=1)
    next_expert = jnp.where(next_expert < n_experts, next_expert, -1)
    i32 = lambda a: a.astype(jnp.int32)
    return (i32(tile_expert), tile_first, i32(tile_rows), i32(tile_base), i32(seg_lo), i32(seg_hi),
            i32(of_expert(order) % 2), i32(of_expert(next_expert)), i32(n_used).reshape(1),
            i32(seg_start.reshape(-1)), i32(cnt.T.reshape(-1)), i32(seg_src.reshape(-1)))


def _combine_kernel(x_ref, pos_ref, gate_ref, gt_ref, gfin_ref, ys_ref, o_ref, *, top_k, final_norm, k_chunk):
    tile, d = x_ref.shape
    n_sorted = ys_ref.shape[0]
    pos = pos_ref[...].astype(F32)
    gates = gate_ref[...]
    acc = None
    for r0 in range(0, n_sorted, k_chunk):
        slot = (lax.broadcasted_iota(jnp.int32, (tile, k_chunk), 1) + r0).astype(F32)
        w = jnp.where(slot == pos[:, 0:1], gates[:, 0:1], 0.0)
        for k in range(1, top_k):
            w = w + jnp.where(slot == pos[:, k:k + 1], gates[:, k:k + 1], 0.0)
        part = _dot(w.astype(BF16), ys_ref[r0:r0 + k_chunk, :])
        acc = part if acc is None else acc + part
    out = x_ref[...] + _mod_rows(gt_ref, tile) * acc
    if final_norm:
        out = _rms_norm(out, gfin_ref[...])
    o_ref[...] = out


def moe_combine(x, ys, pos, gates, mod, g_final, *, tile, rows_per_seq, final_norm, block_offset, n_sorted):
    n, d = x.shape
    return pl.pallas_call(
        functools.partial(_combine_kernel, top_k=TOP_K, final_norm=final_norm, k_chunk=COMBINE_K_CHUNK),
        grid=(n // tile,),
        in_specs=[
            pl.BlockSpec((tile, d), lambda i: (i, 0)),
            pl.BlockSpec((tile, LANES), lambda i: (i, 0)),
            pl.BlockSpec((tile, LANES), lambda i: (i, 0)),
            _mod_spec(mod, 5, d, tile, rows_per_seq),
            pl.BlockSpec((1, d), lambda i: (0, 0)),
            pl.BlockSpec((n_sorted, d), lambda i: (i + block_offset, 0)),
        ],
        out_specs=pl.BlockSpec((tile, d), lambda i: (i, 0)),
        out_shape=jax.ShapeDtypeStruct((n, d), F32),
        compiler_params=_params("parallel"),
        name="moe_combine",
    )(x, pos, gates, mod, g_final.reshape(1, d), ys)


def _pool_groups(h, window_sum, counts, w_ref, scale):
    n_groups = w_ref.shape[0]
    dg = h.shape[-1] // n_groups
    outs = []
    for gi in range(n_groups):
        cols = slice(gi * dg, (gi + 1) * dg)
        pooled = window_sum(gi, cols) / counts[gi] - h[:, cols]
        outs.append(_dot(pooled.astype(BF16), w_ref[gi]))
    return jnp.concatenate(outs, axis=-1) * scale


def _pool_sample_kernel(x_ref, buf_ref, sh_ref, sc_ref, g_ref, w_ref, scale_ref, y_ref, cache_ref,
                        *, windows, start_pos):
    t_len = x_ref.shape[0]
    n_prev = buf_ref.shape[0]
    hs = [_rms_norm(x_ref[t], g_ref[...]) * (1.0 + sc_ref[...]) + sh_ref[...] for t in range(t_len)]

    def ext(r):
        return buf_ref[r] if r < n_prev else hs[r - n_prev]

    for t in range(t_len):
        def window_sum(gi, cols):
            acc = hs[t][:, cols]
            for s in range(1, windows[gi]):
                acc = acc + ext(n_prev + t - s)[:, cols]
            return acc
        counts = [float(min(start_pos + t + 1, w)) for w in windows]
        y_ref[t] = _pool_groups(hs[t], window_sum, counts, w_ref, scale_ref[...])
    for r in range(n_prev):
        cache_ref[r] = ext(t_len + r)


def pool_mixer_sample(x_t, buf_t, mod, g, w_grp_bf16, scale, *, seq_block, windows, start_pos):
    t_len, n_seq, d = x_t.shape
    n_prev = buf_t.shape[0]
    assert start_pos >= n_prev >= max(windows) - 1
    mod_spec = lambda k: pl.BlockSpec((seq_block, d), lambda i: (i, k))
    return pl.pallas_call(
        functools.partial(_pool_sample_kernel, windows=windows, start_pos=start_pos),
        grid=(n_seq // seq_block,),
        in_specs=[
            pl.BlockSpec((t_len, seq_block, d), lambda i: (0, i, 0)),
            pl.BlockSpec((n_prev, seq_block, d), lambda i: (0, i, 0)),
            mod_spec(0), mod_spec(1),
            pl.BlockSpec((1, d), lambda i: (0, 0)),
            pl.BlockSpec(w_grp_bf16.shape, lambda i: (0, 0, 0)),
            pl.BlockSpec((1, d), lambda i: (0, 0)),
        ],
        out_specs=[pl.BlockSpec((t_len, seq_block, d), lambda i: (0, i, 0)),
                   pl.BlockSpec((n_prev, seq_block, d), lambda i: (0, i, 0))],
        out_shape=[jax.ShapeDtypeStruct((t_len, n_seq, d), F32), jax.ShapeDtypeStruct((n_prev, n_seq, d), F32)],
        compiler_params=_params("parallel"),
        name="pool_mixer_sample",
    )(x_t, buf_t, mod, mod, g.reshape(1, d), w_grp_bf16, scale.reshape(1, d))


TOP_K = 4
SWIGLU_LIMIT = 7.0
SWIGLU_ALPHA = 1.702
POOL_WINDOWS = (2, 4, 8, 16)
PAST_LEN = 16384
MOE_TOKEN_TILE = 512
MOE_ROW_TILE = 512
ROUTER_ROW_CHUNK = 256
COMBINE_K_CHUNK = 512
HGRN_TIME_BLOCK = 1024
HGRN_CHUNK = 256
HGRN_SUB = 32
SAMPLE_T_PAD = 8
SAMPLE_SEQ_BLOCK = 8


def kernel(x_prompt, x_sample, c_prompt, c_sample, state_hgrn, cache_pool, g_norm_mix, g_norm_ffn, w_ada, b_ada, w_in_hgrn, lb_logits, g_out_hgrn, w_out_hgrn, w_grp_pool, scale_pool, w_router, b_router, w_gate_up, b_gate_up, w_down, b_down, g_final):
    bp, tp, d = x_prompt.shape
    bs, ts, _ = x_sample.shape
    n_p, n_s = bp * tp, bs * ts
    n_experts = w_router.shape[-1]
    hk = w_out_hgrn.shape[1]
    assert n_s == MOE_TOKEN_TILE and n_p % MOE_TOKEN_TILE == 0
    blocks_p = n_p // MOE_TOKEN_TILE
    n_blocks = blocks_p + 1
    n_sorted = _sorted_rows(MOE_TOKEN_TILE, TOP_K, n_experts)
    n_row_tiles = -(-(n_blocks * n_sorted + n_experts * (MOE_ROW_TILE - BF16_ROWS)) // MOE_ROW_TILE)

    mod = adaln(jnp.concatenate([c_prompt, c_sample], axis=0), w_ada, b_ada)
    mod_p = [mod[l, :bp][:, None, :] for l in range(mod.shape[0])]
    mod_s = [mod[l, bp:] for l in range(mod.shape[0])]

    xp = x_prompt.reshape(n_p, d)
    xs = x_sample.transpose(1, 0, 2).reshape(n_s, d)

    def moe(layer, x_p, y_p, x_s, y_s, w_out, final_norm, pool_p=None):
        route = functools.partial(resid_router, g=g_norm_ffn[layer], w_r=w_router[layer], b_r=b_router[layer],
                                  tile=MOE_TOKEN_TILE, top_k=TOP_K, n_blocks_total=n_blocks)
        x1_p, sorted_rows, pos_p, gate_p, cnt_p, *cache = route(
            x_p, y_p, w_out, mod_p[layer], sorted_in=None, rows_per_seq=tp, block_offset=0, pool=pool_p)
        x1_s, sorted_rows, pos_s, gate_s, cnt_s = route(
            x_s, y_s, w_out, mod_s[layer], sorted_in=sorted_rows, rows_per_seq=None, block_offset=blocks_p)
        cnt8 = jnp.concatenate([cnt_p, cnt_s], axis=0)[:, 0, :n_experts]
        tables = _expert_tables(cnt8, n_sorted, n_row_tiles, MOE_ROW_TILE)
        ys = moe_experts(sorted_rows, w_gate_up, b_gate_up[layer], w_down, b_down[layer], tables,
                         layer=layer, row_tile=MOE_ROW_TILE, limit=SWIGLU_LIMIT, alpha=SWIGLU_ALPHA)
        combine = functools.partial(moe_combine, ys=ys, g_final=g_final, tile=MOE_TOKEN_TILE,
                                    final_norm=final_norm, n_sorted=n_sorted)
        out_p = combine(x1_p, pos=pos_p, gates=gate_p, mod=mod_p[layer], rows_per_seq=tp, block_offset=0)
        out_s = combine(x1_s, pos=pos_s, gates=gate_s, mod=mod_s[layer], rows_per_seq=None,
                        block_offset=blocks_p)
        return (out_p, out_s, *cache)

    w_in = w_in_hgrn[0].astype(BF16)
    proj_s = norm_proj(xs, mod_s[0], g_norm_mix[0], w_in, tile=bs, rows_per_seq=None)
    o_p, state_p = hgrn_recurrence(None, lb_logits, g_out_hgrn[0], None,
                                   layer=0, seq_block=1, time_block=HGRN_TIME_BLOCK, chunk=HGRN_CHUNK,
                                   c_sub=HGRN_SUB, n_valid=HGRN_CHUNK,
                                   norm_proj_of=(x_prompt, mod_p[0], g_norm_mix[0], w_in),
                                   out_dtype=BF16)
    proj_sb = jnp.pad(proj_s.reshape(ts, bs, 4 * hk).transpose(1, 0, 2), ((0, 0), (0, SAMPLE_T_PAD - ts), (0, 0)))
    o_s, state_s = hgrn_recurrence(proj_sb, lb_logits, g_out_hgrn[0], state_hgrn[0],
                                   layer=0, seq_block=SAMPLE_SEQ_BLOCK, time_block=SAMPLE_T_PAD,
                                   chunk=SAMPLE_T_PAD, c_sub=SAMPLE_T_PAD, n_valid=ts)
    o_s = o_s[:, :ts].transpose(1, 0, 2).reshape(n_s, hk)
    x_p, x_s = moe(0, xp, o_p.reshape(n_p, hk), xs, o_s, w_out_hgrn[0].astype(BF16), False)

    w_grp = w_grp_pool[0].astype(BF16)
    n_keep = cache_pool.shape[2]
    y_s, cache_s = pool_mixer_sample(x_s.reshape(ts, bs, d), cache_pool[0].transpose(1, 0, 2), mod_s[1],
                                     g_norm_mix[1], w_grp, scale_pool[0],
                                     seq_block=32, windows=POOL_WINDOWS, start_pos=PAST_LEN)
    x_p, x_s, cache_p = moe(1, x_p, None, x_s, y_s.reshape(n_s, d), None, True,
                            pool_p=(g_norm_mix[1], w_grp, scale_pool[0], POOL_WINDOWS, n_keep))

    return (x_p.reshape(bp, tp, d), x_s.reshape(ts, bs, d).transpose(1, 0, 2),
            state_p[None], state_s[None], cache_p[None], cache_s.transpose(1, 0, 2)[None])
```

```python
import functools

import jax
import jax.numpy as jnp
from jax import lax
from jax.experimental import pallas as pl
from jax.experimental.pallas import tpu as pltpu

F32 = jnp.float32
BF16 = jnp.bfloat16

RMS_EPS = 1e-6
LANES = 128
SUBLANES = 8
BF16_ROWS = 16
HEAD_DIM = 128
VMEM_LIMIT = 56 * 1024 * 1024

_dot = functools.partial(jnp.dot, preferred_element_type=F32)


def _params(*semantics):
    return pltpu.CompilerParams(dimension_semantics=semantics, vmem_limit_bytes=VMEM_LIMIT)


def _split_bf16(x, n):
    parts, r = [], x
    for _ in range(n):
        p = r.astype(BF16)
        parts.append(p)
        r = r - p.astype(F32)
    return parts


def _dot_hp(a, b):
    a_hi, a_lo = _split_bf16(a, 2)
    b_hi, b_lo = _split_bf16(b, 2)
    return _dot(a_hi, b_hi) + (_dot(a_hi, b_lo) + _dot(a_lo, b_hi))


def _sigmoid(x):
    return 1.0 / (1.0 + jnp.exp(-x))


def _silu(x):
    return x * _sigmoid(x)


def _rms_norm(x, g):
    ms = jnp.mean(x * x, axis=-1, keepdims=True)
    return x * lax.rsqrt(ms + RMS_EPS) * g


def _adaln_kernel(c_ref, w_ref, b_ref, o_ref):
    o_ref[0] = _dot_hp(_silu(c_ref[...]), w_ref[0]) + b_ref[0]


def adaln(c_all, w_ada, b_ada, *, col_block=1536):
    n_seq, d = c_all.shape
    n_layers, _, d6 = w_ada.shape
    return pl.pallas_call(
        _adaln_kernel,
        grid=(n_layers, d6 // col_block),
        in_specs=[
            pl.BlockSpec((n_seq, d), lambda l, j: (0, 0)),
            pl.BlockSpec((1, d, col_block), lambda l, j: (l, 0, j)),
            pl.BlockSpec((1, 1, col_block), lambda l, j: (l, 0, j)),
        ],
        out_specs=pl.BlockSpec((1, n_seq, col_block), lambda l, j: (l, 0, j)),
        out_shape=jax.ShapeDtypeStruct((n_layers, n_seq, d6), F32),
        compiler_params=_params("parallel", "parallel"),
        name="adaln",
    )(c_all, w_ada, b_ada.reshape(n_layers, 1, d6))


def _mod_spec(mod, k, d, tile, rows_per_seq):
    if rows_per_seq is None:
        return pl.BlockSpec((mod.shape[0], d), lambda i: (0, k))
    tiles_per_seq = rows_per_seq // tile
    return pl.BlockSpec((1, 1, d), lambda i: (i // tiles_per_seq, 0, k))


def _mod_rows(ref, tile):
    m = ref[...].reshape(-1, ref.shape[-1])
    if m.shape[0] not in (1, tile):
        m = jnp.concatenate([m] * (tile // m.shape[0]), axis=0)
    return m


def _norm_proj_kernel(x_ref, sh_ref, sc_ref, g_ref, w_ref, o_ref):
    tile = x_ref.shape[0]
    h = _rms_norm(x_ref[...], g_ref[...]) * (1.0 + _mod_rows(sc_ref, tile)) + _mod_rows(sh_ref, tile)
    o_ref[...] = _dot(h.astype(BF16), w_ref[...])


def norm_proj(x, mod, g, w_bf16, *, tile, rows_per_seq):
    n, d = x.shape
    p = w_bf16.shape[1]
    return pl.pallas_call(
        _norm_proj_kernel,
        grid=(n // tile,),
        in_specs=[
            pl.BlockSpec((tile, d), lambda i: (i, 0)),
            _mod_spec(mod, 0, d, tile, rows_per_seq),
            _mod_spec(mod, 1, d, tile, rows_per_seq),
            pl.BlockSpec((1, d), lambda i: (0, 0)),
            pl.BlockSpec((d, p), lambda i: (0, 0)),
        ],
        out_specs=pl.BlockSpec((tile, p), lambda i: (i, 0)),
        out_shape=jax.ShapeDtypeStruct((n, p), F32),
        compiler_params=_params("parallel"),
        name="hgrn_norm_proj",
    )(x, mod, mod, g.reshape(1, d), w_bf16)


def _cumsum_rows(x, tri):
    hi, mid, lo = _split_bf16(x, 3)
    return _dot(tri, hi) + (_dot(tri, mid) + _dot(tri, lo))


MAX_LOG_DECAY_RANGE = 80.0
MLP_ROW_STEPS = 4


def _hgrn_prep(proj, lb, n_valid):
    c = proj.shape[0]
    hk = proj.shape[1] // 4
    row = lax.broadcasted_iota(jnp.int32, (c, c), 0)
    col = lax.broadcasted_iota(jnp.int32, (c, c), 1)
    zf = proj[:, hk:2 * hk]
    e = jnp.exp(-jnp.abs(zf))
    r = 1.0 / (1.0 + e)
    pos = zf >= 0
    sig_p = jnp.where(pos, 1.0, e) * r
    sig_n = jnp.where(pos, e, 1.0) * r
    logf = jnp.log(lb + (1.0 - lb) * sig_p)
    k = (1.0 - lb) * sig_n
    if n_valid < c:
        live = lax.broadcasted_iota(jnp.int32, (c, 1), 0) < n_valid
        logf = jnp.where(live, logf, 0.0)
        k = jnp.where(live, k, 0.0)
    b = _cumsum_rows(logf, (row >= col).astype(BF16))
    return _silu(proj[:, :hk]), k, proj[:, 2 * hk:3 * hk], _silu(proj[:, 3 * hk:]), b


def _decay_range(b, c_sub):
    c = b.shape[0]
    worst = None
    for i in range(c // c_sub):
        span = b[i * c_sub:i * c_sub + 1, :] - b[(i + 1) * c_sub - 1:(i + 1) * c_sub, :]
        worst = span if worst is None else jnp.maximum(worst, span)
    return jnp.max(worst)


def _head_norm_gate(o, gate, gout):
    outs = []
    for h in range(o.shape[1] // HEAD_DIM):
        hs = slice(h * HEAD_DIM, (h + 1) * HEAD_DIM)
        oh = o[:, hs]
        outs.append(oh * lax.rsqrt(jnp.mean(oh * oh, axis=-1, keepdims=True) + RMS_EPS) * gout * gate[:, hs])
    return jnp.concatenate(outs, axis=-1)


def _hgrn_chunk(prep, gout, st_refs, seq, c_sub):
    q, k, v, gate, b = prep
    c = q.shape[0]
    n_heads = q.shape[1] // HEAD_DIM
    row = lax.broadcasted_iota(jnp.int32, (c, c), 0)
    col = lax.broadcasted_iota(jnp.int32, (c, c), 1)
    causal = row >= col
    n_sub = c // c_sub
    subs = [slice(i * c_sub, (i + 1) * c_sub) for i in range(n_sub)]

    intra, inter = [], []
    for h in range(n_heads):
        hs = slice(h * HEAD_DIM, (h + 1) * HEAD_DIM)
        bh, qh, kh, vh = b[:, hs], q[:, hs], k[:, hs], v[:, hs]
        vb = vh.astype(BF16)
        refs = [bh[i * c_sub + c_sub // 2:i * c_sub + c_sub // 2 + 1, :] for i in range(n_sub)]
        k_own = [kh[rs] * jnp.exp(jnp.minimum(ref - bh[rs], MAX_LOG_DECAY_RANGE))
                 for rs, ref in zip(subs, refs)]
        a_rows = []
        for i in range(n_sub):
            q_hat = (qh[subs[i]] * jnp.exp(bh[subs[i]] - refs[i])).astype(BF16)
            parts = [k_own[j] * jnp.exp(refs[i] - refs[j]) for j in range(i)] + [k_own[i]]
            parts += [jnp.zeros((c_sub, HEAD_DIM), F32)] * (n_sub - 1 - i)
            k_hat = (jnp.concatenate(parts, axis=0) if n_sub > 1 else parts[0]).astype(BF16)
            a_rows.append(lax.dot_general(q_hat, k_hat, (((1,), (1,)), ((), ())),
                                          preferred_element_type=F32))
        att = jnp.where(causal, jnp.concatenate(a_rows, axis=0) if len(a_rows) > 1 else a_rows[0], 0.0)
        intra.append(_dot(att.astype(BF16), vb))
        st = st_refs[seq, h]
        inter.append(lax.dot_general((qh * jnp.exp(bh)).astype(BF16), st.astype(BF16),
                                     (((1,), (1,)), ((), ())), preferred_element_type=F32))
        b_last = bh[c - 1:c, :]
        k_dec = (kh * jnp.exp(b_last - bh)).astype(BF16)
        st_refs[seq, h] = st * jnp.exp(b_last) + lax.dot_general(
            vb, k_dec, (((0,), (0,)), ((), ())), preferred_element_type=F32)
    inter = jnp.concatenate(inter, axis=-1)
    return _head_norm_gate(jnp.concatenate(intra, axis=-1) + inter, gate, gout), inter


def _hgrn_chunk_exact(prep, inter, gout, q_ref, b_ref, oi_ref):
    q, k, v, gate, b = prep
    c = q.shape[0]
    n_heads = q.shape[1] // HEAD_DIM
    q_ref[...] = q
    b_ref[...] = b
    key_row = lax.broadcasted_iota(jnp.int32, (c, 1), 0)

    def row_group(g, carry):
        rows = pl.ds(pl.multiple_of(g * SUBLANES, SUBLANES), SUBLANES)
        for h in range(n_heads):
            hs = slice(h * HEAD_DIM, (h + 1) * HEAD_DIM)
            q_g, b_g = q_ref[rows, hs], b_ref[rows, hs]
            o_rows = []
            for r in range(SUBLANES):
                decay = jnp.exp(jnp.minimum(b_g[r:r + 1] - b[:, hs], 0.0))
                score = jnp.sum(decay * k[:, hs] * q_g[r:r + 1], axis=-1, keepdims=True)
                score = jnp.where(key_row <= g * SUBLANES + r, score, 0.0)
                o_rows.append(jnp.sum(score * v[:, hs], axis=0, keepdims=True))
            oi_ref[rows, hs] = jnp.concatenate(o_rows, axis=0)
        return carry
    lax.fori_loop(0, c // SUBLANES, row_group, 0)
    return _head_norm_gate(oi_ref[...] + inter, gate, gout)


def _lower_bound(lb_logits, layer):
    e = jnp.exp(lb_logits - jnp.max(lb_logits, axis=0, keepdims=True))
    return jnp.sum(e[:layer + 1], axis=0, keepdims=True) / jnp.sum(e, axis=0, keepdims=True)


def _hgrn_rec_kernel(*refs, chunk, c_sub, n_valid, has_state, layer, fused_proj):
    refs = list(refs)
    if fused_proj:
        x_ref, sh_ref, sc_ref, g_ref, w_ref = refs[:5]
        del refs[:5]
        proj_ref = None
    else:
        proj_ref = refs.pop(0)
    lb_ref, gout_ref = refs[:2]
    del refs[:2]
    s0_ref = refs.pop(0) if has_state else None
    o_ref, sout_ref, st_ref, inter_ref, q_ref, b_ref, oi_ref = refs[:7]
    proj_buf = refs[7] if fused_proj else None
    bb, tb, _ = o_ref.shape
    n_heads = st_ref.shape[1]
    n_chunks = tb // chunk
    j = pl.program_id(1)

    @pl.when(j == 0)
    def _():
        if has_state:
            for s in range(bb):
                for h in range(n_heads):
                    st_ref[s, h] = s0_ref[s, h].T
        else:
            st_ref[...] = jnp.zeros_like(st_ref)

    lb = _lower_bound(lb_ref[...], layer)
    gout = gout_ref[...]

    def project(s, ci):
        rows = pl.ds(pl.multiple_of(ci * chunk, chunk), chunk)
        h = _rms_norm(x_ref[s, rows, :], g_ref[...]) * (1.0 + sc_ref[s]) + sh_ref[s]
        return _dot(h.astype(BF16), w_ref[...])

    if fused_proj:
        for s in range(bb):
            proj_buf[s, 0] = project(s, 0)

    def chunk_body(ci, carry):
        rows = pl.ds(pl.multiple_of(ci * chunk, chunk), chunk)

        def load_proj(s):
            return proj_buf[s, ci % 2] if fused_proj else proj_ref[s, rows, :]
        span = None
        preps = [_hgrn_prep(load_proj(s), lb, n_valid) for s in range(bb)]
        for s, prep in enumerate(preps):
            o, inter_ref[s] = _hgrn_chunk(prep, gout, st_ref, s, c_sub)
            o_ref[s, rows, :] = o.astype(o_ref.dtype)
            worst = _decay_range(prep[4], c_sub)
            span = worst if span is None else jnp.maximum(span, worst)
            if fused_proj:
                proj_buf[s, (ci + 1) % 2] = project(s, jnp.minimum(ci + 1, n_chunks - 1))

        @pl.when(jnp.logical_not(span <= MAX_LOG_DECAY_RANGE))
        def _():
            for s in range(bb):
                prep = _hgrn_prep(load_proj(s), lb, n_valid)
                o_ref[s, rows, :] = _hgrn_chunk_exact(prep, inter_ref[s], gout, q_ref, b_ref,
                                                      oi_ref).astype(o_ref.dtype)
        return carry
    lax.fori_loop(0, n_chunks, chunk_body, 0)

    @pl.when(j == pl.num_programs(1) - 1)
    def _():
        for s in range(bb):
            for h in range(n_heads):
                sout_ref[s, h] = st_ref[s, h].T


def hgrn_recurrence(proj, lb_logits, g_out, s0, *, layer, seq_block, time_block, chunk, c_sub, n_valid,
                    norm_proj_of=None, out_dtype=F32):
    fused = norm_proj_of is not None
    if fused:
        x, mod, g, w_in = norm_proj_of
        bsz, t, d = x.shape
        p = w_in.shape[1]
        mod_spec = lambda k: pl.BlockSpec((seq_block, 1, d), lambda i, j: (i, 0, k))
        in_specs = [pl.BlockSpec((seq_block, time_block, d), lambda i, j: (i, j, 0)), mod_spec(0), mod_spec(1),
                    pl.BlockSpec((1, d), lambda i, j: (0, 0)), pl.BlockSpec((d, p), lambda i, j: (0, 0))]
        args = [x, mod, mod, g.reshape(1, d), w_in]
    else:
        bsz, t, p = proj.shape
        in_specs = [pl.BlockSpec((seq_block, time_block, p), lambda i, j: (i, j, 0))]
        args = [proj]
    hk = p // 4
    n_heads = hk // HEAD_DIM
    has_state = s0 is not None
    st_shape = (seq_block, n_heads, HEAD_DIM, HEAD_DIM)
    st_spec = pl.BlockSpec(st_shape, lambda i, j: (i, 0, 0, 0))
    in_specs += [pl.BlockSpec(lb_logits.shape, lambda i, j: (0, 0)),
                 pl.BlockSpec((1, HEAD_DIM), lambda i, j: (0, 0))]
    args += [lb_logits, g_out.reshape(1, HEAD_DIM)]
    if has_state:
        in_specs.append(st_spec)
        args.append(s0)
    scratch = [pltpu.VMEM(st_shape, F32), pltpu.VMEM((seq_block, chunk, hk), F32)]
    scratch += [pltpu.VMEM((chunk, hk), F32)] * 3
    if fused:
        scratch.append(pltpu.VMEM((seq_block, 2, chunk, p), F32))
    return pl.pallas_call(
        functools.partial(_hgrn_rec_kernel, chunk=chunk, c_sub=c_sub, n_valid=n_valid,
                          has_state=has_state, layer=layer, fused_proj=fused),
        grid=(bsz // seq_block, t // time_block),
        in_specs=in_specs,
        out_specs=[pl.BlockSpec((seq_block, time_block, hk), lambda i, j: (i, j, 0)), st_spec],
        out_shape=[jax.ShapeDtypeStruct((bsz, t, hk), out_dtype),
                   jax.ShapeDtypeStruct((bsz, n_heads, HEAD_DIM, HEAD_DIM), F32)],
        scratch_shapes=scratch,
        compiler_params=_params("parallel", "arbitrary"),
        name="hgrn_recurrence",
    )(*args)


def _sorted_rows(tile, top_k, n_experts):
    return tile * top_k + n_experts * BF16_ROWS


def _resid_router_kernel(*refs, top_k, n_experts, has_w_out, chained, row_chunk, pool):
    refs = list(refs)
    x_ref = refs.pop(0)
    if pool is None:
        y_ref = refs.pop(0)
        wo_ref = refs.pop(0) if has_w_out else None
    else:
        msh_ref, msc_ref, mg_ref, wgrp_ref, mscale_ref = refs[:5]
        del refs[:5]
    gt_ref, sh_ref, sc_ref, g_ref, wr_ref, br_ref = refs[:6]
    del refs[:6]
    if chained:
        refs.pop(0)
    x1_ref, xs_ref, pos_ref, gate_ref, cnt_ref = refs[:5]
    tile, d = x_ref.shape
    n_sorted = xs_ref.shape[0]

    if pool is None:
        y = y_ref[...]
        if has_w_out:
            y = _dot(y.astype(BF16), wo_ref[...])
    else:
        windows, halo, tiles_per_seq = pool
        cache_ref, ext_ref = refs[5:7]
        part = pl.program_id(0) % tiles_per_seq

        @pl.when(part == 0)
        def _():
            ext_ref[0:halo, :] = jnp.zeros((halo, d), F32)

        hm = _rms_norm(x_ref[...], mg_ref[...]) * (1.0 + _mod_rows(msc_ref, tile)) + _mod_rows(msh_ref, tile)
        ext_ref[halo:halo + tile, :] = hm
        token = part * tile + lax.broadcasted_iota(jnp.int32, (tile, 1), 0)

        def window_sum(gi, cols):
            acc = hm[:, cols]
            for s in range(1, windows[gi]):
                acc = acc + ext_ref[halo - s:halo - s + tile, cols]
            return acc
        counts = [jnp.minimum(token + 1, w).astype(F32) for w in windows]
        y = _pool_groups(hm, window_sum, counts, wgrp_ref, mscale_ref[...])

        n_keep = cache_ref.shape[1]
        @pl.when(part == tiles_per_seq - 1)
        def _():
            cache_ref[0] = ext_ref[halo + tile - n_keep:halo + tile, :]
        ext_ref[0:halo, :] = ext_ref[tile:tile + halo, :]
    x1 = x_ref[...] + _mod_rows(gt_ref, tile) * y
    x1_ref[...] = x1
    h = _rms_norm(x1, g_ref[...]) * (1.0 + _mod_rows(sc_ref, tile)) + _mod_rows(sh_ref, tile)

    lane = lax.broadcasted_iota(jnp.int32, (tile, LANES), 1).astype(F32)
    logits = jnp.where(lane < n_experts, _dot_hp(h, wr_ref[...]) + br_ref[...], -jnp.inf)
    picks, vals = [], []
    for _ in range(top_k):
        m = jnp.max(logits, axis=-1, keepdims=True)
        pick = jnp.min(jnp.where(logits == m, lane, float(LANES)), axis=-1, keepdims=True)
        picks.append(pick)
        vals.append(m)
        logits = jnp.where(lane == pick, -jnp.inf, logits)
    exps = [jnp.exp(v - vals[0]) for v in vals]
    denom = exps[0]
    for e in exps[1:]:
        denom = denom + e

    onehots = [(lane == p).astype(F32) for p in picks]
    oh_sum = onehots[0]
    for oh in onehots[1:]:
        oh_sum = oh_sum + oh
    row = lax.broadcasted_iota(jnp.int32, (tile, tile), 0)
    col = lax.broadcasted_iota(jnp.int32, (tile, tile), 1)
    before = _dot((row > col).astype(BF16), oh_sum.astype(BF16))
    count = jnp.sum(oh_sum, axis=0, keepdims=True)
    cnt_pad = jnp.floor((count + (BF16_ROWS - 1.0)) * (1.0 / BF16_ROWS)) * BF16_ROWS
    lane8 = lax.broadcasted_iota(jnp.int32, (SUBLANES, LANES), 1)
    run = jnp.broadcast_to(cnt_pad, (SUBLANES, LANES))
    shift = 1
    while shift < n_experts:
        run = run + jnp.where(lane8 >= shift, pltpu.roll(run, shift, 1), 0.0)
        shift *= 2
    pos = before + (run[0:1] - cnt_pad)
    pos_out = jnp.zeros((tile, LANES), F32)
    gate_out = jnp.zeros((tile, LANES), F32)
    for k in range(top_k):
        pos_k = jnp.sum(onehots[k] * pos, axis=-1, keepdims=True)
        pos_out = jnp.where(lane == k, pos_k, pos_out)
        gate_out = jnp.where(lane == k, exps[k] / denom, gate_out)
    pos_ref[...] = pos_out.astype(jnp.int32)
    gate_ref[...] = gate_out
    cnt_ref[0] = cnt_pad.astype(jnp.int32)

    pos_t = pos_out.T
    hb = h.astype(BF16)
    for r0 in range(0, n_sorted, row_chunk):
        slot = (lax.broadcasted_iota(jnp.int32, (row_chunk, tile), 0) + r0).astype(F32)
        sel = jnp.where(slot == pos_t[0:1], 1.0, 0.0)
        for k in range(1, top_k):
            sel = sel + jnp.where(slot == pos_t[k:k + 1], 1.0, 0.0)
        xs_ref[r0:r0 + row_chunk, :] = _dot(sel.astype(BF16), hb).astype(BF16)


def resid_router(x, y, w_out_bf16, mod, g, w_r, b_r, sorted_in, *, tile, rows_per_seq, top_k,
                 block_offset, n_blocks_total, pool=None):
    n, d = x.shape
    n_experts = w_r.shape[1]
    n_sorted = _sorted_rows(tile, top_k, n_experts)
    w_r_pad = jnp.pad(w_r, ((0, 0), (0, LANES - n_experts)))
    b_r_pad = jnp.pad(b_r, (0, LANES - n_experts)).reshape(1, LANES)
    has_w_out = w_out_bf16 is not None
    chained = sorted_in is not None
    row_spec = pl.BlockSpec((tile, d), lambda i: (i, 0))
    lane_spec = pl.BlockSpec((tile, LANES), lambda i: (i, 0))
    full = lambda a: pl.BlockSpec(a.shape, lambda i: (0,) * a.ndim)
    extra_out_specs, extra_out_shape, scratch, pool_static = [], [], [], None
    if pool is None:
        in_specs = [row_spec, pl.BlockSpec((tile, y.shape[1]), lambda i: (i, 0))]
        args = [x, y]
        if has_w_out:
            in_specs.append(full(w_out_bf16))
            args.append(w_out_bf16)
    else:
        g_mix, w_grp, scale, windows, n_keep = pool
        halo = 2 * SUBLANES
        tiles_per_seq = rows_per_seq // tile
        assert max(windows) <= halo <= tile and n_keep <= tile
        in_specs = [row_spec, _mod_spec(mod, 0, d, tile, rows_per_seq), _mod_spec(mod, 1, d, tile, rows_per_seq),
                    pl.BlockSpec((1, d), lambda i: (0, 0)), full(w_grp), pl.BlockSpec((1, d), lambda i: (0, 0))]
        args = [x, mod, mod, g_mix.reshape(1, d), w_grp, scale.reshape(1, d)]
        extra_out_specs = [pl.BlockSpec((1, n_keep, d), lambda i: (i // tiles_per_seq, 0, 0))]
        extra_out_shape = [jax.ShapeDtypeStruct((n // rows_per_seq, n_keep, d), F32)]
        scratch = [pltpu.VMEM((halo + tile, d), F32)]
        pool_static = (windows, halo, tiles_per_seq)
    in_specs += [_mod_spec(mod, 2, d, tile, rows_per_seq), _mod_spec(mod, 3, d, tile, rows_per_seq),
                 _mod_spec(mod, 4, d, tile, rows_per_seq), pl.BlockSpec((1, d), lambda i: (0, 0)),
                 full(w_r_pad), full(b_r_pad)]
    args += [mod, mod, mod, g.reshape(1, d), w_r_pad, b_r_pad]
    aliases = {}
    if chained:
        aliases = {len(args): 1}
        in_specs.append(pl.BlockSpec(memory_space=pl.ANY))
        args.append(sorted_in)
    n_tiles = n // tile
    return pl.pallas_call(
        functools.partial(_resid_router_kernel, top_k=top_k, n_experts=n_experts, has_w_out=has_w_out,
                          chained=chained, row_chunk=ROUTER_ROW_CHUNK, pool=pool_static),
        grid=(n_tiles,),
        in_specs=in_specs,
        out_specs=[row_spec,
                   pl.BlockSpec((n_sorted, d), lambda i: (i + block_offset, 0)),
                   lane_spec, lane_spec,
                   pl.BlockSpec((1, 1, LANES), lambda i: (i, 0, 0))] + extra_out_specs,
        out_shape=[jax.ShapeDtypeStruct((n, d), F32),
                   jax.ShapeDtypeStruct((n_blocks_total * n_sorted, d), BF16),
                   jax.ShapeDtypeStruct((n, LANES), jnp.int32), jax.ShapeDtypeStruct((n, LANES), F32),
                   jax.ShapeDtypeStruct((n_tiles, 1, LANES), jnp.int32)] + extra_out_shape,
        scratch_shapes=scratch,
        input_output_aliases=aliases,
        compiler_params=_params("arbitrary" if pool is not None else "parallel"),
        name="resid_router",
    )(*args)


def _experts_kernel(te_ref, first_ref, rows_ref, base_ref, slo_ref, shi_ref, wslot_ref, nexte_ref, used_ref,
                    sstart_ref, slen_ref, ssrc_ref,
                    xs_hbm, wgu_hbm, bgu_ref, wdn_hbm, bdn_ref, ys_hbm,
                    xbuf, ybuf, wgu_f, wdn_f, wgu_b, wdn_b, in_sem, out_sem, w_sem,
                    *, layer, limit, alpha, col_chunk):
    del ys_hbm
    i = pl.program_id(0)
    used = used_ref[0]
    tm = xbuf.shape[1]

    def copy(src_rows, dst_rows, slot, inbound):
        if inbound:
            return pltpu.make_async_copy(xs_hbm.at[src_rows, :], xbuf.at[slot, dst_rows, :], in_sem.at[slot])
        return pltpu.make_async_copy(ybuf.at[slot, dst_rows, :], xs_hbm.at[src_rows, :], out_sem.at[slot])

    def piece_copies(tile_idx, slot, inbound, wait):
        if wait:
            rows = pl.ds(0, pl.multiple_of(rows_ref[tile_idx], BF16_ROWS))
            copy(rows, rows, slot, inbound).wait()
            return
        base = base_ref[tile_idx]

        def piece(s, c):
            first = sstart_ref[s] - base
            lo = jnp.maximum(first, 0)
            n_rows = pl.multiple_of(jnp.minimum(first + slen_ref[s], tm) - lo, BF16_ROWS)

            @pl.when(n_rows > 0)
            def _():
                src = pl.multiple_of(ssrc_ref[s] + (lo - first), BF16_ROWS)
                copy(pl.ds(src, n_rows), pl.ds(pl.multiple_of(lo, BF16_ROWS), n_rows), slot, inbound).start()
            return c
        lax.fori_loop(slo_ref[tile_idx], shi_ref[tile_idx], piece, 0)

    def weight_copies(e, slot):
        return (pltpu.make_async_copy(wgu_hbm.at[layer, e], wgu_f.at[slot], w_sem.at[slot]),
                pltpu.make_async_copy(wdn_hbm.at[layer, e], wdn_f.at[slot], w_sem.at[slot]))

    @pl.when(i == 0)
    def _():
        xbuf[...] = jnp.zeros_like(xbuf)
        for cp in weight_copies(te_ref[0], 0):
            cp.start()
        piece_copies(0, 0, True, False)

    @pl.when(i < used)
    def _():
        slot = i % 2

        @pl.when(i + 1 < used)
        def _():
            piece_copies(i + 1, 1 - slot, True, False)

        @pl.when(first_ref[i] == 1)
        def _():
            ws = wslot_ref[i]
            for cp in weight_copies(te_ref[i], ws):
                cp.wait()

            @pl.when(nexte_ref[i] >= 0)
            def _():
                for cp in weight_copies(nexte_ref[i], 1 - ws):
                    cp.start()

            @pl.when(rows_ref[i] <= tm - tm // MLP_ROW_STEPS)
            def _():
                wgu_b[...] = wgu_f[ws].astype(BF16)
                wdn_b[...] = wdn_f[ws].astype(BF16)

        piece_copies(i, slot, True, True)

        @pl.when(i >= 2)
        def _():
            piece_copies(i - 2, slot, False, True)

        d_ff = wdn_b.shape[0]
        e = te_ref[i]
        b_gu = bgu_ref[pl.ds(e, 1), :]
        b_dn = bdn_ref[pl.ds(e, 1), :]

        def mlp(n_rows, cast_slot=None):
            x = xbuf[slot, :n_rows, :]
            y = None
            for c0 in range(0, d_ff, col_chunk):
                cs = slice(c0, c0 + col_chunk)
                us = slice(d_ff + c0, d_ff + c0 + col_chunk)
                if cast_slot is None:
                    w_gate, w_up, w_down = wgu_b[:, cs], wgu_b[:, us], wdn_b[cs, :]
                else:
                    w_gate = wgu_f[cast_slot, :, cs].astype(BF16)
                    w_up = wgu_f[cast_slot, :, us].astype(BF16)
                    w_down = wdn_f[cast_slot, cs, :].astype(BF16)
                    wgu_b[:, cs], wgu_b[:, us], wdn_b[cs, :] = w_gate, w_up, w_down
                gate = jnp.minimum(_dot(x, w_gate) + b_gu[:, cs], limit)
                up = jnp.clip(_dot(x, w_up) + b_gu[:, us], -limit, limit)
                act = ((up + 1.0) * (gate * _sigmoid(alpha * gate))).astype(BF16)
                part = _dot(act, w_down)
                y = part if y is None else y + part
            ybuf[slot, :n_rows, :] = (y + b_dn).astype(BF16)

        step = tm // MLP_ROW_STEPS
        for part in range(1, MLP_ROW_STEPS):
            @pl.when((rows_ref[i] > (part - 1) * step) & (rows_ref[i] <= part * step))
            def _():
                mlp(part * step)

        @pl.when((rows_ref[i] > tm - step) & (first_ref[i] != 1))
        def _():
            mlp(tm)

        @pl.when((rows_ref[i] > tm - step) & (first_ref[i] == 1))
        def _():
            mlp(tm, cast_slot=wslot_ref[i])
        piece_copies(i, slot, False, False)

    @pl.when(i == pl.num_programs(0) - 1)
    def _():
        @pl.when(used >= 2)
        def _():
            piece_copies(used - 2, used % 2, False, True)
        piece_copies(used - 1, (used - 1) % 2, False, True)


def moe_experts(sorted_rows, w_gu, b_gu, w_dn, b_dn, tables, *, layer, row_tile, limit, alpha, col_chunk=512):
    n_rows, d = sorted_rows.shape
    d_gu = w_gu.shape[-1]
    d_ff = w_dn.shape[-2]
    n_tiles = tables[0].shape[0]
    vmem = lambda a: pl.BlockSpec(a.shape, lambda i, *_: (0,) * a.ndim)
    any_spec = pl.BlockSpec(memory_space=pl.ANY)
    return pl.pallas_call(
        functools.partial(_experts_kernel, layer=layer, limit=limit, alpha=alpha,
                          col_chunk=min(col_chunk, d_ff)),
        grid_spec=pltpu.PrefetchScalarGridSpec(
            num_scalar_prefetch=len(tables),
            grid=(n_tiles,),
            in_specs=[any_spec, any_spec, vmem(b_gu), any_spec, vmem(b_dn)],
            out_specs=any_spec,
            scratch_shapes=[
                pltpu.VMEM((2, row_tile, d), BF16), pltpu.VMEM((2, row_tile, d), BF16),
                pltpu.VMEM((2, d, d_gu), F32), pltpu.VMEM((2, d_ff, d), F32),
                pltpu.VMEM((d, d_gu), BF16), pltpu.VMEM((d_ff, d), BF16),
                pltpu.SemaphoreType.DMA((2,)), pltpu.SemaphoreType.DMA((2,)), pltpu.SemaphoreType.DMA((2,)),
            ],
        ),
        out_shape=jax.ShapeDtypeStruct((n_rows, d), BF16),
        input_output_aliases={len(tables): 0},
        compiler_params=_params("arbitrary"),
        name="moe_experts",
    )(*tables, sorted_rows, w_gu, b_gu, w_dn, b_dn)


def _expert_tables(cnt, n_sorted, n_tiles, row_tile):
    n_blocks, n_experts = cnt.shape

    def prefix_sum(a):
        n = a.shape[-1]
        upto = jnp.arange(n)[:, None] <= jnp.arange(n)[None, :]
        return jnp.sum(jnp.where(upto, a[..., :, None], 0), axis=-2)

    local_off = prefix_sum(cnt) - cnt
    seg_end = prefix_sum(cnt.T)
    seg_start = seg_end - cnt.T
    seg_src = jnp.arange(n_blocks, dtype=jnp.int32)[None, :] * n_sorted + local_off.T
    total = seg_end[:, -1]
    padded = (total + row_tile - 1) // row_tile * row_tile
    pad_end = prefix_sum(padded)
    pad_start = pad_end - padded
    tiles = jnp.arange(n_tiles, dtype=jnp.int32)
    experts = jnp.arange(n_experts, dtype=jnp.int32)
    n_used = pad_end[-1] // row_tile
    tile_expert = jnp.minimum(jnp.sum(tiles[:, None] * row_tile >= pad_end[None, :], axis=1), n_experts - 1)
    is_expert = tile_expert[:, None] == experts[None, :]

    def of_expert(a):
        if a.ndim == 1:
            return jnp.sum(jnp.where(is_expert, a[None, :], 0), axis=1)
        return jnp.sum(jnp.where(is_expert[:, :, None], a[None, :, :], 0), axis=1)

    live = tiles < n_used
    tile_base = tiles * row_tile - of_expert(pad_start)
    tile_rows = jnp.where(live, jnp.clip(of_expert(total) - tile_base, 0, row_tile), 0)
    tile_first = (live & (tile_base == 0)).astype(jnp.int32)
    seg_lo = jnp.sum(of_expert(seg_end) <= tile_base[:, None], axis=1)
    seg_hi = jnp.sum(of_expert(seg_start) < tile_base[:, None] + row_tile, axis=1)
    seg_lo = tile_expert * n_blocks + jnp.minimum(seg_lo, seg_hi)
    seg_hi = tile_expert * n_blocks + seg_hi
    owns = total > 0
    order = prefix_sum(owns.astype(jnp.int32)) - 1
    experts = jnp.arange(n_experts, dtype=jnp.int32)
    later = (experts[None, :] > experts[:, None]) & owns[None, :]
    next_expert = jnp.min(jnp.where(later, experts[None, :], n_experts), axis=1)
    next_expert = jnp.where(next_expert < n_experts, next_expert, -1)
    i32 = lambda a: a.astype(jnp.int32)
    return (i32(tile_expert), tile_first, i32(tile_rows), i32(tile_base), i32(seg_lo), i32(seg_hi),
            i32(of_expert(order) % 2), i32(of_expert(next_expert)), i32(n_used).reshape(1),
            i32(seg_start.reshape(-1)), i32(cnt.T.reshape(-1)), i32(seg_src.reshape(-1)))


def _combine_kernel(x_ref, pos_ref, gate_ref, gt_ref, gfin_ref, ys_ref, o_ref, *, top_k, final_norm, k_chunk):
    tile, d = x_ref.shape
    n_sorted = ys_ref.shape[0]
    pos = pos_ref[...].astype(F32)
    gates = gate_ref[...]
    acc = None
    for r0 in range(0, n_sorted, k_chunk):
        slot = (lax.broadcasted_iota(jnp.int32, (tile, k_chunk), 1) + r0).astype(F32)
        w = jnp.where(slot == pos[:, 0:1], gates[:, 0:1], 0.0)
        for k in range(1, top_k):
            w = w + jnp.where(slot == pos[:, k:k + 1], gates[:, k:k + 1], 0.0)
        part = _dot(w.astype(BF16), ys_ref[r0:r0 + k_chunk, :])
        acc = part if acc is None else acc + part
    out = x_ref[...] + _mod_rows(gt_ref, tile) * acc
    if final_norm:
        out = _rms_norm(out, gfin_ref[...])
    o_ref[...] = out


def moe_combine(x, ys, pos, gates, mod, g_final, *, tile, rows_per_seq, final_norm, block_offset, n_sorted):
    n, d = x.shape
    return pl.pallas_call(
        functools.partial(_combine_kernel, top_k=TOP_K, final_norm=final_norm, k_chunk=COMBINE_K_CHUNK),
        grid=(n // tile,),
        in_specs=[
            pl.BlockSpec((tile, d), lambda i: (i, 0)),
            pl.BlockSpec((tile, LANES), lambda i: (i, 0)),
            pl.BlockSpec((tile, LANES), lambda i: (i, 0)),
            _mod_spec(mod, 5, d, tile, rows_per_seq),
            pl.BlockSpec((1, d), lambda i: (0, 0)),
            pl.BlockSpec((n_sorted, d), lambda i: (i + block_offset, 0)),
        ],
        out_specs=pl.BlockSpec((tile, d), lambda i: (i, 0)),
        out_shape=jax.ShapeDtypeStruct((n, d), F32),
        compiler_params=_params("parallel"),
        name="moe_combine",
    )(x, pos, gates, mod, g_final.reshape(1, d), ys)


def _pool_groups(h, window_sum, counts, w_ref, scale):
    n_groups = w_ref.shape[0]
    dg = h.shape[-1] // n_groups
    outs = []
    for gi in range(n_groups):
        cols = slice(gi * dg, (gi + 1) * dg)
        pooled = window_sum(gi, cols) / counts[gi] - h[:, cols]
        outs.append(_dot(pooled.astype(BF16), w_ref[gi]))
    return jnp.concatenate(outs, axis=-1) * scale


def _pool_sample_kernel(x_ref, buf_ref, sh_ref, sc_ref, g_ref, w_ref, scale_ref, y_ref, cache_ref,
                        *, windows, start_pos):
    t_len = x_ref.shape[0]
    n_prev = buf_ref.shape[0]
    hs = [_rms_norm(x_ref[t], g_ref[...]) * (1.0 + sc_ref[...]) + sh_ref[...] for t in range(t_len)]

    def ext(r):
        return buf_ref[r] if r < n_prev else hs[r - n_prev]

    for t in range(t_len):
        def window_sum(gi, cols):
            acc = hs[t][:, cols]
            for s in range(1, windows[gi]):
                acc = acc + ext(n_prev + t - s)[:, cols]
            return acc
        counts = [float(min(start_pos + t + 1, w)) for w in windows]
        y_ref[t] = _pool_groups(hs[t], window_sum, counts, w_ref, scale_ref[...])
    for r in range(n_prev):
        cache_ref[r] = ext(t_len + r)


def pool_mixer_sample(x_t, buf_t, mod, g, w_grp_bf16, scale, *, seq_block, windows, start_pos):
    t_len, n_seq, d = x_t.shape
    n_prev = buf_t.shape[0]
    assert start_pos >= n_prev >= max(windows) - 1
    mod_spec = lambda k: pl.BlockSpec((seq_block, d), lambda i: (i, k))
    return pl.pallas_call(
        functools.partial(_pool_sample_kernel, windows=windows, start_pos=start_pos),
        grid=(n_seq // seq_block,),
        in_specs=[
            pl.BlockSpec((t_len, seq_block, d), lambda i: (0, i, 0)),
            pl.BlockSpec((n_prev, seq_block, d), lambda i: (0, i, 0)),
            mod_spec(0), mod_spec(1),
            pl.BlockSpec((1, d), lambda i: (0, 0)),
            pl.BlockSpec(w_grp_bf16.shape, lambda i: (0, 0, 0)),
            pl.BlockSpec((1, d), lambda i: (0, 0)),
        ],
        out_specs=[pl.BlockSpec((t_len, seq_block, d), lambda i: (0, i, 0)),
                   pl.BlockSpec((n_prev, seq_block, d), lambda i: (0, i, 0))],
        out_shape=[jax.ShapeDtypeStruct((t_len, n_seq, d), F32), jax.ShapeDtypeStruct((n_prev, n_seq, d), F32)],
        compiler_params=_params("parallel"),
        name="pool_mixer_sample",
    )(x_t, buf_t, mod, mod, g.reshape(1, d), w_grp_bf16, scale.reshape(1, d))


TOP_K = 4
SWIGLU_LIMIT = 7.0
SWIGLU_ALPHA = 1.702
POOL_WINDOWS = (2, 4, 8, 16)
PAST_LEN = 16384
MOE_TOKEN_TILE = 512
MOE_ROW_TILE = 512
ROUTER_ROW_CHUNK = 256
COMBINE_K_CHUNK = 512
HGRN_TIME_BLOCK = 1024
HGRN_CHUNK = 256
HGRN_SUB = 32
SAMPLE_T_PAD = 8
SAMPLE_SEQ_BLOCK = 8


def kernel(x_prompt, x_sample, c_prompt, c_sample, state_hgrn, cache_pool, g_norm_mix, g_norm_ffn, w_ada, b_ada, w_in_hgrn, lb_logits, g_out_hgrn, w_out_hgrn, w_grp_pool, scale_pool, w_router, b_router, w_gate_up, b_gate_up, w_down, b_down, g_final):
    bp, tp, d = x_prompt.shape
    bs, ts, _ = x_sample.shape
    n_p, n_s = bp * tp, bs * ts
    n_experts = w_router.shape[-1]
    hk = w_out_hgrn.shape[1]
    assert n_s == MOE_TOKEN_TILE and n_p % MOE_TOKEN_TILE == 0
    blocks_p = n_p // MOE_TOKEN_TILE
    n_blocks = blocks_p + 1
    n_sorted = _sorted_rows(MOE_TOKEN_TILE, TOP_K, n_experts)
    n_row_tiles = -(-(n_blocks * n_sorted + n_experts * (MOE_ROW_TILE - BF16_ROWS)) // MOE_ROW_TILE)

    mod = adaln(jnp.concatenate([c_prompt, c_sample], axis=0), w_ada, b_ada)
    mod_p = [mod[l, :bp][:, None, :] for l in range(mod.shape[0])]
    mod_s = [mod[l, bp:] for l in range(mod.shape[0])]

    xp = x_prompt.reshape(n_p, d)
    xs = x_sample.transpose(1, 0, 2).reshape(n_s, d)

    def moe(layer, x_p, y_p, x_s, y_s, w_out, final_norm, pool_p=None):
        route = functools.partial(resid_router, g=g_norm_ffn[layer], w_r=w_router[layer], b_r=b_router[layer],
                                  tile=MOE_TOKEN_TILE, top_k=TOP_K, n_blocks_total=n_blocks)
        x1_p, sorted_rows, pos_p, gate_p, cnt_p, *cache = route(
            x_p, y_p, w_out, mod_p[layer], sorted_in=None, rows_per_seq=tp, block_offset=0, pool=pool_p)
        x1_s, sorted_rows, pos_s, gate_s, cnt_s = route(
            x_s, y_s, w_out, mod_s[layer], sorted_in=sorted_rows, rows_per_seq=None, block_offset=blocks_p)
        cnt8 = jnp.concatenate([cnt_p, cnt_s], axis=0)[:, 0, :n_experts]
        tables = _expert_tables(cnt8, n_sorted, n_row_tiles, MOE_ROW_TILE)
        ys = moe_experts(sorted_rows, w_gate_up, b_gate_up[layer], w_down, b_down[layer], tables,
                         layer=layer, row_tile=MOE_ROW_TILE, limit=SWIGLU_LIMIT, alpha=SWIGLU_ALPHA)
        combine = functools.partial(moe_combine, ys=ys, g_final=g_final, tile=MOE_TOKEN_TILE,
                                    final_norm=final_norm, n_sorted=n_sorted)
        out_p = combine(x1_p, pos=pos_p, gates=gate_p, mod=mod_p[layer], rows_per_seq=tp, block_offset=0)
        out_s = combine(x1_s, pos=pos_s, gates=gate_s, mod=mod_s[layer], rows_per_seq=None,
                        block_offset=blocks_p)
        return (out_p, out_s, *cache)

    w_in = w_in_hgrn[0].astype(BF16)
    proj_s = norm_proj(xs, mod_s[0], g_norm_mix[0], w_in, tile=bs, rows_per_seq=None)
    o_p, state_p = hgrn_recurrence(None, lb_logits, g_out_hgrn[0], None,
                                   layer=0, seq_block=1, time_block=HGRN_TIME_BLOCK, chunk=HGRN_CHUNK,
                                   c_sub=HGRN_SUB, n_valid=HGRN_CHUNK,
                                   norm_proj_of=(x_prompt, mod_p[0], g_norm_mix[0], w_in),
                                   out_dtype=BF16)
    proj_sb = jnp.pad(proj_s.reshape(ts, bs, 4 * hk).transpose(1, 0, 2), ((0, 0), (0, SAMPLE_T_PAD - ts), (0, 0)))
    o_s, state_s = hgrn_recurrence(proj_sb, lb_logits, g_out_hgrn[0], state_hgrn[0],
                                   layer=0, seq_block=SAMPLE_SEQ_BLOCK, time_block=SAMPLE_T_PAD,
                                   chunk=SAMPLE_T_PAD, c_sub=SAMPLE_T_PAD, n_valid=ts)
    o_s = o_s[:, :ts].transpose(1, 0, 2).reshape(n_s, hk)
    x_p, x_s = moe(0, xp, o_p.reshape(n_p, hk), xs, o_s, w_out_hgrn[0].astype(BF16), False)

    w_grp = w_grp_pool[0].astype(BF16)
    n_keep = cache_pool.shape[2]
    y_s, cache_s = pool_mixer_sample(x_s.reshape(ts, bs, d), cache_pool[0].transpose(1, 0, 2), mod_s[1],
                                     g_norm_mix[1], w_grp, scale_pool[0],
                                     seq_block=32, windows=POOL_WINDOWS, start_pos=PAST_LEN)
    x_p, x_s, cache_p = moe(1, x_p, None, x_s, y_s.reshape(n_s, d), None, True,
                            pool_p=(g_norm_mix[1], w_grp, scale_pool[0], POOL_WINDOWS, n_keep))

    return (x_p.reshape(bp, tp, d), x_s.reshape(ts, bs, d).transpose(1, 0, 2),
            state_p[None], state_s[None], cache_p[None], cache_s.transpose(1, 0, 2)[None])
```

```python
import functools

import jax
import jax.numpy as jnp
from jax import lax
from jax.experimental import pallas as pl
from jax.experimental.pallas import tpu as pltpu

F32 = jnp.float32
BF16 = jnp.bfloat16

RMS_EPS = 1e-6
LANES = 128
SUBLANES = 8
BF16_ROWS = 16
HEAD_DIM = 128
VMEM_LIMIT = 56 * 1024 * 1024

_dot = functools.partial(jnp.dot, preferred_element_type=F32)


def _params(*semantics):
    return pltpu.CompilerParams(dimension_semantics=semantics, vmem_limit_bytes=VMEM_LIMIT)


def _split_bf16(x, n):
    parts, r = [], x
    for _ in range(n):
        p = r.astype(BF16)
        parts.append(p)
        r = r - p.astype(F32)
    return parts


def _dot_hp(a, b):
    a_hi, a_lo = _split_bf16(a, 2)
    b_hi, b_lo = _split_bf16(b, 2)
    return _dot(a_hi, b_hi) + (_dot(a_hi, b_lo) + _dot(a_lo, b_hi))


def _sigmoid(x):
    return 1.0 / (1.0 + jnp.exp(-x))


def _silu(x):
    return x * _sigmoid(x)


def _rms_norm(x, g):
    ms = jnp.mean(x * x, axis=-1, keepdims=True)
    return x * lax.rsqrt(ms + RMS_EPS) * g


def _adaln_kernel(c_ref, w_ref, b_ref, o_ref):
    o_ref[0] = _dot_hp(_silu(c_ref[...]), w_ref[0]) + b_ref[0]


def adaln(c_all, w_ada, b_ada, *, col_block=1536):
    n_seq, d = c_all.shape
    n_layers, _, d6 = w_ada.shape
    return pl.pallas_call(
        _adaln_kernel,
        grid=(n_layers, d6 // col_block),
        in_specs=[
            pl.BlockSpec((n_seq, d), lambda l, j: (0, 0)),
            pl.BlockSpec((1, d, col_block), lambda l, j: (l, 0, j)),
            pl.BlockSpec((1, 1, col_block), lambda l, j: (l, 0, j)),
        ],
        out_specs=pl.BlockSpec((1, n_seq, col_block), lambda l, j: (l, 0, j)),
        out_shape=jax.ShapeDtypeStruct((n_layers, n_seq, d6), F32),
        compiler_params=_params("parallel", "parallel"),
        name="adaln",
    )(c_all, w_ada, b_ada.reshape(n_layers, 1, d6))


def _mod_spec(mod, k, d, tile, rows_per_seq):
    if rows_per_seq is None:
        return pl.BlockSpec((mod.shape[0], d), lambda i: (0, k))
    tiles_per_seq = rows_per_seq // tile
    return pl.BlockSpec((1, 1, d), lambda i: (i // tiles_per_seq, 0, k))


def _mod_rows(ref, tile):
    m = ref[...].reshape(-1, ref.shape[-1])
    if m.shape[0] not in (1, tile):
        m = jnp.concatenate([m] * (tile // m.shape[0]), axis=0)
    return m


def _norm_proj_kernel(x_ref, sh_ref, sc_ref, g_ref, w_ref, o_ref, *, n_in_tiles):
    tile = x_ref.shape[0]

    @pl.when(pl.program_id(0) < n_in_tiles)
    def _():
        h = _rms_norm(x_ref[...], g_ref[...]) * (1.0 + _mod_rows(sc_ref, tile)) + _mod_rows(sh_ref, tile)
        o_ref[...] = _dot(h.astype(BF16), w_ref[...])

    @pl.when(pl.program_id(0) >= n_in_tiles)
    def _():
        o_ref[...] = jnp.zeros_like(o_ref)


def norm_proj(x, mod, g, w_bf16, *, tile, rows_per_seq, n_out_tiles=None):
    n, d = x.shape
    p = w_bf16.shape[1]
    n_in_tiles = n // tile
    n_out_tiles = n_in_tiles if n_out_tiles is None else n_out_tiles
    return pl.pallas_call(
        functools.partial(_norm_proj_kernel, n_in_tiles=n_in_tiles),
        grid=(n_out_tiles,),
        in_specs=[
            pl.BlockSpec((tile, d), lambda i: (jnp.minimum(i, n_in_tiles - 1), 0)),
            _mod_spec(mod, 0, d, tile, rows_per_seq),
            _mod_spec(mod, 1, d, tile, rows_per_seq),
            pl.BlockSpec((1, d), lambda i: (0, 0)),
            pl.BlockSpec((d, p), lambda i: (0, 0)),
        ],
        out_specs=pl.BlockSpec((tile, p), lambda i: (i, 0)),
        out_shape=jax.ShapeDtypeStruct((n_out_tiles * tile, p), F32),
        compiler_params=_params("parallel"),
        name="hgrn_norm_proj",
    )(x, mod, mod, g.reshape(1, d), w_bf16)


def _cumsum_rows(x, tri):
    hi, mid, lo = _split_bf16(x, 3)
    return _dot(tri, hi) + (_dot(tri, mid) + _dot(tri, lo))


MAX_LOG_DECAY_RANGE = 80.0
MLP_ROW_STEPS = 4


def _hgrn_prep(proj, lb, n_valid):
    c = proj.shape[0]
    hk = proj.shape[1] // 4
    row = lax.broadcasted_iota(jnp.int32, (c, c), 0)
    col = lax.broadcasted_iota(jnp.int32, (c, c), 1)
    zf = proj[:, hk:2 * hk]
    e = jnp.exp(-jnp.abs(zf))
    r = 1.0 / (1.0 + e)
    pos = zf >= 0
    sig_p = jnp.where(pos, 1.0, e) * r
    sig_n = jnp.where(pos, e, 1.0) * r
    logf = jnp.log(lb + (1.0 - lb) * sig_p)
    k = (1.0 - lb) * sig_n
    if n_valid < c:
        live = lax.broadcasted_iota(jnp.int32, (c, 1), 0) < n_valid
        logf = jnp.where(live, logf, 0.0)
        k = jnp.where(live, k, 0.0)
    b = _cumsum_rows(logf, (row >= col).astype(BF16))
    return _silu(proj[:, :hk]), k, proj[:, 2 * hk:3 * hk], _silu(proj[:, 3 * hk:]), b


def _decay_range(b, c_sub):
    c = b.shape[0]
    worst = None
    for i in range(c // c_sub):
        span = b[i * c_sub:i * c_sub + 1, :] - b[(i + 1) * c_sub - 1:(i + 1) * c_sub, :]
        worst = span if worst is None else jnp.maximum(worst, span)
    return jnp.max(worst)


def _head_norm_gate(o, gate, gout):
    outs = []
    for h in range(o.shape[1] // HEAD_DIM):
        hs = slice(h * HEAD_DIM, (h + 1) * HEAD_DIM)
        oh = o[:, hs]
        outs.append(oh * lax.rsqrt(jnp.mean(oh * oh, axis=-1, keepdims=True) + RMS_EPS) * gout * gate[:, hs])
    return jnp.concatenate(outs, axis=-1)


def _hgrn_chunk(prep, gout, st_refs, seq, c_sub):
    q, k, v, gate, b = prep
    c = q.shape[0]
    n_heads = q.shape[1] // HEAD_DIM
    row = lax.broadcasted_iota(jnp.int32, (c, c), 0)
    col = lax.broadcasted_iota(jnp.int32, (c, c), 1)
    causal = row >= col
    n_sub = c // c_sub
    subs = [slice(i * c_sub, (i + 1) * c_sub) for i in range(n_sub)]

    intra, inter = [], []
    for h in range(n_heads):
        hs = slice(h * HEAD_DIM, (h + 1) * HEAD_DIM)
        bh, qh, kh, vh = b[:, hs], q[:, hs], k[:, hs], v[:, hs]
        vb = vh.astype(BF16)
        refs = [bh[i * c_sub + c_sub // 2:i * c_sub + c_sub // 2 + 1, :] for i in range(n_sub)]
        k_own = [kh[rs] * jnp.exp(jnp.minimum(ref - bh[rs], MAX_LOG_DECAY_RANGE))
                 for rs, ref in zip(subs, refs)]
        a_rows = []
        for i in range(n_sub):
            q_hat = (qh[subs[i]] * jnp.exp(bh[subs[i]] - refs[i])).astype(BF16)
            parts = [k_own[j] * jnp.exp(refs[i] - refs[j]) for j in range(i)] + [k_own[i]]
            parts += [jnp.zeros((c_sub, HEAD_DIM), F32)] * (n_sub - 1 - i)
            k_hat = (jnp.concatenate(parts, axis=0) if n_sub > 1 else parts[0]).astype(BF16)
            a_rows.append(lax.dot_general(q_hat, k_hat, (((1,), (1,)), ((), ())),
                                          preferred_element_type=F32))
        att = jnp.where(causal, jnp.concatenate(a_rows, axis=0) if len(a_rows) > 1 else a_rows[0], 0.0)
        intra.append(_dot(att.astype(BF16), vb))
        st = st_refs[seq, h]
        inter.append(lax.dot_general((qh * jnp.exp(bh)).astype(BF16), st.astype(BF16),
                                     (((1,), (1,)), ((), ())), preferred_element_type=F32))
        b_last = bh[c - 1:c, :]
        k_dec = (kh * jnp.exp(b_last - bh)).astype(BF16)
        st_refs[seq, h] = st * jnp.exp(b_last) + lax.dot_general(
            vb, k_dec, (((0,), (0,)), ((), ())), preferred_element_type=F32)
    inter = jnp.concatenate(inter, axis=-1)
    return _head_norm_gate(jnp.concatenate(intra, axis=-1) + inter, gate, gout), inter


def _hgrn_chunk_exact(prep, inter, gout, q_ref, b_ref, oi_ref):
    q, k, v, gate, b = prep
    c = q.shape[0]
    n_heads = q.shape[1] // HEAD_DIM
    q_ref[...] = q
    b_ref[...] = b
    key_row = lax.broadcasted_iota(jnp.int32, (c, 1), 0)

    def row_group(g, carry):
        rows = pl.ds(pl.multiple_of(g * SUBLANES, SUBLANES), SUBLANES)
        for h in range(n_heads):
            hs = slice(h * HEAD_DIM, (h + 1) * HEAD_DIM)
            q_g, b_g = q_ref[rows, hs], b_ref[rows, hs]
            o_rows = []
            for r in range(SUBLANES):
                decay = jnp.exp(jnp.minimum(b_g[r:r + 1] - b[:, hs], 0.0))
                score = jnp.sum(decay * k[:, hs] * q_g[r:r + 1], axis=-1, keepdims=True)
                score = jnp.where(key_row <= g * SUBLANES + r, score, 0.0)
                o_rows.append(jnp.sum(score * v[:, hs], axis=0, keepdims=True))
            oi_ref[rows, hs] = jnp.concatenate(o_rows, axis=0)
        return carry
    lax.fori_loop(0, c // SUBLANES, row_group, 0)
    return _head_norm_gate(oi_ref[...] + inter, gate, gout)


def _lower_bound(lb_logits, layer):
    e = jnp.exp(lb_logits - jnp.max(lb_logits, axis=0, keepdims=True))
    return jnp.sum(e[:layer + 1], axis=0, keepdims=True) / jnp.sum(e, axis=0, keepdims=True)


def _hgrn_rec_kernel(*refs, chunk, c_sub, n_valid, has_state, layer, fused_proj, time_major):
    refs = list(refs)
    if fused_proj:
        x_ref, sh_ref, sc_ref, g_ref, w_ref = refs[:5]
        del refs[:5]
        proj_ref = None
    else:
        proj_ref = refs.pop(0)
    lb_ref, gout_ref = refs[:2]
    del refs[:2]
    s0_ref = refs.pop(0) if has_state else None
    o_ref, sout_ref, st_ref, inter_ref, q_ref, b_ref, oi_ref = refs[:7]
    proj_buf = refs[7] if fused_proj else None
    bb, tb, _ = o_ref.shape
    n_heads = st_ref.shape[1]
    n_chunks = tb // chunk
    j = pl.program_id(1)

    @pl.when(j == 0)
    def _():
        if has_state:
            for s in range(bb):
                for h in range(n_heads):
                    st_ref[s, h] = s0_ref[s, h].T
        else:
            st_ref[...] = jnp.zeros_like(st_ref)

    lb = _lower_bound(lb_ref[...], layer)
    gout = gout_ref[...]

    def project(s, ci):
        rows = pl.ds(pl.multiple_of(ci * chunk, chunk), chunk)
        h = _rms_norm(x_ref[s, rows, :], g_ref[...]) * (1.0 + sc_ref[s]) + sh_ref[s]
        return _dot(h.astype(BF16), w_ref[...])

    if fused_proj:
        for s in range(bb):
            proj_buf[s, 0] = project(s, 0)

    def chunk_body(ci, carry):
        rows = pl.ds(pl.multiple_of(ci * chunk, chunk), chunk)

        def load_proj(s):
            if fused_proj:
                return proj_buf[s, ci % 2]
            return proj_ref[rows, s, :] if time_major else proj_ref[s, rows, :]
        span = None
        preps = [_hgrn_prep(load_proj(s), lb, n_valid) for s in range(bb)]
        for s, prep in enumerate(preps):
            o, inter_ref[s] = _hgrn_chunk(prep, gout, st_ref, s, c_sub)
            o_ref[s, rows, :] = o.astype(o_ref.dtype)
            worst = _decay_range(prep[4], c_sub)
            span = worst if span is None else jnp.maximum(span, worst)
            if fused_proj:
                proj_buf[s, (ci + 1) % 2] = project(s, jnp.minimum(ci + 1, n_chunks - 1))

        @pl.when(jnp.logical_not(span <= MAX_LOG_DECAY_RANGE))
        def _():
            for s in range(bb):
                prep = _hgrn_prep(load_proj(s), lb, n_valid)
                o_ref[s, rows, :] = _hgrn_chunk_exact(prep, inter_ref[s], gout, q_ref, b_ref,
                                                      oi_ref).astype(o_ref.dtype)
        return carry
    lax.fori_loop(0, n_chunks, chunk_body, 0)

    @pl.when(j == pl.num_programs(1) - 1)
    def _():
        for s in range(bb):
            for h in range(n_heads):
                sout_ref[s, h] = st_ref[s, h].T


def hgrn_recurrence(proj, lb_logits, g_out, s0, *, layer, seq_block, time_block, chunk, c_sub, n_valid,
                    norm_proj_of=None, out_dtype=F32, time_major=False):
    fused = norm_proj_of is not None
    if fused:
        x, mod, g, w_in = norm_proj_of
        bsz, t, d = x.shape
        p = w_in.shape[1]
        mod_spec = lambda k: pl.BlockSpec((seq_block, 1, d), lambda i, j: (i, 0, k))
        in_specs = [pl.BlockSpec((seq_block, time_block, d), lambda i, j: (i, j, 0)), mod_spec(0), mod_spec(1),
                    pl.BlockSpec((1, d), lambda i, j: (0, 0)), pl.BlockSpec((d, p), lambda i, j: (0, 0))]
        args = [x, mod, mod, g.reshape(1, d), w_in]
    elif time_major:
        t, bsz, p = proj.shape
        in_specs = [pl.BlockSpec((time_block, seq_block, p), lambda i, j: (j, i, 0))]
        args = [proj]
    else:
        bsz, t, p = proj.shape
        in_specs = [pl.BlockSpec((seq_block, time_block, p), lambda i, j: (i, j, 0))]
        args = [proj]
    hk = p // 4
    n_heads = hk // HEAD_DIM
    has_state = s0 is not None
    st_shape = (seq_block, n_heads, HEAD_DIM, HEAD_DIM)
    st_spec = pl.BlockSpec(st_shape, lambda i, j: (i, 0, 0, 0))
    in_specs += [pl.BlockSpec(lb_logits.shape, lambda i, j: (0, 0)),
                 pl.BlockSpec((1, HEAD_DIM), lambda i, j: (0, 0))]
    args += [lb_logits, g_out.reshape(1, HEAD_DIM)]
    if has_state:
        in_specs.append(st_spec)
        args.append(s0)
    scratch = [pltpu.VMEM(st_shape, F32), pltpu.VMEM((seq_block, chunk, hk), F32)]
    scratch += [pltpu.VMEM((chunk, hk), F32)] * 3
    if fused:
        scratch.append(pltpu.VMEM((seq_block, 2, chunk, p), F32))
    return pl.pallas_call(
        functools.partial(_hgrn_rec_kernel, chunk=chunk, c_sub=c_sub, n_valid=n_valid,
                          has_state=has_state, layer=layer, fused_proj=fused, time_major=time_major),
        grid=(bsz // seq_block, t // time_block),
        in_specs=in_specs,
        out_specs=[pl.BlockSpec((seq_block, time_block, hk), lambda i, j: (i, j, 0)), st_spec],
        out_shape=[jax.ShapeDtypeStruct((bsz, t, hk), out_dtype),
                   jax.ShapeDtypeStruct((bsz, n_heads, HEAD_DIM, HEAD_DIM), F32)],
        scratch_shapes=scratch,
        compiler_params=_params("parallel", "arbitrary"),
        name="hgrn_recurrence",
    )(*args)


def _sorted_rows(tile, top_k, n_experts):
    return tile * top_k + n_experts * BF16_ROWS


def _resid_router_kernel(*refs, top_k, n_experts, has_w_out, chained, row_chunk, pool):
    refs = list(refs)
    x_ref = refs.pop(0)
    if pool is None:
        y_ref = refs.pop(0)
        wo_ref = refs.pop(0) if has_w_out else None
    else:
        msh_ref, msc_ref, mg_ref, wgrp_ref, mscale_ref = refs[:5]
        del refs[:5]
    gt_ref, sh_ref, sc_ref, g_ref, wr_ref, br_ref = refs[:6]
    del refs[:6]
    if chained:
        refs.pop(0)
    x1_ref, xs_ref, pos_ref, gate_ref, cnt_ref = refs[:5]
    tile, d = x_ref.shape
    n_sorted = xs_ref.shape[0]

    if pool is None:
        y = y_ref[...]
        if has_w_out:
            y = _dot(y.astype(BF16), wo_ref[...])
    else:
        windows, halo, tiles_per_seq = pool
        cache_ref, ext_ref = refs[5:7]
        part = pl.program_id(0) % tiles_per_seq

        @pl.when(part == 0)
        def _():
            ext_ref[0:halo, :] = jnp.zeros((halo, d), F32)

        hm = _rms_norm(x_ref[...], mg_ref[...]) * (1.0 + _mod_rows(msc_ref, tile)) + _mod_rows(msh_ref, tile)
        ext_ref[halo:halo + tile, :] = hm
        token = part * tile + lax.broadcasted_iota(jnp.int32, (tile, 1), 0)

        def window_sum(gi, cols):
            acc = hm[:, cols]
            for s in range(1, windows[gi]):
                acc = acc + ext_ref[halo - s:halo - s + tile, cols]
            return acc
        counts = [jnp.minimum(token + 1, w).astype(F32) for w in windows]
        y = _pool_groups(hm, window_sum, counts, wgrp_ref, mscale_ref[...])

        n_keep = cache_ref.shape[1]
        @pl.when(part == tiles_per_seq - 1)
        def _():
            cache_ref[0] = ext_ref[halo + tile - n_keep:halo + tile, :]
        ext_ref[0:halo, :] = ext_ref[tile:tile + halo, :]
    x1 = x_ref[...] + _mod_rows(gt_ref, tile) * y
    x1_ref[...] = x1
    h = _rms_norm(x1, g_ref[...]) * (1.0 + _mod_rows(sc_ref, tile)) + _mod_rows(sh_ref, tile)

    lane = lax.broadcasted_iota(jnp.int32, (tile, LANES), 1).astype(F32)
    logits = jnp.where(lane < n_experts, _dot_hp(h, wr_ref[...]) + br_ref[...], -jnp.inf)
    picks, vals = [], []
    for _ in range(top_k):
        m = jnp.max(logits, axis=-1, keepdims=True)
        pick = jnp.min(jnp.where(logits == m, lane, float(LANES)), axis=-1, keepdims=True)
        picks.append(pick)
        vals.append(m)
        logits = jnp.where(lane == pick, -jnp.inf, logits)
    exps = [jnp.exp(v - vals[0]) for v in vals]
    denom = exps[0]
    for e in exps[1:]:
        denom = denom + e

    onehots = [(lane == p).astype(F32) for p in picks]
    oh_sum = onehots[0]
    for oh in onehots[1:]:
        oh_sum = oh_sum + oh
    row = lax.broadcasted_iota(jnp.int32, (tile, tile), 0)
    col = lax.broadcasted_iota(jnp.int32, (tile, tile), 1)
    before = _dot((row > col).astype(BF16), oh_sum.astype(BF16))
    count = jnp.sum(oh_sum, axis=0, keepdims=True)
    cnt_pad = jnp.floor((count + (BF16_ROWS - 1.0)) * (1.0 / BF16_ROWS)) * BF16_ROWS
    lane8 = lax.broadcasted_iota(jnp.int32, (SUBLANES, LANES), 1)
    run = jnp.broadcast_to(cnt_pad, (SUBLANES, LANES))
    shift = 1
    while shift < n_experts:
        run = run + jnp.where(lane8 >= shift, pltpu.roll(run, shift, 1), 0.0)
        shift *= 2
    pos = before + (run[0:1] - cnt_pad)
    pos_out = jnp.zeros((tile, LANES), F32)
    gate_out = jnp.zeros((tile, LANES), F32)
    for k in range(top_k):
        pos_k = jnp.sum(onehots[k] * pos, axis=-1, keepdims=True)
        pos_out = jnp.where(lane == k, pos_k, pos_out)
        gate_out = jnp.where(lane == k, exps[k] / denom, gate_out)
    pos_ref[...] = pos_out.astype(jnp.int32)
    gate_ref[...] = gate_out
    cnt_ref[0] = cnt_pad.astype(jnp.int32)

    pos_t = pos_out.T
    hb = h.astype(BF16)
    for r0 in range(0, n_sorted, row_chunk):
        slot = (lax.broadcasted_iota(jnp.int32, (row_chunk, tile), 0) + r0).astype(F32)
        sel = jnp.where(slot == pos_t[0:1], 1.0, 0.0)
        for k in range(1, top_k):
            sel = sel + jnp.where(slot == pos_t[k:k + 1], 1.0, 0.0)
        xs_ref[r0:r0 + row_chunk, :] = _dot(sel.astype(BF16), hb).astype(BF16)


def resid_router(x, y, w_out_bf16, mod, g, w_r, b_r, sorted_in, *, tile, rows_per_seq, top_k,
                 block_offset, n_blocks_total, pool=None):
    n, d = x.shape
    n_experts = w_r.shape[1]
    n_sorted = _sorted_rows(tile, top_k, n_experts)
    w_r_pad = jnp.pad(w_r, ((0, 0), (0, LANES - n_experts)))
    b_r_pad = jnp.pad(b_r, (0, LANES - n_experts)).reshape(1, LANES)
    has_w_out = w_out_bf16 is not None
    chained = sorted_in is not None
    row_spec = pl.BlockSpec((tile, d), lambda i: (i, 0))
    lane_spec = pl.BlockSpec((tile, LANES), lambda i: (i, 0))
    full = lambda a: pl.BlockSpec(a.shape, lambda i: (0,) * a.ndim)
    extra_out_specs, extra_out_shape, scratch, pool_static = [], [], [], None
    if pool is None:
        in_specs = [row_spec, pl.BlockSpec((tile, y.shape[1]), lambda i: (i, 0))]
        args = [x, y]
        if has_w_out:
            in_specs.append(full(w_out_bf16))
            args.append(w_out_bf16)
    else:
        g_mix, w_grp, scale, windows, n_keep = pool
        halo = 2 * SUBLANES
        tiles_per_seq = rows_per_seq // tile
        assert max(windows) <= halo <= tile and n_keep <= tile
        in_specs = [row_spec, _mod_spec(mod, 0, d, tile, rows_per_seq), _mod_spec(mod, 1, d, tile, rows_per_seq),
                    pl.BlockSpec((1, d), lambda i: (0, 0)), full(w_grp), pl.BlockSpec((1, d), lambda i: (0, 0))]
        args = [x, mod, mod, g_mix.reshape(1, d), w_grp, scale.reshape(1, d)]
        extra_out_specs = [pl.BlockSpec((1, n_keep, d), lambda i: (i // tiles_per_seq, 0, 0))]
        extra_out_shape = [jax.ShapeDtypeStruct((n // rows_per_seq, n_keep, d), F32)]
        scratch = [pltpu.VMEM((halo + tile, d), F32)]
        pool_static = (windows, halo, tiles_per_seq)
    in_specs += [_mod_spec(mod, 2, d, tile, rows_per_seq), _mod_spec(mod, 3, d, tile, rows_per_seq),
                 _mod_spec(mod, 4, d, tile, rows_per_seq), pl.BlockSpec((1, d), lambda i: (0, 0)),
                 full(w_r_pad), full(b_r_pad)]
    args += [mod, mod, mod, g.reshape(1, d), w_r_pad, b_r_pad]
    aliases = {}
    if chained:
        aliases = {len(args): 1}
        in_specs.append(pl.BlockSpec(memory_space=pl.ANY))
        args.append(sorted_in)
    n_tiles = n // tile
    return pl.pallas_call(
        functools.partial(_resid_router_kernel, top_k=top_k, n_experts=n_experts, has_w_out=has_w_out,
                          chained=chained, row_chunk=ROUTER_ROW_CHUNK, pool=pool_static),
        grid=(n_tiles,),
        in_specs=in_specs,
        out_specs=[row_spec,
                   pl.BlockSpec((n_sorted, d), lambda i: (i + block_offset, 0)),
                   lane_spec, lane_spec,
                   pl.BlockSpec((1, 1, LANES), lambda i: (i, 0, 0))] + extra_out_specs,
        out_shape=[jax.ShapeDtypeStruct((n, d), F32),
                   jax.ShapeDtypeStruct((n_blocks_total * n_sorted, d), BF16),
                   jax.ShapeDtypeStruct((n, LANES), jnp.int32), jax.ShapeDtypeStruct((n, LANES), F32),
                   jax.ShapeDtypeStruct((n_tiles, 1, LANES), jnp.int32)] + extra_out_shape,
        scratch_shapes=scratch,
        input_output_aliases=aliases,
        compiler_params=_params("arbitrary" if pool is not None else "parallel"),
        name="resid_router",
    )(*args)


def _experts_kernel(te_ref, first_ref, rows_ref, base_ref, slo_ref, shi_ref, wslot_ref, nexte_ref, used_ref,
                    sstart_ref, slen_ref, ssrc_ref,
                    xs_hbm, wgu_hbm, bgu_ref, wdn_hbm, bdn_ref, ys_hbm,
                    xbuf, ybuf, wgu_f, wdn_f, wgu_b, wdn_b, in_sem, out_sem, w_sem,
                    *, layer, limit, alpha, col_chunk):
    del ys_hbm
    i = pl.program_id(0)
    used = used_ref[0]
    tm = xbuf.shape[1]

    def copy(src_rows, dst_rows, slot, inbound):
        if inbound:
            return pltpu.make_async_copy(xs_hbm.at[src_rows, :], xbuf.at[slot, dst_rows, :], in_sem.at[slot])
        return pltpu.make_async_copy(ybuf.at[slot, dst_rows, :], xs_hbm.at[src_rows, :], out_sem.at[slot])

    def piece_copies(tile_idx, slot, inbound, wait):
        if wait:
            rows = pl.ds(0, pl.multiple_of(rows_ref[tile_idx], BF16_ROWS))
            copy(rows, rows, slot, inbound).wait()
            return
        base = base_ref[tile_idx]

        def piece(s, c):
            first = sstart_ref[s] - base
            lo = jnp.maximum(first, 0)
            n_rows = pl.multiple_of(jnp.minimum(first + slen_ref[s], tm) - lo, BF16_ROWS)

            @pl.when(n_rows > 0)
            def _():
                src = pl.multiple_of(ssrc_ref[s] + (lo - first), BF16_ROWS)
                copy(pl.ds(src, n_rows), pl.ds(pl.multiple_of(lo, BF16_ROWS), n_rows), slot, inbound).start()
            return c
        lax.fori_loop(slo_ref[tile_idx], shi_ref[tile_idx], piece, 0)

    def weight_copies(e, slot):
        return (pltpu.make_async_copy(wgu_hbm.at[layer, e], wgu_f.at[slot], w_sem.at[slot]),
                pltpu.make_async_copy(wdn_hbm.at[layer, e], wdn_f.at[slot], w_sem.at[slot]))

    @pl.when(i == 0)
    def _():
        xbuf[...] = jnp.zeros_like(xbuf)
        for cp in weight_copies(te_ref[0], 0):
            cp.start()
        piece_copies(0, 0, True, False)

    @pl.when(i < used)
    def _():
        slot = i % 2

        @pl.when(i + 1 < used)
        def _():
            piece_copies(i + 1, 1 - slot, True, False)

        @pl.when(first_ref[i] == 1)
        def _():
            ws = wslot_ref[i]
            for cp in weight_copies(te_ref[i], ws):
                cp.wait()

            @pl.when(nexte_ref[i] >= 0)
            def _():
                for cp in weight_copies(nexte_ref[i], 1 - ws):
                    cp.start()

            @pl.when(rows_ref[i] <= tm - tm // MLP_ROW_STEPS)
            def _():
                wgu_b[...] = wgu_f[ws].astype(BF16)
                wdn_b[...] = wdn_f[ws].astype(BF16)

        piece_copies(i, slot, True, True)

        @pl.when(i >= 2)
        def _():
            piece_copies(i - 2, slot, False, True)

        d_ff = wdn_b.shape[0]
        e = te_ref[i]
        b_gu = bgu_ref[pl.ds(e, 1), :]
        b_dn = bdn_ref[pl.ds(e, 1), :]

        def mlp(n_rows, cast_slot=None):
            x = xbuf[slot, :n_rows, :]
            y = None
            for c0 in range(0, d_ff, col_chunk):
                cs = slice(c0, c0 + col_chunk)
                us = slice(d_ff + c0, d_ff + c0 + col_chunk)
                if cast_slot is None:
                    w_gate, w_up, w_down = wgu_b[:, cs], wgu_b[:, us], wdn_b[cs, :]
                else:
                    w_gate = wgu_f[cast_slot, :, cs].astype(BF16)
                    w_up = wgu_f[cast_slot, :, us].astype(BF16)
                    w_down = wdn_f[cast_slot, cs, :].astype(BF16)
                    wgu_b[:, cs], wgu_b[:, us], wdn_b[cs, :] = w_gate, w_up, w_down
                gate = jnp.minimum(_dot(x, w_gate) + b_gu[:, cs], limit)
                up = jnp.clip(_dot(x, w_up) + b_gu[:, us], -limit, limit)
                act = ((up + 1.0) * (gate * _sigmoid(alpha * gate))).astype(BF16)
                part = _dot(act, w_down)
                y = part if y is None else y + part
            ybuf[slot, :n_rows, :] = (y + b_dn).astype(BF16)

        step = tm // MLP_ROW_STEPS
        for part in range(1, MLP_ROW_STEPS):
            @pl.when((rows_ref[i] > (part - 1) * step) & (rows_ref[i] <= part * step))
            def _():
                mlp(part * step)

        @pl.when((rows_ref[i] > tm - step) & (first_ref[i] != 1))
        def _():
            mlp(tm)

        @pl.when((rows_ref[i] > tm - step) & (first_ref[i] == 1))
        def _():
            mlp(tm, cast_slot=wslot_ref[i])
        piece_copies(i, slot, False, False)

    @pl.when(i == pl.num_programs(0) - 1)
    def _():
        @pl.when(used >= 2)
        def _():
            piece_copies(used - 2, used % 2, False, True)
        piece_copies(used - 1, (used - 1) % 2, False, True)


def moe_experts(sorted_rows, w_gu, b_gu, w_dn, b_dn, tables, *, layer, row_tile, limit, alpha, col_chunk=512):
    n_rows, d = sorted_rows.shape
    d_gu = w_gu.shape[-1]
    d_ff = w_dn.shape[-2]
    n_tiles = tables[0].shape[0]
    vmem = lambda a: pl.BlockSpec(a.shape, lambda i, *_: (0,) * a.ndim)
    any_spec = pl.BlockSpec(memory_space=pl.ANY)
    return pl.pallas_call(
        functools.partial(_experts_kernel, layer=layer, limit=limit, alpha=alpha,
                          col_chunk=min(col_chunk, d_ff)),
        grid_spec=pltpu.PrefetchScalarGridSpec(
            num_scalar_prefetch=len(tables),
            grid=(n_tiles,),
            in_specs=[any_spec, any_spec, vmem(b_gu), any_spec, vmem(b_dn)],
            out_specs=any_spec,
            scratch_shapes=[
                pltpu.VMEM((2, row_tile, d), BF16), pltpu.VMEM((2, row_tile, d), BF16),
                pltpu.VMEM((2, d, d_gu), F32), pltpu.VMEM((2, d_ff, d), F32),
                pltpu.VMEM((d, d_gu), BF16), pltpu.VMEM((d_ff, d), BF16),
                pltpu.SemaphoreType.DMA((2,)), pltpu.SemaphoreType.DMA((2,)), pltpu.SemaphoreType.DMA((2,)),
            ],
        ),
        out_shape=jax.ShapeDtypeStruct((n_rows, d), BF16),
        input_output_aliases={len(tables): 0},
        compiler_params=_params("arbitrary"),
        name="moe_experts",
    )(*tables, sorted_rows, w_gu, b_gu, w_dn, b_dn)


def _expert_tables(cnt, n_sorted, n_tiles, row_tile):
    n_blocks, n_experts = cnt.shape

    def prefix_sum(a):
        n = a.shape[-1]
        upto = jnp.arange(n)[:, None] <= jnp.arange(n)[None, :]
        return jnp.sum(jnp.where(upto, a[..., :, None], 0), axis=-2)

    local_off = prefix_sum(cnt) - cnt
    seg_end = prefix_sum(cnt.T)
    seg_start = seg_end - cnt.T
    seg_src = jnp.arange(n_blocks, dtype=jnp.int32)[None, :] * n_sorted + local_off.T
    total = seg_end[:, -1]
    padded = (total + row_tile - 1) // row_tile * row_tile
    pad_end = prefix_sum(padded)
    pad_start = pad_end - padded
    tiles = jnp.arange(n_tiles, dtype=jnp.int32)
    experts = jnp.arange(n_experts, dtype=jnp.int32)
    n_used = pad_end[-1] // row_tile
    tile_expert = jnp.minimum(jnp.sum(tiles[:, None] * row_tile >= pad_end[None, :], axis=1), n_experts - 1)
    is_expert = tile_expert[:, None] == experts[None, :]

    def of_expert(a):
        if a.ndim == 1:
            return jnp.sum(jnp.where(is_expert, a[None, :], 0), axis=1)
        return jnp.sum(jnp.where(is_expert[:, :, None], a[None, :, :], 0), axis=1)

    live = tiles < n_used
    tile_base = tiles * row_tile - of_expert(pad_start)
    tile_rows = jnp.where(live, jnp.clip(of_expert(total) - tile_base, 0, row_tile), 0)
    tile_first = (live & (tile_base == 0)).astype(jnp.int32)
    seg_lo = jnp.sum(of_expert(seg_end) <= tile_base[:, None], axis=1)
    seg_hi = jnp.sum(of_expert(seg_start) < tile_base[:, None] + row_tile, axis=1)
    seg_lo = tile_expert * n_blocks + jnp.minimum(seg_lo, seg_hi)
    seg_hi = tile_expert * n_blocks + seg_hi
    owns = total > 0
    order = prefix_sum(owns.astype(jnp.int32)) - 1
    experts = jnp.arange(n_experts, dtype=jnp.int32)
    later = (experts[None, :] > experts[:, None]) & owns[None, :]
    next_expert = jnp.min(jnp.where(later, experts[None, :], n_experts), axis=1)
    next_expert = jnp.where(next_expert < n_experts, next_expert, -1)
    i32 = lambda a: a.astype(jnp.int32)
    return (i32(tile_expert), tile_first, i32(tile_rows), i32(tile_base), i32(seg_lo), i32(seg_hi),
            i32(of_expert(order) % 2), i32(of_expert(next_expert)), i32(n_used).reshape(1),
            i32(seg_start.reshape(-1)), i32(cnt.T.reshape(-1)), i32(seg_src.reshape(-1)))


def _combine_kernel(x_ref, pos_ref, gate_ref, gt_ref, gfin_ref, ys_ref, o_ref, *, top_k, final_norm, k_chunk):
    tile, d = x_ref.shape
    n_sorted = ys_ref.shape[0]
    pos = pos_ref[...].astype(F32)
    gates = gate_ref[...]
    acc = None
    for r0 in range(0, n_sorted, k_chunk):
        slot = (lax.broadcasted_iota(jnp.int32, (tile, k_chunk), 1) + r0).astype(F32)
        w = jnp.where(slot == pos[:, 0:1], gates[:, 0:1], 0.0)
        for k in range(1, top_k):
            w = w + jnp.where(slot == pos[:, k:k + 1], gates[:, k:k + 1], 0.0)
        part = _dot(w.astype(BF16), ys_ref[r0:r0 + k_chunk, :])
        acc = part if acc is None else acc + part
    out = x_ref[...] + _mod_rows(gt_ref, tile) * acc
    if final_norm:
        out = _rms_norm(out, gfin_ref[...])
    o_ref[...] = out


def moe_combine(x, ys, pos, gates, mod, g_final, *, tile, rows_per_seq, final_norm, block_offset, n_sorted):
    n, d = x.shape
    return pl.pallas_call(
        functools.partial(_combine_kernel, top_k=TOP_K, final_norm=final_norm, k_chunk=COMBINE_K_CHUNK),
        grid=(n // tile,),
        in_specs=[
            pl.BlockSpec((tile, d), lambda i: (i, 0)),
            pl.BlockSpec((tile, LANES), lambda i: (i, 0)),
            pl.BlockSpec((tile, LANES), lambda i: (i, 0)),
            _mod_spec(mod, 5, d, tile, rows_per_seq),
            pl.BlockSpec((1, d), lambda i: (0, 0)),
            pl.BlockSpec((n_sorted, d), lambda i: (i + block_offset, 0)),
        ],
        out_specs=pl.BlockSpec((tile, d), lambda i: (i, 0)),
        out_shape=jax.ShapeDtypeStruct((n, d), F32),
        compiler_params=_params("parallel"),
        name="moe_combine",
    )(x, pos, gates, mod, g_final.reshape(1, d), ys)


def _pool_groups(h, window_sum, counts, w_ref, scale):
    n_groups = w_ref.shape[0]
    dg = h.shape[-1] // n_groups
    outs = []
    for gi in range(n_groups):
        cols = slice(gi * dg, (gi + 1) * dg)
        pooled = window_sum(gi, cols) / counts[gi] - h[:, cols]
        outs.append(_dot(pooled.astype(BF16), w_ref[gi]))
    return jnp.concatenate(outs, axis=-1) * scale


def _pool_sample_kernel(x_ref, buf_ref, sh_ref, sc_ref, g_ref, w_ref, scale_ref, y_ref, cache_ref,
                        *, windows, start_pos):
    t_len = x_ref.shape[0]
    n_prev = buf_ref.shape[0]
    hs = [_rms_norm(x_ref[t], g_ref[...]) * (1.0 + sc_ref[...]) + sh_ref[...] for t in range(t_len)]

    def ext(r):
        return buf_ref[r] if r < n_prev else hs[r - n_prev]

    for t in range(t_len):
        def window_sum(gi, cols):
            acc = hs[t][:, cols]
            for s in range(1, windows[gi]):
                acc = acc + ext(n_prev + t - s)[:, cols]
            return acc
        counts = [float(min(start_pos + t + 1, w)) for w in windows]
        y_ref[t] = _pool_groups(hs[t], window_sum, counts, w_ref, scale_ref[...])
    for r in range(n_prev):
        cache_ref[r] = ext(t_len + r)


def pool_mixer_sample(x_t, buf_t, mod, g, w_grp_bf16, scale, *, seq_block, windows, start_pos):
    t_len, n_seq, d = x_t.shape
    n_prev = buf_t.shape[0]
    assert start_pos >= n_prev >= max(windows) - 1
    mod_spec = lambda k: pl.BlockSpec((seq_block, d), lambda i: (i, k))
    return pl.pallas_call(
        functools.partial(_pool_sample_kernel, windows=windows, start_pos=start_pos),
        grid=(n_seq // seq_block,),
        in_specs=[
            pl.BlockSpec((t_len, seq_block, d), lambda i: (0, i, 0)),
            pl.BlockSpec((n_prev, seq_block, d), lambda i: (0, i, 0)),
            mod_spec(0), mod_spec(1),
            pl.BlockSpec((1, d), lambda i: (0, 0)),
            pl.BlockSpec(w_grp_bf16.shape, lambda i: (0, 0, 0)),
            pl.BlockSpec((1, d), lambda i: (0, 0)),
        ],
        out_specs=[pl.BlockSpec((t_len, seq_block, d), lambda i: (0, i, 0)),
                   pl.BlockSpec((n_prev, seq_block, d), lambda i: (0, i, 0))],
        out_shape=[jax.ShapeDtypeStruct((t_len, n_seq, d), F32), jax.ShapeDtypeStruct((n_prev, n_seq, d), F32)],
        compiler_params=_params("parallel"),
        name="pool_mixer_sample",
    )(x_t, buf_t, mod, mod, g.reshape(1, d), w_grp_bf16, scale.reshape(1, d))


TOP_K = 4
SWIGLU_LIMIT = 7.0
SWIGLU_ALPHA = 1.702
POOL_WINDOWS = (2, 4, 8, 16)
PAST_LEN = 16384
MOE_TOKEN_TILE = 512
MOE_ROW_TILE = 512
ROUTER_ROW_CHUNK = 256
COMBINE_K_CHUNK = 512
HGRN_TIME_BLOCK = 1024
HGRN_CHUNK = 256
HGRN_SUB = 32
SAMPLE_T_PAD = 8
SAMPLE_SEQ_BLOCK = 8


def kernel(x_prompt, x_sample, c_prompt, c_sample, state_hgrn, cache_pool, g_norm_mix, g_norm_ffn, w_ada, b_ada, w_in_hgrn, lb_logits, g_out_hgrn, w_out_hgrn, w_grp_pool, scale_pool, w_router, b_router, w_gate_up, b_gate_up, w_down, b_down, g_final):
    bp, tp, d = x_prompt.shape
    bs, ts, _ = x_sample.shape
    n_p, n_s = bp * tp, bs * ts
    n_experts = w_router.shape[-1]
    hk = w_out_hgrn.shape[1]
    assert n_s == MOE_TOKEN_TILE and n_p % MOE_TOKEN_TILE == 0
    blocks_p = n_p // MOE_TOKEN_TILE
    n_blocks = blocks_p + 1
    n_sorted = _sorted_rows(MOE_TOKEN_TILE, TOP_K, n_experts)
    n_row_tiles = -(-(n_blocks * n_sorted + n_experts * (MOE_ROW_TILE - BF16_ROWS)) // MOE_ROW_TILE)

    mod = adaln(jnp.concatenate([c_prompt, c_sample], axis=0), w_ada, b_ada)
    mod_p = [mod[l, :bp][:, None, :] for l in range(mod.shape[0])]
    mod_s = [mod[l, bp:] for l in range(mod.shape[0])]

    xp = x_prompt.reshape(n_p, d)
    xs = x_sample.transpose(1, 0, 2).reshape(n_s, d)

    def moe(layer, x_p, y_p, x_s, y_s, w_out, final_norm, pool_p=None):
        route = functools.partial(resid_router, g=g_norm_ffn[layer], w_r=w_router[layer], b_r=b_router[layer],
                                  tile=MOE_TOKEN_TILE, top_k=TOP_K, n_blocks_total=n_blocks)
        x1_p, sorted_rows, pos_p, gate_p, cnt_p, *cache = route(
            x_p, y_p, w_out, mod_p[layer], sorted_in=None, rows_per_seq=tp, block_offset=0, pool=pool_p)
        x1_s, sorted_rows, pos_s, gate_s, cnt_s = route(
            x_s, y_s, w_out, mod_s[layer], sorted_in=sorted_rows, rows_per_seq=None, block_offset=blocks_p)
        cnt8 = jnp.concatenate([cnt_p, cnt_s], axis=0)[:, 0, :n_experts]
        tables = _expert_tables(cnt8, n_sorted, n_row_tiles, MOE_ROW_TILE)
        ys = moe_experts(sorted_rows, w_gate_up, b_gate_up[layer], w_down, b_down[layer], tables,
                         layer=layer, row_tile=MOE_ROW_TILE, limit=SWIGLU_LIMIT, alpha=SWIGLU_ALPHA)
        combine = functools.partial(moe_combine, ys=ys, g_final=g_final, tile=MOE_TOKEN_TILE,
                                    final_norm=final_norm, n_sorted=n_sorted)
        out_p = combine(x1_p, pos=pos_p, gates=gate_p, mod=mod_p[layer], rows_per_seq=tp, block_offset=0)
        out_s = combine(x1_s, pos=pos_s, gates=gate_s, mod=mod_s[layer], rows_per_seq=None,
                        block_offset=blocks_p)
        return (out_p, out_s, *cache)

    w_in = w_in_hgrn[0].astype(BF16)
    proj_s = norm_proj(xs, mod_s[0], g_norm_mix[0], w_in, tile=bs, rows_per_seq=None,
                       n_out_tiles=SAMPLE_T_PAD)
    o_p, state_p = hgrn_recurrence(None, lb_logits, g_out_hgrn[0], None,
                                   layer=0, seq_block=1, time_block=HGRN_TIME_BLOCK, chunk=HGRN_CHUNK,
                                   c_sub=HGRN_SUB, n_valid=HGRN_CHUNK,
                                   norm_proj_of=(x_prompt, mod_p[0], g_norm_mix[0], w_in),
                                   out_dtype=BF16)
    o_s, state_s = hgrn_recurrence(proj_s.reshape(SAMPLE_T_PAD, bs, 4 * hk), lb_logits, g_out_hgrn[0],
                                   state_hgrn[0], layer=0, seq_block=SAMPLE_SEQ_BLOCK,
                                   time_block=SAMPLE_T_PAD, chunk=SAMPLE_T_PAD, c_sub=SAMPLE_T_PAD, n_valid=ts,
                                   time_major=True)
    o_s = o_s[:, :ts].transpose(1, 0, 2).reshape(n_s, hk)
    x_p, x_s = moe(0, xp, o_p.reshape(n_p, hk), xs, o_s, w_out_hgrn[0].astype(BF16), False)

    w_grp = w_grp_pool[0].astype(BF16)
    n_keep = cache_pool.shape[2]
    y_s, cache_s = pool_mixer_sample(x_s.reshape(ts, bs, d), cache_pool[0].transpose(1, 0, 2), mod_s[1],
                                     g_norm_mix[1], w_grp, scale_pool[0],
                                     seq_block=32, windows=POOL_WINDOWS, start_pos=PAST_LEN)
    x_p, x_s, cache_p = moe(1, x_p, None, x_s, y_s.reshape(n_s, d), None, True,
                            pool_p=(g_norm_mix[1], w_grp, scale_pool[0], POOL_WINDOWS, n_keep))

    return (x_p.reshape(bp, tp, d), x_s.reshape(ts, bs, d).transpose(1, 0, 2),
            state_p[None], state_s[None], cache_p[None], cache_s.transpose(1, 0, 2)[None])
```

```python
import functools

import jax
import jax.numpy as jnp
from jax import lax
from jax.experimental import pallas as pl
from jax.experimental.pallas import tpu as pltpu

F32 = jnp.float32
BF16 = jnp.bfloat16

RMS_EPS = 1e-6
LANES = 128
SUBLANES = 8
BF16_ROWS = 16
HEAD_DIM = 128
VMEM_LIMIT = 56 * 1024 * 1024

_dot = functools.partial(jnp.dot, preferred_element_type=F32)


def _params(*semantics):
    return pltpu.CompilerParams(dimension_semantics=semantics, vmem_limit_bytes=VMEM_LIMIT)


def _split_bf16(x, n):
    parts, r = [], x
    for _ in range(n):
        p = r.astype(BF16)
        parts.append(p)
        r = r - p.astype(F32)
    return parts


def _dot_hp(a, b):
    a_hi, a_lo = _split_bf16(a, 2)
    b_hi, b_lo = _split_bf16(b, 2)
    return _dot(a_hi, b_hi) + (_dot(a_hi, b_lo) + _dot(a_lo, b_hi))


def _sigmoid(x):
    return 1.0 / (1.0 + jnp.exp(-x))


def _silu(x):
    return x * _sigmoid(x)


def _rms_norm(x, g):
    ms = jnp.mean(x * x, axis=-1, keepdims=True)
    return x * lax.rsqrt(ms + RMS_EPS) * g


def _adaln_kernel(c_ref, w_ref, b_ref, o_ref):
    o_ref[0] = _dot_hp(_silu(c_ref[...]), w_ref[0]) + b_ref[0]


def adaln(c_all, w_ada, b_ada, *, col_block=1536):
    n_seq, d = c_all.shape
    n_layers, _, d6 = w_ada.shape
    return pl.pallas_call(
        _adaln_kernel,
        grid=(n_layers, d6 // col_block),
        in_specs=[
            pl.BlockSpec((n_seq, d), lambda l, j: (0, 0)),
            pl.BlockSpec((1, d, col_block), lambda l, j: (l, 0, j)),
            pl.BlockSpec((1, 1, col_block), lambda l, j: (l, 0, j)),
        ],
        out_specs=pl.BlockSpec((1, n_seq, col_block), lambda l, j: (l, 0, j)),
        out_shape=jax.ShapeDtypeStruct((n_layers, n_seq, d6), F32),
        compiler_params=_params("parallel", "parallel"),
        name="adaln",
    )(c_all, w_ada, b_ada.reshape(n_layers, 1, d6))


def _mod_spec(mod, k, d, tile, rows_per_seq):
    if rows_per_seq is None:
        return pl.BlockSpec((mod.shape[0], d), lambda i: (0, k))
    tiles_per_seq = rows_per_seq // tile
    return pl.BlockSpec((1, 1, d), lambda i: (i // tiles_per_seq, 0, k))


def _mod_rows(ref, tile):
    m = ref[...].reshape(-1, ref.shape[-1])
    if m.shape[0] not in (1, tile):
        m = jnp.concatenate([m] * (tile // m.shape[0]), axis=0)
    return m


def _norm_proj_kernel(x_ref, sh_ref, sc_ref, g_ref, w_ref, o_ref, *, n_in_tiles):
    tile = x_ref.shape[0]

    @pl.when(pl.program_id(0) < n_in_tiles)
    def _():
        h = _rms_norm(x_ref[...], g_ref[...]) * (1.0 + _mod_rows(sc_ref, tile)) + _mod_rows(sh_ref, tile)
        o_ref[...] = _dot(h.astype(BF16), w_ref[...])

    @pl.when(pl.program_id(0) >= n_in_tiles)
    def _():
        o_ref[...] = jnp.zeros_like(o_ref)


def norm_proj(x, mod, g, w_bf16, *, tile, rows_per_seq, n_out_tiles=None):
    n, d = x.shape
    p = w_bf16.shape[1]
    n_in_tiles = n // tile
    n_out_tiles = n_in_tiles if n_out_tiles is None else n_out_tiles
    return pl.pallas_call(
        functools.partial(_norm_proj_kernel, n_in_tiles=n_in_tiles),
        grid=(n_out_tiles,),
        in_specs=[
            pl.BlockSpec((tile, d), lambda i: (jnp.minimum(i, n_in_tiles - 1), 0)),
            _mod_spec(mod, 0, d, tile, rows_per_seq),
            _mod_spec(mod, 1, d, tile, rows_per_seq),
            pl.BlockSpec((1, d), lambda i: (0, 0)),
            pl.BlockSpec((d, p), lambda i: (0, 0)),
        ],
        out_specs=pl.BlockSpec((tile, p), lambda i: (i, 0)),
        out_shape=jax.ShapeDtypeStruct((n_out_tiles * tile, p), F32),
        compiler_params=_params("parallel"),
        name="hgrn_norm_proj",
    )(x, mod, mod, g.reshape(1, d), w_bf16)


def _cumsum_rows(x, tri):
    hi, mid, lo = _split_bf16(x, 3)
    return _dot(tri, hi) + (_dot(tri, mid) + _dot(tri, lo))


MAX_LOG_DECAY_RANGE = 80.0
MLP_ROW_STEPS = 8


def _hgrn_prep(proj, lb, n_valid):
    c = proj.shape[0]
    hk = proj.shape[1] // 4
    row = lax.broadcasted_iota(jnp.int32, (c, c), 0)
    col = lax.broadcasted_iota(jnp.int32, (c, c), 1)
    zf = proj[:, hk:2 * hk]
    e = jnp.exp(-jnp.abs(zf))
    r = 1.0 / (1.0 + e)
    pos = zf >= 0
    sig_p = jnp.where(pos, 1.0, e) * r
    sig_n = jnp.where(pos, e, 1.0) * r
    logf = jnp.log(lb + (1.0 - lb) * sig_p)
    k = (1.0 - lb) * sig_n
    if n_valid < c:
        live = lax.broadcasted_iota(jnp.int32, (c, 1), 0) < n_valid
        logf = jnp.where(live, logf, 0.0)
        k = jnp.where(live, k, 0.0)
    b = _cumsum_rows(logf, (row >= col).astype(BF16))
    return _silu(proj[:, :hk]), k, proj[:, 2 * hk:3 * hk], _silu(proj[:, 3 * hk:]), b


def _decay_range(b, c_sub):
    c = b.shape[0]
    worst = None
    for i in range(c // c_sub):
        span = b[i * c_sub:i * c_sub + 1, :] - b[(i + 1) * c_sub - 1:(i + 1) * c_sub, :]
        worst = span if worst is None else jnp.maximum(worst, span)
    return jnp.max(worst)


def _head_norm_gate(o, gate, gout):
    outs = []
    for h in range(o.shape[1] // HEAD_DIM):
        hs = slice(h * HEAD_DIM, (h + 1) * HEAD_DIM)
        oh = o[:, hs]
        outs.append(oh * lax.rsqrt(jnp.mean(oh * oh, axis=-1, keepdims=True) + RMS_EPS) * gout * gate[:, hs])
    return jnp.concatenate(outs, axis=-1)


def _hgrn_chunk(prep, gout, st_refs, seq, c_sub):
    q, k, v, gate, b = prep
    c = q.shape[0]
    n_heads = q.shape[1] // HEAD_DIM
    row = lax.broadcasted_iota(jnp.int32, (c, c), 0)
    col = lax.broadcasted_iota(jnp.int32, (c, c), 1)
    causal = row >= col
    n_sub = c // c_sub
    subs = [slice(i * c_sub, (i + 1) * c_sub) for i in range(n_sub)]

    intra, inter = [], []
    for h in range(n_heads):
        hs = slice(h * HEAD_DIM, (h + 1) * HEAD_DIM)
        bh, qh, kh, vh = b[:, hs], q[:, hs], k[:, hs], v[:, hs]
        vb = vh.astype(BF16)
        refs = [bh[i * c_sub + c_sub // 2:i * c_sub + c_sub // 2 + 1, :] for i in range(n_sub)]
        k_own = [kh[rs] * jnp.exp(jnp.minimum(ref - bh[rs], MAX_LOG_DECAY_RANGE))
                 for rs, ref in zip(subs, refs)]
        a_rows = []
        for i in range(n_sub):
            q_hat = (qh[subs[i]] * jnp.exp(bh[subs[i]] - refs[i])).astype(BF16)
            parts = [k_own[j] * jnp.exp(refs[i] - refs[j]) for j in range(i)] + [k_own[i]]
            parts += [jnp.zeros((c_sub, HEAD_DIM), F32)] * (n_sub - 1 - i)
            k_hat = (jnp.concatenate(parts, axis=0) if n_sub > 1 else parts[0]).astype(BF16)
            a_rows.append(lax.dot_general(q_hat, k_hat, (((1,), (1,)), ((), ())),
                                          preferred_element_type=F32))
        att = jnp.where(causal, jnp.concatenate(a_rows, axis=0) if len(a_rows) > 1 else a_rows[0], 0.0)
        intra.append(_dot(att.astype(BF16), vb))
        st = st_refs[seq, h]
        inter.append(lax.dot_general((qh * jnp.exp(bh)).astype(BF16), st.astype(BF16),
                                     (((1,), (1,)), ((), ())), preferred_element_type=F32))
        b_last = bh[c - 1:c, :]
        k_dec = (kh * jnp.exp(b_last - bh)).astype(BF16)
        st_refs[seq, h] = st * jnp.exp(b_last) + lax.dot_general(
            vb, k_dec, (((0,), (0,)), ((), ())), preferred_element_type=F32)
    inter = jnp.concatenate(inter, axis=-1)
    return _head_norm_gate(jnp.concatenate(intra, axis=-1) + inter, gate, gout), inter


def _hgrn_chunk_exact(prep, inter, gout, q_ref, b_ref, oi_ref):
    q, k, v, gate, b = prep
    c = q.shape[0]
    n_heads = q.shape[1] // HEAD_DIM
    q_ref[...] = q
    b_ref[...] = b
    key_row = lax.broadcasted_iota(jnp.int32, (c, 1), 0)

    def row_group(g, carry):
        rows = pl.ds(pl.multiple_of(g * SUBLANES, SUBLANES), SUBLANES)
        for h in range(n_heads):
            hs = slice(h * HEAD_DIM, (h + 1) * HEAD_DIM)
            q_g, b_g = q_ref[rows, hs], b_ref[rows, hs]
            o_rows = []
            for r in range(SUBLANES):
                decay = jnp.exp(jnp.minimum(b_g[r:r + 1] - b[:, hs], 0.0))
                score = jnp.sum(decay * k[:, hs] * q_g[r:r + 1], axis=-1, keepdims=True)
                score = jnp.where(key_row <= g * SUBLANES + r, score, 0.0)
                o_rows.append(jnp.sum(score * v[:, hs], axis=0, keepdims=True))
            oi_ref[rows, hs] = jnp.concatenate(o_rows, axis=0)
        return carry
    lax.fori_loop(0, c // SUBLANES, row_group, 0)
    return _head_norm_gate(oi_ref[...] + inter, gate, gout)


def _lower_bound(lb_logits, layer):
    e = jnp.exp(lb_logits - jnp.max(lb_logits, axis=0, keepdims=True))
    return jnp.sum(e[:layer + 1], axis=0, keepdims=True) / jnp.sum(e, axis=0, keepdims=True)


def _hgrn_rec_kernel(*refs, chunk, c_sub, n_valid, has_state, layer, fused_proj, time_major):
    refs = list(refs)
    if fused_proj:
        x_ref, sh_ref, sc_ref, g_ref, w_ref = refs[:5]
        del refs[:5]
        proj_ref = None
    else:
        proj_ref = refs.pop(0)
    lb_ref, gout_ref = refs[:2]
    del refs[:2]
    s0_ref = refs.pop(0) if has_state else None
    o_ref, sout_ref, st_ref, inter_ref, q_ref, b_ref, oi_ref = refs[:7]
    proj_buf = refs[7] if fused_proj else None
    bb, tb, _ = o_ref.shape
    n_heads = st_ref.shape[1]
    n_chunks = tb // chunk
    j = pl.program_id(1)

    @pl.when(j == 0)
    def _():
        if has_state:
            for s in range(bb):
                for h in range(n_heads):
                    st_ref[s, h] = s0_ref[s, h].T
        else:
            st_ref[...] = jnp.zeros_like(st_ref)

    lb = _lower_bound(lb_ref[...], layer)
    gout = gout_ref[...]

    def project(s, ci):
        rows = pl.ds(pl.multiple_of(ci * chunk, chunk), chunk)
        h = _rms_norm(x_ref[s, rows, :], g_ref[...]) * (1.0 + sc_ref[s]) + sh_ref[s]
        return _dot(h.astype(BF16), w_ref[...])

    if fused_proj:
        for s in range(bb):
            proj_buf[s, 0] = project(s, 0)

    def chunk_body(ci, carry):
        rows = pl.ds(pl.multiple_of(ci * chunk, chunk), chunk)

        def load_proj(s):
            if fused_proj:
                return proj_buf[s, ci % 2]
            return proj_ref[rows, s, :] if time_major else proj_ref[s, rows, :]
        span = None
        preps = [_hgrn_prep(load_proj(s), lb, n_valid) for s in range(bb)]
        for s, prep in enumerate(preps):
            o, inter_ref[s] = _hgrn_chunk(prep, gout, st_ref, s, c_sub)
            o_ref[s, rows, :] = o.astype(o_ref.dtype)
            worst = _decay_range(prep[4], c_sub)
            span = worst if span is None else jnp.maximum(span, worst)
            if fused_proj:
                proj_buf[s, (ci + 1) % 2] = project(s, jnp.minimum(ci + 1, n_chunks - 1))

        @pl.when(jnp.logical_not(span <= MAX_LOG_DECAY_RANGE))
        def _():
            for s in range(bb):
                prep = _hgrn_prep(load_proj(s), lb, n_valid)
                o_ref[s, rows, :] = _hgrn_chunk_exact(prep, inter_ref[s], gout, q_ref, b_ref,
                                                      oi_ref).astype(o_ref.dtype)
        return carry
    lax.fori_loop(0, n_chunks, chunk_body, 0)

    @pl.when(j == pl.num_programs(1) - 1)
    def _():
        for s in range(bb):
            for h in range(n_heads):
                sout_ref[s, h] = st_ref[s, h].T


def hgrn_recurrence(proj, lb_logits, g_out, s0, *, layer, seq_block, time_block, chunk, c_sub, n_valid,
                    norm_proj_of=None, out_dtype=F32, time_major=False):
    fused = norm_proj_of is not None
    if fused:
        x, mod, g, w_in = norm_proj_of
        bsz, t, d = x.shape
        p = w_in.shape[1]
        mod_spec = lambda k: pl.BlockSpec((seq_block, 1, d), lambda i, j: (i, 0, k))
        in_specs = [pl.BlockSpec((seq_block, time_block, d), lambda i, j: (i, j, 0)), mod_spec(0), mod_spec(1),
                    pl.BlockSpec((1, d), lambda i, j: (0, 0)), pl.BlockSpec((d, p), lambda i, j: (0, 0))]
        args = [x, mod, mod, g.reshape(1, d), w_in]
    elif time_major:
        t, bsz, p = proj.shape
        in_specs = [pl.BlockSpec((time_block, seq_block, p), lambda i, j: (j, i, 0))]
        args = [proj]
    else:
        bsz, t, p = proj.shape
        in_specs = [pl.BlockSpec((seq_block, time_block, p), lambda i, j: (i, j, 0))]
        args = [proj]
    hk = p // 4
    n_heads = hk // HEAD_DIM
    has_state = s0 is not None
    st_shape = (seq_block, n_heads, HEAD_DIM, HEAD_DIM)
    st_spec = pl.BlockSpec(st_shape, lambda i, j: (i, 0, 0, 0))
    in_specs += [pl.BlockSpec(lb_logits.shape, lambda i, j: (0, 0)),
                 pl.BlockSpec((1, HEAD_DIM), lambda i, j: (0, 0))]
    args += [lb_logits, g_out.reshape(1, HEAD_DIM)]
    if has_state:
        in_specs.append(st_spec)
        args.append(s0)
    scratch = [pltpu.VMEM(st_shape, F32), pltpu.VMEM((seq_block, chunk, hk), F32)]
    scratch += [pltpu.VMEM((chunk, hk), F32)] * 3
    if fused:
        scratch.append(pltpu.VMEM((seq_block, 2, chunk, p), F32))
    return pl.pallas_call(
        functools.partial(_hgrn_rec_kernel, chunk=chunk, c_sub=c_sub, n_valid=n_valid,
                          has_state=has_state, layer=layer, fused_proj=fused, time_major=time_major),
        grid=(bsz // seq_block, t // time_block),
        in_specs=in_specs,
        out_specs=[pl.BlockSpec((seq_block, time_block, hk), lambda i, j: (i, j, 0)), st_spec],
        out_shape=[jax.ShapeDtypeStruct((bsz, t, hk), out_dtype),
                   jax.ShapeDtypeStruct((bsz, n_heads, HEAD_DIM, HEAD_DIM), F32)],
        scratch_shapes=scratch,
        compiler_params=_params("parallel", "arbitrary"),
        name="hgrn_recurrence",
    )(*args)


def _sorted_rows(tile, top_k, n_experts):
    return tile * top_k + n_experts * BF16_ROWS


def _resid_router_kernel(*refs, top_k, n_experts, has_w_out, chained, row_chunk, pool):
    refs = list(refs)
    x_ref = refs.pop(0)
    if pool is None:
        y_ref = refs.pop(0)
        wo_ref = refs.pop(0) if has_w_out else None
    else:
        msh_ref, msc_ref, mg_ref, wgrp_ref, mscale_ref = refs[:5]
        del refs[:5]
    gt_ref, sh_ref, sc_ref, g_ref, wr_ref, br_ref = refs[:6]
    del refs[:6]
    if chained:
        refs.pop(0)
    x1_ref, xs_ref, pos_ref, gate_ref, cnt_ref = refs[:5]
    tile, d = x_ref.shape
    n_sorted = xs_ref.shape[0]

    if pool is None:
        y = y_ref[...]
        if has_w_out:
            y = _dot(y.astype(BF16), wo_ref[...])
    else:
        windows, halo, tiles_per_seq = pool
        cache_ref, ext_ref = refs[5:7]
        part = pl.program_id(0) % tiles_per_seq

        @pl.when(part == 0)
        def _():
            ext_ref[0:halo, :] = jnp.zeros((halo, d), F32)

        hm = _rms_norm(x_ref[...], mg_ref[...]) * (1.0 + _mod_rows(msc_ref, tile)) + _mod_rows(msh_ref, tile)
        ext_ref[halo:halo + tile, :] = hm
        token = part * tile + lax.broadcasted_iota(jnp.int32, (tile, 1), 0)

        def window_sum(gi, cols):
            acc = hm[:, cols]
            for s in range(1, windows[gi]):
                acc = acc + ext_ref[halo - s:halo - s + tile, cols]
            return acc
        counts = [jnp.minimum(token + 1, w).astype(F32) for w in windows]
        y = _pool_groups(hm, window_sum, counts, wgrp_ref, mscale_ref[...])

        n_keep = cache_ref.shape[1]
        @pl.when(part == tiles_per_seq - 1)
        def _():
            cache_ref[0] = ext_ref[halo + tile - n_keep:halo + tile, :]
        ext_ref[0:halo, :] = ext_ref[tile:tile + halo, :]
    x1 = x_ref[...] + _mod_rows(gt_ref, tile) * y
    x1_ref[...] = x1
    h = _rms_norm(x1, g_ref[...]) * (1.0 + _mod_rows(sc_ref, tile)) + _mod_rows(sh_ref, tile)

    lane = lax.broadcasted_iota(jnp.int32, (tile, LANES), 1).astype(F32)
    logits = jnp.where(lane < n_experts, _dot_hp(h, wr_ref[...]) + br_ref[...], -jnp.inf)
    picks, vals = [], []
    for _ in range(top_k):
        m = jnp.max(logits, axis=-1, keepdims=True)
        pick = jnp.min(jnp.where(logits == m, lane, float(LANES)), axis=-1, keepdims=True)
        picks.append(pick)
        vals.append(m)
        logits = jnp.where(lane == pick, -jnp.inf, logits)
    exps = [jnp.exp(v - vals[0]) for v in vals]
    denom = exps[0]
    for e in exps[1:]:
        denom = denom + e

    onehots = [(lane == p).astype(F32) for p in picks]
    oh_sum = onehots[0]
    for oh in onehots[1:]:
        oh_sum = oh_sum + oh
    row = lax.broadcasted_iota(jnp.int32, (tile, tile), 0)
    col = lax.broadcasted_iota(jnp.int32, (tile, tile), 1)
    before = _dot((row > col).astype(BF16), oh_sum.astype(BF16))
    count = jnp.sum(oh_sum, axis=0, keepdims=True)
    cnt_pad = jnp.floor((count + (BF16_ROWS - 1.0)) * (1.0 / BF16_ROWS)) * BF16_ROWS
    lane8 = lax.broadcasted_iota(jnp.int32, (SUBLANES, LANES), 1)
    run = jnp.broadcast_to(cnt_pad, (SUBLANES, LANES))
    shift = 1
    while shift < n_experts:
        run = run + jnp.where(lane8 >= shift, pltpu.roll(run, shift, 1), 0.0)
        shift *= 2
    pos = before + (run[0:1] - cnt_pad)
    pos_out = jnp.zeros((tile, LANES), F32)
    gate_out = jnp.zeros((tile, LANES), F32)
    for k in range(top_k):
        pos_k = jnp.sum(onehots[k] * pos, axis=-1, keepdims=True)
        pos_out = jnp.where(lane == k, pos_k, pos_out)
        gate_out = jnp.where(lane == k, exps[k] / denom, gate_out)
    pos_ref[...] = pos_out.astype(jnp.int32)
    gate_ref[...] = gate_out
    cnt_ref[0] = cnt_pad.astype(jnp.int32)

    pos_t = pos_out.T
    hb = h.astype(BF16)
    for r0 in range(0, n_sorted, row_chunk):
        slot = (lax.broadcasted_iota(jnp.int32, (row_chunk, tile), 0) + r0).astype(F32)
        sel = jnp.where(slot == pos_t[0:1], 1.0, 0.0)
        for k in range(1, top_k):
            sel = sel + jnp.where(slot == pos_t[k:k + 1], 1.0, 0.0)
        xs_ref[r0:r0 + row_chunk, :] = _dot(sel.astype(BF16), hb).astype(BF16)


def resid_router(x, y, w_out_bf16, mod, g, w_r, b_r, sorted_in, *, tile, rows_per_seq, top_k,
                 block_offset, n_blocks_total, pool=None):
    n, d = x.shape
    n_experts = w_r.shape[1]
    n_sorted = _sorted_rows(tile, top_k, n_experts)
    w_r_pad = jnp.pad(w_r, ((0, 0), (0, LANES - n_experts)))
    b_r_pad = jnp.pad(b_r, (0, LANES - n_experts)).reshape(1, LANES)
    has_w_out = w_out_bf16 is not None
    chained = sorted_in is not None
    row_spec = pl.BlockSpec((tile, d), lambda i: (i, 0))
    lane_spec = pl.BlockSpec((tile, LANES), lambda i: (i, 0))
    full = lambda a: pl.BlockSpec(a.shape, lambda i: (0,) * a.ndim)
    extra_out_specs, extra_out_shape, scratch, pool_static = [], [], [], None
    if pool is None:
        in_specs = [row_spec, pl.BlockSpec((tile, y.shape[1]), lambda i: (i, 0))]
        args = [x, y]
        if has_w_out:
            in_specs.append(full(w_out_bf16))
            args.append(w_out_bf16)
    else:
        g_mix, w_grp, scale, windows, n_keep = pool
        halo = 2 * SUBLANES
        tiles_per_seq = rows_per_seq // tile
        assert max(windows) <= halo <= tile and n_keep <= tile
        in_specs = [row_spec, _mod_spec(mod, 0, d, tile, rows_per_seq), _mod_spec(mod, 1, d, tile, rows_per_seq),
                    pl.BlockSpec((1, d), lambda i: (0, 0)), full(w_grp), pl.BlockSpec((1, d), lambda i: (0, 0))]
        args = [x, mod, mod, g_mix.reshape(1, d), w_grp, scale.reshape(1, d)]
        extra_out_specs = [pl.BlockSpec((1, n_keep, d), lambda i: (i // tiles_per_seq, 0, 0))]
        extra_out_shape = [jax.ShapeDtypeStruct((n // rows_per_seq, n_keep, d), F32)]
        scratch = [pltpu.VMEM((halo + tile, d), F32)]
        pool_static = (windows, halo, tiles_per_seq)
    in_specs += [_mod_spec(mod, 2, d, tile, rows_per_seq), _mod_spec(mod, 3, d, tile, rows_per_seq),
                 _mod_spec(mod, 4, d, tile, rows_per_seq), pl.BlockSpec((1, d), lambda i: (0, 0)),
                 full(w_r_pad), full(b_r_pad)]
    args += [mod, mod, mod, g.reshape(1, d), w_r_pad, b_r_pad]
    aliases = {}
    if chained:
        aliases = {len(args): 1}
        in_specs.append(pl.BlockSpec(memory_space=pl.ANY))
        args.append(sorted_in)
    n_tiles = n // tile
    return pl.pallas_call(
        functools.partial(_resid_router_kernel, top_k=top_k, n_experts=n_experts, has_w_out=has_w_out,
                          chained=chained, row_chunk=ROUTER_ROW_CHUNK, pool=pool_static),
        grid=(n_tiles,),
        in_specs=in_specs,
        out_specs=[row_spec,
                   pl.BlockSpec((n_sorted, d), lambda i: (i + block_offset, 0)),
                   lane_spec, lane_spec,
                   pl.BlockSpec((1, 1, LANES), lambda i: (i, 0, 0))] + extra_out_specs,
        out_shape=[jax.ShapeDtypeStruct((n, d), F32),
                   jax.ShapeDtypeStruct((n_blocks_total * n_sorted, d), BF16),
                   jax.ShapeDtypeStruct((n, LANES), jnp.int32), jax.ShapeDtypeStruct((n, LANES), F32),
                   jax.ShapeDtypeStruct((n_tiles, 1, LANES), jnp.int32)] + extra_out_shape,
        scratch_shapes=scratch,
        input_output_aliases=aliases,
        compiler_params=_params("arbitrary" if pool is not None else "parallel"),
        name="resid_router",
    )(*args)


def _experts_kernel(te_ref, first_ref, rows_ref, base_ref, slo_ref, shi_ref, wslot_ref, nexte_ref, used_ref,
                    sstart_ref, slen_ref, ssrc_ref,
                    xs_hbm, wgu_hbm, bgu_ref, wdn_hbm, bdn_ref, ys_hbm,
                    xbuf, ybuf, wgu_f, wdn_f, wgu_b, wdn_b, in_sem, out_sem, w_sem,
                    *, layer, limit, alpha, col_chunk):
    del ys_hbm
    i = pl.program_id(0)
    used = used_ref[0]
    tm = xbuf.shape[1]

    def copy(src_rows, dst_rows, slot, inbound):
        if inbound:
            return pltpu.make_async_copy(xs_hbm.at[src_rows, :], xbuf.at[slot, dst_rows, :], in_sem.at[slot])
        return pltpu.make_async_copy(ybuf.at[slot, dst_rows, :], xs_hbm.at[src_rows, :], out_sem.at[slot])

    def piece_copies(tile_idx, slot, inbound, wait):
        if wait:
            rows = pl.ds(0, pl.multiple_of(rows_ref[tile_idx], BF16_ROWS))
            copy(rows, rows, slot, inbound).wait()
            return
        base = base_ref[tile_idx]

        def piece(s, c):
            first = sstart_ref[s] - base
            lo = jnp.maximum(first, 0)
            n_rows = pl.multiple_of(jnp.minimum(first + slen_ref[s], tm) - lo, BF16_ROWS)

            @pl.when(n_rows > 0)
            def _():
                src = pl.multiple_of(ssrc_ref[s] + (lo - first), BF16_ROWS)
                copy(pl.ds(src, n_rows), pl.ds(pl.multiple_of(lo, BF16_ROWS), n_rows), slot, inbound).start()
            return c
        lax.fori_loop(slo_ref[tile_idx], shi_ref[tile_idx], piece, 0)

    def weight_copies(e, slot):
        return (pltpu.make_async_copy(wgu_hbm.at[layer, e], wgu_f.at[slot], w_sem.at[slot]),
                pltpu.make_async_copy(wdn_hbm.at[layer, e], wdn_f.at[slot], w_sem.at[slot]))

    @pl.when(i == 0)
    def _():
        xbuf[...] = jnp.zeros_like(xbuf)
        for cp in weight_copies(te_ref[0], 0):
            cp.start()
        piece_copies(0, 0, True, False)

    @pl.when(i < used)
    def _():
        slot = i % 2

        @pl.when(i + 1 < used)
        def _():
            piece_copies(i + 1, 1 - slot, True, False)

        @pl.when(first_ref[i] == 1)
        def _():
            ws = wslot_ref[i]
            for cp in weight_copies(te_ref[i], ws):
                cp.wait()

            @pl.when(nexte_ref[i] >= 0)
            def _():
                for cp in weight_copies(nexte_ref[i], 1 - ws):
                    cp.start()

            @pl.when(rows_ref[i] <= tm - tm // MLP_ROW_STEPS)
            def _():
                wgu_b[...] = wgu_f[ws].astype(BF16)
                wdn_b[...] = wdn_f[ws].astype(BF16)

        piece_copies(i, slot, True, True)

        @pl.when(i >= 2)
        def _():
            piece_copies(i - 2, slot, False, True)

        d_ff = wdn_b.shape[0]
        e = te_ref[i]
        b_gu = bgu_ref[pl.ds(e, 1), :]
        b_dn = bdn_ref[pl.ds(e, 1), :]

        def mlp(n_rows, cast_slot=None):
            x = xbuf[slot, :n_rows, :]
            y = None
            for c0 in range(0, d_ff, col_chunk):
                cs = slice(c0, c0 + col_chunk)
                us = slice(d_ff + c0, d_ff + c0 + col_chunk)
                if cast_slot is None:
                    w_gate, w_up, w_down = wgu_b[:, cs], wgu_b[:, us], wdn_b[cs, :]
                else:
                    w_gate = wgu_f[cast_slot, :, cs].astype(BF16)
                    w_up = wgu_f[cast_slot, :, us].astype(BF16)
                    w_down = wdn_f[cast_slot, cs, :].astype(BF16)
                    wgu_b[:, cs], wgu_b[:, us], wdn_b[cs, :] = w_gate, w_up, w_down
                gate = jnp.minimum(_dot(x, w_gate) + b_gu[:, cs], limit)
                up = jnp.clip(_dot(x, w_up) + b_gu[:, us], -limit, limit)
                act = ((up + 1.0) * (gate * _sigmoid(alpha * gate))).astype(BF16)
                part = _dot(act, w_down)
                y = part if y is None else y + part
            ybuf[slot, :n_rows, :] = (y + b_dn).astype(BF16)

        step = tm // MLP_ROW_STEPS
        for part in range(1, MLP_ROW_STEPS):
            @pl.when((rows_ref[i] > (part - 1) * step) & (rows_ref[i] <= part * step))
            def _():
                mlp(part * step)

        @pl.when((rows_ref[i] > tm - step) & (first_ref[i] != 1))
        def _():
            mlp(tm)

        @pl.when((rows_ref[i] > tm - step) & (first_ref[i] == 1))
        def _():
            mlp(tm, cast_slot=wslot_ref[i])
        piece_copies(i, slot, False, False)

    @pl.when(i == pl.num_programs(0) - 1)
    def _():
        @pl.when(used >= 2)
        def _():
            piece_copies(used - 2, used % 2, False, True)
        piece_copies(used - 1, (used - 1) % 2, False, True)


def moe_experts(sorted_rows, w_gu, b_gu, w_dn, b_dn, tables, *, layer, row_tile, limit, alpha, col_chunk=512):
    n_rows, d = sorted_rows.shape
    d_gu = w_gu.shape[-1]
    d_ff = w_dn.shape[-2]
    n_tiles = tables[0].shape[0]
    vmem = lambda a: pl.BlockSpec(a.shape, lambda i, *_: (0,) * a.ndim)
    any_spec = pl.BlockSpec(memory_space=pl.ANY)
    return pl.pallas_call(
        functools.partial(_experts_kernel, layer=layer, limit=limit, alpha=alpha,
                          col_chunk=min(col_chunk, d_ff)),
        grid_spec=pltpu.PrefetchScalarGridSpec(
            num_scalar_prefetch=len(tables),
            grid=(n_tiles,),
            in_specs=[any_spec, any_spec, vmem(b_gu), any_spec, vmem(b_dn)],
            out_specs=any_spec,
            scratch_shapes=[
                pltpu.VMEM((2, row_tile, d), BF16), pltpu.VMEM((2, row_tile, d), BF16),
                pltpu.VMEM((2, d, d_gu), F32), pltpu.VMEM((2, d_ff, d), F32),
                pltpu.VMEM((d, d_gu), BF16), pltpu.VMEM((d_ff, d), BF16),
                pltpu.SemaphoreType.DMA((2,)), pltpu.SemaphoreType.DMA((2,)), pltpu.SemaphoreType.DMA((2,)),
            ],
        ),
        out_shape=jax.ShapeDtypeStruct((n_rows, d), BF16),
        input_output_aliases={len(tables): 0},
        compiler_params=_params("arbitrary"),
        name="moe_experts",
    )(*tables, sorted_rows, w_gu, b_gu, w_dn, b_dn)


def _expert_tables(cnt, n_sorted, n_tiles, row_tile):
    n_blocks, n_experts = cnt.shape

    def prefix_sum(a):
        n = a.shape[-1]
        upto = jnp.arange(n)[:, None] <= jnp.arange(n)[None, :]
        return jnp.sum(jnp.where(upto, a[..., :, None], 0), axis=-2)

    local_off = prefix_sum(cnt) - cnt
    seg_end = prefix_sum(cnt.T)
    seg_start = seg_end - cnt.T
    seg_src = jnp.arange(n_blocks, dtype=jnp.int32)[None, :] * n_sorted + local_off.T
    total = seg_end[:, -1]
    padded = (total + row_tile - 1) // row_tile * row_tile
    pad_end = prefix_sum(padded)
    pad_start = pad_end - padded
    tiles = jnp.arange(n_tiles, dtype=jnp.int32)
    experts = jnp.arange(n_experts, dtype=jnp.int32)
    n_used = pad_end[-1] // row_tile
    tile_expert = jnp.minimum(jnp.sum(tiles[:, None] * row_tile >= pad_end[None, :], axis=1), n_experts - 1)
    is_expert = tile_expert[:, None] == experts[None, :]

    def of_expert(a):
        if a.ndim == 1:
            return jnp.sum(jnp.where(is_expert, a[None, :], 0), axis=1)
        return jnp.sum(jnp.where(is_expert[:, :, None], a[None, :, :], 0), axis=1)

    live = tiles < n_used
    tile_base = tiles * row_tile - of_expert(pad_start)
    tile_rows = jnp.where(live, jnp.clip(of_expert(total) - tile_base, 0, row_tile), 0)
    tile_first = (live & (tile_base == 0)).astype(jnp.int32)
    seg_lo = jnp.sum(of_expert(seg_end) <= tile_base[:, None], axis=1)
    seg_hi = jnp.sum(of_expert(seg_start) < tile_base[:, None] + row_tile, axis=1)
    seg_lo = tile_expert * n_blocks + jnp.minimum(seg_lo, seg_hi)
    seg_hi = tile_expert * n_blocks + seg_hi
    owns = total > 0
    order = prefix_sum(owns.astype(jnp.int32)) - 1
    experts = jnp.arange(n_experts, dtype=jnp.int32)
    later = (experts[None, :] > experts[:, None]) & owns[None, :]
    next_expert = jnp.min(jnp.where(later, experts[None, :], n_experts), axis=1)
    next_expert = jnp.where(next_expert < n_experts, next_expert, -1)
    i32 = lambda a: a.astype(jnp.int32)
    return (i32(tile_expert), tile_first, i32(tile_rows), i32(tile_base), i32(seg_lo), i32(seg_hi),
            i32(of_expert(order) % 2), i32(of_expert(next_expert)), i32(n_used).reshape(1),
            i32(seg_start.reshape(-1)), i32(cnt.T.reshape(-1)), i32(seg_src.reshape(-1)))


def _combine_kernel(x_ref, pos_ref, gate_ref, gt_ref, gfin_ref, ys_ref, o_ref, *, top_k, final_norm, k_chunk):
    tile, d = x_ref.shape
    n_sorted = ys_ref.shape[0]
    pos = pos_ref[...].astype(F32)
    gates = gate_ref[...]
    acc = None
    for r0 in range(0, n_sorted, k_chunk):
        slot = (lax.broadcasted_iota(jnp.int32, (tile, k_chunk), 1) + r0).astype(F32)
        w = jnp.where(slot == pos[:, 0:1], gates[:, 0:1], 0.0)
        for k in range(1, top_k):
            w = w + jnp.where(slot == pos[:, k:k + 1], gates[:, k:k + 1], 0.0)
        part = _dot(w.astype(BF16), ys_ref[r0:r0 + k_chunk, :])
        acc = part if acc is None else acc + part
    out = x_ref[...] + _mod_rows(gt_ref, tile) * acc
    if final_norm:
        out = _rms_norm(out, gfin_ref[...])
    o_ref[...] = out


def moe_combine(x, ys, pos, gates, mod, g_final, *, tile, rows_per_seq, final_norm, block_offset, n_sorted):
    n, d = x.shape
    return pl.pallas_call(
        functools.partial(_combine_kernel, top_k=TOP_K, final_norm=final_norm, k_chunk=COMBINE_K_CHUNK),
        grid=(n // tile,),
        in_specs=[
            pl.BlockSpec((tile, d), lambda i: (i, 0)),
            pl.BlockSpec((tile, LANES), lambda i: (i, 0)),
            pl.BlockSpec((tile, LANES), lambda i: (i, 0)),
            _mod_spec(mod, 5, d, tile, rows_per_seq),
            pl.BlockSpec((1, d), lambda i: (0, 0)),
            pl.BlockSpec((n_sorted, d), lambda i: (i + block_offset, 0)),
        ],
        out_specs=pl.BlockSpec((tile, d), lambda i: (i, 0)),
        out_shape=jax.ShapeDtypeStruct((n, d), F32),
        compiler_params=_params("parallel"),
        name="moe_combine",
    )(x, pos, gates, mod, g_final.reshape(1, d), ys)


def _pool_groups(h, window_sum, counts, w_ref, scale):
    n_groups = w_ref.shape[0]
    dg = h.shape[-1] // n_groups
    outs = []
    for gi in range(n_groups):
        cols = slice(gi * dg, (gi + 1) * dg)
        pooled = window_sum(gi, cols) / counts[gi] - h[:, cols]
        outs.append(_dot(pooled.astype(BF16), w_ref[gi]))
    return jnp.concatenate(outs, axis=-1) * scale


def _pool_sample_kernel(x_ref, buf_ref, sh_ref, sc_ref, g_ref, w_ref, scale_ref, y_ref, cache_ref,
                        *, windows, start_pos):
    t_len = x_ref.shape[0]
    n_prev = buf_ref.shape[0]
    hs = [_rms_norm(x_ref[t], g_ref[...]) * (1.0 + sc_ref[...]) + sh_ref[...] for t in range(t_len)]

    def ext(r):
        return buf_ref[r] if r < n_prev else hs[r - n_prev]

    for t in range(t_len):
        def window_sum(gi, cols):
            acc = hs[t][:, cols]
            for s in range(1, windows[gi]):
                acc = acc + ext(n_prev + t - s)[:, cols]
            return acc
        counts = [float(min(start_pos + t + 1, w)) for w in windows]
        y_ref[t] = _pool_groups(hs[t], window_sum, counts, w_ref, scale_ref[...])
    for r in range(n_prev):
        cache_ref[r] = ext(t_len + r)


def pool_mixer_sample(x_t, buf_t, mod, g, w_grp_bf16, scale, *, seq_block, windows, start_pos):
    t_len, n_seq, d = x_t.shape
    n_prev = buf_t.shape[0]
    assert start_pos >= n_prev >= max(windows) - 1
    mod_spec = lambda k: pl.BlockSpec((seq_block, d), lambda i: (i, k))
    return pl.pallas_call(
        functools.partial(_pool_sample_kernel, windows=windows, start_pos=start_pos),
        grid=(n_seq // seq_block,),
        in_specs=[
            pl.BlockSpec((t_len, seq_block, d), lambda i: (0, i, 0)),
            pl.BlockSpec((n_prev, seq_block, d), lambda i: (0, i, 0)),
            mod_spec(0), mod_spec(1),
            pl.BlockSpec((1, d), lambda i: (0, 0)),
            pl.BlockSpec(w_grp_bf16.shape, lambda i: (0, 0, 0)),
            pl.BlockSpec((1, d), lambda i: (0, 0)),
        ],
        out_specs=[pl.BlockSpec((t_len, seq_block, d), lambda i: (0, i, 0)),
                   pl.BlockSpec((n_prev, seq_block, d), lambda i: (0, i, 0))],
        out_shape=[jax.ShapeDtypeStruct((t_len, n_seq, d), F32), jax.ShapeDtypeStruct((n_prev, n_seq, d), F32)],
        compiler_params=_params("parallel"),
        name="pool_mixer_sample",
    )(x_t, buf_t, mod, mod, g.reshape(1, d), w_grp_bf16, scale.reshape(1, d))


TOP_K = 4
SWIGLU_LIMIT = 7.0
SWIGLU_ALPHA = 1.702
POOL_WINDOWS = (2, 4, 8, 16)
PAST_LEN = 16384
MOE_TOKEN_TILE = 512
MOE_ROW_TILE = 512
ROUTER_ROW_CHUNK = 256
COMBINE_K_CHUNK = 512
HGRN_TIME_BLOCK = 1024
HGRN_CHUNK = 256
HGRN_SUB = 32
SAMPLE_T_PAD = 8
SAMPLE_SEQ_BLOCK = 8


def kernel(x_prompt, x_sample, c_prompt, c_sample, state_hgrn, cache_pool, g_norm_mix, g_norm_ffn, w_ada, b_ada, w_in_hgrn, lb_logits, g_out_hgrn, w_out_hgrn, w_grp_pool, scale_pool, w_router, b_router, w_gate_up, b_gate_up, w_down, b_down, g_final):
    bp, tp, d = x_prompt.shape
    bs, ts, _ = x_sample.shape
    n_p, n_s = bp * tp, bs * ts
    n_experts = w_router.shape[-1]
    hk = w_out_hgrn.shape[1]
    assert n_s == MOE_TOKEN_TILE and n_p % MOE_TOKEN_TILE == 0
    blocks_p = n_p // MOE_TOKEN_TILE
    n_blocks = blocks_p + 1
    n_sorted = _sorted_rows(MOE_TOKEN_TILE, TOP_K, n_experts)
    n_row_tiles = -(-(n_blocks * n_sorted + n_experts * (MOE_ROW_TILE - BF16_ROWS)) // MOE_ROW_TILE)

    mod = adaln(jnp.concatenate([c_prompt, c_sample], axis=0), w_ada, b_ada)
    mod_p = [mod[l, :bp][:, None, :] for l in range(mod.shape[0])]
    mod_s = [mod[l, bp:] for l in range(mod.shape[0])]

    xp = x_prompt.reshape(n_p, d)
    xs = x_sample.transpose(1, 0, 2).reshape(n_s, d)

    def moe(layer, x_p, y_p, x_s, y_s, w_out, final_norm, pool_p=None):
        route = functools.partial(resid_router, g=g_norm_ffn[layer], w_r=w_router[layer], b_r=b_router[layer],
                                  tile=MOE_TOKEN_TILE, top_k=TOP_K, n_blocks_total=n_blocks)
        x1_p, sorted_rows, pos_p, gate_p, cnt_p, *cache = route(
            x_p, y_p, w_out, mod_p[layer], sorted_in=None, rows_per_seq=tp, block_offset=0, pool=pool_p)
        x1_s, sorted_rows, pos_s, gate_s, cnt_s = route(
            x_s, y_s, w_out, mod_s[layer], sorted_in=sorted_rows, rows_per_seq=None, block_offset=blocks_p)
        cnt8 = jnp.concatenate([cnt_p, cnt_s], axis=0)[:, 0, :n_experts]
        tables = _expert_tables(cnt8, n_sorted, n_row_tiles, MOE_ROW_TILE)
        ys = moe_experts(sorted_rows, w_gate_up, b_gate_up[layer], w_down, b_down[layer], tables,
                         layer=layer, row_tile=MOE_ROW_TILE, limit=SWIGLU_LIMIT, alpha=SWIGLU_ALPHA)
        combine = functools.partial(moe_combine, ys=ys, g_final=g_final, tile=MOE_TOKEN_TILE,
                                    final_norm=final_norm, n_sorted=n_sorted)
        out_p = combine(x1_p, pos=pos_p, gates=gate_p, mod=mod_p[layer], rows_per_seq=tp, block_offset=0)
        out_s = combine(x1_s, pos=pos_s, gates=gate_s, mod=mod_s[layer], rows_per_seq=None,
                        block_offset=blocks_p)
        return (out_p, out_s, *cache)

    w_in = w_in_hgrn[0].astype(BF16)
    proj_s = norm_proj(xs, mod_s[0], g_norm_mix[0], w_in, tile=bs, rows_per_seq=None,
                       n_out_tiles=SAMPLE_T_PAD)
    o_p, state_p = hgrn_recurrence(None, lb_logits, g_out_hgrn[0], None,
                                   layer=0, seq_block=1, time_block=HGRN_TIME_BLOCK, chunk=HGRN_CHUNK,
                                   c_sub=HGRN_SUB, n_valid=HGRN_CHUNK,
                                   norm_proj_of=(x_prompt, mod_p[0], g_norm_mix[0], w_in),
                                   out_dtype=BF16)
    o_s, state_s = hgrn_recurrence(proj_s.reshape(SAMPLE_T_PAD, bs, 4 * hk), lb_logits, g_out_hgrn[0],
                                   state_hgrn[0], layer=0, seq_block=SAMPLE_SEQ_BLOCK,
                                   time_block=SAMPLE_T_PAD, chunk=SAMPLE_T_PAD, c_sub=SAMPLE_T_PAD, n_valid=ts,
                                   time_major=True)
    o_s = o_s[:, :ts].transpose(1, 0, 2).reshape(n_s, hk)
    x_p, x_s = moe(0, xp, o_p.reshape(n_p, hk), xs, o_s, w_out_hgrn[0].astype(BF16), False)

    w_grp = w_grp_pool[0].astype(BF16)
    n_keep = cache_pool.shape[2]
    y_s, cache_s = pool_mixer_sample(x_s.reshape(ts, bs, d), cache_pool[0].transpose(1, 0, 2), mod_s[1],
                                     g_norm_mix[1], w_grp, scale_pool[0],
                                     seq_block=32, windows=POOL_WINDOWS, start_pos=PAST_LEN)
    x_p, x_s, cache_p = moe(1, x_p, None, x_s, y_s.reshape(n_s, d), None, True,
                            pool_p=(g_norm_mix[1], w_grp, scale_pool[0], POOL_WINDOWS, n_keep))

    return (x_p.reshape(bp, tp, d), x_s.reshape(ts, bs, d).transpose(1, 0, 2),
            state_p[None], state_s[None], cache_p[None], cache_s.transpose(1, 0, 2)[None])
```

```python
import functools

import jax
import jax.numpy as jnp
from jax import lax
from jax.experimental import pallas as pl
from jax.experimental.pallas import tpu as pltpu

F32 = jnp.float32
BF16 = jnp.bfloat16

RMS_EPS = 1e-6
LANES = 128
SUBLANES = 8
BF16_ROWS = 16
HEAD_DIM = 128
VMEM_LIMIT = 56 * 1024 * 1024

_dot = functools.partial(jnp.dot, preferred_element_type=F32)


def _params(*semantics):
    return pltpu.CompilerParams(dimension_semantics=semantics, vmem_limit_bytes=VMEM_LIMIT)


def _split_bf16(x, n):
    parts, r = [], x
    for _ in range(n):
        p = r.astype(BF16)
        parts.append(p)
        r = r - p.astype(F32)
    return parts


def _dot_hp(a, b):
    a_hi, a_lo = _split_bf16(a, 2)
    b_hi, b_lo = _split_bf16(b, 2)
    return _dot(a_hi, b_hi) + (_dot(a_hi, b_lo) + _dot(a_lo, b_hi))


def _sigmoid(x):
    return 1.0 / (1.0 + jnp.exp(-x))


def _silu(x):
    return x * _sigmoid(x)


def _rms_norm(x, g):
    ms = jnp.mean(x * x, axis=-1, keepdims=True)
    return x * lax.rsqrt(ms + RMS_EPS) * g


def _adaln_kernel(c_ref, w_ref, b_ref, o_ref):
    o_ref[0] = _dot_hp(_silu(c_ref[...]), w_ref[0]) + b_ref[0]


def adaln(c_all, w_ada, b_ada, *, col_block=1536):
    n_seq, d = c_all.shape
    n_layers, _, d6 = w_ada.shape
    return pl.pallas_call(
        _adaln_kernel,
        grid=(n_layers, d6 // col_block),
        in_specs=[
            pl.BlockSpec((n_seq, d), lambda l, j: (0, 0)),
            pl.BlockSpec((1, d, col_block), lambda l, j: (l, 0, j)),
            pl.BlockSpec((1, 1, col_block), lambda l, j: (l, 0, j)),
        ],
        out_specs=pl.BlockSpec((1, n_seq, col_block), lambda l, j: (l, 0, j)),
        out_shape=jax.ShapeDtypeStruct((n_layers, n_seq, d6), F32),
        compiler_params=_params("parallel", "parallel"),
        name="adaln",
    )(c_all, w_ada, b_ada.reshape(n_layers, 1, d6))


def _mod_spec(mod, k, d, tile, rows_per_seq):
    if rows_per_seq is None:
        return pl.BlockSpec((mod.shape[0], d), lambda i: (0, k))
    tiles_per_seq = rows_per_seq // tile
    return pl.BlockSpec((1, 1, d), lambda i: (i // tiles_per_seq, 0, k))


def _mod_rows(ref, tile):
    m = ref[...].reshape(-1, ref.shape[-1])
    if m.shape[0] not in (1, tile):
        m = jnp.concatenate([m] * (tile // m.shape[0]), axis=0)
    return m


def _norm_proj_kernel(x_ref, sh_ref, sc_ref, g_ref, w_ref, o_ref, *, n_in_tiles):
    tile = x_ref.shape[0]

    @pl.when(pl.program_id(0) < n_in_tiles)
    def _():
        h = _rms_norm(x_ref[...], g_ref[...]) * (1.0 + _mod_rows(sc_ref, tile)) + _mod_rows(sh_ref, tile)
        o_ref[...] = _dot(h.astype(BF16), w_ref[...])

    @pl.when(pl.program_id(0) >= n_in_tiles)
    def _():
        o_ref[...] = jnp.zeros_like(o_ref)


def norm_proj(x, mod, g, w_bf16, *, tile, rows_per_seq, n_out_tiles=None):
    n, d = x.shape
    p = w_bf16.shape[1]
    n_in_tiles = n // tile
    n_out_tiles = n_in_tiles if n_out_tiles is None else n_out_tiles
    return pl.pallas_call(
        functools.partial(_norm_proj_kernel, n_in_tiles=n_in_tiles),
        grid=(n_out_tiles,),
        in_specs=[
            pl.BlockSpec((tile, d), lambda i: (jnp.minimum(i, n_in_tiles - 1), 0)),
            _mod_spec(mod, 0, d, tile, rows_per_seq),
            _mod_spec(mod, 1, d, tile, rows_per_seq),
            pl.BlockSpec((1, d), lambda i: (0, 0)),
            pl.BlockSpec((d, p), lambda i: (0, 0)),
        ],
        out_specs=pl.BlockSpec((tile, p), lambda i: (i, 0)),
        out_shape=jax.ShapeDtypeStruct((n_out_tiles * tile, p), F32),
        compiler_params=_params("parallel"),
        name="hgrn_norm_proj",
    )(x, mod, mod, g.reshape(1, d), w_bf16)


def _cumsum_rows(x, tri):
    hi, mid, lo = _split_bf16(x, 3)
    return _dot(tri, hi) + (_dot(tri, mid) + _dot(tri, lo))


MAX_LOG_DECAY_RANGE = 80.0
MLP_ROW_STEPS = 4
WEIGHT_DMA_PRIORITY = 1


def _hgrn_prep(proj, lb, n_valid):
    c = proj.shape[0]
    hk = proj.shape[1] // 4
    row = lax.broadcasted_iota(jnp.int32, (c, c), 0)
    col = lax.broadcasted_iota(jnp.int32, (c, c), 1)
    zf = proj[:, hk:2 * hk]
    e = jnp.exp(-jnp.abs(zf))
    r = 1.0 / (1.0 + e)
    pos = zf >= 0
    sig_p = jnp.where(pos, 1.0, e) * r
    sig_n = jnp.where(pos, e, 1.0) * r
    logf = jnp.log(lb + (1.0 - lb) * sig_p)
    k = (1.0 - lb) * sig_n
    if n_valid < c:
        live = lax.broadcasted_iota(jnp.int32, (c, 1), 0) < n_valid
        logf = jnp.where(live, logf, 0.0)
        k = jnp.where(live, k, 0.0)
    b = _cumsum_rows(logf, (row >= col).astype(BF16))
    return _silu(proj[:, :hk]), k, proj[:, 2 * hk:3 * hk], _silu(proj[:, 3 * hk:]), b


def _decay_range(b, c_sub):
    c = b.shape[0]
    worst = None
    for i in range(c // c_sub):
        span = b[i * c_sub:i * c_sub + 1, :] - b[(i + 1) * c_sub - 1:(i + 1) * c_sub, :]
        worst = span if worst is None else jnp.maximum(worst, span)
    return jnp.max(worst)


def _head_norm_gate(o, gate, gout):
    outs = []
    for h in range(o.shape[1] // HEAD_DIM):
        hs = slice(h * HEAD_DIM, (h + 1) * HEAD_DIM)
        oh = o[:, hs]
        outs.append(oh * lax.rsqrt(jnp.mean(oh * oh, axis=-1, keepdims=True) + RMS_EPS) * gout * gate[:, hs])
    return jnp.concatenate(outs, axis=-1)


def _hgrn_chunk(prep, gout, st_refs, seq, c_sub):
    q, k, v, gate, b = prep
    c = q.shape[0]
    n_heads = q.shape[1] // HEAD_DIM
    row = lax.broadcasted_iota(jnp.int32, (c, c), 0)
    col = lax.broadcasted_iota(jnp.int32, (c, c), 1)
    causal = row >= col
    n_sub = c // c_sub
    subs = [slice(i * c_sub, (i + 1) * c_sub) for i in range(n_sub)]

    intra, inter = [], []
    for h in range(n_heads):
        hs = slice(h * HEAD_DIM, (h + 1) * HEAD_DIM)
        bh, qh, kh, vh = b[:, hs], q[:, hs], k[:, hs], v[:, hs]
        vb = vh.astype(BF16)
        refs = [bh[i * c_sub + c_sub // 2:i * c_sub + c_sub // 2 + 1, :] for i in range(n_sub)]
        k_own = [kh[rs] * jnp.exp(jnp.minimum(ref - bh[rs], MAX_LOG_DECAY_RANGE))
                 for rs, ref in zip(subs, refs)]
        a_rows = []
        for i in range(n_sub):
            q_hat = (qh[subs[i]] * jnp.exp(bh[subs[i]] - refs[i])).astype(BF16)
            parts = [k_own[j] * jnp.exp(refs[i] - refs[j]) for j in range(i)] + [k_own[i]]
            parts += [jnp.zeros((c_sub, HEAD_DIM), F32)] * (n_sub - 1 - i)
            k_hat = (jnp.concatenate(parts, axis=0) if n_sub > 1 else parts[0]).astype(BF16)
            a_rows.append(lax.dot_general(q_hat, k_hat, (((1,), (1,)), ((), ())),
                                          preferred_element_type=F32))
        att = jnp.where(causal, jnp.concatenate(a_rows, axis=0) if len(a_rows) > 1 else a_rows[0], 0.0)
        intra.append(_dot(att.astype(BF16), vb))
        st = st_refs[seq, h]
        inter.append(lax.dot_general((qh * jnp.exp(bh)).astype(BF16), st.astype(BF16),
                                     (((1,), (1,)), ((), ())), preferred_element_type=F32))
        b_last = bh[c - 1:c, :]
        k_dec = (kh * jnp.exp(b_last - bh)).astype(BF16)
        st_refs[seq, h] = st * jnp.exp(b_last) + lax.dot_general(
            vb, k_dec, (((0,), (0,)), ((), ())), preferred_element_type=F32)
    inter = jnp.concatenate(inter, axis=-1)
    return _head_norm_gate(jnp.concatenate(intra, axis=-1) + inter, gate, gout), inter


def _hgrn_chunk_exact(prep, inter, gout, q_ref, b_ref, oi_ref):
    q, k, v, gate, b = prep
    c = q.shape[0]
    n_heads = q.shape[1] // HEAD_DIM
    q_ref[...] = q
    b_ref[...] = b
    key_row = lax.broadcasted_iota(jnp.int32, (c, 1), 0)

    def row_group(g, carry):
        rows = pl.ds(pl.multiple_of(g * SUBLANES, SUBLANES), SUBLANES)
        for h in range(n_heads):
            hs = slice(h * HEAD_DIM, (h + 1) * HEAD_DIM)
            q_g, b_g = q_ref[rows, hs], b_ref[rows, hs]
            o_rows = []
            for r in range(SUBLANES):
                decay = jnp.exp(jnp.minimum(b_g[r:r + 1] - b[:, hs], 0.0))
                score = jnp.sum(decay * k[:, hs] * q_g[r:r + 1], axis=-1, keepdims=True)
                score = jnp.where(key_row <= g * SUBLANES + r, score, 0.0)
                o_rows.append(jnp.sum(score * v[:, hs], axis=0, keepdims=True))
            oi_ref[rows, hs] = jnp.concatenate(o_rows, axis=0)
        return carry
    lax.fori_loop(0, c // SUBLANES, row_group, 0)
    return _head_norm_gate(oi_ref[...] + inter, gate, gout)


def _lower_bound(lb_logits, layer):
    e = jnp.exp(lb_logits - jnp.max(lb_logits, axis=0, keepdims=True))
    return jnp.sum(e[:layer + 1], axis=0, keepdims=True) / jnp.sum(e, axis=0, keepdims=True)


def _hgrn_rec_kernel(*refs, chunk, c_sub, n_valid, has_state, layer, fused_proj, time_major):
    refs = list(refs)
    if fused_proj:
        x_ref, sh_ref, sc_ref, g_ref, w_ref = refs[:5]
        del refs[:5]
        proj_ref = None
    else:
        proj_ref = refs.pop(0)
    lb_ref, gout_ref = refs[:2]
    del refs[:2]
    s0_ref = refs.pop(0) if has_state else None
    o_ref, sout_ref, st_ref, inter_ref, q_ref, b_ref, oi_ref = refs[:7]
    proj_buf = refs[7] if fused_proj else None
    bb, tb, _ = o_ref.shape
    n_heads = st_ref.shape[1]
    n_chunks = tb // chunk
    j = pl.program_id(1)

    @pl.when(j == 0)
    def _():
        if has_state:
            for s in range(bb):
                for h in range(n_heads):
                    st_ref[s, h] = s0_ref[s, h].T
        else:
            st_ref[...] = jnp.zeros_like(st_ref)

    lb = _lower_bound(lb_ref[...], layer)
    gout = gout_ref[...]

    def project(s, ci):
        rows = pl.ds(pl.multiple_of(ci * chunk, chunk), chunk)
        h = _rms_norm(x_ref[s, rows, :], g_ref[...]) * (1.0 + sc_ref[s]) + sh_ref[s]
        return _dot(h.astype(BF16), w_ref[...])

    if fused_proj:
        for s in range(bb):
            proj_buf[s, 0] = project(s, 0)

    def chunk_body(ci, carry):
        rows = pl.ds(pl.multiple_of(ci * chunk, chunk), chunk)

        def load_proj(s):
            if fused_proj:
                return proj_buf[s, ci % 2]
            return proj_ref[rows, s, :] if time_major else proj_ref[s, rows, :]
        span = None
        preps = [_hgrn_prep(load_proj(s), lb, n_valid) for s in range(bb)]
        for s, prep in enumerate(preps):
            o, inter_ref[s] = _hgrn_chunk(prep, gout, st_ref, s, c_sub)
            o_ref[s, rows, :] = o.astype(o_ref.dtype)
            worst = _decay_range(prep[4], c_sub)
            span = worst if span is None else jnp.maximum(span, worst)
            if fused_proj:
                proj_buf[s, (ci + 1) % 2] = project(s, jnp.minimum(ci + 1, n_chunks - 1))

        @pl.when(jnp.logical_not(span <= MAX_LOG_DECAY_RANGE))
        def _():
            for s in range(bb):
                prep = _hgrn_prep(load_proj(s), lb, n_valid)
                o_ref[s, rows, :] = _hgrn_chunk_exact(prep, inter_ref[s], gout, q_ref, b_ref,
                                                      oi_ref).astype(o_ref.dtype)
        return carry
    lax.fori_loop(0, n_chunks, chunk_body, 0)

    @pl.when(j == pl.num_programs(1) - 1)
    def _():
        for s in range(bb):
            for h in range(n_heads):
                sout_ref[s, h] = st_ref[s, h].T


def hgrn_recurrence(proj, lb_logits, g_out, s0, *, layer, seq_block, time_block, chunk, c_sub, n_valid,
                    norm_proj_of=None, out_dtype=F32, time_major=False):
    fused = norm_proj_of is not None
    if fused:
        x, mod, g, w_in = norm_proj_of
        bsz, t, d = x.shape
        p = w_in.shape[1]
        mod_spec = lambda k: pl.BlockSpec((seq_block, 1, d), lambda i, j: (i, 0, k))
        in_specs = [pl.BlockSpec((seq_block, time_block, d), lambda i, j: (i, j, 0)), mod_spec(0), mod_spec(1),
                    pl.BlockSpec((1, d), lambda i, j: (0, 0)), pl.BlockSpec((d, p), lambda i, j: (0, 0))]
        args = [x, mod, mod, g.reshape(1, d), w_in]
    elif time_major:
        t, bsz, p = proj.shape
        in_specs = [pl.BlockSpec((time_block, seq_block, p), lambda i, j: (j, i, 0))]
        args = [proj]
    else:
        bsz, t, p = proj.shape
        in_specs = [pl.BlockSpec((seq_block, time_block, p), lambda i, j: (i, j, 0))]
        args = [proj]
    hk = p // 4
    n_heads = hk // HEAD_DIM
    has_state = s0 is not None
    st_shape = (seq_block, n_heads, HEAD_DIM, HEAD_DIM)
    st_spec = pl.BlockSpec(st_shape, lambda i, j: (i, 0, 0, 0))
    in_specs += [pl.BlockSpec(lb_logits.shape, lambda i, j: (0, 0)),
                 pl.BlockSpec((1, HEAD_DIM), lambda i, j: (0, 0))]
    args += [lb_logits, g_out.reshape(1, HEAD_DIM)]
    if has_state:
        in_specs.append(st_spec)
        args.append(s0)
    scratch = [pltpu.VMEM(st_shape, F32), pltpu.VMEM((seq_block, chunk, hk), F32)]
    scratch += [pltpu.VMEM((chunk, hk), F32)] * 3
    if fused:
        scratch.append(pltpu.VMEM((seq_block, 2, chunk, p), F32))
    return pl.pallas_call(
        functools.partial(_hgrn_rec_kernel, chunk=chunk, c_sub=c_sub, n_valid=n_valid,
                          has_state=has_state, layer=layer, fused_proj=fused, time_major=time_major),
        grid=(bsz // seq_block, t // time_block),
        in_specs=in_specs,
        out_specs=[pl.BlockSpec((seq_block, time_block, hk), lambda i, j: (i, j, 0)), st_spec],
        out_shape=[jax.ShapeDtypeStruct((bsz, t, hk), out_dtype),
                   jax.ShapeDtypeStruct((bsz, n_heads, HEAD_DIM, HEAD_DIM), F32)],
        scratch_shapes=scratch,
        compiler_params=_params("parallel", "arbitrary"),
        name="hgrn_recurrence",
    )(*args)


def _sorted_rows(tile, top_k, n_experts):
    return tile * top_k + n_experts * BF16_ROWS


def _resid_router_kernel(*refs, top_k, n_experts, has_w_out, chained, row_chunk, pool):
    refs = list(refs)
    x_ref = refs.pop(0)
    if pool is None:
        y_ref = refs.pop(0)
        wo_ref = refs.pop(0) if has_w_out else None
    else:
        msh_ref, msc_ref, mg_ref, wgrp_ref, mscale_ref = refs[:5]
        del refs[:5]
    gt_ref, sh_ref, sc_ref, g_ref, wr_ref, br_ref = refs[:6]
    del refs[:6]
    if chained:
        refs.pop(0)
    x1_ref, xs_ref, pos_ref, gate_ref, cnt_ref = refs[:5]
    tile, d = x_ref.shape
    n_sorted = xs_ref.shape[0]

    if pool is None:
        y = y_ref[...]
        if has_w_out:
            y = _dot(y.astype(BF16), wo_ref[...])
    else:
        windows, halo, tiles_per_seq = pool
        cache_ref, ext_ref = refs[5:7]
        part = pl.program_id(0) % tiles_per_seq

        @pl.when(part == 0)
        def _():
            ext_ref[0:halo, :] = jnp.zeros((halo, d), F32)

        hm = _rms_norm(x_ref[...], mg_ref[...]) * (1.0 + _mod_rows(msc_ref, tile)) + _mod_rows(msh_ref, tile)
        ext_ref[halo:halo + tile, :] = hm
        token = part * tile + lax.broadcasted_iota(jnp.int32, (tile, 1), 0)

        def window_sum(gi, cols):
            acc = hm[:, cols]
            for s in range(1, windows[gi]):
                acc = acc + ext_ref[halo - s:halo - s + tile, cols]
            return acc
        counts = [jnp.minimum(token + 1, w).astype(F32) for w in windows]
        y = _pool_groups(hm, window_sum, counts, wgrp_ref, mscale_ref[...])

        n_keep = cache_ref.shape[1]
        @pl.when(part == tiles_per_seq - 1)
        def _():
            cache_ref[0] = ext_ref[halo + tile - n_keep:halo + tile, :]
        ext_ref[0:halo, :] = ext_ref[tile:tile + halo, :]
    x1 = x_ref[...] + _mod_rows(gt_ref, tile) * y
    x1_ref[...] = x1
    h = _rms_norm(x1, g_ref[...]) * (1.0 + _mod_rows(sc_ref, tile)) + _mod_rows(sh_ref, tile)

    lane = lax.broadcasted_iota(jnp.int32, (tile, LANES), 1).astype(F32)
    logits = jnp.where(lane < n_experts, _dot_hp(h, wr_ref[...]) + br_ref[...], -jnp.inf)
    picks, vals = [], []
    for _ in range(top_k):
        m = jnp.max(logits, axis=-1, keepdims=True)
        pick = jnp.min(jnp.where(logits == m, lane, float(LANES)), axis=-1, keepdims=True)
        picks.append(pick)
        vals.append(m)
        logits = jnp.where(lane == pick, -jnp.inf, logits)
    exps = [jnp.exp(v - vals[0]) for v in vals]
    denom = exps[0]
    for e in exps[1:]:
        denom = denom + e

    onehots = [(lane == p).astype(F32) for p in picks]
    oh_sum = onehots[0]
    for oh in onehots[1:]:
        oh_sum = oh_sum + oh
    row = lax.broadcasted_iota(jnp.int32, (tile, tile), 0)
    col = lax.broadcasted_iota(jnp.int32, (tile, tile), 1)
    before = _dot((row > col).astype(BF16), oh_sum.astype(BF16))
    count = jnp.sum(oh_sum, axis=0, keepdims=True)
    cnt_pad = jnp.floor((count + (BF16_ROWS - 1.0)) * (1.0 / BF16_ROWS)) * BF16_ROWS
    lane8 = lax.broadcasted_iota(jnp.int32, (SUBLANES, LANES), 1)
    run = jnp.broadcast_to(cnt_pad, (SUBLANES, LANES))
    shift = 1
    while shift < n_experts:
        run = run + jnp.where(lane8 >= shift, pltpu.roll(run, shift, 1), 0.0)
        shift *= 2
    pos = before + (run[0:1] - cnt_pad)
    pos_out = jnp.zeros((tile, LANES), F32)
    gate_out = jnp.zeros((tile, LANES), F32)
    for k in range(top_k):
        pos_k = jnp.sum(onehots[k] * pos, axis=-1, keepdims=True)
        pos_out = jnp.where(lane == k, pos_k, pos_out)
        gate_out = jnp.where(lane == k, exps[k] / denom, gate_out)
    pos_ref[...] = pos_out.astype(jnp.int32)
    gate_ref[...] = gate_out
    cnt_ref[0] = cnt_pad.astype(jnp.int32)

    pos_t = pos_out.T
    hb = h.astype(BF16)
    for r0 in range(0, n_sorted, row_chunk):
        slot = (lax.broadcasted_iota(jnp.int32, (row_chunk, tile), 0) + r0).astype(F32)
        sel = jnp.where(slot == pos_t[0:1], 1.0, 0.0)
        for k in range(1, top_k):
            sel = sel + jnp.where(slot == pos_t[k:k + 1], 1.0, 0.0)
        xs_ref[r0:r0 + row_chunk, :] = _dot(sel.astype(BF16), hb).astype(BF16)


def resid_router(x, y, w_out_bf16, mod, g, w_r, b_r, sorted_in, *, tile, rows_per_seq, top_k,
                 block_offset, n_blocks_total, pool=None):
    n, d = x.shape
    n_experts = w_r.shape[1]
    n_sorted = _sorted_rows(tile, top_k, n_experts)
    w_r_pad = jnp.pad(w_r, ((0, 0), (0, LANES - n_experts)))
    b_r_pad = jnp.pad(b_r, (0, LANES - n_experts)).reshape(1, LANES)
    has_w_out = w_out_bf16 is not None
    chained = sorted_in is not None
    row_spec = pl.BlockSpec((tile, d), lambda i: (i, 0))
    lane_spec = pl.BlockSpec((tile, LANES), lambda i: (i, 0))
    full = lambda a: pl.BlockSpec(a.shape, lambda i: (0,) * a.ndim)
    extra_out_specs, extra_out_shape, scratch, pool_static = [], [], [], None
    if pool is None:
        in_specs = [row_spec, pl.BlockSpec((tile, y.shape[1]), lambda i: (i, 0))]
        args = [x, y]
        if has_w_out:
            in_specs.append(full(w_out_bf16))
            args.append(w_out_bf16)
    else:
        g_mix, w_grp, scale, windows, n_keep = pool
        halo = 2 * SUBLANES
        tiles_per_seq = rows_per_seq // tile
        assert max(windows) <= halo <= tile and n_keep <= tile
        in_specs = [row_spec, _mod_spec(mod, 0, d, tile, rows_per_seq), _mod_spec(mod, 1, d, tile, rows_per_seq),
                    pl.BlockSpec((1, d), lambda i: (0, 0)), full(w_grp), pl.BlockSpec((1, d), lambda i: (0, 0))]
        args = [x, mod, mod, g_mix.reshape(1, d), w_grp, scale.reshape(1, d)]
        extra_out_specs = [pl.BlockSpec((1, n_keep, d), lambda i: (i // tiles_per_seq, 0, 0))]
        extra_out_shape = [jax.ShapeDtypeStruct((n // rows_per_seq, n_keep, d), F32)]
        scratch = [pltpu.VMEM((halo + tile, d), F32)]
        pool_static = (windows, halo, tiles_per_seq)
    in_specs += [_mod_spec(mod, 2, d, tile, rows_per_seq), _mod_spec(mod, 3, d, tile, rows_per_seq),
                 _mod_spec(mod, 4, d, tile, rows_per_seq), pl.BlockSpec((1, d), lambda i: (0, 0)),
                 full(w_r_pad), full(b_r_pad)]
    args += [mod, mod, mod, g.reshape(1, d), w_r_pad, b_r_pad]
    aliases = {}
    if chained:
        aliases = {len(args): 1}
        in_specs.append(pl.BlockSpec(memory_space=pl.ANY))
        args.append(sorted_in)
    n_tiles = n // tile
    return pl.pallas_call(
        functools.partial(_resid_router_kernel, top_k=top_k, n_experts=n_experts, has_w_out=has_w_out,
                          chained=chained, row_chunk=ROUTER_ROW_CHUNK, pool=pool_static),
        grid=(n_tiles,),
        in_specs=in_specs,
        out_specs=[row_spec,
                   pl.BlockSpec((n_sorted, d), lambda i: (i + block_offset, 0)),
                   lane_spec, lane_spec,
                   pl.BlockSpec((1, 1, LANES), lambda i: (i, 0, 0))] + extra_out_specs,
        out_shape=[jax.ShapeDtypeStruct((n, d), F32),
                   jax.ShapeDtypeStruct((n_blocks_total * n_sorted, d), BF16),
                   jax.ShapeDtypeStruct((n, LANES), jnp.int32), jax.ShapeDtypeStruct((n, LANES), F32),
                   jax.ShapeDtypeStruct((n_tiles, 1, LANES), jnp.int32)] + extra_out_shape,
        scratch_shapes=scratch,
        input_output_aliases=aliases,
        compiler_params=_params("arbitrary" if pool is not None else "parallel"),
        name="resid_router",
    )(*args)


def _experts_kernel(te_ref, first_ref, rows_ref, base_ref, slo_ref, shi_ref, wslot_ref, nexte_ref, used_ref,
                    sstart_ref, slen_ref, ssrc_ref,
                    xs_hbm, wgu_hbm, bgu_ref, wdn_hbm, bdn_ref, ys_hbm,
                    xbuf, ybuf, wgu_f, wdn_f, wgu_b, wdn_b, in_sem, out_sem, w_sem,
                    *, layer, limit, alpha, col_chunk):
    del ys_hbm
    i = pl.program_id(0)
    used = used_ref[0]
    tm = xbuf.shape[1]

    def copy(src_rows, dst_rows, slot, inbound):
        if inbound:
            return pltpu.make_async_copy(xs_hbm.at[src_rows, :], xbuf.at[slot, dst_rows, :], in_sem.at[slot])
        return pltpu.make_async_copy(ybuf.at[slot, dst_rows, :], xs_hbm.at[src_rows, :], out_sem.at[slot])

    def piece_copies(tile_idx, slot, inbound, wait):
        if wait:
            rows = pl.ds(0, pl.multiple_of(rows_ref[tile_idx], BF16_ROWS))
            copy(rows, rows, slot, inbound).wait()
            return
        base = base_ref[tile_idx]

        def piece(s, c):
            first = sstart_ref[s] - base
            lo = jnp.maximum(first, 0)
            n_rows = pl.multiple_of(jnp.minimum(first + slen_ref[s], tm) - lo, BF16_ROWS)

            @pl.when(n_rows > 0)
            def _():
                src = pl.multiple_of(ssrc_ref[s] + (lo - first), BF16_ROWS)
                copy(pl.ds(src, n_rows), pl.ds(pl.multiple_of(lo, BF16_ROWS), n_rows), slot, inbound).start()
            return c
        lax.fori_loop(slo_ref[tile_idx], shi_ref[tile_idx], piece, 0)

    def weight_copies(e, slot):
        return (pltpu.make_async_copy(wgu_hbm.at[layer, e], wgu_f.at[slot], w_sem.at[slot]),
                pltpu.make_async_copy(wdn_hbm.at[layer, e], wdn_f.at[slot], w_sem.at[slot]))

    @pl.when(i == 0)
    def _():
        xbuf[...] = jnp.zeros_like(xbuf)
        for cp in weight_copies(te_ref[0], 0):
            cp.start(priority=WEIGHT_DMA_PRIORITY)
        piece_copies(0, 0, True, False)

    @pl.when(i < used)
    def _():
        slot = i % 2

        @pl.when(i + 1 < used)
        def _():
            piece_copies(i + 1, 1 - slot, True, False)

        @pl.when(first_ref[i] == 1)
        def _():
            ws = wslot_ref[i]
            for cp in weight_copies(te_ref[i], ws):
                cp.wait()

            @pl.when(nexte_ref[i] >= 0)
            def _():
                for cp in weight_copies(nexte_ref[i], 1 - ws):
                    cp.start(priority=WEIGHT_DMA_PRIORITY)

            @pl.when(rows_ref[i] <= tm - tm // MLP_ROW_STEPS)
            def _():
                wgu_b[...] = wgu_f[ws].astype(BF16)
                wdn_b[...] = wdn_f[ws].astype(BF16)

        piece_copies(i, slot, True, True)

        @pl.when(i >= 2)
        def _():
            piece_copies(i - 2, slot, False, True)

        d_ff = wdn_b.shape[0]
        e = te_ref[i]
        b_gu = bgu_ref[pl.ds(e, 1), :]
        b_dn = bdn_ref[pl.ds(e, 1), :]

        def mlp(n_rows, cast_slot=None):
            x = xbuf[slot, :n_rows, :]
            y = None
            for c0 in range(0, d_ff, col_chunk):
                cs = slice(c0, c0 + col_chunk)
                us = slice(d_ff + c0, d_ff + c0 + col_chunk)
                if cast_slot is None:
                    w_gate, w_up, w_down = wgu_b[:, cs], wgu_b[:, us], wdn_b[cs, :]
                else:
                    w_gate = wgu_f[cast_slot, :, cs].astype(BF16)
                    w_up = wgu_f[cast_slot, :, us].astype(BF16)
                    w_down = wdn_f[cast_slot, cs, :].astype(BF16)
                    wgu_b[:, cs], wgu_b[:, us], wdn_b[cs, :] = w_gate, w_up, w_down
                gate = jnp.minimum(_dot(x, w_gate) + b_gu[:, cs], limit)
                up = jnp.clip(_dot(x, w_up) + b_gu[:, us], -limit, limit)
                act = ((up + 1.0) * (gate * _sigmoid(alpha * gate))).astype(BF16)
                part = _dot(act, w_down)
                y = part if y is None else y + part
            ybuf[slot, :n_rows, :] = (y + b_dn).astype(BF16)

        step = tm // MLP_ROW_STEPS
        for part in range(1, MLP_ROW_STEPS):
            @pl.when((rows_ref[i] > (part - 1) * step) & (rows_ref[i] <= part * step))
            def _():
                mlp(part * step)

        @pl.when((rows_ref[i] > tm - step) & (first_ref[i] != 1))
        def _():
            mlp(tm)

        @pl.when((rows_ref[i] > tm - step) & (first_ref[i] == 1))
        def _():
            mlp(tm, cast_slot=wslot_ref[i])
        piece_copies(i, slot, False, False)

    @pl.when(i == pl.num_programs(0) - 1)
    def _():
        @pl.when(used >= 2)
        def _():
            piece_copies(used - 2, used % 2, False, True)
        piece_copies(used - 1, (used - 1) % 2, False, True)


def moe_experts(sorted_rows, w_gu, b_gu, w_dn, b_dn, tables, *, layer, row_tile, limit, alpha, col_chunk=512):
    n_rows, d = sorted_rows.shape
    d_gu = w_gu.shape[-1]
    d_ff = w_dn.shape[-2]
    n_tiles = tables[0].shape[0]
    vmem = lambda a: pl.BlockSpec(a.shape, lambda i, *_: (0,) * a.ndim)
    any_spec = pl.BlockSpec(memory_space=pl.ANY)
    return pl.pallas_call(
        functools.partial(_experts_kernel, layer=layer, limit=limit, alpha=alpha,
                          col_chunk=min(col_chunk, d_ff)),
        grid_spec=pltpu.PrefetchScalarGridSpec(
            num_scalar_prefetch=len(tables),
            grid=(n_tiles,),
            in_specs=[any_spec, any_spec, vmem(b_gu), any_spec, vmem(b_dn)],
            out_specs=any_spec,
            scratch_shapes=[
                pltpu.VMEM((2, row_tile, d), BF16), pltpu.VMEM((2, row_tile, d), BF16),
                pltpu.VMEM((2, d, d_gu), F32), pltpu.VMEM((2, d_ff, d), F32),
                pltpu.VMEM((d, d_gu), BF16), pltpu.VMEM((d_ff, d), BF16),
                pltpu.SemaphoreType.DMA((2,)), pltpu.SemaphoreType.DMA((2,)), pltpu.SemaphoreType.DMA((2,)),
            ],
        ),
        out_shape=jax.ShapeDtypeStruct((n_rows, d), BF16),
        input_output_aliases={len(tables): 0},
        compiler_params=_params("arbitrary"),
        name="moe_experts",
    )(*tables, sorted_rows, w_gu, b_gu, w_dn, b_dn)


def _expert_tables(cnt, n_sorted, n_tiles, row_tile):
    n_blocks, n_experts = cnt.shape

    def prefix_sum(a):
        n = a.shape[-1]
        upto = jnp.arange(n)[:, None] <= jnp.arange(n)[None, :]
        return jnp.sum(jnp.where(upto, a[..., :, None], 0), axis=-2)

    local_off = prefix_sum(cnt) - cnt
    seg_end = prefix_sum(cnt.T)
    seg_start = seg_end - cnt.T
    seg_src = jnp.arange(n_blocks, dtype=jnp.int32)[None, :] * n_sorted + local_off.T
    total = seg_end[:, -1]
    padded = (total + row_tile - 1) // row_tile * row_tile
    pad_end = prefix_sum(padded)
    pad_start = pad_end - padded
    tiles = jnp.arange(n_tiles, dtype=jnp.int32)
    experts = jnp.arange(n_experts, dtype=jnp.int32)
    n_used = pad_end[-1] // row_tile
    tile_expert = jnp.minimum(jnp.sum(tiles[:, None] * row_tile >= pad_end[None, :], axis=1), n_experts - 1)
    is_expert = tile_expert[:, None] == experts[None, :]

    def of_expert(a):
        if a.ndim == 1:
            return jnp.sum(jnp.where(is_expert, a[None, :], 0), axis=1)
        return jnp.sum(jnp.where(is_expert[:, :, None], a[None, :, :], 0), axis=1)

    live = tiles < n_used
    tile_base = tiles * row_tile - of_expert(pad_start)
    tile_rows = jnp.where(live, jnp.clip(of_expert(total) - tile_base, 0, row_tile), 0)
    tile_first = (live & (tile_base == 0)).astype(jnp.int32)
    seg_lo = jnp.sum(of_expert(seg_end) <= tile_base[:, None], axis=1)
    seg_hi = jnp.sum(of_expert(seg_start) < tile_base[:, None] + row_tile, axis=1)
    seg_lo = tile_expert * n_blocks + jnp.minimum(seg_lo, seg_hi)
    seg_hi = tile_expert * n_blocks + seg_hi
    owns = total > 0
    order = prefix_sum(owns.astype(jnp.int32)) - 1
    experts = jnp.arange(n_experts, dtype=jnp.int32)
    later = (experts[None, :] > experts[:, None]) & owns[None, :]
    next_expert = jnp.min(jnp.where(later, experts[None, :], n_experts), axis=1)
    next_expert = jnp.where(next_expert < n_experts, next_expert, -1)
    i32 = lambda a: a.astype(jnp.int32)
    return (i32(tile_expert), tile_first, i32(tile_rows), i32(tile_base), i32(seg_lo), i32(seg_hi),
            i32(of_expert(order) % 2), i32(of_expert(next_expert)), i32(n_used).reshape(1),
            i32(seg_start.reshape(-1)), i32(cnt.T.reshape(-1)), i32(seg_src.reshape(-1)))


def _combine_kernel(x_ref, pos_ref, gate_ref, gt_ref, gfin_ref, ys_ref, o_ref, *, top_k, final_norm, k_chunk):
    tile, d = x_ref.shape
    n_sorted = ys_ref.shape[0]
    pos = pos_ref[...].astype(F32)
    gates = gate_ref[...]
    acc = None
    for r0 in range(0, n_sorted, k_chunk):
        slot = (lax.broadcasted_iota(jnp.int32, (tile, k_chunk), 1) + r0).astype(F32)
        w = jnp.where(slot == pos[:, 0:1], gates[:, 0:1], 0.0)
        for k in range(1, top_k):
            w = w + jnp.where(slot == pos[:, k:k + 1], gates[:, k:k + 1], 0.0)
        part = _dot(w.astype(BF16), ys_ref[r0:r0 + k_chunk, :])
        acc = part if acc is None else acc + part
    out = x_ref[...] + _mod_rows(gt_ref, tile) * acc
    if final_norm:
        out = _rms_norm(out, gfin_ref[...])
    o_ref[...] = out


def moe_combine(x, ys, pos, gates, mod, g_final, *, tile, rows_per_seq, final_norm, block_offset, n_sorted):
    n, d = x.shape
    return pl.pallas_call(
        functools.partial(_combine_kernel, top_k=TOP_K, final_norm=final_norm, k_chunk=COMBINE_K_CHUNK),
        grid=(n // tile,),
        in_specs=[
            pl.BlockSpec((tile, d), lambda i: (i, 0)),
            pl.BlockSpec((tile, LANES), lambda i: (i, 0)),
            pl.BlockSpec((tile, LANES), lambda i: (i, 0)),
            _mod_spec(mod, 5, d, tile, rows_per_seq),
            pl.BlockSpec((1, d), lambda i: (0, 0)),
            pl.BlockSpec((n_sorted, d), lambda i: (i + block_offset, 0)),
        ],
        out_specs=pl.BlockSpec((tile, d), lambda i: (i, 0)),
        out_shape=jax.ShapeDtypeStruct((n, d), F32),
        compiler_params=_params("parallel"),
        name="moe_combine",
    )(x, pos, gates, mod, g_final.reshape(1, d), ys)


def _pool_groups(h, window_sum, counts, w_ref, scale):
    n_groups = w_ref.shape[0]
    dg = h.shape[-1] // n_groups
    outs = []
    for gi in range(n_groups):
        cols = slice(gi * dg, (gi + 1) * dg)
        pooled = window_sum(gi, cols) / counts[gi] - h[:, cols]
        outs.append(_dot(pooled.astype(BF16), w_ref[gi]))
    return jnp.concatenate(outs, axis=-1) * scale


def _pool_sample_kernel(x_ref, buf_ref, sh_ref, sc_ref, g_ref, w_ref, scale_ref, y_ref, cache_ref,
                        *, windows, start_pos):
    t_len = x_ref.shape[0]
    n_prev = buf_ref.shape[0]
    hs = [_rms_norm(x_ref[t], g_ref[...]) * (1.0 + sc_ref[...]) + sh_ref[...] for t in range(t_len)]

    def ext(r):
        return buf_ref[r] if r < n_prev else hs[r - n_prev]

    for t in range(t_len):
        def window_sum(gi, cols):
            acc = hs[t][:, cols]
            for s in range(1, windows[gi]):
                acc = acc + ext(n_prev + t - s)[:, cols]
            return acc
        counts = [float(min(start_pos + t + 1, w)) for w in windows]
        y_ref[t] = _pool_groups(hs[t], window_sum, counts, w_ref, scale_ref[...])
    for r in range(n_prev):
        cache_ref[r] = ext(t_len + r)


def pool_mixer_sample(x_t, buf_t, mod, g, w_grp_bf16, scale, *, seq_block, windows, start_pos):
    t_len, n_seq, d = x_t.shape
    n_prev = buf_t.shape[0]
    assert start_pos >= n_prev >= max(windows) - 1
    mod_spec = lambda k: pl.BlockSpec((seq_block, d), lambda i: (i, k))
    return pl.pallas_call(
        functools.partial(_pool_sample_kernel, windows=windows, start_pos=start_pos),
        grid=(n_seq // seq_block,),
        in_specs=[
            pl.BlockSpec((t_len, seq_block, d), lambda i: (0, i, 0)),
            pl.BlockSpec((n_prev, seq_block, d), lambda i: (0, i, 0)),
            mod_spec(0), mod_spec(1),
            pl.BlockSpec((1, d), lambda i: (0, 0)),
            pl.BlockSpec(w_grp_bf16.shape, lambda i: (0, 0, 0)),
            pl.BlockSpec((1, d), lambda i: (0, 0)),
        ],
        out_specs=[pl.BlockSpec((t_len, seq_block, d), lambda i: (0, i, 0)),
                   pl.BlockSpec((n_prev, seq_block, d), lambda i: (0, i, 0))],
        out_shape=[jax.ShapeDtypeStruct((t_len, n_seq, d), F32), jax.ShapeDtypeStruct((n_prev, n_seq, d), F32)],
        compiler_params=_params("parallel"),
        name="pool_mixer_sample",
    )(x_t, buf_t, mod, mod, g.reshape(1, d), w_grp_bf16, scale.reshape(1, d))


TOP_K = 4
SWIGLU_LIMIT = 7.0
SWIGLU_ALPHA = 1.702
POOL_WINDOWS = (2, 4, 8, 16)
PAST_LEN = 16384
MOE_TOKEN_TILE = 512
MOE_ROW_TILE = 512
ROUTER_ROW_CHUNK = 256
COMBINE_K_CHUNK = 512
HGRN_TIME_BLOCK = 1024
HGRN_CHUNK = 256
HGRN_SUB = 32
SAMPLE_T_PAD = 8
SAMPLE_SEQ_BLOCK = 8


def kernel(x_prompt, x_sample, c_prompt, c_sample, state_hgrn, cache_pool, g_norm_mix, g_norm_ffn, w_ada, b_ada, w_in_hgrn, lb_logits, g_out_hgrn, w_out_hgrn, w_grp_pool, scale_pool, w_router, b_router, w_gate_up, b_gate_up, w_down, b_down, g_final):
    bp, tp, d = x_prompt.shape
    bs, ts, _ = x_sample.shape
    n_p, n_s = bp * tp, bs * ts
    n_experts = w_router.shape[-1]
    hk = w_out_hgrn.shape[1]
    assert n_s == MOE_TOKEN_TILE and n_p % MOE_TOKEN_TILE == 0
    blocks_p = n_p // MOE_TOKEN_TILE
    n_blocks = blocks_p + 1
    n_sorted = _sorted_rows(MOE_TOKEN_TILE, TOP_K, n_experts)
    n_row_tiles = -(-(n_blocks * n_sorted + n_experts * (MOE_ROW_TILE - BF16_ROWS)) // MOE_ROW_TILE)

    mod = adaln(jnp.concatenate([c_prompt, c_sample], axis=0), w_ada, b_ada)
    mod_p = [mod[l, :bp][:, None, :] for l in range(mod.shape[0])]
    mod_s = [mod[l, bp:] for l in range(mod.shape[0])]

    xp = x_prompt.reshape(n_p, d)
    xs = x_sample.transpose(1, 0, 2).reshape(n_s, d)

    def moe(layer, x_p, y_p, x_s, y_s, w_out, final_norm, pool_p=None):
        route = functools.partial(resid_router, g=g_norm_ffn[layer], w_r=w_router[layer], b_r=b_router[layer],
                                  tile=MOE_TOKEN_TILE, top_k=TOP_K, n_blocks_total=n_blocks)
        x1_p, sorted_rows, pos_p, gate_p, cnt_p, *cache = route(
            x_p, y_p, w_out, mod_p[layer], sorted_in=None, rows_per_seq=tp, block_offset=0, pool=pool_p)
        x1_s, sorted_rows, pos_s, gate_s, cnt_s = route(
            x_s, y_s, w_out, mod_s[layer], sorted_in=sorted_rows, rows_per_seq=None, block_offset=blocks_p)
        cnt8 = jnp.concatenate([cnt_p, cnt_s], axis=0)[:, 0, :n_experts]
        tables = _expert_tables(cnt8, n_sorted, n_row_tiles, MOE_ROW_TILE)
        ys = moe_experts(sorted_rows, w_gate_up, b_gate_up[layer], w_down, b_down[layer], tables,
                         layer=layer, row_tile=MOE_ROW_TILE, limit=SWIGLU_LIMIT, alpha=SWIGLU_ALPHA)
        combine = functools.partial(moe_combine, ys=ys, g_final=g_final, tile=MOE_TOKEN_TILE,
                                    final_norm=final_norm, n_sorted=n_sorted)
        out_p = combine(x1_p, pos=pos_p, gates=gate_p, mod=mod_p[layer], rows_per_seq=tp, block_offset=0)
        out_s = combine(x1_s, pos=pos_s, gates=gate_s, mod=mod_s[layer], rows_per_seq=None,
                        block_offset=blocks_p)
        return (out_p, out_s, *cache)

    w_in = w_in_hgrn[0].astype(BF16)
    proj_s = norm_proj(xs, mod_s[0], g_norm_mix[0], w_in, tile=bs, rows_per_seq=None,
                       n_out_tiles=SAMPLE_T_PAD)
    o_p, state_p = hgrn_recurrence(None, lb_logits, g_out_hgrn[0], None,
                                   layer=0, seq_block=1, time_block=HGRN_TIME_BLOCK, chunk=HGRN_CHUNK,
                                   c_sub=HGRN_SUB, n_valid=HGRN_CHUNK,
                                   norm_proj_of=(x_prompt, mod_p[0], g_norm_mix[0], w_in),
                                   out_dtype=BF16)
    o_s, state_s = hgrn_recurrence(proj_s.reshape(SAMPLE_T_PAD, bs, 4 * hk), lb_logits, g_out_hgrn[0],
                                   state_hgrn[0], layer=0, seq_block=SAMPLE_SEQ_BLOCK,
                                   time_block=SAMPLE_T_PAD, chunk=SAMPLE_T_PAD, c_sub=SAMPLE_T_PAD, n_valid=ts,
                                   time_major=True)
    o_s = o_s[:, :ts].transpose(1, 0, 2).reshape(n_s, hk)
    x_p, x_s = moe(0, xp, o_p.reshape(n_p, hk), xs, o_s, w_out_hgrn[0].astype(BF16), False)

    w_grp = w_grp_pool[0].astype(BF16)
    n_keep = cache_pool.shape[2]
    y_s, cache_s = pool_mixer_sample(x_s.reshape(ts, bs, d), cache_pool[0].transpose(1, 0, 2), mod_s[1],
                                     g_norm_mix[1], w_grp, scale_pool[0],
                                     seq_block=32, windows=POOL_WINDOWS, start_pos=PAST_LEN)
    x_p, x_s, cache_p = moe(1, x_p, None, x_s, y_s.reshape(n_s, d), None, True,
                            pool_p=(g_norm_mix[1], w_grp, scale_pool[0], POOL_WINDOWS, n_keep))

    return (x_p.reshape(bp, tp, d), x_s.reshape(ts, bs, d).transpose(1, 0, 2),
            state_p[None], state_s[None], cache_p[None], cache_s.transpose(1, 0, 2)[None])
```
